```python
import math
import jax, jax.numpy as jnp
from jax import lax
import numpy as np

D_MODEL = 1024
BATCH = 8
SEQ = 8192
DEPTH = 2

GRID_W = 64
NORM_EPS = 1e-6
NEG_INF = -1e30

HY_D = D_MODEL // 4
HY_EMB = 33
HY_BANDS = (HY_EMB - 1) // 2
HY_FFN = 64
HY_MIN_DECAY = math.log(1e-2) / 1.5
HY_MAX_DECAY = math.log(1e-2) / 0.3

RW_N = 64
RW_D = D_MODEL // 2
RW_H = RW_D // RW_N
RW_W_LORA = 64
RW_A_LORA = 64
RW_G_LORA = 128
RW_GN_EPS = 64e-5

NA_HD = 64
NA_D = D_MODEL // 4
NA_H = NA_D // NA_HD
NA_KR = 8
NA_KC = 16

MIX_D = HY_D + RW_D + NA_D
IN_SIZES = (3 * HY_D, 3 * RW_D, RW_G_LORA, 2 * RW_W_LORA, 2 * RW_A_LORA, 3 * NA_D)
IN_D = sum(IN_SIZES)

MOE_GROUPS = 4
MOE_PER_GROUP = 8
MOE_EXPERTS = MOE_GROUPS * MOE_PER_GROUP
MOE_TOPK = 2
MOE_FF = 512
MOE_BLOCK = 256

kernel_name = 'hybrid_hyena_rwkv7_natten_hmoe_encoder'


def rmsnorm(x, g):
    x32 = x.astype(jnp.float32)
    y = x32 * lax.rsqrt(jnp.mean(x32 * x32, axis=-1, keepdims=True) + NORM_EPS)
    return y.astype(x.dtype) * g


def shift_prev(u):
    return jnp.pad(u, ((0, 0), (1, 0), (0, 0)))[:, :-1]


def shift_next(u):
    return jnp.pad(u, ((0, 0), (0, 1), (0, 0)))[:, 1:]


def hyena_positional_features(L):
    t = jnp.linspace(0.0, 1.0, L, dtype=jnp.float32)[:, None]
    w = (2.0 * math.pi / L) * jnp.arange(L, dtype=jnp.float32)[:, None]
    f = jnp.linspace(1e-4, HY_BANDS - 1, HY_BANDS, dtype=jnp.float32)[None, :]
    z = jnp.concatenate([t, jnp.cos(f * w), -jnp.sin(f * w)], axis=-1)
    return z, t


def hyena_filters(z, t, w1, b1, w2, b2, w3, b3, wout, freq):
    f32 = jnp.float32
    fr = freq.astype(f32)
    act = lambda u: jnp.sin(fr * u)
    h = act(z @ w1.astype(f32) + b1.astype(f32))
    h = act(h @ w2.astype(f32) + b2.astype(f32))
    h = act(h @ w3.astype(f32) + b3.astype(f32))
    h = (h @ wout.astype(f32)).reshape(-1, 2, HY_D)
    deltas = jnp.abs(jnp.linspace(HY_MIN_DECAY, HY_MAX_DECAY, HY_D, dtype=f32))
    h = h * jnp.exp(-t[:, :, None] * deltas)
    return h * lax.rsqrt(jnp.sum(h * h, axis=(0, 1), keepdims=True) + 1e-6)


def hyena_mixer(u, conv_w, conv_b, filt, skip):
    up = jnp.pad(u, ((0, 0), (1, 1), (0, 0)))
    uc = up[:, :-2] * conv_w[0] + up[:, 1:-1] * conv_w[1] + up[:, 2:] * conv_w[2] + conv_b
    x0, x1, v = jnp.split(uc, 3, axis=-1)
    z = (v * x1).astype(jnp.float32)
    L = z.shape[1]
    k2 = jnp.concatenate([filt[:1, 0] + filt[:1, 1], filt[1:, 0],
                          jnp.zeros((1, HY_D), jnp.float32), filt[1:, 1][::-1]], axis=0)
    y = jnp.fft.irfft(jnp.fft.rfft(z, n=2 * L, axis=1) * jnp.fft.rfft(k2, axis=0)[None],
                      n=2 * L, axis=1)[:, :L]
    y = y + z * skip.astype(jnp.float32)
    return (x0.astype(jnp.float32) * y).astype(u.dtype)


def wkv7_scan(r, w, k, v, a, b, reverse):
    Bn, L, H, N = r.shape
    xs = tuple(jnp.moveaxis(u, 1, 0) for u in (r, w, k, v, a, b))

    def step(S, inp):
        rt, wt, kt, vt, at, bt = inp
        sa = jnp.einsum('bhvk,bhk->bhv', S, at)
        S = S * wt[:, :, None, :] + sa[..., None] * bt[:, :, None, :] + vt[..., None] * kt[:, :, None, :]
        return S, jnp.einsum('bhvk,bhk->bhv', S, rt)

    _, y = lax.scan(step, jnp.zeros((Bn, H, N, N), jnp.float32), xs, reverse=reverse)
    return jnp.moveaxis(y, 0, 1)


def rwkv7_mixer(r, k, v, g_lo, w_lo, a_lo, mu, w0, w2, a0, a2, k_k, k_a, r_k, g2, ln_w, ln_b):
    f32 = jnp.float32
    Bn, L, C = r.shape

    def tshift(u, m):
        return u + m[0] * (shift_prev(u) - u) + m[1] * (shift_next(u) - u)

    r, k, v = (tshift(u, m).astype(f32) for u, m in zip((r, k, v), mu))
    w_lo = w_lo.astype(f32).reshape(Bn, L, 2, RW_W_LORA)
    a_lo = a_lo.astype(f32).reshape(Bn, L, 2, RW_A_LORA)
    w_log = -jax.nn.softplus(-(w0.astype(f32) + jnp.einsum('bldr,drc->bldc', jnp.tanh(w_lo), w2.astype(f32)))) - 0.5
    decay = jnp.exp(-jnp.exp(w_log))
    a = jax.nn.sigmoid(a0.astype(f32) + jnp.einsum('bldr,drc->bldc', a_lo, a2.astype(f32)))
    heads = lambda u: u.reshape(u.shape[:-1] + (RW_H, RW_N))
    kk = heads(k * k_k.astype(f32))
    kk = (kk * lax.rsqrt(jnp.maximum(jnp.sum(kk * kk, axis=-1, keepdims=True), 1e-24))).reshape(Bn, L, C)
    k_dir = k[:, :, None] * (1.0 + (a - 1.0) * k_a.astype(f32))
    rh, vh = heads(r), heads(v)
    y_f = wkv7_scan(rh, heads(decay[:, :, 0]), heads(k_dir[:, :, 0]), vh, heads(-kk), heads(kk * a[:, :, 0]), False)
    y_b = wkv7_scan(rh, heads(decay[:, :, 1]), heads(k_dir[:, :, 1]), vh, heads(-kk), heads(kk * a[:, :, 1]), True)
    y = y_f + y_b
    mean = jnp.mean(y, axis=-1, keepdims=True)
    var = jnp.mean(jnp.square(y - mean), axis=-1, keepdims=True)
    y = ((y - mean) * lax.rsqrt(var + RW_GN_EPS)).reshape(Bn, L, C) * ln_w.astype(f32) + ln_b.astype(f32)
    bonus = jnp.sum(rh * heads(k_dir[:, :, 0] + k_dir[:, :, 1]) * r_k.astype(f32), axis=-1, keepdims=True) * vh
    gate = jax.nn.sigmoid(g_lo.astype(f32)) @ g2.astype(f32)
    return ((y + bonus.reshape(Bn, L, C)) * gate).astype(g_lo.dtype)


def neighborhood_attention(q, k, v, rpb):
    Bn, L, H, Dh = q.shape
    R = L // GRID_W
    KR = min(NA_KR, R)
    f32 = jnp.float32
    grid = lambda u: u.reshape(Bn, R, GRID_W, H, Dh)
    rows = jnp.arange(R)
    row_idx = jnp.clip(rows - KR // 2, 0, R - KR)[:, None] + jnp.arange(KR)[None, :]
    k_rows = grid(k)[:, row_idx]
    v_rows = grid(v)[:, row_idx]
    cols = jnp.arange(GRID_W)
    col_start = jnp.clip(cols - NA_KC // 2, 0, GRID_W - NA_KC)
    in_band = (cols[None, :] >= col_start[:, None]) & (cols[None, :] < col_start[:, None] + NA_KC)
    dr = row_idx - rows[:, None] + (NA_KR - 1)
    dc = jnp.clip(cols[None, :] - cols[:, None], -(NA_KC - 1), NA_KC - 1) + (NA_KC - 1)
    bias = rpb.astype(f32)[:, dr[:, None, :, None], dc[None, :, None, :]]
    bias = jnp.where(in_band[None, None, :, None, :], bias, NEG_INF)
    s = jnp.einsum('brqhd,brikhd->bhrqik', grid(q).astype(f32), k_rows.astype(f32)) * (Dh ** -0.5)
    p = jax.nn.softmax(s + bias[None], axis=(-2, -1))
    o = jnp.einsum('bhrqik,brikhd->brqhd', p, v_rows.astype(f32))
    return o.reshape(Bn, L, H * Dh).astype(q.dtype)


def hier_moe(h, wg, bg, we, be, w1, w3, w2):
    N, Dm = h.shape
    f32 = jnp.float32
    h32 = h.astype(f32)
    g_logits = h32 @ wg.astype(f32) + bg.astype(f32)
    g_sel = jnp.argmax(g_logits, axis=-1)
    g_prob = jnp.take_along_axis(jax.nn.softmax(g_logits, axis=-1), g_sel[:, None], axis=-1)
    e_logits = (h32 @ we.astype(f32) + be.astype(f32)).reshape(N, MOE_GROUPS, MOE_PER_GROUP)
    e_logits = jnp.take_along_axis(e_logits, g_sel[:, None, None], axis=1)[:, 0]
    top_val, top_idx = lax.top_k(e_logits, MOE_TOPK)
    gate = g_prob * jax.nn.softmax(top_val, axis=-1)
    expert = g_sel[:, None] * MOE_PER_GROUP + top_idx
    M = N * MOE_TOPK
    flat_e = expert.reshape(M)
    order = jnp.argsort(flat_e)
    e_sorted = flat_e[order]
    counts = jnp.bincount(flat_e, length=MOE_EXPERTS)
    padded = (counts + MOE_BLOCK - 1) // MOE_BLOCK * MOE_BLOCK
    pad_end = jnp.cumsum(padded)
    first = (jnp.cumsum(counts) - counts)[e_sorted]
    slot = (pad_end - padded)[e_sorted] + jnp.arange(M) - first
    n_blocks = -(-M // MOE_BLOCK) + MOE_EXPERTS
    slot_src = jnp.full((n_blocks * MOE_BLOCK,), M, jnp.int32).at[slot].set(order.astype(jnp.int32))
    tok_src = jnp.append(jnp.arange(M, dtype=jnp.int32) // MOE_TOPK, N)[slot_src].reshape(n_blocks, MOE_BLOCK)
    w_src = jnp.append(gate.reshape(M), 0.0)[slot_src].reshape(n_blocks, MOE_BLOCK)
    block_e = jnp.minimum(jnp.searchsorted(pad_end, jnp.arange(n_blocks) * MOE_BLOCK, side='right'),
                          MOE_EXPERTS - 1)
    h_pad = jnp.concatenate([h, jnp.zeros((1, Dm), h.dtype)], axis=0)

    def expert_block(args):
        tok, wt, e = args
        xb = h_pad[tok]
        yb = (jax.nn.silu(xb @ w1[e]) * (xb @ w3[e])) @ w2[e]
        return yb * wt[:, None].astype(yb.dtype)

    y = lax.map(expert_block, (tok_src, w_src, block_e))
    return jnp.zeros((N + 1, Dm), h.dtype).at[tok_src.reshape(-1)].add(y.reshape(-1, Dm))[:N]


def setup_inputs(seed: int = 0) -> dict:
    key = jax.random.key(seed)
    ks = iter(jax.random.split(key, 48))

    def nrm(shape, scale):
        return scale * jax.random.normal(next(ks), shape, jnp.float32)

    Ld = DEPTH
    w0_ramp = jnp.linspace(-6.0, -1.0, RW_D, dtype=jnp.float32)
    return {
        'x': nrm((BATCH, SEQ, D_MODEL), 1.0),
        'norm1_g': 1.0 + nrm((Ld, D_MODEL), 0.05),
        'w_in': nrm((Ld, D_MODEL, IN_D), D_MODEL ** -0.5),
        'hy_conv_w': nrm((Ld, 3, 3 * HY_D), 3 ** -0.5),
        'hy_conv_b': nrm((Ld, 3 * HY_D), 0.1),
        'hy_w1': nrm((Ld, HY_EMB, HY_FFN), HY_EMB ** -0.5),
        'hy_b1': nrm((Ld, HY_FFN), 0.1),
        'hy_w2': nrm((Ld, HY_FFN, HY_FFN), HY_FFN ** -0.5),
        'hy_b2': nrm((Ld, HY_FFN), 0.1),
        'hy_w3': nrm((Ld, HY_FFN, HY_FFN), HY_FFN ** -0.5),
        'hy_b3': nrm((Ld, HY_FFN), 0.1),
        'hy_wout': nrm((Ld, HY_FFN, 2 * HY_D), HY_FFN ** -0.5),
        'hy_freq': 1.0 + nrm((Ld, HY_FFN), 0.1),
        'hy_skip': nrm((Ld, HY_D), 0.5),
        'rw_mu': jax.random.uniform(next(ks), (Ld, 3, 2, RW_D), jnp.float32, 0.0, 0.5),
        'rw_w0': w0_ramp + nrm((Ld, 2, RW_D), 0.1),
        'rw_w2': nrm((Ld, 2, RW_W_LORA, RW_D), 0.1 * RW_W_LORA ** -0.5),
        'rw_a0': nrm((Ld, 2, RW_D), 0.1),
        'rw_a2': nrm((Ld, 2, RW_A_LORA, RW_D), 0.1 * RW_A_LORA ** -0.5),
        'rw_kk': 0.85 + nrm((Ld, RW_D), 0.05),
        'rw_ka': 1.0 + nrm((Ld, RW_D), 0.05),
        'rw_rk': nrm((Ld, RW_H, RW_N), 0.1),
        'rw_g2': nrm((Ld, RW_G_LORA, RW_D), RW_G_LORA ** -0.5),
        'rw_ln_w': 1.0 + nrm((Ld, RW_D), 0.05),
        'rw_ln_b': nrm((Ld, RW_D), 0.01),
        'na_rpb': nrm((Ld, NA_H, 2 * NA_KR - 1, 2 * NA_KC - 1), 0.02),
        'w_out': nrm((Ld, MIX_D, D_MODEL), MIX_D ** -0.5),
        'norm2_g': 1.0 + nrm((Ld, D_MODEL), 0.05),
        'moe_wg': nrm((Ld, D_MODEL, MOE_GROUPS), D_MODEL ** -0.5),
        'moe_bg': nrm((Ld, MOE_GROUPS), 0.01),
        'moe_we': nrm((Ld, D_MODEL, MOE_EXPERTS), D_MODEL ** -0.5),
        'moe_be': nrm((Ld, MOE_EXPERTS), 0.01),
        'moe_w1': nrm((Ld, MOE_EXPERTS, D_MODEL, MOE_FF), D_MODEL ** -0.5),
        'moe_w3': nrm((Ld, MOE_EXPERTS, D_MODEL, MOE_FF), D_MODEL ** -0.5),
        'moe_w2': nrm((Ld, MOE_EXPERTS, MOE_FF, D_MODEL), MOE_FF ** -0.5),
        'norm_f_g': 1.0 + nrm((D_MODEL,), 0.05),
    }


def reference(x, norm1_g, w_in, hy_conv_w, hy_conv_b, hy_w1, hy_b1, hy_w2, hy_b2, hy_w3, hy_b3,
              hy_wout, hy_freq, hy_skip, rw_mu, rw_w0, rw_w2, rw_a0, rw_a2, rw_kk, rw_ka, rw_rk,
              rw_g2, rw_ln_w, rw_ln_b, na_rpb, w_out, norm2_g, moe_wg, moe_bg, moe_we, moe_be,
              moe_w1, moe_w3, moe_w2, norm_f_g):
    Bn, L, _ = x.shape
    z_pos, t_pos = hyena_positional_features(L)
    splits = np.cumsum(IN_SIZES)[:-1].tolist()
    for l in range(DEPTH):
        h = rmsnorm(x, norm1_g[l])
        proj = h @ w_in[l]
        hy_u, rw_rkv, rw_g, rw_w, rw_a, na_qkv = jnp.split(proj, splits, axis=-1)
        filt = hyena_filters(z_pos, t_pos, hy_w1[l], hy_b1[l], hy_w2[l], hy_b2[l],
                             hy_w3[l], hy_b3[l], hy_wout[l], hy_freq[l])
        y_hy = hyena_mixer(hy_u, hy_conv_w[l], hy_conv_b[l], filt, hy_skip[l])
        r, k, v = jnp.split(rw_rkv, 3, axis=-1)
        y_rw = rwkv7_mixer(r, k, v, rw_g, rw_w, rw_a, rw_mu[l], rw_w0[l], rw_w2[l], rw_a0[l],
                           rw_a2[l], rw_kk[l], rw_ka[l], rw_rk[l], rw_g2[l], rw_ln_w[l], rw_ln_b[l])
        q, kn, vn = (u.reshape(Bn, L, NA_H, NA_HD) for u in jnp.split(na_qkv, 3, axis=-1))
        y_na = neighborhood_attention(q, kn, vn, na_rpb[l])
        x = x + jnp.concatenate([y_hy, y_rw, y_na], axis=-1) @ w_out[l]
        h = rmsnorm(x, norm2_g[l]).reshape(Bn * L, D_MODEL)
        x = x + hier_moe(h, moe_wg[l], moe_bg[l], moe_we[l], moe_be[l],
                         moe_w1[l], moe_w3[l], moe_w2[l]).reshape(Bn, L, D_MODEL)
    return rmsnorm(x, norm_f_g)
```

```python
import functools
import math

import jax
import jax.numpy as jnp
import numpy as np
from jax import lax
from jax.experimental import pallas as pl
from jax.experimental.pallas import tpu as pltpu

f32 = jnp.float32
bf16 = jnp.bfloat16

D_MODEL = 1024
DEPTH = 2
GRID_W = 64
NORM_EPS = 1e-6
NEG_INF = -1e30

HY_D = D_MODEL // 4
HY_EMB = 33
HY_BANDS = (HY_EMB - 1) // 2
HY_FFN = 64
HY_MIN_DECAY = math.log(1e-2) / 1.5
HY_MAX_DECAY = math.log(1e-2) / 0.3

RW_N = 64
RW_D = D_MODEL // 2
RW_H = RW_D // RW_N
RW_W_LORA = 64
RW_A_LORA = 64
RW_G_LORA = 128
RW_GN_EPS = 64e-5

NA_HD = 64
NA_D = D_MODEL // 4
NA_H = NA_D // NA_HD
NA_KR = 8
NA_KC = 16

MIX_D = HY_D + RW_D + NA_D
IN_SIZES = (3 * HY_D, 3 * RW_D, RW_G_LORA, 2 * RW_W_LORA, 2 * RW_A_LORA, 3 * NA_D)
IN_D = sum(IN_SIZES)

MOE_GROUPS = 4
MOE_PER_GROUP = 8
MOE_EXPERTS = MOE_GROUPS * MOE_PER_GROUP
MOE_TOPK = 2
MOE_FF = 512
MOE_BLOCK = 256

LANES = 128
WKV_CHUNK = 64
WKV_TIME_BLOCK = 256

_DN = {'nn': (((1,), (0,)), ((), ())), 'nt': (((1,), (1,)), ((), ())), 'tn': (((0,), (0,)), ((), ()))}


def _mm(a, b, dims='nn'):
    return lax.dot_general(a.astype(bf16), b.astype(bf16), _DN[dims], preferred_element_type=f32)


def _wkv_kernel(r_ref, lw_ref, k_ref, v_ref, a_ref, b_ref, y_ref, ht_ref, *, tb):
    T = WKV_CHUNK
    H2 = 2 * T

    @pl.when(pl.program_id(2) == 0)
    def _():
        ht_ref[...] = jnp.zeros_like(ht_ref)

    row = lax.broadcasted_iota(jnp.int32, (H2, H2), 0)
    col = lax.broadcasted_iota(jnp.int32, (H2, H2), 1)
    same64 = (row // T) == (col // T)
    strict = same64 & ((row % T) > (col % T))
    incl = same64 & ((row % T) >= (col % T))
    same16 = (row // 16) == (col // 16)
    same32 = (row // 32) == (col // 32)
    eye = jnp.where(row == col, 1.0, 0.0).astype(f32)
    trow = lax.broadcasted_iota(jnp.int32, (T, T), 0)
    tcol = lax.broadcasted_iota(jnp.int32, (T, T), 1)
    tri = jnp.where(trow >= tcol, 1.0, 0.0).astype(bf16)
    lo_lane = lax.broadcasted_iota(jnp.int32, (T, LANES), 1) < T

    def stack(x):
        return jnp.concatenate([jnp.where(lo_lane, x, 0.0), jnp.where(lo_lane, 0.0, x)], axis=0)

    for c in range(tb // T):
        sl = pl.ds(c * T, T)
        r = r_ref[0, sl, :]
        lw = lw_ref[0, sl, :]
        k = k_ref[0, sl, :]
        v = v_ref[0, sl, :]
        a = a_ref[0, sl, :]
        b = b_ref[0, sl, :]
        l1 = lw.astype(bf16)
        r1 = lw - l1.astype(f32)
        l2 = r1.astype(bf16)
        l3 = (r1 - l2.astype(f32)).astype(bf16)
        dd = lambda x: lax.dot_general(tri, x, _DN['nn'], preferred_element_type=f32)
        cs = dd(l1) + (dd(l2) + dd(l3))
        cs_end = cs[T - 1:T, :]
        ep = jnp.exp(cs)
        em = jnp.exp(-cs)
        eprev = jnp.exp(cs - lw)
        e_end = jnp.exp(cs_end - cs)
        g_end = jnp.exp(cs_end)
        As = stack(a * eprev)
        Rs = stack(r * ep)
        Bs = stack(b * em)
        Ks = stack(k * em)
        Vs = stack(v)
        Bgs = stack(b * e_end)
        Kgs = stack(k * e_end)
        S = _mm(jnp.concatenate([As, Rs], axis=0), jnp.concatenate([Bs, Ks], axis=0), 'nt')
        N = jnp.where(strict, S[:H2, :H2], 0.0)
        Aak = jnp.where(strict, S[:H2, H2:], 0.0)
        Arb = jnp.where(incl, S[H2:, :H2], 0.0)
        Ark = jnp.where(incl, S[H2:, H2:], 0.0)
        Nd = jnp.where(same16, N, 0.0)
        N32 = jnp.where(same32 & jnp.logical_not(same16), N, 0.0)
        N64 = jnp.where(jnp.logical_not(same32), N, 0.0)
        X = eye + Nd
        P = _mm(Nd, Nd)
        X = X + _mm(P, X)
        P = _mm(P, P)
        X = X + _mm(P, X)
        P = _mm(P, P)
        X = X + _mm(P, X)
        X = X + _mm(X, _mm(N32, X))
        X = X + _mm(X, _mm(N64, X))
        Ht = ht_ref[...]
        W0 = _mm(Aak, Vs) + _mm(As, Ht, 'nt')
        U = _mm(X, W0)
        Y = _mm(Rs, Ht, 'nt') + (_mm(Arb, U) + _mm(Ark, Vs))
        y_ref[0, sl, :] = Y[:T] + Y[T:]
        upd = _mm(jnp.concatenate([U, Vs], axis=0), jnp.concatenate([Bgs, Kgs], axis=0), 'tn')
        ht_ref[...] = jnp.where(same64, Ht * g_end + upd, 0.0)


def wkv7_chunked(r, lw, k, v, a, b):
    S, L, C = r.shape
    tb = WKV_TIME_BLOCK
    spec = pl.BlockSpec((1, tb, LANES), lambda i, h, t: (i, t, h))
    return pl.pallas_call(
        functools.partial(_wkv_kernel, tb=tb),
        grid=(S, C // LANES, L // tb),
        in_specs=[spec] * 6,
        out_specs=spec,
        out_shape=jax.ShapeDtypeStruct((S, L, C), f32),
        scratch_shapes=[pltpu.VMEM((LANES, LANES), f32)],
        compiler_params=pltpu.CompilerParams(dimension_semantics=("parallel", "parallel", "arbitrary")),
        name="wkv7_chunked",
    )(r, lw, k, v, a, b)


def _final_norm_kernel(x_ref, g_ref, o_ref):
    x = x_ref[...]
    o_ref[...] = x * lax.rsqrt(jnp.mean(x * x, axis=-1, keepdims=True) + NORM_EPS) * g_ref[...]


def final_rmsnorm(x, g):
    Bn, L, D = x.shape
    tm = 1024
    out = pl.pallas_call(
        _final_norm_kernel,
        grid=(Bn * L // tm,),
        in_specs=[pl.BlockSpec((tm, D), lambda i: (i, 0)), pl.BlockSpec((1, D), lambda i: (0, 0))],
        out_specs=pl.BlockSpec((tm, D), lambda i: (i, 0)),
        out_shape=jax.ShapeDtypeStruct((Bn * L, D), x.dtype),
        name="final_rmsnorm",
    )(x.reshape(Bn * L, D), g.reshape(1, D))
    return out.reshape(Bn, L, D)


def rmsnorm(x, g):
    x32 = x.astype(f32)
    y = x32 * lax.rsqrt(jnp.mean(x32 * x32, axis=-1, keepdims=True) + NORM_EPS)
    return y.astype(x.dtype) * g


def shift_prev(u):
    return jnp.pad(u, ((0, 0), (1, 0), (0, 0)))[:, :-1]


def shift_next(u):
    return jnp.pad(u, ((0, 0), (0, 1), (0, 0)))[:, 1:]


def hyena_positional_features(L):
    t = jnp.linspace(0.0, 1.0, L, dtype=f32)[:, None]
    w = (2.0 * math.pi / L) * jnp.arange(L, dtype=f32)[:, None]
    f = jnp.linspace(1e-4, HY_BANDS - 1, HY_BANDS, dtype=f32)[None, :]
    z = jnp.concatenate([t, jnp.cos(f * w), -jnp.sin(f * w)], axis=-1)
    return z, t


def hyena_filters(z, t, w1, b1, w2, b2, w3, b3, wout, freq):
    fr = freq.astype(f32)
    act = lambda u: jnp.sin(fr * u)
    h = act(z @ w1.astype(f32) + b1.astype(f32))
    h = act(h @ w2.astype(f32) + b2.astype(f32))
    h = act(h @ w3.astype(f32) + b3.astype(f32))
    h = (h @ wout.astype(f32)).reshape(-1, 2, HY_D)
    deltas = jnp.abs(jnp.linspace(HY_MIN_DECAY, HY_MAX_DECAY, HY_D, dtype=f32))
    h = h * jnp.exp(-t[:, :, None] * deltas)
    return h * lax.rsqrt(jnp.sum(h * h, axis=(0, 1), keepdims=True) + 1e-6)


def hyena_mixer(u, conv_w, conv_b, filt, skip):
    up = jnp.pad(u, ((0, 0), (1, 1), (0, 0)))
    uc = up[:, :-2] * conv_w[0] + up[:, 1:-1] * conv_w[1] + up[:, 2:] * conv_w[2] + conv_b
    x0, x1, v = jnp.split(uc, 3, axis=-1)
    z = (v * x1).astype(f32)
    L = z.shape[1]
    k2 = jnp.concatenate([filt[:1, 0] + filt[:1, 1], filt[1:, 0],
                          jnp.zeros((1, HY_D), f32), filt[1:, 1][::-1]], axis=0)
    y = jnp.fft.irfft(jnp.fft.rfft(z, n=2 * L, axis=1) * jnp.fft.rfft(k2, axis=0)[None],
                      n=2 * L, axis=1)[:, :L]
    y = y + z * skip.astype(f32)
    return (x0.astype(f32) * y).astype(u.dtype)


def rwkv7_mixer(r, k, v, g_lo, w_lo, a_lo, mu, w0, w2, a0, a2, k_k, k_a, r_k, g2, ln_w, ln_b):
    Bn, L, C = r.shape

    def tshift(u, m):
        return u + m[0] * (shift_prev(u) - u) + m[1] * (shift_next(u) - u)

    r, k, v = (tshift(u, m).astype(f32) for u, m in zip((r, k, v), mu))
    w_lo = w_lo.astype(f32).reshape(Bn, L, 2, RW_W_LORA)
    a_lo = a_lo.astype(f32).reshape(Bn, L, 2, RW_A_LORA)
    w_log = -jax.nn.softplus(-(w0.astype(f32) + jnp.einsum('bldr,drc->bldc', jnp.tanh(w_lo), w2.astype(f32)))) - 0.5
    log_decay = -jnp.exp(w_log)
    a = jax.nn.sigmoid(a0.astype(f32) + jnp.einsum('bldr,drc->bldc', a_lo, a2.astype(f32)))
    heads = lambda u: u.reshape(u.shape[:-1] + (RW_H, RW_N))
    kk = heads(k * k_k.astype(f32))
    kk = (kk * lax.rsqrt(jnp.maximum(jnp.sum(kk * kk, axis=-1, keepdims=True), 1e-24))).reshape(Bn, L, C)
    k_dir = k[:, :, None] * (1.0 + (a - 1.0) * k_a.astype(f32))
    rh, vh = heads(r), heads(v)
    both = lambda fwd, bwd: jnp.concatenate([fwd, bwd[:, ::-1]], axis=0)
    y2 = wkv7_chunked(both(r, r), both(log_decay[:, :, 0], log_decay[:, :, 1]),
                      both(k_dir[:, :, 0], k_dir[:, :, 1]), both(v, v), both(-kk, -kk),
                      both(kk * a[:, :, 0], kk * a[:, :, 1]))
    y = heads(y2[:Bn] + y2[Bn:, ::-1])
    mean = jnp.mean(y, axis=-1, keepdims=True)
    var = jnp.mean(jnp.square(y - mean), axis=-1, keepdims=True)
    y = ((y - mean) * lax.rsqrt(var + RW_GN_EPS)).reshape(Bn, L, C) * ln_w.astype(f32) + ln_b.astype(f32)
    bonus = jnp.sum(rh * heads(k_dir[:, :, 0] + k_dir[:, :, 1]) * r_k.astype(f32), axis=-1, keepdims=True) * vh
    gate = jax.nn.sigmoid(g_lo.astype(f32)) @ g2.astype(f32)
    return ((y + bonus.reshape(Bn, L, C)) * gate).astype(g_lo.dtype)


def neighborhood_attention(q, k, v, rpb):
    Bn, L, H, Dh = q.shape
    R = L // GRID_W
    KR = min(NA_KR, R)
    grid = lambda u: u.reshape(Bn, R, GRID_W, H, Dh)
    rows = jnp.arange(R)
    row_idx = jnp.clip(rows - KR // 2, 0, R - KR)[:, None] + jnp.arange(KR)[None, :]
    k_rows = grid(k)[:, row_idx]
    v_rows = grid(v)[:, row_idx]
    cols = jnp.arange(GRID_W)
    col_start = jnp.clip(cols - NA_KC // 2, 0, GRID_W - NA_KC)
    in_band = (cols[None, :] >= col_start[:, None]) & (cols[None, :] < col_start[:, None] + NA_KC)
    dr = row_idx - rows[:, None] + (NA_KR - 1)
    dc = jnp.clip(cols[None, :] - cols[:, None], -(NA_KC - 1), NA_KC - 1) + (NA_KC - 1)
    bias = rpb.astype(f32)[:, dr[:, None, :, None], dc[None, :, None, :]]
    bias = jnp.where(in_band[None, None, :, None, :], bias, NEG_INF)
    s = jnp.einsum('brqhd,brikhd->bhrqik', grid(q).astype(f32), k_rows.astype(f32)) * (Dh ** -0.5)
    p = jax.nn.softmax(s + bias[None], axis=(-2, -1))
    o = jnp.einsum('bhrqik,brikhd->brqhd', p, v_rows.astype(f32))
    return o.reshape(Bn, L, H * Dh).astype(q.dtype)


def hier_moe(h, wg, bg, we, be, w1, w3, w2):
    N, Dm = h.shape
    h32 = h.astype(f32)
    g_logits = h32 @ wg.astype(f32) + bg.astype(f32)
    g_sel = jnp.argmax(g_logits, axis=-1)
    g_prob = jnp.take_along_axis(jax.nn.softmax(g_logits, axis=-1), g_sel[:, None], axis=-1)
    e_logits = (h32 @ we.astype(f32) + be.astype(f32)).reshape(N, MOE_GROUPS, MOE_PER_GROUP)
    e_logits = jnp.take_along_axis(e_logits, g_sel[:, None, None], axis=1)[:, 0]
    top_val, top_idx = lax.top_k(e_logits, MOE_TOPK)
    gate = g_prob * jax.nn.softmax(top_val, axis=-1)
    expert = g_sel[:, None] * MOE_PER_GROUP + top_idx
    M = N * MOE_TOPK
    flat_e = expert.reshape(M)
    order = jnp.argsort(flat_e)
    e_sorted = flat_e[order]
    counts = jnp.bincount(flat_e, length=MOE_EXPERTS)
    padded = (counts + MOE_BLOCK - 1) // MOE_BLOCK * MOE_BLOCK
    pad_end = jnp.cumsum(padded)
    first = (jnp.cumsum(counts) - counts)[e_sorted]
    slot = (pad_end - padded)[e_sorted] + jnp.arange(M) - first
    n_blocks = -(-M // MOE_BLOCK) + MOE_EXPERTS
    slot_src = jnp.full((n_blocks * MOE_BLOCK,), M, jnp.int32).at[slot].set(order.astype(jnp.int32))
    tok_src = jnp.append(jnp.arange(M, dtype=jnp.int32) // MOE_TOPK, N)[slot_src].reshape(n_blocks, MOE_BLOCK)
    w_src = jnp.append(gate.reshape(M), 0.0)[slot_src].reshape(n_blocks, MOE_BLOCK)
    block_e = jnp.minimum(jnp.searchsorted(pad_end, jnp.arange(n_blocks) * MOE_BLOCK, side='right'),
                          MOE_EXPERTS - 1)
    h_pad = jnp.concatenate([h, jnp.zeros((1, Dm), h.dtype)], axis=0)

    def expert_block(args):
        tok, wt, e = args
        xb = h_pad[tok]
        yb = (jax.nn.silu(xb @ w1[e]) * (xb @ w3[e])) @ w2[e]
        return yb * wt[:, None].astype(yb.dtype)

    y = lax.map(expert_block, (tok_src, w_src, block_e))
    return jnp.zeros((N + 1, Dm), h.dtype).at[tok_src.reshape(-1)].add(y.reshape(-1, Dm))[:N]


def kernel(x, norm1_g, w_in, hy_conv_w, hy_conv_b, hy_w1, hy_b1, hy_w2, hy_b2, hy_w3, hy_b3, hy_wout, hy_freq, hy_skip, rw_mu, rw_w0, rw_w2, rw_a0, rw_a2, rw_kk, rw_ka, rw_rk, rw_g2, rw_ln_w, rw_ln_b, na_rpb, w_out, norm2_g, moe_wg, moe_bg, moe_we, moe_be, moe_w1, moe_w3, moe_w2, norm_f_g):
    Bn, L, _ = x.shape
    z_pos, t_pos = hyena_positional_features(L)
    splits = np.cumsum(IN_SIZES)[:-1].tolist()
    for l in range(DEPTH):
        h = rmsnorm(x, norm1_g[l])
        proj = h @ w_in[l]
        hy_u, rw_rkv, rw_g, rw_w, rw_a, na_qkv = jnp.split(proj, splits, axis=-1)
        filt = hyena_filters(z_pos, t_pos, hy_w1[l], hy_b1[l], hy_w2[l], hy_b2[l],
                             hy_w3[l], hy_b3[l], hy_wout[l], hy_freq[l])
        y_hy = hyena_mixer(hy_u, hy_conv_w[l], hy_conv_b[l], filt, hy_skip[l])
        r, k, v = jnp.split(rw_rkv, 3, axis=-1)
        y_rw = rwkv7_mixer(r, k, v, rw_g, rw_w, rw_a, rw_mu[l], rw_w0[l], rw_w2[l], rw_a0[l],
                           rw_a2[l], rw_kk[l], rw_ka[l], rw_rk[l], rw_g2[l], rw_ln_w[l], rw_ln_b[l])
        q, kn, vn = (u.reshape(Bn, L, NA_H, NA_HD) for u in jnp.split(na_qkv, 3, axis=-1))
        y_na = neighborhood_attention(q, kn, vn, na_rpb[l])
        x = x + jnp.concatenate([y_hy, y_rw, y_na], axis=-1) @ w_out[l]
        h = rmsnorm(x, norm2_g[l]).reshape(Bn * L, D_MODEL)
        x = x + hier_moe(h, moe_wg[l], moe_bg[l], moe_we[l], moe_be[l],
                         moe_w1[l], moe_w3[l], moe_w2[l]).reshape(Bn, L, D_MODEL)
    return final_rmsnorm(x, norm_f_g)
```

```python
import functools
import math

import jax
import jax.numpy as jnp
import numpy as np
from jax import lax
from jax.experimental import pallas as pl
from jax.experimental.pallas import tpu as pltpu

f32 = jnp.float32
bf16 = jnp.bfloat16

D_MODEL = 1024
DEPTH = 2
GRID_W = 64
NORM_EPS = 1e-6
NEG_INF = -1e30

HY_D = D_MODEL // 4
HY_EMB = 33
HY_BANDS = (HY_EMB - 1) // 2
HY_FFN = 64
HY_MIN_DECAY = math.log(1e-2) / 1.5
HY_MAX_DECAY = math.log(1e-2) / 0.3

RW_N = 64
RW_D = D_MODEL // 2
RW_H = RW_D // RW_N
RW_W_LORA = 64
RW_A_LORA = 64
RW_G_LORA = 128
RW_GN_EPS = 64e-5

NA_HD = 64
NA_D = D_MODEL // 4
NA_H = NA_D // NA_HD
NA_KR = 8
NA_KC = 16

MIX_D = HY_D + RW_D + NA_D
IN_SIZES = (3 * HY_D, 3 * RW_D, RW_G_LORA, 2 * RW_W_LORA, 2 * RW_A_LORA, 3 * NA_D)
IN_D = sum(IN_SIZES)

MOE_GROUPS = 4
MOE_PER_GROUP = 8
MOE_EXPERTS = MOE_GROUPS * MOE_PER_GROUP
MOE_TOPK = 2
MOE_FF = 512
MOE_BLOCK = 256

LANES = 128
WKV_CHUNK = 64
WKV_TIME_BLOCK = 128
NA_ROWS_PER_STEP = 4
VMEM_LIMIT_BYTES = 48 * 1024 * 1024

_DN = {'nn': (((1,), (0,)), ((), ())), 'nt': (((1,), (1,)), ((), ())), 'tn': (((0,), (0,)), ((), ()))}


def _mm(a, b, dims='nn'):
    return lax.dot_general(a.astype(bf16), b.astype(bf16), _DN[dims], preferred_element_type=f32)


def _wkv_kernel(r_ref, k_ref, v_ref, kk_ref, lw_ref, a_ref, ka_ref, y_ref, ht_ref, *, tb, batch):
    T = WKV_CHUNK
    H2 = 2 * T
    nc = tb // T
    npairs = r_ref.shape[-1] // LANES
    mm = _mm

    @pl.when(pl.program_id(1) == 0)
    def _():
        ht_ref[...] = jnp.zeros_like(ht_ref)

    d = pl.program_id(0) // batch
    sign = 1 - 2 * d
    row = lax.broadcasted_iota(jnp.int32, (H2, H2), 0)
    col = lax.broadcasted_iota(jnp.int32, (H2, H2), 1)
    same64 = (row // T) == (col // T)
    tdiff = (row % T - col % T) * sign
    strict = same64 & (tdiff > 0)
    incl = same64 & (tdiff >= 0)
    same16 = (row // 16) == (col // 16)
    same32 = (row // 32) == (col // 32)
    off32 = same32 & jnp.logical_not(same16)
    off64 = jnp.logical_not(same32)
    eye = jnp.where(row == col, 1.0, 0.0).astype(f32)
    trow = lax.broadcasted_iota(jnp.int32, (T, T), 0)
    tcol = lax.broadcasted_iota(jnp.int32, (T, T), 1)
    tri = jnp.where((trow - tcol) * sign >= 0, 1.0, 0.0).astype(bf16)
    lo_lane = lax.broadcasted_iota(jnp.int32, (T, LANES), 1) < T
    is_bwd = d == 1

    def stack(x):
        return jnp.concatenate([jnp.where(lo_lane, x, 0.0), jnp.where(lo_lane, 0.0, x)], axis=0)

    inst = [(s, p) for s in range(nc) for p in range(npairs)]
    offs = [pl.multiple_of((s + d * (nc - 1 - 2 * s)) * T, T) for s in range(nc)]

    def load(ref, s, p):
        return ref[0, pl.ds(offs[s], T), p * LANES:(p + 1) * LANES]

    cs_l = []
    for s, p in inst:
        lw = load(lw_ref, s, p)
        l1 = lw.astype(bf16)
        r1 = lw - l1.astype(f32)
        l2 = r1.astype(bf16)
        l3 = (r1 - l2.astype(f32)).astype(bf16)
        dd = lambda x: lax.dot_general(tri, x, _DN['nn'], preferred_element_type=f32)
        cs_l.append((dd(l1) + (dd(l2) + dd(l3)), lw))
    ops = []
    for (s, p), (cs, lw) in zip(inst, cs_l):
        r = load(r_ref, s, p)
        k = load(k_ref, s, p)
        v = load(v_ref, s, p)
        kk = load(kk_ref, s, p)
        a = load(a_ref, s, p)
        ka = ka_ref[:, p * LANES:(p + 1) * LANES]
        kd = k * (1.0 + (a - 1.0) * ka)
        b = kk * a
        cs_end = jnp.where(is_bwd, cs[0:1, :], cs[T - 1:T, :])
        em = jnp.exp(-cs)
        e_end = jnp.exp(cs_end - cs)
        ops.append(dict(
            As=stack(-kk * jnp.exp(cs - lw)), Rs=stack(r * jnp.exp(cs)), Bs=stack(b * em), Ks=stack(kd * em),
            Vs=stack(v), Bgs=stack(b * e_end), Kgs=stack(kd * e_end), g_end=jnp.exp(cs_end)))
    for o in ops:
        S = mm(jnp.concatenate([o['As'], o['Rs']], axis=0), jnp.concatenate([o['Bs'], o['Ks']], axis=0), 'nt')
        N = jnp.where(strict, S[:H2, :H2], 0.0)
        o['Aak'] = jnp.where(strict, S[:H2, H2:], 0.0)
        o['Arb'] = jnp.where(incl, S[H2:, :H2], 0.0)
        o['Ark'] = jnp.where(incl, S[H2:, H2:], 0.0)
        o['Nd'] = jnp.where(same16, N, 0.0)
        o['N32'] = jnp.where(off32, N, 0.0)
        o['N64'] = jnp.where(off64, N, 0.0)
    for o in ops:
        o['X'] = eye + o['Nd']
        o['P'] = mm(o['Nd'], o['Nd'])
    for it in range(3):
        for o in ops:
            o['X'] = o['X'] + mm(o['P'], o['X'])
        if it < 2:
            for o in ops:
                o['P'] = mm(o['P'], o['P'])
    for key in ('N32', 'N64'):
        for o in ops:
            o['Z'] = mm(o[key], o['X'])
        for o in ops:
            o['X'] = o['X'] + mm(o['X'], o['Z'])
    for o in ops:
        o['W0'] = mm(o['Aak'], o['Vs'])
        o['Yv'] = mm(o['Ark'], o['Vs'])
    hts = [ht_ref[p] for p in range(npairs)]
    for s in range(nc):
        cur = [ops[s * npairs + p] for p in range(npairs)]
        ah = [mm(o['As'], hts[p], 'nt') for p, o in enumerate(cur)]
        yh = [mm(o['Rs'], hts[p], 'nt') for p, o in enumerate(cur)]
        us = [mm(o['X'], o['W0'] + ah[p]) for p, o in enumerate(cur)]
        ys = [yh[p] + (mm(o['Arb'], us[p]) + o['Yv']) for p, o in enumerate(cur)]
        upd = [mm(jnp.concatenate([us[p], o['Vs']], axis=0), jnp.concatenate([o['Bgs'], o['Kgs']], axis=0), 'tn')
               for p, o in enumerate(cur)]
        hts = [jnp.where(same64, hts[p] * o['g_end'] + upd[p], 0.0) for p, o in enumerate(cur)]
        for p in range(npairs):
            y_ref[0, pl.ds(offs[s], T), p * LANES:(p + 1) * LANES] = ys[p][:T] + ys[p][T:]
    for p in range(npairs):
        ht_ref[p] = hts[p]


def wkv7_chunked(r, k, v, kk, lw2, a2, ka):
    Bn, L, C = r.shape
    tb = WKV_TIME_BLOCK
    nt = L // tb
    tmap = lambda i, t: t + (i // Bn) * (nt - 1 - 2 * t)
    shared = pl.BlockSpec((1, tb, C), lambda i, t: (i % Bn, tmap(i, t), 0))
    per_dir = pl.BlockSpec((1, tb, C), lambda i, t: (i % Bn, tmap(i, t), i // Bn))
    return pl.pallas_call(
        functools.partial(_wkv_kernel, tb=tb, batch=Bn),
        grid=(2 * Bn, nt),
        in_specs=[shared, shared, shared, shared, per_dir, per_dir, pl.BlockSpec((1, C), lambda i, t: (0, 0))],
        out_specs=pl.BlockSpec((1, tb, C), lambda i, t: (i, tmap(i, t), 0)),
        out_shape=jax.ShapeDtypeStruct((2 * Bn, L, C), f32),
        scratch_shapes=[pltpu.VMEM((C // LANES, LANES, LANES), f32)],
        compiler_params=pltpu.CompilerParams(dimension_semantics=("parallel", "arbitrary")),
        name="wkv7_chunked",
    )(r, k, v, kk, lw2, a2, ka)


def _na_kernel(q_ref, k_ref, v_ref, bias_ref, o_ref, kb_ref, vb_ref, *, rows_per_step, n_rows, kr):
    W = GRID_W
    rb = pl.program_id(2)

    @pl.when(rb == 0)
    def _():
        kb_ref[...] = k_ref[0].astype(bf16)
        vb_ref[...] = v_ref[0].astype(bf16)

    lo_lane = lax.broadcasted_iota(jnp.int32, (W, LANES), 1) < NA_HD
    scale = NA_HD ** -0.5
    rows = []
    for j in range(rows_per_step):
        r = rb * rows_per_step + j
        start = jnp.clip(r - kr // 2, 0, n_rows - kr)
        rows.append((start, start - r + (NA_KR - 1)))
    s_list = []
    for j, (start, didx) in enumerate(rows):
        q = q_ref[0, j * W:(j + 1) * W, :] * scale
        qs = jnp.concatenate([jnp.where(lo_lane, q, 0.0), jnp.where(lo_lane, 0.0, q)], axis=0)
        kw = kb_ref[pl.ds(pl.multiple_of(start * W, W), kr * W), :]
        s_list.append(_mm(qs, kw, 'nt') + bias_ref[didx, 0])
    p_list = []
    for s in s_list:
        m = jnp.max(s, axis=-1, keepdims=True)
        p = jnp.exp(s - m)
        p_list.append((p, jnp.sum(p, axis=-1, keepdims=True)))
    for j, ((start, _), (p, l)) in enumerate(zip(rows, p_list)):
        vw = vb_ref[pl.ds(pl.multiple_of(start * W, W), kr * W), :]
        o = _mm(p, vw) / l
        o_ref[0, j * W:(j + 1) * W, :] = jnp.where(lo_lane, o[:W], o[W:])


def na_bias_table(rpb, n_rows):
    W = GRID_W
    kr = min(NA_KR, n_rows)
    cols = jnp.arange(W)
    col_start = jnp.clip(cols - NA_KC // 2, 0, W - NA_KC)
    in_band = (cols[None, :] >= col_start[:, None]) & (cols[None, :] < col_start[:, None] + NA_KC)
    dc = jnp.clip(cols[None, :] - cols[:, None], -(NA_KC - 1), NA_KC - 1) + (NA_KC - 1)
    dr = jnp.arange(8)[:, None] + jnp.arange(kr)[None, :]
    dr = jnp.minimum(dr, 2 * NA_KR - 2)
    tab = rpb.astype(f32)[:, dr[:, None, :, None], dc[None, :, None, :]]
    tab = jnp.where(in_band[None, None, :, None, :], tab, NEG_INF)
    H = rpb.shape[0]
    tab = tab.reshape(H // 2, 2, 8, W, kr * W).transpose(2, 0, 1, 3, 4)
    return tab.reshape(8, H // 2, 2 * W, kr * W)


def neighborhood_attention_pallas(proj, rpb, lane_block0):
    Bn, L, _ = proj.shape
    W = GRID_W
    n_rows = L // W
    kr = min(NA_KR, n_rows)
    hp = NA_H * NA_HD // LANES
    rps = NA_ROWS_PER_STEP
    bias = na_bias_table(rpb, n_rows)
    kern = functools.partial(_na_kernel, rows_per_step=rps, n_rows=n_rows, kr=kr)
    return pl.pallas_call(
        kern,
        grid=(Bn, hp, n_rows // rps),
        in_specs=[pl.BlockSpec((1, rps * W, LANES), lambda b, h, r: (b, r, lane_block0 + h)),
                  pl.BlockSpec((1, L, LANES), lambda b, h, r: (b, 0, lane_block0 + hp + h)),
                  pl.BlockSpec((1, L, LANES), lambda b, h, r: (b, 0, lane_block0 + 2 * hp + h)),
                  pl.BlockSpec((8, 1, 2 * W, kr * W), lambda b, h, r: (0, h, 0, 0))],
        out_specs=pl.BlockSpec((1, rps * W, LANES), lambda b, h, r: (b, r, h)),
        out_shape=jax.ShapeDtypeStruct((Bn, L, NA_H * NA_HD), f32),
        scratch_shapes=[pltpu.VMEM((L, LANES), bf16), pltpu.VMEM((L, LANES), bf16)],
        compiler_params=pltpu.CompilerParams(dimension_semantics=("parallel", "parallel", "arbitrary"),
                                             vmem_limit_bytes=VMEM_LIMIT_BYTES),
        name="neighborhood_attention",
    )(proj, proj, proj, bias)


def _moe_ffn_kernel(be_ref, x_ref, wt_ref, w1_ref, w3_ref, w2_ref, o_ref):
    x = x_ref[...].astype(bf16)
    h1 = jnp.dot(x, w1_ref[0], preferred_element_type=f32)
    h3 = jnp.dot(x, w3_ref[0], preferred_element_type=f32)
    g = h1 * jax.nn.sigmoid(h1) * h3
    y = jnp.dot(g.astype(bf16), w2_ref[0], preferred_element_type=f32)
    o_ref[...] = y * wt_ref[...]


def moe_expert_ffn(xs, wt, block_e, w1, w3, w2):
    rows, Dm = xs.shape
    n_blocks = rows // MOE_BLOCK
    grid_spec = pltpu.PrefetchScalarGridSpec(
        num_scalar_prefetch=1,
        grid=(n_blocks,),
        in_specs=[pl.BlockSpec((MOE_BLOCK, Dm), lambda i, be: (i, 0)),
                  pl.BlockSpec((MOE_BLOCK, 1), lambda i, be: (i, 0)),
                  pl.BlockSpec((1, Dm, MOE_FF), lambda i, be: (be[i], 0, 0)),
                  pl.BlockSpec((1, Dm, MOE_FF), lambda i, be: (be[i], 0, 0)),
                  pl.BlockSpec((1, MOE_FF, Dm), lambda i, be: (be[i], 0, 0))],
        out_specs=pl.BlockSpec((MOE_BLOCK, Dm), lambda i, be: (i, 0)),
    )
    return pl.pallas_call(
        _moe_ffn_kernel,
        grid_spec=grid_spec,
        out_shape=jax.ShapeDtypeStruct((rows, Dm), f32),
        compiler_params=pltpu.CompilerParams(dimension_semantics=("arbitrary",)),
        name="moe_expert_ffn",
    )(block_e.astype(jnp.int32), xs, wt, w1.astype(bf16), w3.astype(bf16), w2.astype(bf16))


def _final_norm_kernel(x_ref, g_ref, o_ref):
    x = x_ref[...]
    o_ref[...] = x * lax.rsqrt(jnp.mean(x * x, axis=-1, keepdims=True) + NORM_EPS) * g_ref[...]


def final_rmsnorm(x, g):
    Bn, L, D = x.shape
    tm = 1024
    out = pl.pallas_call(
        _final_norm_kernel,
        grid=(Bn * L // tm,),
        in_specs=[pl.BlockSpec((tm, D), lambda i: (i, 0)), pl.BlockSpec((1, D), lambda i: (0, 0))],
        out_specs=pl.BlockSpec((tm, D), lambda i: (i, 0)),
        out_shape=jax.ShapeDtypeStruct((Bn * L, D), x.dtype),
        name="final_rmsnorm",
    )(x.reshape(Bn * L, D), g.reshape(1, D))
    return out.reshape(Bn, L, D)


def rmsnorm(x, g):
    x32 = x.astype(f32)
    y = x32 * lax.rsqrt(jnp.mean(x32 * x32, axis=-1, keepdims=True) + NORM_EPS)
    return y.astype(x.dtype) * g


def shift_prev(u):
    return jnp.pad(u, ((0, 0), (1, 0), (0, 0)))[:, :-1]


def shift_next(u):
    return jnp.pad(u, ((0, 0), (0, 1), (0, 0)))[:, 1:]


def hyena_positional_features(L):
    t = jnp.linspace(0.0, 1.0, L, dtype=f32)[:, None]
    w = (2.0 * math.pi / L) * jnp.arange(L, dtype=f32)[:, None]
    f = jnp.linspace(1e-4, HY_BANDS - 1, HY_BANDS, dtype=f32)[None, :]
    z = jnp.concatenate([t, jnp.cos(f * w), -jnp.sin(f * w)], axis=-1)
    return z, t


def hyena_filters(z, t, w1, b1, w2, b2, w3, b3, wout, freq):
    fr = freq.astype(f32)
    act = lambda u: jnp.sin(fr * u)
    h = act(z @ w1.astype(f32) + b1.astype(f32))
    h = act(h @ w2.astype(f32) + b2.astype(f32))
    h = act(h @ w3.astype(f32) + b3.astype(f32))
    h = (h @ wout.astype(f32)).reshape(-1, 2, HY_D)
    deltas = jnp.abs(jnp.linspace(HY_MIN_DECAY, HY_MAX_DECAY, HY_D, dtype=f32))
    h = h * jnp.exp(-t[:, :, None] * deltas)
    return h * lax.rsqrt(jnp.sum(h * h, axis=(0, 1), keepdims=True) + 1e-6)


def hyena_mixer(u, conv_w, conv_b, filt, skip):
    up = jnp.pad(u, ((0, 0), (1, 1), (0, 0)))
    uc = up[:, :-2] * conv_w[0] + up[:, 1:-1] * conv_w[1] + up[:, 2:] * conv_w[2] + conv_b
    x0, x1, v = jnp.split(uc, 3, axis=-1)
    z = (v * x1).astype(f32)
    L = z.shape[1]
    k2 = jnp.concatenate([filt[:1, 0] + filt[:1, 1], filt[1:, 0],
                          jnp.zeros((1, HY_D), f32), filt[1:, 1][::-1]], axis=0)
    y = jnp.fft.irfft(jnp.fft.rfft(z, n=2 * L, axis=1) * jnp.fft.rfft(k2, axis=0)[None],
                      n=2 * L, axis=1)[:, :L]
    y = y + z * skip.astype(f32)
    return (x0.astype(f32) * y).astype(u.dtype)


def rwkv7_mixer(r, k, v, g_lo, w_lo, a_lo, mu, w0, w2, a0, a2, k_k, k_a, r_k, g2, ln_w, ln_b):
    Bn, L, C = r.shape

    def tshift(u, m):
        return u + m[0] * (shift_prev(u) - u) + m[1] * (shift_next(u) - u)

    r, k, v = (tshift(u, m).astype(f32) for u, m in zip((r, k, v), mu))
    w_lo = w_lo.astype(f32).reshape(Bn, L, 2, RW_W_LORA)
    a_lo = a_lo.astype(f32).reshape(Bn, L, 2, RW_A_LORA)
    w_log = -jax.nn.softplus(-(w0.astype(f32) + jnp.einsum('bldr,drc->bldc', jnp.tanh(w_lo), w2.astype(f32)))) - 0.5
    log_decay = -jnp.exp(w_log)
    a = jax.nn.sigmoid(a0.astype(f32) + jnp.einsum('bldr,drc->bldc', a_lo, a2.astype(f32)))
    heads = lambda u: u.reshape(u.shape[:-1] + (RW_H, RW_N))
    kk = heads(k * k_k.astype(f32))
    kk = (kk * lax.rsqrt(jnp.maximum(jnp.sum(kk * kk, axis=-1, keepdims=True), 1e-24))).reshape(Bn, L, C)
    k_dir = k[:, :, None] * (1.0 + (a - 1.0) * k_a.astype(f32))
    rh, vh = heads(r), heads(v)
    y2 = wkv7_chunked(r, k, v, kk, log_decay.reshape(Bn, L, 2 * C), a.reshape(Bn, L, 2 * C),
                      k_a.astype(f32).reshape(1, C))
    y = heads(y2[:Bn] + y2[Bn:])
    mean = jnp.mean(y, axis=-1, keepdims=True)
    var = jnp.mean(jnp.square(y - mean), axis=-1, keepdims=True)
    y = ((y - mean) * lax.rsqrt(var + RW_GN_EPS)).reshape(Bn, L, C) * ln_w.astype(f32) + ln_b.astype(f32)
    bonus = jnp.sum(rh * heads(k_dir[:, :, 0] + k_dir[:, :, 1]) * r_k.astype(f32), axis=-1, keepdims=True) * vh
    gate = jax.nn.sigmoid(g_lo.astype(f32)) @ g2.astype(f32)
    return ((y + bonus.reshape(Bn, L, C)) * gate).astype(g_lo.dtype)


def hier_moe(h, wg, bg, we, be, w1, w3, w2):
    N, Dm = h.shape
    h32 = h.astype(f32)
    g_logits = h32 @ wg.astype(f32) + bg.astype(f32)
    g_sel = jnp.argmax(g_logits, axis=-1)
    g_prob = jnp.take_along_axis(jax.nn.softmax(g_logits, axis=-1), g_sel[:, None], axis=-1)
    e_logits = (h32 @ we.astype(f32) + be.astype(f32)).reshape(N, MOE_GROUPS, MOE_PER_GROUP)
    e_logits = jnp.take_along_axis(e_logits, g_sel[:, None, None], axis=1)[:, 0]
    top_val, top_idx = lax.top_k(e_logits, MOE_TOPK)
    gate = g_prob * jax.nn.softmax(top_val, axis=-1)
    expert = g_sel[:, None] * MOE_PER_GROUP + top_idx
    M = N * MOE_TOPK
    flat_e = expert.reshape(M)
    order = jnp.argsort(flat_e)
    e_sorted = flat_e[order]
    counts = jnp.bincount(flat_e, length=MOE_EXPERTS)
    padded = (counts + MOE_BLOCK - 1) // MOE_BLOCK * MOE_BLOCK
    pad_end = jnp.cumsum(padded)
    first = (jnp.cumsum(counts) - counts)[e_sorted]
    slot = (pad_end - padded)[e_sorted] + jnp.arange(M) - first
    n_blocks = -(-M // MOE_BLOCK) + MOE_EXPERTS
    slot_src = jnp.full((n_blocks * MOE_BLOCK,), M, jnp.int32).at[slot].set(order.astype(jnp.int32))
    tok_src = jnp.append(jnp.arange(M, dtype=jnp.int32) // MOE_TOPK, N)[slot_src]
    w_src = jnp.append(gate.reshape(M), 0.0)[slot_src]
    block_e = jnp.minimum(jnp.searchsorted(pad_end, jnp.arange(n_blocks) * MOE_BLOCK, side='right'),
                          MOE_EXPERTS - 1)
    h_pad = jnp.concatenate([h, jnp.zeros((1, Dm), h.dtype)], axis=0)
    y = moe_expert_ffn(h_pad[tok_src], w_src[:, None], block_e, w1, w3, w2)
    return jnp.zeros((N + 1, Dm), h.dtype).at[tok_src].add(y)[:N]


def kernel(x, norm1_g, w_in, hy_conv_w, hy_conv_b, hy_w1, hy_b1, hy_w2, hy_b2, hy_w3, hy_b3, hy_wout, hy_freq, hy_skip, rw_mu, rw_w0, rw_w2, rw_a0, rw_a2, rw_kk, rw_ka, rw_rk, rw_g2, rw_ln_w, rw_ln_b, na_rpb, w_out, norm2_g, moe_wg, moe_bg, moe_we, moe_be, moe_w1, moe_w3, moe_w2, norm_f_g):
    Bn, L, _ = x.shape
    z_pos, t_pos = hyena_positional_features(L)
    splits = np.cumsum(IN_SIZES)[:-1].tolist()
    na_lane_block0 = splits[-1] // LANES
    assert na_lane_block0 * LANES == splits[-1]
    for l in range(DEPTH):
        h = rmsnorm(x, norm1_g[l])
        proj = h @ w_in[l]
        hy_u, rw_rkv, rw_g, rw_w, rw_a, _ = jnp.split(proj, splits, axis=-1)
        filt = hyena_filters(z_pos, t_pos, hy_w1[l], hy_b1[l], hy_w2[l], hy_b2[l],
                             hy_w3[l], hy_b3[l], hy_wout[l], hy_freq[l])
        y_hy = hyena_mixer(hy_u, hy_conv_w[l], hy_conv_b[l], filt, hy_skip[l])
        r, k, v = jnp.split(rw_rkv, 3, axis=-1)
        y_rw = rwkv7_mixer(r, k, v, rw_g, rw_w, rw_a, rw_mu[l], rw_w0[l], rw_w2[l], rw_a0[l],
                           rw_a2[l], rw_kk[l], rw_ka[l], rw_rk[l], rw_g2[l], rw_ln_w[l], rw_ln_b[l])
        y_na = neighborhood_attention_pallas(proj, na_rpb[l], na_lane_block0)
        x = x + jnp.concatenate([y_hy, y_rw, y_na], axis=-1) @ w_out[l]
        h = rmsnorm(x, norm2_g[l]).reshape(Bn * L, D_MODEL)
        x = x + hier_moe(h, moe_wg[l], moe_bg[l], moe_we[l], moe_be[l],
                         moe_w1[l], moe_w3[l], moe_w2[l]).reshape(Bn, L, D_MODEL)
    return final_rmsnorm(x, norm_f_g)
```

```python
import functools
import math

import jax
import jax.numpy as jnp
import numpy as np
from jax import lax
from jax.experimental import pallas as pl
from jax.experimental.pallas import tpu as pltpu

f32 = jnp.float32
bf16 = jnp.bfloat16

D_MODEL = 1024
DEPTH = 2
GRID_W = 64
NORM_EPS = 1e-6
NEG_INF = -1e30

HY_D = D_MODEL // 4
HY_EMB = 33
HY_BANDS = (HY_EMB - 1) // 2
HY_FFN = 64
HY_MIN_DECAY = math.log(1e-2) / 1.5
HY_MAX_DECAY = math.log(1e-2) / 0.3

RW_N = 64
RW_D = D_MODEL // 2
RW_H = RW_D // RW_N
RW_W_LORA = 64
RW_A_LORA = 64
RW_G_LORA = 128
RW_GN_EPS = 64e-5

NA_HD = 64
NA_D = D_MODEL // 4
NA_H = NA_D // NA_HD
NA_KR = 8
NA_KC = 16

MIX_D = HY_D + RW_D + NA_D
IN_SIZES = (3 * HY_D, 3 * RW_D, RW_G_LORA, 2 * RW_W_LORA, 2 * RW_A_LORA, 3 * NA_D)
IN_D = sum(IN_SIZES)

MOE_GROUPS = 4
MOE_PER_GROUP = 8
MOE_EXPERTS = MOE_GROUPS * MOE_PER_GROUP
MOE_TOPK = 2
MOE_FF = 512
MOE_BLOCK = 256

LANES = 128
WKV_CHUNK = 64
WKV_TIME_BLOCK = 128
NA_ROWS_PER_STEP = 4
RW_ROWS_PER_STEP = 512
SUBLANES = 8
VMEM_LIMIT_BYTES = 48 * 1024 * 1024

_DN = {'nn': (((1,), (0,)), ((), ())), 'nt': (((1,), (1,)), ((), ())), 'tn': (((0,), (0,)), ((), ()))}


def _mm(a, b, dims='nn'):
    return lax.dot_general(a.astype(bf16), b.astype(bf16), _DN[dims], preferred_element_type=f32)


def _wkv_kernel(r_ref, k_ref, v_ref, kk_ref, lw_ref, a_ref, ka_ref, y_ref, ht_ref, *, tb, batch):
    T = WKV_CHUNK
    H2 = 2 * T
    nc = tb // T
    npairs = r_ref.shape[-1] // LANES
    mm = _mm

    @pl.when(pl.program_id(1) == 0)
    def _():
        ht_ref[...] = jnp.zeros_like(ht_ref)

    d = pl.program_id(0) // batch
    sign = 1 - 2 * d
    row = lax.broadcasted_iota(jnp.int32, (H2, H2), 0)
    col = lax.broadcasted_iota(jnp.int32, (H2, H2), 1)
    same64 = (row // T) == (col // T)
    tdiff = (row % T - col % T) * sign
    strict = same64 & (tdiff > 0)
    incl = same64 & (tdiff >= 0)
    same16 = (row // 16) == (col // 16)
    same32 = (row // 32) == (col // 32)
    off32 = same32 & jnp.logical_not(same16)
    off64 = jnp.logical_not(same32)
    eye = jnp.where(row == col, 1.0, 0.0).astype(f32)
    trow = lax.broadcasted_iota(jnp.int32, (T, T), 0)
    tcol = lax.broadcasted_iota(jnp.int32, (T, T), 1)
    tri = jnp.where((trow - tcol) * sign >= 0, 1.0, 0.0).astype(bf16)
    lo_lane = lax.broadcasted_iota(jnp.int32, (T, LANES), 1) < T
    is_bwd = d == 1

    def stack(x):
        return jnp.concatenate([jnp.where(lo_lane, x, 0.0), jnp.where(lo_lane, 0.0, x)], axis=0)

    inst = [(s, p) for s in range(nc) for p in range(npairs)]
    offs = [pl.multiple_of((s + d * (nc - 1 - 2 * s)) * T, T) for s in range(nc)]

    def load(ref, s, p):
        return ref[0, pl.ds(offs[s], T), p * LANES:(p + 1) * LANES]

    cs_l = []
    for s, p in inst:
        lw = load(lw_ref, s, p)
        l1 = lw.astype(bf16)
        r1 = lw - l1.astype(f32)
        l2 = r1.astype(bf16)
        l3 = (r1 - l2.astype(f32)).astype(bf16)
        dd = lambda x: lax.dot_general(tri, x, _DN['nn'], preferred_element_type=f32)
        cs_l.append((dd(l1) + (dd(l2) + dd(l3)), lw))
    ops = []
    for (s, p), (cs, lw) in zip(inst, cs_l):
        r = load(r_ref, s, p)
        k = load(k_ref, s, p)
        v = load(v_ref, s, p)
        kk = load(kk_ref, s, p)
        a = load(a_ref, s, p)
        ka = ka_ref[:, p * LANES:(p + 1) * LANES]
        kd = k * (1.0 + (a - 1.0) * ka)
        b = kk * a
        cs_end = jnp.where(is_bwd, cs[0:1, :], cs[T - 1:T, :])
        em = jnp.exp(-cs)
        e_end = jnp.exp(cs_end - cs)
        ops.append(dict(
            As=stack(-kk * jnp.exp(cs - lw)), Rs=stack(r * jnp.exp(cs)), Bs=stack(b * em), Ks=stack(kd * em),
            Vs=stack(v), Bgs=stack(b * e_end), Kgs=stack(kd * e_end), g_end=jnp.exp(cs_end)))
    for o in ops:
        S = mm(jnp.concatenate([o['As'], o['Rs']], axis=0), jnp.concatenate([o['Bs'], o['Ks']], axis=0), 'nt')
        N = jnp.where(strict, S[:H2, :H2], 0.0)
        o['Aak'] = jnp.where(strict, S[:H2, H2:], 0.0)
        o['Arb'] = jnp.where(incl, S[H2:, :H2], 0.0)
        o['Ark'] = jnp.where(incl, S[H2:, H2:], 0.0)
        o['Nd'] = jnp.where(same16, N, 0.0)
        o['N32'] = jnp.where(off32, N, 0.0)
        o['N64'] = jnp.where(off64, N, 0.0)
    for o in ops:
        o['X'] = eye + o['Nd']
        o['P'] = mm(o['Nd'], o['Nd'])
    for it in range(3):
        for o in ops:
            o['X'] = o['X'] + mm(o['P'], o['X'])
        if it < 2:
            for o in ops:
                o['P'] = mm(o['P'], o['P'])
    for key in ('N32', 'N64'):
        for o in ops:
            o['Z'] = mm(o[key], o['X'])
        for o in ops:
            o['X'] = o['X'] + mm(o['X'], o['Z'])
    for o in ops:
        o['W0'] = mm(o['Aak'], o['Vs'])
        o['Yv'] = mm(o['Ark'], o['Vs'])
    hts = [ht_ref[p] for p in range(npairs)]
    for s in range(nc):
        cur = [ops[s * npairs + p] for p in range(npairs)]
        ah = [mm(o['As'], hts[p], 'nt') for p, o in enumerate(cur)]
        yh = [mm(o['Rs'], hts[p], 'nt') for p, o in enumerate(cur)]
        us = [mm(o['X'], o['W0'] + ah[p]) for p, o in enumerate(cur)]
        ys = [yh[p] + (mm(o['Arb'], us[p]) + o['Yv']) for p, o in enumerate(cur)]
        upd = [mm(jnp.concatenate([us[p], o['Vs']], axis=0), jnp.concatenate([o['Bgs'], o['Kgs']], axis=0), 'tn')
               for p, o in enumerate(cur)]
        hts = [jnp.where(same64, hts[p] * o['g_end'] + upd[p], 0.0) for p, o in enumerate(cur)]
        for p in range(npairs):
            y_ref[0, pl.ds(offs[s], T), p * LANES:(p + 1) * LANES] = ys[p][:T] + ys[p][T:]
    for p in range(npairs):
        ht_ref[p] = hts[p]


def wkv7_chunked(r, k, v, kk, lw2, a2, ka):
    Bn, L, C = r.shape
    tb = WKV_TIME_BLOCK
    nt = L // tb
    tmap = lambda i, t: t + (i // Bn) * (nt - 1 - 2 * t)
    shared = pl.BlockSpec((1, tb, C), lambda i, t: (i % Bn, tmap(i, t), 0))
    per_dir = pl.BlockSpec((1, tb, C), lambda i, t: (i % Bn, tmap(i, t), i // Bn))
    return pl.pallas_call(
        functools.partial(_wkv_kernel, tb=tb, batch=Bn),
        grid=(2 * Bn, nt),
        in_specs=[shared, shared, shared, shared, per_dir, per_dir, pl.BlockSpec((1, C), lambda i, t: (0, 0))],
        out_specs=pl.BlockSpec((1, tb, C), lambda i, t: (i, tmap(i, t), 0)),
        out_shape=jax.ShapeDtypeStruct((2 * Bn, L, C), f32),
        scratch_shapes=[pltpu.VMEM((C // LANES, LANES, LANES), f32)],
        compiler_params=pltpu.CompilerParams(dimension_semantics=("parallel", "arbitrary")),
        name="wkv7_chunked",
    )(r, k, v, kk, lw2, a2, ka)


def _mm_exact_rhs(a, b_bf16):
    ah = a.astype(bf16)
    al = (a - ah.astype(f32)).astype(bf16)
    d = lambda x: lax.dot_general(x, b_bf16, _DN['nn'], preferred_element_type=f32)
    return d(ah) + d(al)


def _head_sum_matrix(scale):
    i = lax.broadcasted_iota(jnp.int32, (RW_D, RW_D), 0) // RW_N
    j = lax.broadcasted_iota(jnp.int32, (RW_D, RW_D), 1) // RW_N
    return jnp.where(i == j, scale, 0.0).astype(bf16)


def _rw_prep_kernel(r_ref, k_ref, v_ref, rp_ref, kp_ref, vp_ref, rn_ref, kn_ref, vn_ref, w_ref, a_ref,
                    mu_ref, w0_ref, w2_ref, a0_ref, a2_ref, kkw_ref,
                    ro_ref, ko_ref, vo_ref, kko_ref, lw_ref, ao_ref, *, rows, nt):
    t = pl.program_id(1)
    has_prev = (t > 0).astype(f32)
    has_next = (t < nt - 1).astype(f32)
    ridx = lax.broadcasted_iota(jnp.int32, (rows, RW_D), 0)

    def tshift(cur_ref, prev_ref, next_ref, j):
        u = cur_ref[0]
        prev_row = prev_ref[0, SUBLANES - 1:SUBLANES, :] * has_prev
        next_row = next_ref[0, 0:1, :] * has_next
        up = jnp.where(ridx == 0, prev_row, pltpu.roll(u, 1, axis=0))
        un = jnp.where(ridx == rows - 1, next_row, pltpu.roll(u, rows - 1, axis=0))
        return u + mu_ref[j, 0:1, :] * (up - u) + mu_ref[j, 1:2, :] * (un - u)

    r = tshift(r_ref, rp_ref, rn_ref, 0)
    k = tshift(k_ref, kp_ref, kn_ref, 1)
    v = tshift(v_ref, vp_ref, vn_ref, 2)
    ro_ref[0] = r
    ko_ref[0] = k
    vo_ref[0] = v
    kk = k * kkw_ref[...]
    ss = _mm_exact_rhs(kk * kk, _head_sum_matrix(1.0))
    kko_ref[0] = kk * lax.rsqrt(jnp.maximum(ss, 1e-24))
    wl = jnp.tanh(w_ref[0])
    al = a_ref[0]
    for d in range(2):
        wlog = w0_ref[d:d + 1, :] + _mm(wl[:, d * RW_W_LORA:(d + 1) * RW_W_LORA], w2_ref[d])
        wlog = -jax.nn.softplus(-wlog) - 0.5
        lw_ref[0, :, d * RW_D:(d + 1) * RW_D] = -jnp.exp(wlog)
        av = a0_ref[d:d + 1, :] + _mm(al[:, d * RW_A_LORA:(d + 1) * RW_A_LORA], a2_ref[d])
        ao_ref[0, :, d * RW_D:(d + 1) * RW_D] = jax.nn.sigmoid(av)


def rw_prep(proj, lora_lane0, mu, w0, w2, a0, a2, k_k):
    Bn, L, _ = proj.shape
    C = RW_D
    rows = RW_ROWS_PER_STEP
    nt = L // rows
    hb = rows // SUBLANES
    lb = lora_lane0 // LANES
    cur = lambda j: pl.BlockSpec((1, rows, C), lambda b, t: (b, t, j))
    prev = lambda j: pl.BlockSpec((1, SUBLANES, C), lambda b, t: (b, jnp.maximum(t * hb - 1, 0), j))
    nxt = lambda j: pl.BlockSpec((1, SUBLANES, C), lambda b, t: (b, jnp.minimum((t + 1) * hb, L // SUBLANES - 1), j))
    lora_w = pl.BlockSpec((1, rows, LANES), lambda b, t: (b, t, lb + 1))
    lora_a = pl.BlockSpec((1, rows, LANES), lambda b, t: (b, t, lb + 2))
    full = lambda shp: pl.BlockSpec(shp, lambda b, t: (0,) * len(shp))
    out_c = pl.BlockSpec((1, rows, C), lambda b, t: (b, t, 0))
    out_2c = pl.BlockSpec((1, rows, 2 * C), lambda b, t: (b, t, 0))
    sds = lambda c: jax.ShapeDtypeStruct((Bn, L, c), f32)
    return pl.pallas_call(
        functools.partial(_rw_prep_kernel, rows=rows, nt=nt),
        grid=(Bn, nt),
        in_specs=[cur(0), cur(1), cur(2), prev(0), prev(1), prev(2), nxt(0), nxt(1), nxt(2), lora_w, lora_a,
                  full((3, 2, C)), full((2, C)), full((2, RW_W_LORA, C)), full((2, C)), full((2, RW_A_LORA, C)),
                  full((1, C))],
        out_specs=[out_c, out_c, out_c, out_c, out_2c, out_2c],
        out_shape=[sds(C), sds(C), sds(C), sds(C), sds(2 * C), sds(2 * C)],
        compiler_params=pltpu.CompilerParams(dimension_semantics=("parallel", "parallel"),
                                             vmem_limit_bytes=VMEM_LIMIT_BYTES),
        name="rwkv_prep",
    )(proj, proj, proj, proj, proj, proj, proj, proj, proj, proj, proj,
      mu.astype(f32), w0.astype(f32), w2.astype(bf16), a0.astype(f32), a2.astype(bf16), k_k.astype(f32).reshape(1, C))


def _rw_post_kernel(yf_ref, yb_ref, r_ref, k_ref, v_ref, a_ref, g_ref, ka_ref, rk_ref, lnw_ref, lnb_ref, g2_ref, o_ref):
    C = RW_D
    y = yf_ref[0] + yb_ref[0]
    avg = _head_sum_matrix(1.0 / RW_N)
    mean = _mm_exact_rhs(y, avg)
    yc = y - mean
    var = _mm_exact_rhs(yc * yc, avg)
    yn = yc * lax.rsqrt(var + RW_GN_EPS) * lnw_ref[...] + lnb_ref[...]
    a = a_ref[0]
    k = k_ref[0]
    ka = ka_ref[...]
    ksum = k * (1.0 + (a[:, :C] - 1.0) * ka) + k * (1.0 + (a[:, C:] - 1.0) * ka)
    coef = _mm_exact_rhs(r_ref[0] * ksum * rk_ref[...], _head_sum_matrix(1.0))
    gate = _mm(jax.nn.sigmoid(g_ref[0]), g2_ref[...])
    o_ref[0] = (yn + coef * v_ref[0]) * gate


def rw_post(y2, r, k, v, a2, proj, g_lane0, k_a, r_k, ln_w, ln_b, g2):
    Bn, L, C = r.shape
    rows = RW_ROWS_PER_STEP
    nt = L // rows
    gb = g_lane0 // LANES
    blk = lambda c: pl.BlockSpec((1, rows, c), lambda b, t: (b, t, 0))
    full = lambda shp: pl.BlockSpec(shp, lambda b, t: (0,) * len(shp))
    row = lambda x: x.astype(f32).reshape(1, C)
    return pl.pallas_call(
        _rw_post_kernel,
        grid=(Bn, nt),
        in_specs=[blk(C), pl.BlockSpec((1, rows, C), lambda b, t: (b + Bn, t, 0)), blk(C), blk(C), blk(C), blk(2 * C),
                  pl.BlockSpec((1, rows, LANES), lambda b, t: (b, t, gb)),
                  full((1, C)), full((1, C)), full((1, C)), full((1, C)), full((RW_G_LORA, C))],
        out_specs=blk(C),
        out_shape=jax.ShapeDtypeStruct((Bn, L, C), f32),
        compiler_params=pltpu.CompilerParams(dimension_semantics=("parallel", "parallel"),
                                             vmem_limit_bytes=VMEM_LIMIT_BYTES),
        name="rwkv_post",
    )(y2, y2, r, k, v, a2, proj, row(k_a), row(r_k), row(ln_w), row(ln_b), g2.astype(bf16))


def _na_kernel(q_ref, k_ref, v_ref, bias_ref, o_ref, kb_ref, vb_ref, *, rows_per_step, n_rows, kr):
    W = GRID_W
    rb = pl.program_id(2)

    @pl.when(rb == 0)
    def _():
        kb_ref[...] = k_ref[0].astype(bf16)
        vb_ref[...] = v_ref[0].astype(bf16)

    lo_lane = lax.broadcasted_iota(jnp.int32, (W, LANES), 1) < NA_HD
    scale = NA_HD ** -0.5
    rows = []
    for j in range(rows_per_step):
        r = rb * rows_per_step + j
        start = jnp.clip(r - kr // 2, 0, n_rows - kr)
        rows.append((start, start - r + (NA_KR - 1)))
    s_list = []
    for j, (start, didx) in enumerate(rows):
        q = q_ref[0, j * W:(j + 1) * W, :] * scale
        qs = jnp.concatenate([jnp.where(lo_lane, q, 0.0), jnp.where(lo_lane, 0.0, q)], axis=0)
        kw = kb_ref[pl.ds(pl.multiple_of(start * W, W), kr * W), :]
        s_list.append(_mm(qs, kw, 'nt') + bias_ref[didx, 0])
    p_list = []
    for s in s_list:
        m = jnp.max(s, axis=-1, keepdims=True)
        p = jnp.exp(s - m)
        p_list.append((p, jnp.sum(p, axis=-1, keepdims=True)))
    for j, ((start, _), (p, l)) in enumerate(zip(rows, p_list)):
        vw = vb_ref[pl.ds(pl.multiple_of(start * W, W), kr * W), :]
        o = _mm(p, vw) / l
        o_ref[0, j * W:(j + 1) * W, :] = jnp.where(lo_lane, o[:W], o[W:])


def na_bias_table(rpb, n_rows):
    W = GRID_W
    kr = min(NA_KR, n_rows)
    cols = jnp.arange(W)
    col_start = jnp.clip(cols - NA_KC // 2, 0, W - NA_KC)
    in_band = (cols[None, :] >= col_start[:, None]) & (cols[None, :] < col_start[:, None] + NA_KC)
    dc = jnp.clip(cols[None, :] - cols[:, None], -(NA_KC - 1), NA_KC - 1) + (NA_KC - 1)
    win = jnp.stack([rpb.astype(f32)[:, d:d + kr, :] for d in range(8)], axis=1)
    onehot = (dc[:, :, None] == jnp.arange(2 * NA_KC - 1)[None, None, :]).astype(f32)
    tab = jnp.einsum('hdic,qkc->hdqik', win, onehot, precision=lax.Precision.HIGHEST)
    tab = jnp.where(in_band[None, None, :, None, :], tab, NEG_INF)
    H = rpb.shape[0]
    tab = tab.reshape(H // 2, 2, 8, W, kr * W).transpose(2, 0, 1, 3, 4)
    return tab.reshape(8, H // 2, 2 * W, kr * W)


def neighborhood_attention_pallas(proj, rpb, lane_block0):
    Bn, L, _ = proj.shape
    W = GRID_W
    n_rows = L // W
    kr = min(NA_KR, n_rows)
    hp = NA_H * NA_HD // LANES
    rps = NA_ROWS_PER_STEP
    bias = na_bias_table(rpb, n_rows)
    kern = functools.partial(_na_kernel, rows_per_step=rps, n_rows=n_rows, kr=kr)
    return pl.pallas_call(
        kern,
        grid=(Bn, hp, n_rows // rps),
        in_specs=[pl.BlockSpec((1, rps * W, LANES), lambda b, h, r: (b, r, lane_block0 + h)),
                  pl.BlockSpec((1, L, LANES), lambda b, h, r: (b, 0, lane_block0 + hp + h)),
                  pl.BlockSpec((1, L, LANES), lambda b, h, r: (b, 0, lane_block0 + 2 * hp + h)),
                  pl.BlockSpec((8, 1, 2 * W, kr * W), lambda b, h, r: (0, h, 0, 0))],
        out_specs=pl.BlockSpec((1, rps * W, LANES), lambda b, h, r: (b, r, h)),
        out_shape=jax.ShapeDtypeStruct((Bn, L, NA_H * NA_HD), f32),
        scratch_shapes=[pltpu.VMEM((L, LANES), bf16), pltpu.VMEM((L, LANES), bf16)],
        compiler_params=pltpu.CompilerParams(dimension_semantics=("parallel", "parallel", "arbitrary"),
                                             vmem_limit_bytes=VMEM_LIMIT_BYTES),
        name="neighborhood_attention",
    )(proj, proj, proj, bias)


def _moe_ffn_kernel(be_ref, x_ref, wt_ref, w1_ref, w3_ref, w2_ref, o_ref):
    x = x_ref[...].astype(bf16)
    h1 = jnp.dot(x, w1_ref[0], preferred_element_type=f32)
    h3 = jnp.dot(x, w3_ref[0], preferred_element_type=f32)
    g = h1 * jax.nn.sigmoid(h1) * h3
    y = jnp.dot(g.astype(bf16), w2_ref[0], preferred_element_type=f32)
    o_ref[...] = y * wt_ref[...]


def moe_expert_ffn(xs, wt, block_e, w1, w3, w2):
    rows, Dm = xs.shape
    n_blocks = rows // MOE_BLOCK
    grid_spec = pltpu.PrefetchScalarGridSpec(
        num_scalar_prefetch=1,
        grid=(n_blocks,),
        in_specs=[pl.BlockSpec((MOE_BLOCK, Dm), lambda i, be: (i, 0)),
                  pl.BlockSpec((MOE_BLOCK, 1), lambda i, be: (i, 0)),
                  pl.BlockSpec((1, Dm, MOE_FF), lambda i, be: (be[i], 0, 0)),
                  pl.BlockSpec((1, Dm, MOE_FF), lambda i, be: (be[i], 0, 0)),
                  pl.BlockSpec((1, MOE_FF, Dm), lambda i, be: (be[i], 0, 0))],
        out_specs=pl.BlockSpec((MOE_BLOCK, Dm), lambda i, be: (i, 0)),
    )
    return pl.pallas_call(
        _moe_ffn_kernel,
        grid_spec=grid_spec,
        out_shape=jax.ShapeDtypeStruct((rows, Dm), f32),
        compiler_params=pltpu.CompilerParams(dimension_semantics=("arbitrary",)),
        name="moe_expert_ffn",
    )(block_e.astype(jnp.int32), xs, wt, w1.astype(bf16), w3.astype(bf16), w2.astype(bf16))


def _final_norm_kernel(x_ref, g_ref, o_ref):
    x = x_ref[...]
    o_ref[...] = x * lax.rsqrt(jnp.mean(x * x, axis=-1, keepdims=True) + NORM_EPS) * g_ref[...]


def final_rmsnorm(x, g):
    Bn, L, D = x.shape
    tm = 1024
    out = pl.pallas_call(
        _final_norm_kernel,
        grid=(Bn * L // tm,),
        in_specs=[pl.BlockSpec((tm, D), lambda i: (i, 0)), pl.BlockSpec((1, D), lambda i: (0, 0))],
        out_specs=pl.BlockSpec((tm, D), lambda i: (i, 0)),
        out_shape=jax.ShapeDtypeStruct((Bn * L, D), x.dtype),
        name="final_rmsnorm",
    )(x.reshape(Bn * L, D), g.reshape(1, D))
    return out.reshape(Bn, L, D)


def rmsnorm(x, g):
    x32 = x.astype(f32)
    y = x32 * lax.rsqrt(jnp.mean(x32 * x32, axis=-1, keepdims=True) + NORM_EPS)
    return y.astype(x.dtype) * g


def shift_prev(u):
    return jnp.pad(u, ((0, 0), (1, 0), (0, 0)))[:, :-1]


def shift_next(u):
    return jnp.pad(u, ((0, 0), (0, 1), (0, 0)))[:, 1:]


def hyena_positional_features(L):
    t = jnp.linspace(0.0, 1.0, L, dtype=f32)[:, None]
    w = (2.0 * math.pi / L) * jnp.arange(L, dtype=f32)[:, None]
    f = jnp.linspace(1e-4, HY_BANDS - 1, HY_BANDS, dtype=f32)[None, :]
    z = jnp.concatenate([t, jnp.cos(f * w), -jnp.sin(f * w)], axis=-1)
    return z, t


def hyena_filters(z, t, w1, b1, w2, b2, w3, b3, wout, freq):
    fr = freq.astype(f32)
    act = lambda u: jnp.sin(fr * u)
    h = act(z @ w1.astype(f32) + b1.astype(f32))
    h = act(h @ w2.astype(f32) + b2.astype(f32))
    h = act(h @ w3.astype(f32) + b3.astype(f32))
    h = (h @ wout.astype(f32)).reshape(-1, 2, HY_D)
    deltas = jnp.abs(jnp.linspace(HY_MIN_DECAY, HY_MAX_DECAY, HY_D, dtype=f32))
    h = h * jnp.exp(-t[:, :, None] * deltas)
    return h * lax.rsqrt(jnp.sum(h * h, axis=(0, 1), keepdims=True) + 1e-6)


def hyena_mixer(u, conv_w, conv_b, filt, skip):
    up = jnp.pad(u, ((0, 0), (1, 1), (0, 0)))
    uc = up[:, :-2] * conv_w[0] + up[:, 1:-1] * conv_w[1] + up[:, 2:] * conv_w[2] + conv_b
    x0, x1, v = jnp.split(uc, 3, axis=-1)
    z = (v * x1).astype(f32)
    L = z.shape[1]
    k2 = jnp.concatenate([filt[:1, 0] + filt[:1, 1], filt[1:, 0],
                          jnp.zeros((1, HY_D), f32), filt[1:, 1][::-1]], axis=0)
    y = jnp.fft.irfft(jnp.fft.rfft(z, n=2 * L, axis=1) * jnp.fft.rfft(k2, axis=0)[None],
                      n=2 * L, axis=1)[:, :L]
    y = y + z * skip.astype(f32)
    return (x0.astype(f32) * y).astype(u.dtype)


def rwkv7_mixer(proj, lora_lane0, mu, w0, w2, a0, a2, k_k, k_a, r_k, g2, ln_w, ln_b):
    r, k, v, kk, lw2, a_both = rw_prep(proj, lora_lane0, mu, w0, w2, a0, a2, k_k)
    y2 = wkv7_chunked(r, k, v, kk, lw2, a_both, k_a.astype(f32).reshape(1, RW_D))
    return rw_post(y2, r, k, v, a_both, proj, lora_lane0, k_a, r_k, ln_w, ln_b, g2)


def hier_moe(h, wg, bg, we, be, w1, w3, w2):
    N, Dm = h.shape
    h32 = h.astype(f32)
    g_logits = h32 @ wg.astype(f32) + bg.astype(f32)
    g_sel = jnp.argmax(g_logits, axis=-1)
    g_prob = jnp.take_along_axis(jax.nn.softmax(g_logits, axis=-1), g_sel[:, None], axis=-1)
    e_logits = (h32 @ we.astype(f32) + be.astype(f32)).reshape(N, MOE_GROUPS, MOE_PER_GROUP)
    e_logits = jnp.take_along_axis(e_logits, g_sel[:, None, None], axis=1)[:, 0]
    top_val, top_idx = lax.top_k(e_logits, MOE_TOPK)
    gate = g_prob * jax.nn.softmax(top_val, axis=-1)
    expert = g_sel[:, None] * MOE_PER_GROUP + top_idx
    M = N * MOE_TOPK
    flat_e = expert.reshape(M)
    order = jnp.argsort(flat_e)
    e_sorted = flat_e[order]
    counts = jnp.bincount(flat_e, length=MOE_EXPERTS)
    padded = (counts + MOE_BLOCK - 1) // MOE_BLOCK * MOE_BLOCK
    pad_end = jnp.cumsum(padded)
    first = (jnp.cumsum(counts) - counts)[e_sorted]
    slot = (pad_end - padded)[e_sorted] + jnp.arange(M) - first
    n_blocks = -(-M // MOE_BLOCK) + MOE_EXPERTS
    slot_src = jnp.full((n_blocks * MOE_BLOCK,), M, jnp.int32).at[slot].set(order.astype(jnp.int32))
    real = slot_src < M
    src = jnp.minimum(slot_src, M - 1)
    tok_src = jnp.where(real, src // MOE_TOPK, 0)
    w_src = jnp.where(real, gate.reshape(M)[src], 0.0)
    block_start = jnp.arange(n_blocks, dtype=pad_end.dtype) * MOE_BLOCK
    block_e = jnp.minimum(jnp.sum(pad_end[None, :] <= block_start[:, None], axis=1), MOE_EXPERTS - 1)
    y = moe_expert_ffn(h[tok_src], w_src[:, None], block_e, w1, w3, w2)
    pos = jnp.zeros((M,), jnp.int32).at[order].set(slot.astype(jnp.int32))
    return y[pos].reshape(N, MOE_TOPK, Dm).sum(axis=1)


def kernel(x, norm1_g, w_in, hy_conv_w, hy_conv_b, hy_w1, hy_b1, hy_w2, hy_b2, hy_w3, hy_b3, hy_wout, hy_freq, hy_skip, rw_mu, rw_w0, rw_w2, rw_a0, rw_a2, rw_kk, rw_ka, rw_rk, rw_g2, rw_ln_w, rw_ln_b, na_rpb, w_out, norm2_g, moe_wg, moe_bg, moe_we, moe_be, moe_w1, moe_w3, moe_w2, norm_f_g):
    Bn, L, _ = x.shape
    z_pos, t_pos = hyena_positional_features(L)
    splits = np.cumsum(IN_SIZES)[:-1].tolist()
    hy_end, rkv_end = splits[0], splits[1]
    hy_lane0 = rkv_end - hy_end
    lora_lane0, na_lane0 = splits[1], splits[4]
    assert RW_D % LANES == 0 and lora_lane0 % LANES == 0 and na_lane0 % LANES == 0
    for l in range(DEPTH):
        h = rmsnorm(x, norm1_g[l])
        w_in_l = jnp.concatenate([w_in[l][:, hy_end:rkv_end], w_in[l][:, :hy_end], w_in[l][:, rkv_end:]], axis=1)
        proj = h @ w_in_l
        hy_u = proj[..., hy_lane0:hy_lane0 + hy_end]
        filt = hyena_filters(z_pos, t_pos, hy_w1[l], hy_b1[l], hy_w2[l], hy_b2[l],
                             hy_w3[l], hy_b3[l], hy_wout[l], hy_freq[l])
        y_hy = hyena_mixer(hy_u, hy_conv_w[l], hy_conv_b[l], filt, hy_skip[l])
        y_rw = rwkv7_mixer(proj, lora_lane0, rw_mu[l], rw_w0[l], rw_w2[l], rw_a0[l],
                           rw_a2[l], rw_kk[l], rw_ka[l], rw_rk[l].reshape(RW_D), rw_g2[l], rw_ln_w[l], rw_ln_b[l])
        y_na = neighborhood_attention_pallas(proj, na_rpb[l], na_lane0 // LANES)
        x = x + jnp.concatenate([y_hy, y_rw, y_na], axis=-1) @ w_out[l]
        h = rmsnorm(x, norm2_g[l]).reshape(Bn * L, D_MODEL)
        x = x + hier_moe(h, moe_wg[l], moe_bg[l], moe_we[l], moe_be[l],
                         moe_w1[l], moe_w3[l], moe_w2[l]).reshape(Bn, L, D_MODEL)
    return final_rmsnorm(x, norm_f_g)
```

```python
import functools
import math

import jax
import jax.numpy as jnp
import numpy as np
from jax import lax
from jax.experimental import pallas as pl
from jax.experimental.pallas import tpu as pltpu

f32 = jnp.float32
bf16 = jnp.bfloat16

D_MODEL = 1024
DEPTH = 2
GRID_W = 64
NORM_EPS = 1e-6
NEG_INF = -1e30

HY_D = D_MODEL // 4
HY_EMB = 33
HY_BANDS = (HY_EMB - 1) // 2
HY_FFN = 64
HY_MIN_DECAY = math.log(1e-2) / 1.5
HY_MAX_DECAY = math.log(1e-2) / 0.3

RW_N = 64
RW_D = D_MODEL // 2
RW_H = RW_D // RW_N
RW_W_LORA = 64
RW_A_LORA = 64
RW_G_LORA = 128
RW_GN_EPS = 64e-5

NA_HD = 64
NA_D = D_MODEL // 4
NA_H = NA_D // NA_HD
NA_KR = 8
NA_KC = 16

MIX_D = HY_D + RW_D + NA_D
IN_SIZES = (3 * HY_D, 3 * RW_D, RW_G_LORA, 2 * RW_W_LORA, 2 * RW_A_LORA, 3 * NA_D)
IN_D = sum(IN_SIZES)

MOE_GROUPS = 4
MOE_PER_GROUP = 8
MOE_EXPERTS = MOE_GROUPS * MOE_PER_GROUP
MOE_TOPK = 2
MOE_FF = 512
MOE_BLOCK = 256

LANES = 128
WKV_CHUNK = 64
WKV_TIME_BLOCK = 128
NA_ROWS_PER_STEP = 4
RW_ROWS_PER_STEP = 512
SUBLANES = 8
FFT_N2 = 128
FFT_LANE_BLOCK = 4096
FFT_K1_PER_STEP = 4
VMEM_LIMIT_BYTES = 48 * 1024 * 1024

_DN = {'nn': (((1,), (0,)), ((), ())), 'nt': (((1,), (1,)), ((), ())), 'tn': (((0,), (0,)), ((), ()))}


def _mm(a, b, dims='nn'):
    return lax.dot_general(a.astype(bf16), b.astype(bf16), _DN[dims], preferred_element_type=f32)


def _wkv_kernel(r_ref, k_ref, v_ref, kk_ref, lw_ref, a_ref, ka_ref, y_ref, ht_ref, *, tb, batch):
    T = WKV_CHUNK
    H2 = 2 * T
    nc = tb // T
    npairs = r_ref.shape[-1] // LANES
    mm = _mm

    @pl.when(pl.program_id(1) == 0)
    def _():
        ht_ref[...] = jnp.zeros_like(ht_ref)

    d = pl.program_id(0) // batch
    sign = 1 - 2 * d
    row = lax.broadcasted_iota(jnp.int32, (H2, H2), 0)
    col = lax.broadcasted_iota(jnp.int32, (H2, H2), 1)
    same64 = (row // T) == (col // T)
    tdiff = (row % T - col % T) * sign
    strict = same64 & (tdiff > 0)
    incl = same64 & (tdiff >= 0)
    same16 = (row // 16) == (col // 16)
    same32 = (row // 32) == (col // 32)
    off32 = same32 & jnp.logical_not(same16)
    off64 = jnp.logical_not(same32)
    eye = jnp.where(row == col, 1.0, 0.0).astype(f32)
    trow = lax.broadcasted_iota(jnp.int32, (T, T), 0)
    tcol = lax.broadcasted_iota(jnp.int32, (T, T), 1)
    tri = jnp.where((trow - tcol) * sign >= 0, 1.0, 0.0).astype(bf16)
    lo_lane = lax.broadcasted_iota(jnp.int32, (T, LANES), 1) < T
    is_bwd = d == 1

    def stack(x):
        return jnp.concatenate([jnp.where(lo_lane, x, 0.0), jnp.where(lo_lane, 0.0, x)], axis=0)

    inst = [(s, p) for s in range(nc) for p in range(npairs)]
    offs = [pl.multiple_of((s + d * (nc - 1 - 2 * s)) * T, T) for s in range(nc)]

    def load(ref, s, p):
        return ref[0, pl.ds(offs[s], T), p * LANES:(p + 1) * LANES]

    cs_l = []
    for s, p in inst:
        lw = load(lw_ref, s, p)
        l1 = lw.astype(bf16)
        r1 = lw - l1.astype(f32)
        l2 = r1.astype(bf16)
        l3 = (r1 - l2.astype(f32)).astype(bf16)
        dd = lambda x: lax.dot_general(tri, x, _DN['nn'], preferred_element_type=f32)
        cs_l.append((dd(l1) + (dd(l2) + dd(l3)), lw))
    ops = []
    for (s, p), (cs, lw) in zip(inst, cs_l):
        r = load(r_ref, s, p)
        k = load(k_ref, s, p)
        v = load(v_ref, s, p)
        kk = load(kk_ref, s, p)
        a = load(a_ref, s, p)
        ka = ka_ref[:, p * LANES:(p + 1) * LANES]
        kd = k * (1.0 + (a - 1.0) * ka)
        b = kk * a
        cs_end = jnp.where(is_bwd, cs[0:1, :], cs[T - 1:T, :])
        em = jnp.exp(-cs)
        e_end = jnp.exp(cs_end - cs)
        ops.append(dict(
            As=stack(-kk * jnp.exp(cs - lw)), Rs=stack(r * jnp.exp(cs)), Bs=stack(b * em), Ks=stack(kd * em),
            Vs=stack(v), Bgs=stack(b * e_end), Kgs=stack(kd * e_end), g_end=jnp.exp(cs_end)))
    for o in ops:
        o['AR'] = jnp.concatenate([o['As'], o['Rs']], axis=0)
        S = mm(o['AR'], jnp.concatenate([o['Bs'], o['Ks']], axis=0), 'nt')
        N = jnp.where(strict, S[:H2, :H2], 0.0)
        o['Aak'] = jnp.where(strict, S[:H2, H2:], 0.0)
        o['Arb'] = jnp.where(incl, S[H2:, :H2], 0.0)
        o['Ark'] = jnp.where(incl, S[H2:, H2:], 0.0)
        o['Nd'] = jnp.where(same16, N, 0.0)
        o['N32'] = jnp.where(off32, N, 0.0)
        o['N64'] = jnp.where(off64, N, 0.0)
    for o in ops:
        o['X'] = eye + o['Nd']
        o['P'] = mm(o['Nd'], o['Nd'])
    for it in range(3):
        if it < 2:
            for o in ops:
                px = mm(o['P'], jnp.concatenate([o['X'], o['P']], axis=1))
                o['X'] = o['X'] + px[:, :H2]
                o['P'] = px[:, H2:]
        else:
            for o in ops:
                o['X'] = o['X'] + mm(o['P'], o['X'])
    for key in ('N32', 'N64'):
        for o in ops:
            o['Z'] = mm(o[key], o['X'])
        for o in ops:
            o['X'] = o['X'] + mm(o['X'], o['Z'])
    for o in ops:
        wy = mm(jnp.concatenate([o['Aak'], o['Ark']], axis=0), o['Vs'])
        o['W0'] = wy[:H2]
        o['Yv'] = wy[H2:]
    hts = [ht_ref[p] for p in range(npairs)]
    for s in range(nc):
        cur = [ops[s * npairs + p] for p in range(npairs)]
        arh = [mm(o['AR'], hts[p], 'nt') for p, o in enumerate(cur)]
        us = [mm(o['X'], o['W0'] + arh[p][:H2]) for p, o in enumerate(cur)]
        ys = [arh[p][H2:] + (mm(o['Arb'], us[p]) + o['Yv']) for p, o in enumerate(cur)]
        upd = [mm(jnp.concatenate([us[p], o['Vs']], axis=0), jnp.concatenate([o['Bgs'], o['Kgs']], axis=0), 'tn')
               for p, o in enumerate(cur)]
        hts = [jnp.where(same64, hts[p] * o['g_end'] + upd[p], 0.0) for p, o in enumerate(cur)]
        for p in range(npairs):
            y_ref[0, pl.ds(offs[s], T), p * LANES:(p + 1) * LANES] = ys[p][:T] + ys[p][T:]
    for p in range(npairs):
        ht_ref[p] = hts[p]


def wkv7_chunked(r, k, v, kk, lw2, a2, ka):
    Bn, L, C = r.shape
    tb = WKV_TIME_BLOCK
    nt = L // tb
    tmap = lambda i, t: t + (i // Bn) * (nt - 1 - 2 * t)
    shared = pl.BlockSpec((1, tb, C), lambda i, t: (i % Bn, tmap(i, t), 0))
    per_dir = pl.BlockSpec((1, tb, C), lambda i, t: (i % Bn, tmap(i, t), i // Bn))
    return pl.pallas_call(
        functools.partial(_wkv_kernel, tb=tb, batch=Bn),
        grid=(2 * Bn, nt),
        in_specs=[shared, shared, shared, shared, per_dir, per_dir, pl.BlockSpec((1, C), lambda i, t: (0, 0))],
        out_specs=pl.BlockSpec((1, tb, C), lambda i, t: (i, tmap(i, t), 0)),
        out_shape=jax.ShapeDtypeStruct((2 * Bn, L, C), f32),
        scratch_shapes=[pltpu.VMEM((C // LANES, LANES, LANES), f32)],
        compiler_params=pltpu.CompilerParams(dimension_semantics=("parallel", "arbitrary")),
        name="wkv7_chunked",
    )(r, k, v, kk, lw2, a2, ka)


def _mm_exact_rhs(a, b_bf16):
    ah = a.astype(bf16)
    al = (a - ah.astype(f32)).astype(bf16)
    d = lambda x: lax.dot_general(x, b_bf16, _DN['nn'], preferred_element_type=f32)
    return d(ah) + d(al)


def _head_sum_matrix(scale):
    i = lax.broadcasted_iota(jnp.int32, (RW_D, RW_D), 0) // RW_N
    j = lax.broadcasted_iota(jnp.int32, (RW_D, RW_D), 1) // RW_N
    return jnp.where(i == j, scale, 0.0).astype(bf16)


def _rw_prep_kernel(r_ref, k_ref, v_ref, rp_ref, kp_ref, vp_ref, rn_ref, kn_ref, vn_ref, w_ref, a_ref,
                    mu_ref, w0_ref, w2_ref, a0_ref, a2_ref, kkw_ref,
                    ro_ref, ko_ref, vo_ref, kko_ref, lw_ref, ao_ref, *, rows, nt):
    t = pl.program_id(1)
    has_prev = (t > 0).astype(f32)
    has_next = (t < nt - 1).astype(f32)
    ridx = lax.broadcasted_iota(jnp.int32, (rows, RW_D), 0)

    def tshift(cur_ref, prev_ref, next_ref, j):
        u = cur_ref[0]
        prev_row = prev_ref[0, SUBLANES - 1:SUBLANES, :] * has_prev
        next_row = next_ref[0, 0:1, :] * has_next
        up = jnp.where(ridx == 0, prev_row, pltpu.roll(u, 1, axis=0))
        un = jnp.where(ridx == rows - 1, next_row, pltpu.roll(u, rows - 1, axis=0))
        return u + mu_ref[j, 0:1, :] * (up - u) + mu_ref[j, 1:2, :] * (un - u)

    r = tshift(r_ref, rp_ref, rn_ref, 0)
    k = tshift(k_ref, kp_ref, kn_ref, 1)
    v = tshift(v_ref, vp_ref, vn_ref, 2)
    ro_ref[0] = r
    ko_ref[0] = k
    vo_ref[0] = v
    kk = k * kkw_ref[...]
    ss = _mm_exact_rhs(kk * kk, _head_sum_matrix(1.0))
    kko_ref[0] = kk * lax.rsqrt(jnp.maximum(ss, 1e-24))
    wl = jnp.tanh(w_ref[0])
    al = a_ref[0]
    for d in range(2):
        wlog = w0_ref[d:d + 1, :] + _mm(wl[:, d * RW_W_LORA:(d + 1) * RW_W_LORA], w2_ref[d])
        wlog = -jax.nn.softplus(-wlog) - 0.5
        lw_ref[0, :, d * RW_D:(d + 1) * RW_D] = -jnp.exp(wlog)
        av = a0_ref[d:d + 1, :] + _mm(al[:, d * RW_A_LORA:(d + 1) * RW_A_LORA], a2_ref[d])
        ao_ref[0, :, d * RW_D:(d + 1) * RW_D] = jax.nn.sigmoid(av)


def rw_prep(proj, lora_lane0, mu, w0, w2, a0, a2, k_k):
    Bn, L, _ = proj.shape
    C = RW_D
    rows = RW_ROWS_PER_STEP
    nt = L // rows
    hb = rows // SUBLANES
    lb = lora_lane0 // LANES
    cur = lambda j: pl.BlockSpec((1, rows, C), lambda b, t: (b, t, j))
    prev = lambda j: pl.BlockSpec((1, SUBLANES, C), lambda b, t: (b, jnp.maximum(t * hb - 1, 0), j))
    nxt = lambda j: pl.BlockSpec((1, SUBLANES, C), lambda b, t: (b, jnp.minimum((t + 1) * hb, L // SUBLANES - 1), j))
    lora_w = pl.BlockSpec((1, rows, LANES), lambda b, t: (b, t, lb + 1))
    lora_a = pl.BlockSpec((1, rows, LANES), lambda b, t: (b, t, lb + 2))
    full = lambda shp: pl.BlockSpec(shp, lambda b, t: (0,) * len(shp))
    out_c = pl.BlockSpec((1, rows, C), lambda b, t: (b, t, 0))
    out_2c = pl.BlockSpec((1, rows, 2 * C), lambda b, t: (b, t, 0))
    sds = lambda c: jax.ShapeDtypeStruct((Bn, L, c), f32)
    return pl.pallas_call(
        functools.partial(_rw_prep_kernel, rows=rows, nt=nt),
        grid=(Bn, nt),
        in_specs=[cur(0), cur(1), cur(2), prev(0), prev(1), prev(2), nxt(0), nxt(1), nxt(2), lora_w, lora_a,
                  full((3, 2, C)), full((2, C)), full((2, RW_W_LORA, C)), full((2, C)), full((2, RW_A_LORA, C)),
                  full((1, C))],
        out_specs=[out_c, out_c, out_c, out_c, out_2c, out_2c],
        out_shape=[sds(C), sds(C), sds(C), sds(C), sds(2 * C), sds(2 * C)],
        compiler_params=pltpu.CompilerParams(dimension_semantics=("parallel", "parallel"),
                                             vmem_limit_bytes=VMEM_LIMIT_BYTES),
        name="rwkv_prep",
    )(proj, proj, proj, proj, proj, proj, proj, proj, proj, proj, proj,
      mu.astype(f32), w0.astype(f32), w2.astype(bf16), a0.astype(f32), a2.astype(bf16), k_k.astype(f32).reshape(1, C))


def _rw_post_kernel(yf_ref, yb_ref, r_ref, k_ref, v_ref, a_ref, g_ref, ka_ref, rk_ref, lnw_ref, lnb_ref, g2_ref, o_ref):
    C = RW_D
    y = yf_ref[0] + yb_ref[0]
    avg = _head_sum_matrix(1.0 / RW_N)
    mean = _mm_exact_rhs(y, avg)
    yc = y - mean
    var = _mm_exact_rhs(yc * yc, avg)
    yn = yc * lax.rsqrt(var + RW_GN_EPS) * lnw_ref[...] + lnb_ref[...]
    a = a_ref[0]
    k = k_ref[0]
    ka = ka_ref[...]
    ksum = k * (1.0 + (a[:, :C] - 1.0) * ka) + k * (1.0 + (a[:, C:] - 1.0) * ka)
    coef = _mm_exact_rhs(r_ref[0] * ksum * rk_ref[...], _head_sum_matrix(1.0))
    gate = _mm(jax.nn.sigmoid(g_ref[0]), g2_ref[...])
    o_ref[0] = (yn + coef * v_ref[0]) * gate


def rw_post(y2, r, k, v, a2, proj, g_lane0, k_a, r_k, ln_w, ln_b, g2):
    Bn, L, C = r.shape
    rows = RW_ROWS_PER_STEP
    nt = L // rows
    gb = g_lane0 // LANES
    blk = lambda c: pl.BlockSpec((1, rows, c), lambda b, t: (b, t, 0))
    full = lambda shp: pl.BlockSpec(shp, lambda b, t: (0,) * len(shp))
    row = lambda x: x.astype(f32).reshape(1, C)
    return pl.pallas_call(
        _rw_post_kernel,
        grid=(Bn, nt),
        in_specs=[blk(C), pl.BlockSpec((1, rows, C), lambda b, t: (b + Bn, t, 0)), blk(C), blk(C), blk(C), blk(2 * C),
                  pl.BlockSpec((1, rows, LANES), lambda b, t: (b, t, gb)),
                  full((1, C)), full((1, C)), full((1, C)), full((1, C)), full((RW_G_LORA, C))],
        out_specs=blk(C),
        out_shape=jax.ShapeDtypeStruct((Bn, L, C), f32),
        compiler_params=pltpu.CompilerParams(dimension_semantics=("parallel", "parallel"),
                                             vmem_limit_bytes=VMEM_LIMIT_BYTES),
        name="rwkv_post",
    )(y2, y2, r, k, v, a2, proj, row(k_a), row(r_k), row(ln_w), row(ln_b), g2.astype(bf16))


def _na_kernel(q_ref, k_ref, v_ref, bias_ref, o_ref, kb_ref, vb_ref, *, rows_per_step, n_rows, kr):
    W = GRID_W
    rb = pl.program_id(2)

    @pl.when(rb == 0)
    def _():
        kb_ref[...] = k_ref[0].astype(bf16)
        vb_ref[...] = v_ref[0].astype(bf16)

    lo_lane = lax.broadcasted_iota(jnp.int32, (W, LANES), 1) < NA_HD
    scale = NA_HD ** -0.5
    rows = []
    for j in range(rows_per_step):
        r = rb * rows_per_step + j
        start = jnp.clip(r - kr // 2, 0, n_rows - kr)
        rows.append((start, start - r + (NA_KR - 1)))
    s_list = []
    for j, (start, didx) in enumerate(rows):
        q = q_ref[0, j * W:(j + 1) * W, :] * scale
        qs = jnp.concatenate([jnp.where(lo_lane, q, 0.0), jnp.where(lo_lane, 0.0, q)], axis=0)
        kw = kb_ref[pl.ds(pl.multiple_of(start * W, W), kr * W), :]
        s_list.append(_mm(qs, kw, 'nt') + bias_ref[didx, 0])
    p_list = []
    for s in s_list:
        m = jnp.max(s, axis=-1, keepdims=True)
        p = jnp.exp(s - m)
        p_list.append((p, jnp.sum(p, axis=-1, keepdims=True)))
    for j, ((start, _), (p, l)) in enumerate(zip(rows, p_list)):
        vw = vb_ref[pl.ds(pl.multiple_of(start * W, W), kr * W), :]
        o = _mm(p, vw) / l
        o_ref[0, j * W:(j + 1) * W, :] = jnp.where(lo_lane, o[:W], o[W:])


def na_bias_table(rpb, n_rows):
    W = GRID_W
    kr = min(NA_KR, n_rows)
    cols = jnp.arange(W)
    col_start = jnp.clip(cols - NA_KC // 2, 0, W - NA_KC)
    in_band = (cols[None, :] >= col_start[:, None]) & (cols[None, :] < col_start[:, None] + NA_KC)
    dc = jnp.clip(cols[None, :] - cols[:, None], -(NA_KC - 1), NA_KC - 1) + (NA_KC - 1)
    win = jnp.stack([rpb.astype(f32)[:, d:d + kr, :] for d in range(8)], axis=1)
    onehot = (dc[:, :, None] == jnp.arange(2 * NA_KC - 1)[None, None, :]).astype(f32)
    tab = jnp.einsum('hdic,qkc->hdqik', win, onehot, precision=lax.Precision.HIGHEST)
    tab = jnp.where(in_band[None, None, :, None, :], tab, NEG_INF)
    H = rpb.shape[0]
    tab = tab.reshape(H // 2, 2, 8, W, kr * W).transpose(2, 0, 1, 3, 4)
    return tab.reshape(8, H // 2, 2 * W, kr * W)


def neighborhood_attention_pallas(proj, rpb, lane_block0):
    Bn, L, _ = proj.shape
    W = GRID_W
    n_rows = L // W
    kr = min(NA_KR, n_rows)
    hp = NA_H * NA_HD // LANES
    rps = NA_ROWS_PER_STEP
    bias = na_bias_table(rpb, n_rows)
    kern = functools.partial(_na_kernel, rows_per_step=rps, n_rows=n_rows, kr=kr)
    return pl.pallas_call(
        kern,
        grid=(Bn, hp, n_rows // rps),
        in_specs=[pl.BlockSpec((1, rps * W, LANES), lambda b, h, r: (b, r, lane_block0 + h)),
                  pl.BlockSpec((1, L, LANES), lambda b, h, r: (b, 0, lane_block0 + hp + h)),
                  pl.BlockSpec((1, L, LANES), lambda b, h, r: (b, 0, lane_block0 + 2 * hp + h)),
                  pl.BlockSpec((8, 1, 2 * W, kr * W), lambda b, h, r: (0, h, 0, 0))],
        out_specs=pl.BlockSpec((1, rps * W, LANES), lambda b, h, r: (b, r, h)),
        out_shape=jax.ShapeDtypeStruct((Bn, L, NA_H * NA_HD), f32),
        scratch_shapes=[pltpu.VMEM((L, LANES), bf16), pltpu.VMEM((L, LANES), bf16)],
        compiler_params=pltpu.CompilerParams(dimension_semantics=("parallel", "parallel", "arbitrary"),
                                             vmem_limit_bytes=VMEM_LIMIT_BYTES),
        name="neighborhood_attention",
    )(proj, proj, proj, bias)


def _moe_ffn_kernel(be_ref, x_ref, wt_ref, w1_ref, w3_ref, w2_ref, o_ref):
    x = x_ref[...].astype(bf16)
    h1 = jnp.dot(x, w1_ref[0], preferred_element_type=f32)
    h3 = jnp.dot(x, w3_ref[0], preferred_element_type=f32)
    g = h1 * jax.nn.sigmoid(h1) * h3
    y = jnp.dot(g.astype(bf16), w2_ref[0], preferred_element_type=f32)
    o_ref[...] = y * wt_ref[...]


def moe_expert_ffn(xs, wt, block_e, w1, w3, w2):
    rows, Dm = xs.shape
    n_blocks = rows // MOE_BLOCK
    grid_spec = pltpu.PrefetchScalarGridSpec(
        num_scalar_prefetch=1,
        grid=(n_blocks,),
        in_specs=[pl.BlockSpec((MOE_BLOCK, Dm), lambda i, be: (i, 0)),
                  pl.BlockSpec((MOE_BLOCK, 1), lambda i, be: (i, 0)),
                  pl.BlockSpec((1, Dm, MOE_FF), lambda i, be: (be[i], 0, 0)),
                  pl.BlockSpec((1, Dm, MOE_FF), lambda i, be: (be[i], 0, 0)),
                  pl.BlockSpec((1, MOE_FF, Dm), lambda i, be: (be[i], 0, 0))],
        out_specs=pl.BlockSpec((MOE_BLOCK, Dm), lambda i, be: (i, 0)),
    )
    return pl.pallas_call(
        _moe_ffn_kernel,
        grid_spec=grid_spec,
        out_shape=jax.ShapeDtypeStruct((rows, Dm), f32),
        compiler_params=pltpu.CompilerParams(dimension_semantics=("arbitrary",)),
        name="moe_expert_ffn",
    )(block_e.astype(jnp.int32), xs, wt, w1.astype(bf16), w3.astype(bf16), w2.astype(bf16))


def _final_norm_kernel(x_ref, g_ref, o_ref):
    x = x_ref[...]
    o_ref[...] = x * lax.rsqrt(jnp.mean(x * x, axis=-1, keepdims=True) + NORM_EPS) * g_ref[...]


def final_rmsnorm(x, g):
    Bn, L, D = x.shape
    tm = 1024
    out = pl.pallas_call(
        _final_norm_kernel,
        grid=(Bn * L // tm,),
        in_specs=[pl.BlockSpec((tm, D), lambda i: (i, 0)), pl.BlockSpec((1, D), lambda i: (0, 0))],
        out_specs=pl.BlockSpec((tm, D), lambda i: (i, 0)),
        out_shape=jax.ShapeDtypeStruct((Bn * L, D), x.dtype),
        name="final_rmsnorm",
    )(x.reshape(Bn * L, D), g.reshape(1, D))
    return out.reshape(Bn, L, D)


def rmsnorm(x, g):
    x32 = x.astype(f32)
    y = x32 * lax.rsqrt(jnp.mean(x32 * x32, axis=-1, keepdims=True) + NORM_EPS)
    return y.astype(x.dtype) * g


def hyena_positional_features(L):
    t = jnp.linspace(0.0, 1.0, L, dtype=f32)[:, None]
    w = (2.0 * math.pi / L) * jnp.arange(L, dtype=f32)[:, None]
    f = jnp.linspace(1e-4, HY_BANDS - 1, HY_BANDS, dtype=f32)[None, :]
    z = jnp.concatenate([t, jnp.cos(f * w), -jnp.sin(f * w)], axis=-1)
    return z, t


def hyena_filters(z, t, w1, b1, w2, b2, w3, b3, wout, freq):
    fr = freq.astype(f32)
    act = lambda u: jnp.sin(fr * u)
    h = act(z @ w1.astype(f32) + b1.astype(f32))
    h = act(h @ w2.astype(f32) + b2.astype(f32))
    h = act(h @ w3.astype(f32) + b3.astype(f32))
    h = (h @ wout.astype(f32)).reshape(-1, 2, HY_D)
    deltas = jnp.abs(jnp.linspace(HY_MIN_DECAY, HY_MAX_DECAY, HY_D, dtype=f32))
    h = h * jnp.exp(-t[:, :, None] * deltas)
    return h * lax.rsqrt(jnp.sum(h * h, axis=(0, 1), keepdims=True) + 1e-6)


def _split_bf16(x):
    hi = x.astype(bf16)
    return hi, (x - hi.astype(f32)).astype(bf16)


def _dot3(m_hi, m_lo, x_hi, x_lo):
    d = lambda a, b: lax.dot_general(a, b, _DN['nn'], preferred_element_type=f32)
    return d(m_hi, x_hi) + (d(m_hi, x_lo) + d(m_lo, x_hi))


def _cmatmul(mr, mi, xr, xi):
    xrh, xrl = _split_bf16(xr)
    rr = _dot3(mr[0], mr[1], xrh, xrl)
    ir = _dot3(mi[0], mi[1], xrh, xrl)
    if xi is None:
        return rr, ir
    xih, xil = _split_bf16(xi)
    ii = _dot3(mi[0], mi[1], xih, xil)
    ri = _dot3(mr[0], mr[1], xih, xil)
    return rr - ii, ri + ir


def _row_dft_kernel(mrh_ref, mrl_ref, mih_ref, mil_ref, *refs, real_input):
    if real_input:
        ur_ref, or_ref, oi_ref = refs
        ui = None
    else:
        ur_ref, ui_ref, or_ref, oi_ref = refs
        ui = ui_ref[0]
    o_r, o_i = _cmatmul((mrh_ref[...], mrl_ref[...]), (mih_ref[...], mil_ref[...]), ur_ref[0], ui)
    or_ref[0] = o_r
    oi_ref[0] = o_i


def row_dft(tabs, u, packed):
    rows, r_in, W = u.shape
    P = rows // 2 if packed else rows
    r_out = tabs[0].shape[0]
    wb = min(FFT_LANE_BLOCK, W)
    tab_spec = pl.BlockSpec((r_out, r_in), lambda p, j: (0, 0))
    re_spec = pl.BlockSpec((1, r_in, wb), lambda p, j: (p, 0, j))
    im_spec = pl.BlockSpec((1, r_in, wb), lambda p, j: (p + P, 0, j))
    out_spec = pl.BlockSpec((1, r_out, wb), lambda p, j: (p, 0, j))
    ins = [u, u] if packed else [u]
    return pl.pallas_call(
        functools.partial(_row_dft_kernel, real_input=not packed),
        grid=(P, W // wb),
        in_specs=[tab_spec] * 4 + ([re_spec, im_spec] if packed else [re_spec]),
        out_specs=[out_spec, out_spec],
        out_shape=[jax.ShapeDtypeStruct((P, r_out, W), f32)] * 2,
        compiler_params=pltpu.CompilerParams(dimension_semantics=("parallel", "parallel"),
                                             vmem_limit_bytes=VMEM_LIMIT_BYTES),
        name="hyena_row_dft",
    )(*tabs, *ins)


def _row_idft_gate_kernel(mrh_ref, mrl_ref, mih_ref, mil_ref, dr_ref, di_ref, z0_ref, z1_ref, x0_ref, x1_ref,
                          skip_ref, o0_ref, o1_ref):
    y_r, y_i = _cmatmul((mrh_ref[...], mrl_ref[...]), (mih_ref[...], mil_ref[...]), dr_ref[0], di_ref[0])
    skip = skip_ref[...]
    o0_ref[0] = x0_ref[0] * (y_r + z0_ref[0] * skip)
    o1_ref[0] = x1_ref[0] * (y_i + z1_ref[0] * skip)


def row_idft_gate(tabs, dr, di, z, x0, skip_row):
    P, r_in, W = dr.shape
    r_out = tabs[0].shape[0]
    wb = min(FFT_LANE_BLOCK, W)
    tab_spec = pl.BlockSpec((r_out, r_in), lambda p, j: (0, 0))
    d_spec = pl.BlockSpec((1, r_in, wb), lambda p, j: (p, 0, j))
    lo = pl.BlockSpec((1, r_out, wb), lambda p, j: (p, 0, j))
    hi = pl.BlockSpec((1, r_out, wb), lambda p, j: (p + P, 0, j))
    o0, o1 = pl.pallas_call(
        _row_idft_gate_kernel,
        grid=(P, W // wb),
        in_specs=[tab_spec] * 4 + [d_spec, d_spec, lo, hi, lo, hi, pl.BlockSpec((1, wb), lambda p, j: (0, j))],
        out_specs=[lo, lo],
        out_shape=[jax.ShapeDtypeStruct((P, r_out, W), f32)] * 2,
        compiler_params=pltpu.CompilerParams(dimension_semantics=("parallel", "parallel"),
                                             vmem_limit_bytes=VMEM_LIMIT_BYTES),
        name="hyena_row_idft_gate",
    )(*tabs, dr, di, z, z, x0, x0, skip_row)
    return jnp.concatenate([o0, o1], axis=0)


def _col_dft_kernel(grh_ref, grl_ref, gih_ref, gil_ref, ar_ref, ai_ref, br_ref, bi_ref, *, k1_per_step):
    for j in range(k1_per_step):
        b_r, b_i = _cmatmul((grh_ref[j], grl_ref[j]), (gih_ref[j], gil_ref[j]), ar_ref[0, j], ai_ref[0, j])
        br_ref[0, j] = b_r
        bi_ref[0, j] = b_i


def _col_conv_kernel(grh_ref, grl_ref, gih_ref, gil_ref, trh_ref, trl_ref, tih_ref, til_ref,
                     ar_ref, ai_ref, kr_ref, ki_ref, dr_ref, di_ref, *, k1_per_step):
    for j in range(k1_per_step):
        b_r, b_i = _cmatmul((grh_ref[j], grl_ref[j]), (gih_ref[j], gil_ref[j]), ar_ref[0, j], ai_ref[0, j])
        k_r = kr_ref[0, j]
        k_i = ki_ref[0, j]
        c_r = b_r * k_r - b_i * k_i
        c_i = b_r * k_i + b_i * k_r
        d_r, d_i = _cmatmul((trh_ref[j], trl_ref[j]), (tih_ref[j], til_ref[j]), c_r, c_i)
        dr_ref[0, j] = d_r
        di_ref[0, j] = d_i


def col_stage(g_tabs, gt_tabs, ar, ai, kr=None, ki=None):
    P, n1, n2, C = ar.shape
    kb = min(FFT_K1_PER_STEP, n1)
    g_spec = pl.BlockSpec((kb, n2, n2), lambda p, j: (j, 0, 0))
    a_spec = pl.BlockSpec((1, kb, n2, C), lambda p, j: (p, j, 0, 0))
    k_spec = pl.BlockSpec((1, kb, n2, C), lambda p, j: (0, j, 0, 0))
    common = dict(
        grid=(P, n1 // kb),
        out_specs=[a_spec, a_spec],
        out_shape=[jax.ShapeDtypeStruct((P, n1, n2, C), f32)] * 2,
        compiler_params=pltpu.CompilerParams(dimension_semantics=("parallel", "parallel"),
                                             vmem_limit_bytes=VMEM_LIMIT_BYTES),
    )
    if kr is None:
        return pl.pallas_call(functools.partial(_col_dft_kernel, k1_per_step=kb),
                              in_specs=[g_spec] * 4 + [a_spec] * 2, name="hyena_col_dft", **common)(*g_tabs, ar, ai)
    return pl.pallas_call(functools.partial(_col_conv_kernel, k1_per_step=kb),
                          in_specs=[g_spec] * 8 + [a_spec] * 2 + [k_spec] * 2, name="hyena_col_conv", **common)(
        *g_tabs, *gt_tabs, ar, ai, kr, ki)


def _bf16_tables(m):
    out = []
    for part in (np.real(m), np.imag(m)):
        x = jnp.asarray(part, f32)
        hi = x.astype(bf16)
        out += [hi, (x - hi.astype(f32)).astype(bf16)]
    return out


def hyena_dft_tables(L):
    n2 = FFT_N2
    n1 = 2 * L // n2
    N = n1 * n2
    a = np.arange(n1)
    fa = np.exp(-2j * np.pi * np.outer(a, a) / n1)
    k1 = np.arange(n1)[:, None, None]
    k2 = np.arange(n2)[None, :, None]
    nn = np.arange(n2)[None, None, :]
    g = np.exp(-2j * np.pi * nn * (k1 + n1 * k2) / N)
    gt = np.conj(np.transpose(g, (0, 2, 1)))
    fc = np.conj(fa).T / N
    return dict(fa=_bf16_tables(fa), fa_half=_bf16_tables(fa[:, :n1 // 2]), g=_bf16_tables(g), gt=_bf16_tables(gt),
                fc_half=_bf16_tables(fc[:n1 // 2]))


def hyena_long_conv_gate(tabs, z, x0, k2, skip):
    Bn, L, C = z.shape
    n2 = FFT_N2
    n1 = 2 * L // n2
    P = Bn // 2
    W = n2 * C
    ar, ai = row_dft(tabs['fa'], k2.reshape(1, n1, W), packed=False)
    kr, ki = col_stage(tabs['g'], tabs['gt'], ar.reshape(1, n1, n2, C), ai.reshape(1, n1, n2, C))
    zv = z.reshape(Bn, n1 // 2, W)
    ar, ai = row_dft(tabs['fa_half'], zv, packed=True)
    dr, di = col_stage(tabs['g'], tabs['gt'], ar.reshape(P, n1, n2, C), ai.reshape(P, n1, n2, C), kr, ki)
    out = row_idft_gate(tabs['fc_half'], dr.reshape(P, n1, W), di.reshape(P, n1, W), zv, x0.reshape(Bn, n1 // 2, W),
                        jnp.tile(skip.astype(f32), n2).reshape(1, W))
    return out.reshape(Bn, L, C)


def hyena_mixer(tabs, u, conv_w, conv_b, filt, skip):
    up = jnp.pad(u, ((0, 0), (1, 1), (0, 0)))
    uc = up[:, :-2] * conv_w[0] + up[:, 1:-1] * conv_w[1] + up[:, 2:] * conv_w[2] + conv_b
    x0, x1, v = jnp.split(uc, 3, axis=-1)
    z = (v * x1).astype(f32)
    k2 = jnp.concatenate([filt[:1, 0] + filt[:1, 1], filt[1:, 0],
                          jnp.zeros((1, HY_D), f32), filt[1:, 1][::-1]], axis=0)
    return hyena_long_conv_gate(tabs, z, x0.astype(f32), k2, skip)


def rwkv7_mixer(proj, lora_lane0, mu, w0, w2, a0, a2, k_k, k_a, r_k, g2, ln_w, ln_b):
    r, k, v, kk, lw2, a_both = rw_prep(proj, lora_lane0, mu, w0, w2, a0, a2, k_k)
    y2 = wkv7_chunked(r, k, v, kk, lw2, a_both, k_a.astype(f32).reshape(1, RW_D))
    return rw_post(y2, r, k, v, a_both, proj, lora_lane0, k_a, r_k, ln_w, ln_b, g2)


def hier_moe(h, wg, bg, we, be, w1, w3, w2):
    N, Dm = h.shape
    h32 = h.astype(f32)
    g_logits = h32 @ wg.astype(f32) + bg.astype(f32)
    g_sel = jnp.argmax(g_logits, axis=-1)
    g_prob = jnp.take_along_axis(jax.nn.softmax(g_logits, axis=-1), g_sel[:, None], axis=-1)
    e_logits = (h32 @ we.astype(f32) + be.astype(f32)).reshape(N, MOE_GROUPS, MOE_PER_GROUP)
    e_logits = jnp.take_along_axis(e_logits, g_sel[:, None, None], axis=1)[:, 0]
    top_val, top_idx = lax.top_k(e_logits, MOE_TOPK)
    gate = g_prob * jax.nn.softmax(top_val, axis=-1)
    expert = g_sel[:, None] * MOE_PER_GROUP + top_idx
    M = N * MOE_TOPK
    flat_e = expert.reshape(M)
    order = jnp.argsort(flat_e)
    e_sorted = flat_e[order]
    counts = jnp.bincount(flat_e, length=MOE_EXPERTS)
    padded = (counts + MOE_BLOCK - 1) // MOE_BLOCK * MOE_BLOCK
    pad_end = jnp.cumsum(padded)
    first = (jnp.cumsum(counts) - counts)[e_sorted]
    slot = (pad_end - padded)[e_sorted] + jnp.arange(M) - first
    n_blocks = -(-M // MOE_BLOCK) + MOE_EXPERTS
    slot_src = jnp.full((n_blocks * MOE_BLOCK,), M, jnp.int32).at[slot].set(order.astype(jnp.int32))
    real = slot_src < M
    src = jnp.minimum(slot_src, M - 1)
    tok_src = jnp.where(real, src // MOE_TOPK, 0)
    w_src = jnp.where(real, gate.reshape(M)[src], 0.0)
    block_start = jnp.arange(n_blocks, dtype=pad_end.dtype) * MOE_BLOCK
    block_e = jnp.minimum(jnp.sum(pad_end[None, :] <= block_start[:, None], axis=1), MOE_EXPERTS - 1)
    y = moe_expert_ffn(h[tok_src], w_src[:, None], block_e, w1, w3, w2)
    pos = jnp.zeros((M,), jnp.int32).at[order].set(slot.astype(jnp.int32))
    return y[pos].reshape(N, MOE_TOPK, Dm).sum(axis=1)


def kernel(x, norm1_g, w_in, hy_conv_w, hy_conv_b, hy_w1, hy_b1, hy_w2, hy_b2, hy_w3, hy_b3, hy_wout, hy_freq, hy_skip, rw_mu, rw_w0, rw_w2, rw_a0, rw_a2, rw_kk, rw_ka, rw_rk, rw_g2, rw_ln_w, rw_ln_b, na_rpb, w_out, norm2_g, moe_wg, moe_bg, moe_we, moe_be, moe_w1, moe_w3, moe_w2, norm_f_g):
    Bn, L, _ = x.shape
    z_pos, t_pos = hyena_positional_features(L)
    dft_tabs = hyena_dft_tables(L)
    splits = np.cumsum(IN_SIZES)[:-1].tolist()
    hy_end, rkv_end = splits[0], splits[1]
    hy_lane0 = rkv_end - hy_end
    lora_lane0, na_lane0 = splits[1], splits[4]
    assert RW_D % LANES == 0 and lora_lane0 % LANES == 0 and na_lane0 % LANES == 0
    for l in range(DEPTH):
        h = rmsnorm(x, norm1_g[l])
        w_in_l = jnp.concatenate([w_in[l][:, hy_end:rkv_end], w_in[l][:, :hy_end], w_in[l][:, rkv_end:]], axis=1)
        proj = h @ w_in_l
        hy_u = proj[..., hy_lane0:hy_lane0 + hy_end]
        filt = hyena_filters(z_pos, t_pos, hy_w1[l], hy_b1[l], hy_w2[l], hy_b2[l],
                             hy_w3[l], hy_b3[l], hy_wout[l], hy_freq[l])
        y_hy = hyena_mixer(dft_tabs, hy_u, hy_conv_w[l], hy_conv_b[l], filt, hy_skip[l])
        y_rw = rwkv7_mixer(proj, lora_lane0, rw_mu[l], rw_w0[l], rw_w2[l], rw_a0[l],
                           rw_a2[l], rw_kk[l], rw_ka[l], rw_rk[l].reshape(RW_D), rw_g2[l], rw_ln_w[l], rw_ln_b[l])
        y_na = neighborhood_attention_pallas(proj, na_rpb[l], na_lane0 // LANES)
        x = x + jnp.concatenate([y_hy, y_rw, y_na], axis=-1) @ w_out[l]
        h = rmsnorm(x, norm2_g[l]).reshape(Bn * L, D_MODEL)
        x = x + hier_moe(h, moe_wg[l], moe_bg[l], moe_we[l], moe_be[l],
                         moe_w1[l], moe_w3[l], moe_w2[l]).reshape(Bn, L, D_MODEL)
    return final_rmsnorm(x, norm_f_g)
```

```python
import functools
import math

import jax
import jax.numpy as jnp
import numpy as np
from jax import lax
from jax.experimental import pallas as pl
from jax.experimental.pallas import tpu as pltpu

f32 = jnp.float32
bf16 = jnp.bfloat16

D_MODEL = 1024
DEPTH = 2
GRID_W = 64
NORM_EPS = 1e-6
NEG_INF = -1e30

HY_D = D_MODEL // 4
HY_EMB = 33
HY_BANDS = (HY_EMB - 1) // 2
HY_FFN = 64
HY_MIN_DECAY = math.log(1e-2) / 1.5
HY_MAX_DECAY = math.log(1e-2) / 0.3

RW_N = 64
RW_D = D_MODEL // 2
RW_H = RW_D // RW_N
RW_W_LORA = 64
RW_A_LORA = 64
RW_G_LORA = 128
RW_GN_EPS = 64e-5

NA_HD = 64
NA_D = D_MODEL // 4
NA_H = NA_D // NA_HD
NA_KR = 8
NA_KC = 16

MIX_D = HY_D + RW_D + NA_D
IN_SIZES = (3 * HY_D, 3 * RW_D, RW_G_LORA, 2 * RW_W_LORA, 2 * RW_A_LORA, 3 * NA_D)
IN_D = sum(IN_SIZES)

MOE_GROUPS = 4
MOE_PER_GROUP = 8
MOE_EXPERTS = MOE_GROUPS * MOE_PER_GROUP
MOE_TOPK = 2
MOE_FF = 512
MOE_BLOCK = 256

LANES = 128
WKV_CHUNK = 64
WKV_TIME_BLOCK = 128
NA_ROWS_PER_STEP = 4
RW_ROWS_PER_STEP = 512
SUBLANES = 8
PROJ_ROWS_PER_STEP = 512
FFT_N2 = 128
FFT_LANE_BLOCK = 4096
FFT_K1_PER_STEP = 4
VMEM_LIMIT_BYTES = 48 * 1024 * 1024

_DN = {'nn': (((1,), (0,)), ((), ())), 'nt': (((1,), (1,)), ((), ())), 'tn': (((0,), (0,)), ((), ()))}


def _mm(a, b, dims='nn'):
    return lax.dot_general(a.astype(bf16), b.astype(bf16), _DN[dims], preferred_element_type=f32)


def _wkv_kernel(r_ref, k_ref, v_ref, kk_ref, lw_ref, a_ref, ka_ref, y_ref, ht_ref, *, tb, batch):
    T = WKV_CHUNK
    H2 = 2 * T
    nc = tb // T
    npairs = r_ref.shape[-1] // LANES
    mm = _mm

    @pl.when(pl.program_id(1) == 0)
    def _():
        ht_ref[...] = jnp.zeros_like(ht_ref)

    d = pl.program_id(0) // batch
    sign = 1 - 2 * d
    row = lax.broadcasted_iota(jnp.int32, (H2, H2), 0)
    col = lax.broadcasted_iota(jnp.int32, (H2, H2), 1)
    same64 = (row // T) == (col // T)
    tdiff = (row % T - col % T) * sign
    strict = same64 & (tdiff > 0)
    incl = same64 & (tdiff >= 0)
    same16 = (row // 16) == (col // 16)
    same32 = (row // 32) == (col // 32)
    off32 = same32 & jnp.logical_not(same16)
    off64 = jnp.logical_not(same32)
    eye = jnp.where(row == col, 1.0, 0.0).astype(f32)
    trow = lax.broadcasted_iota(jnp.int32, (T, T), 0)
    tcol = lax.broadcasted_iota(jnp.int32, (T, T), 1)
    tri = jnp.where((trow - tcol) * sign >= 0, 1.0, 0.0).astype(bf16)
    lo_lane = lax.broadcasted_iota(jnp.int32, (T, LANES), 1) < T
    is_bwd = d == 1

    def stack(x):
        return jnp.concatenate([jnp.where(lo_lane, x, 0.0), jnp.where(lo_lane, 0.0, x)], axis=0)

    inst = [(s, p) for s in range(nc) for p in range(npairs)]
    offs = [pl.multiple_of((s + d * (nc - 1 - 2 * s)) * T, T) for s in range(nc)]

    def load(ref, s, p):
        return ref[0, pl.ds(offs[s], T), p * LANES:(p + 1) * LANES]

    cs_l = []
    for s, p in inst:
        lw = load(lw_ref, s, p)
        l1 = lw.astype(bf16)
        r1 = lw - l1.astype(f32)
        l2 = r1.astype(bf16)
        l3 = (r1 - l2.astype(f32)).astype(bf16)
        dd = lambda x: lax.dot_general(tri, x, _DN['nn'], preferred_element_type=f32)
        cs_l.append((dd(l1) + (dd(l2) + dd(l3)), lw))
    ops = []
    for (s, p), (cs, lw) in zip(inst, cs_l):
        r = load(r_ref, s, p)
        k = load(k_ref, s, p)
        v = load(v_ref, s, p)
        kk = load(kk_ref, s, p)
        a = load(a_ref, s, p)
        ka = ka_ref[:, p * LANES:(p + 1) * LANES]
        kd = k * (1.0 + (a - 1.0) * ka)
        b = kk * a
        cs_end = jnp.where(is_bwd, cs[0:1, :], cs[T - 1:T, :])
        em = jnp.exp(-cs)
        e_end = jnp.exp(cs_end - cs)
        ops.append(dict(
            As=stack(-kk * jnp.exp(cs - lw)), Rs=stack(r * jnp.exp(cs)), Bs=stack(b * em), Ks=stack(kd * em),
            Vs=stack(v), Bgs=stack(b * e_end), Kgs=stack(kd * e_end), g_end=jnp.exp(cs_end)))
    for o in ops:
        o['AR'] = jnp.concatenate([o['As'], o['Rs']], axis=0)
        S = mm(o['AR'], jnp.concatenate([o['Bs'], o['Ks']], axis=0), 'nt')
        N = jnp.where(strict, S[:H2, :H2], 0.0)
        o['Aak'] = jnp.where(strict, S[:H2, H2:], 0.0)
        o['Arb'] = jnp.where(incl, S[H2:, :H2], 0.0)
        o['Ark'] = jnp.where(incl, S[H2:, H2:], 0.0)
        o['Nd'] = jnp.where(same16, N, 0.0)
        o['N32'] = jnp.where(off32, N, 0.0)
        o['N64'] = jnp.where(off64, N, 0.0)
    for o in ops:
        o['X'] = eye + o['Nd']
        o['P'] = mm(o['Nd'], o['Nd'])
    for it in range(3):
        if it < 2:
            for o in ops:
                px = mm(o['P'], jnp.concatenate([o['X'], o['P']], axis=1))
                o['X'] = o['X'] + px[:, :H2]
                o['P'] = px[:, H2:]
        else:
            for o in ops:
                o['X'] = o['X'] + mm(o['P'], o['X'])
    for key in ('N32', 'N64'):
        for o in ops:
            o['Z'] = mm(o[key], o['X'])
        for o in ops:
            o['X'] = o['X'] + mm(o['X'], o['Z'])
    for o in ops:
        wy = mm(jnp.concatenate([o['Aak'], o['Ark']], axis=0), o['Vs'])
        o['W0'] = wy[:H2]
        o['Yv'] = wy[H2:]
    hts = [ht_ref[p] for p in range(npairs)]
    for s in range(nc):
        cur = [ops[s * npairs + p] for p in range(npairs)]
        arh = [mm(o['AR'], hts[p], 'nt') for p, o in enumerate(cur)]
        us = [mm(o['X'], o['W0'] + arh[p][:H2]) for p, o in enumerate(cur)]
        ys = [arh[p][H2:] + (mm(o['Arb'], us[p]) + o['Yv']) for p, o in enumerate(cur)]
        upd = [mm(jnp.concatenate([us[p], o['Vs']], axis=0), jnp.concatenate([o['Bgs'], o['Kgs']], axis=0), 'tn')
               for p, o in enumerate(cur)]
        hts = [jnp.where(same64, hts[p] * o['g_end'] + upd[p], 0.0) for p, o in enumerate(cur)]
        for p in range(npairs):
            y_ref[0, pl.ds(offs[s], T), p * LANES:(p + 1) * LANES] = ys[p][:T] + ys[p][T:]
    for p in range(npairs):
        ht_ref[p] = hts[p]


def wkv7_chunked(r, k, v, kk, lw2, a2, ka):
    Bn, L, C = r.shape
    tb = WKV_TIME_BLOCK
    nt = L // tb
    tmap = lambda i, t: t + (i // Bn) * (nt - 1 - 2 * t)
    shared = pl.BlockSpec((1, tb, C), lambda i, t: (i % Bn, tmap(i, t), 0))
    per_dir = pl.BlockSpec((1, tb, C), lambda i, t: (i % Bn, tmap(i, t), i // Bn))
    return pl.pallas_call(
        functools.partial(_wkv_kernel, tb=tb, batch=Bn),
        grid=(2 * Bn, nt),
        in_specs=[shared, shared, shared, shared, per_dir, per_dir, pl.BlockSpec((1, C), lambda i, t: (0, 0))],
        out_specs=pl.BlockSpec((1, tb, C), lambda i, t: (i, tmap(i, t), 0)),
        out_shape=jax.ShapeDtypeStruct((2 * Bn, L, C), f32),
        scratch_shapes=[pltpu.VMEM((C // LANES, LANES, LANES), f32)],
        compiler_params=pltpu.CompilerParams(dimension_semantics=("parallel", "arbitrary")),
        name="wkv7_chunked",
    )(r, k, v, kk, lw2, a2, ka)


def _mm_exact_rhs(a, b_bf16):
    ah = a.astype(bf16)
    al = (a - ah.astype(f32)).astype(bf16)
    d = lambda x: lax.dot_general(x, b_bf16, _DN['nn'], preferred_element_type=f32)
    return d(ah) + d(al)


def _head_sum_matrix(scale):
    i = lax.broadcasted_iota(jnp.int32, (RW_D, RW_D), 0) // RW_N
    j = lax.broadcasted_iota(jnp.int32, (RW_D, RW_D), 1) // RW_N
    return jnp.where(i == j, scale, 0.0).astype(bf16)


def _rw_prep_kernel(r_ref, k_ref, v_ref, rp_ref, kp_ref, vp_ref, rn_ref, kn_ref, vn_ref, w_ref, a_ref,
                    mu_ref, w0_ref, w2_ref, a0_ref, a2_ref, kkw_ref,
                    ro_ref, ko_ref, vo_ref, kko_ref, lw_ref, ao_ref, *, rows, nt):
    t = pl.program_id(1)
    has_prev = (t > 0).astype(f32)
    has_next = (t < nt - 1).astype(f32)
    ridx = lax.broadcasted_iota(jnp.int32, (rows, RW_D), 0)

    def tshift(cur_ref, prev_ref, next_ref, j):
        u = cur_ref[0]
        prev_row = prev_ref[0, SUBLANES - 1:SUBLANES, :] * has_prev
        next_row = next_ref[0, 0:1, :] * has_next
        up = jnp.where(ridx == 0, prev_row, pltpu.roll(u, 1, axis=0))
        un = jnp.where(ridx == rows - 1, next_row, pltpu.roll(u, rows - 1, axis=0))
        return u + mu_ref[j, 0:1, :] * (up - u) + mu_ref[j, 1:2, :] * (un - u)

    r = tshift(r_ref, rp_ref, rn_ref, 0)
    k = tshift(k_ref, kp_ref, kn_ref, 1)
    v = tshift(v_ref, vp_ref, vn_ref, 2)
    ro_ref[0] = r
    ko_ref[0] = k
    vo_ref[0] = v
    kk = k * kkw_ref[...]
    ss = _mm_exact_rhs(kk * kk, _head_sum_matrix(1.0))
    kko_ref[0] = kk * lax.rsqrt(jnp.maximum(ss, 1e-24))
    wl = jnp.tanh(w_ref[0])
    al = a_ref[0]
    for d in range(2):
        wlog = w0_ref[d:d + 1, :] + _mm(wl[:, d * RW_W_LORA:(d + 1) * RW_W_LORA], w2_ref[d])
        wlog = -jax.nn.softplus(-wlog) - 0.5
        lw_ref[0, :, d * RW_D:(d + 1) * RW_D] = -jnp.exp(wlog)
        av = a0_ref[d:d + 1, :] + _mm(al[:, d * RW_A_LORA:(d + 1) * RW_A_LORA], a2_ref[d])
        ao_ref[0, :, d * RW_D:(d + 1) * RW_D] = jax.nn.sigmoid(av)


def rw_prep(proj, lora_lane0, mu, w0, w2, a0, a2, k_k):
    Bn, L, _ = proj.shape
    C = RW_D
    rows = RW_ROWS_PER_STEP
    nt = L // rows
    hb = rows // SUBLANES
    lb = lora_lane0 // LANES
    cur = lambda j: pl.BlockSpec((1, rows, C), lambda b, t: (b, t, j))
    prev = lambda j: pl.BlockSpec((1, SUBLANES, C), lambda b, t: (b, jnp.maximum(t * hb - 1, 0), j))
    nxt = lambda j: pl.BlockSpec((1, SUBLANES, C), lambda b, t: (b, jnp.minimum((t + 1) * hb, L // SUBLANES - 1), j))
    lora_w = pl.BlockSpec((1, rows, LANES), lambda b, t: (b, t, lb + 1))
    lora_a = pl.BlockSpec((1, rows, LANES), lambda b, t: (b, t, lb + 2))
    full = lambda shp: pl.BlockSpec(shp, lambda b, t: (0,) * len(shp))
    out_c = pl.BlockSpec((1, rows, C), lambda b, t: (b, t, 0))
    out_2c = pl.BlockSpec((1, rows, 2 * C), lambda b, t: (b, t, 0))
    sds = lambda c: jax.ShapeDtypeStruct((Bn, L, c), f32)
    return pl.pallas_call(
        functools.partial(_rw_prep_kernel, rows=rows, nt=nt),
        grid=(Bn, nt),
        in_specs=[cur(0), cur(1), cur(2), prev(0), prev(1), prev(2), nxt(0), nxt(1), nxt(2), lora_w, lora_a,
                  full((3, 2, C)), full((2, C)), full((2, RW_W_LORA, C)), full((2, C)), full((2, RW_A_LORA, C)),
                  full((1, C))],
        out_specs=[out_c, out_c, out_c, out_c, out_2c, out_2c],
        out_shape=[sds(C), sds(C), sds(C), sds(C), sds(2 * C), sds(2 * C)],
        compiler_params=pltpu.CompilerParams(dimension_semantics=("parallel", "parallel"),
                                             vmem_limit_bytes=VMEM_LIMIT_BYTES),
        name="rwkv_prep",
    )(proj, proj, proj, proj, proj, proj, proj, proj, proj, proj, proj,
      mu.astype(f32), w0.astype(f32), w2.astype(bf16), a0.astype(f32), a2.astype(bf16), k_k.astype(f32).reshape(1, C))


def _rw_post_kernel(yf_ref, yb_ref, r_ref, k_ref, v_ref, a_ref, g_ref, ka_ref, rk_ref, lnw_ref, lnb_ref, g2_ref, o_ref):
    C = RW_D
    y = yf_ref[0] + yb_ref[0]
    avg = _head_sum_matrix(1.0 / RW_N)
    mean = _mm_exact_rhs(y, avg)
    yc = y - mean
    var = _mm_exact_rhs(yc * yc, avg)
    yn = yc * lax.rsqrt(var + RW_GN_EPS) * lnw_ref[...] + lnb_ref[...]
    a = a_ref[0]
    k = k_ref[0]
    ka = ka_ref[...]
    ksum = k * (1.0 + (a[:, :C] - 1.0) * ka) + k * (1.0 + (a[:, C:] - 1.0) * ka)
    coef = _mm_exact_rhs(r_ref[0] * ksum * rk_ref[...], _head_sum_matrix(1.0))
    gate = _mm(jax.nn.sigmoid(g_ref[0]), g2_ref[...])
    o_ref[0] = (yn + coef * v_ref[0]) * gate


def rw_post(y2, r, k, v, a2, proj, g_lane0, k_a, r_k, ln_w, ln_b, g2):
    Bn, L, C = r.shape
    rows = RW_ROWS_PER_STEP
    nt = L // rows
    gb = g_lane0 // LANES
    blk = lambda c: pl.BlockSpec((1, rows, c), lambda b, t: (b, t, 0))
    full = lambda shp: pl.BlockSpec(shp, lambda b, t: (0,) * len(shp))
    row = lambda x: x.astype(f32).reshape(1, C)
    return pl.pallas_call(
        _rw_post_kernel,
        grid=(Bn, nt),
        in_specs=[blk(C), pl.BlockSpec((1, rows, C), lambda b, t: (b + Bn, t, 0)), blk(C), blk(C), blk(C), blk(2 * C),
                  pl.BlockSpec((1, rows, LANES), lambda b, t: (b, t, gb)),
                  full((1, C)), full((1, C)), full((1, C)), full((1, C)), full((RW_G_LORA, C))],
        out_specs=blk(C),
        out_shape=jax.ShapeDtypeStruct((Bn, L, C), f32),
        compiler_params=pltpu.CompilerParams(dimension_semantics=("parallel", "parallel"),
                                             vmem_limit_bytes=VMEM_LIMIT_BYTES),
        name="rwkv_post",
    )(y2, y2, r, k, v, a2, proj, row(k_a), row(r_k), row(ln_w), row(ln_b), g2.astype(bf16))


def _na_kernel(q_ref, k_ref, v_ref, bias_ref, o_ref, kb_ref, vb_ref, *, rows_per_step, n_rows, kr):
    W = GRID_W
    rb = pl.program_id(2)

    @pl.when(rb == 0)
    def _():
        kb_ref[...] = k_ref[0].astype(bf16)
        vb_ref[...] = v_ref[0].astype(bf16)

    lo_lane = lax.broadcasted_iota(jnp.int32, (W, LANES), 1) < NA_HD
    scale = NA_HD ** -0.5
    rows = []
    for j in range(rows_per_step):
        r = rb * rows_per_step + j
        start = jnp.clip(r - kr // 2, 0, n_rows - kr)
        rows.append((start, start - r + (NA_KR - 1)))
    s_list = []
    for j, (start, didx) in enumerate(rows):
        q = q_ref[0, j * W:(j + 1) * W, :] * scale
        qs = jnp.concatenate([jnp.where(lo_lane, q, 0.0), jnp.where(lo_lane, 0.0, q)], axis=0)
        kw = kb_ref[pl.ds(pl.multiple_of(start * W, W), kr * W), :]
        s_list.append(_mm(qs, kw, 'nt') + bias_ref[didx, 0])
    p_list = []
    for s in s_list:
        m = jnp.max(s, axis=-1, keepdims=True)
        p = jnp.exp(s - m)
        p_list.append((p, jnp.sum(p, axis=-1, keepdims=True)))
    for j, ((start, _), (p, l)) in enumerate(zip(rows, p_list)):
        vw = vb_ref[pl.ds(pl.multiple_of(start * W, W), kr * W), :]
        o = _mm(p, vw) / l
        o_ref[0, j * W:(j + 1) * W, :] = jnp.where(lo_lane, o[:W], o[W:])


def na_bias_table(rpb, n_rows):
    W = GRID_W
    kr = min(NA_KR, n_rows)
    cols = jnp.arange(W)
    col_start = jnp.clip(cols - NA_KC // 2, 0, W - NA_KC)
    in_band = (cols[None, :] >= col_start[:, None]) & (cols[None, :] < col_start[:, None] + NA_KC)
    dc = jnp.clip(cols[None, :] - cols[:, None], -(NA_KC - 1), NA_KC - 1) + (NA_KC - 1)
    win = jnp.stack([rpb.astype(f32)[:, d:d + kr, :] for d in range(8)], axis=1)
    onehot = (dc[:, :, None] == jnp.arange(2 * NA_KC - 1)[None, None, :]).astype(f32)
    tab = jnp.einsum('hdic,qkc->hdqik', win, onehot, precision=lax.Precision.HIGHEST)
    tab = jnp.where(in_band[None, None, :, None, :], tab, NEG_INF)
    H = rpb.shape[0]
    tab = tab.reshape(H // 2, 2, 8, W, kr * W).transpose(2, 0, 1, 3, 4)
    return tab.reshape(8, H // 2, 2 * W, kr * W)


def neighborhood_attention_pallas(proj, rpb, lane_block0):
    Bn, L, _ = proj.shape
    W = GRID_W
    n_rows = L // W
    kr = min(NA_KR, n_rows)
    hp = NA_H * NA_HD // LANES
    rps = NA_ROWS_PER_STEP
    bias = na_bias_table(rpb, n_rows)
    kern = functools.partial(_na_kernel, rows_per_step=rps, n_rows=n_rows, kr=kr)
    return pl.pallas_call(
        kern,
        grid=(Bn, hp, n_rows // rps),
        in_specs=[pl.BlockSpec((1, rps * W, LANES), lambda b, h, r: (b, r, lane_block0 + h)),
                  pl.BlockSpec((1, L, LANES), lambda b, h, r: (b, 0, lane_block0 + hp + h)),
                  pl.BlockSpec((1, L, LANES), lambda b, h, r: (b, 0, lane_block0 + 2 * hp + h)),
                  pl.BlockSpec((8, 1, 2 * W, kr * W), lambda b, h, r: (0, h, 0, 0))],
        out_specs=pl.BlockSpec((1, rps * W, LANES), lambda b, h, r: (b, r, h)),
        out_shape=jax.ShapeDtypeStruct((Bn, L, NA_H * NA_HD), f32),
        scratch_shapes=[pltpu.VMEM((L, LANES), bf16), pltpu.VMEM((L, LANES), bf16)],
        compiler_params=pltpu.CompilerParams(dimension_semantics=("parallel", "parallel", "arbitrary"),
                                             vmem_limit_bytes=VMEM_LIMIT_BYTES),
        name="neighborhood_attention",
    )(proj, proj, proj, bias)


def _moe_ffn_kernel(be_ref, x_ref, w1_ref, w3_ref, w2_ref, o_ref):
    x = x_ref[...].astype(bf16)
    h1 = jnp.dot(x, w1_ref[0], preferred_element_type=f32)
    h3 = jnp.dot(x, w3_ref[0], preferred_element_type=f32)
    g = h1 * jax.nn.sigmoid(h1) * h3
    o_ref[...] = jnp.dot(g.astype(bf16), w2_ref[0], preferred_element_type=f32)


def moe_expert_ffn(xs, block_e, w1, w3, w2):
    rows, Dm = xs.shape
    n_blocks = rows // MOE_BLOCK
    grid_spec = pltpu.PrefetchScalarGridSpec(
        num_scalar_prefetch=1,
        grid=(n_blocks,),
        in_specs=[pl.BlockSpec((MOE_BLOCK, Dm), lambda i, be: (i, 0)),
                  pl.BlockSpec((1, Dm, MOE_FF), lambda i, be: (be[i], 0, 0)),
                  pl.BlockSpec((1, Dm, MOE_FF), lambda i, be: (be[i], 0, 0)),
                  pl.BlockSpec((1, MOE_FF, Dm), lambda i, be: (be[i], 0, 0))],
        out_specs=pl.BlockSpec((MOE_BLOCK, Dm), lambda i, be: (i, 0)),
    )
    return pl.pallas_call(
        _moe_ffn_kernel,
        grid_spec=grid_spec,
        out_shape=jax.ShapeDtypeStruct((rows, Dm), f32),
        compiler_params=pltpu.CompilerParams(dimension_semantics=("arbitrary",)),
        name="moe_expert_ffn",
    )(block_e.astype(jnp.int32), xs, w1.astype(bf16), w3.astype(bf16), w2.astype(bf16))


def _rms(x):
    return x * lax.rsqrt(jnp.mean(x * x, axis=-1, keepdims=True) + NORM_EPS)


def _norm_proj_kernel(x_ref, g_ref, w_ref, o_ref):
    h = (_rms(x_ref[...]) * g_ref[...]).astype(bf16)
    o_ref[...] = jnp.dot(h, w_ref[...], preferred_element_type=f32)


def norm_proj(x2, g, w):
    N, D = x2.shape
    F = w.shape[1]
    tm = PROJ_ROWS_PER_STEP
    return pl.pallas_call(
        _norm_proj_kernel,
        grid=(N // tm,),
        in_specs=[pl.BlockSpec((tm, D), lambda i: (i, 0)), pl.BlockSpec((1, D), lambda i: (0, 0)),
                  pl.BlockSpec((D, F), lambda i: (0, 0))],
        out_specs=pl.BlockSpec((tm, F), lambda i: (i, 0)),
        out_shape=jax.ShapeDtypeStruct((N, F), f32),
        compiler_params=pltpu.CompilerParams(dimension_semantics=("parallel",), vmem_limit_bytes=VMEM_LIMIT_BYTES),
        name="norm_proj",
    )(x2, g.astype(f32).reshape(1, D), w.astype(bf16))


def _out_proj_kernel(x_ref, yh_ref, yr_ref, yn_ref, wh_ref, wr_ref, wn_ref, g_ref, wrt_ref, xo_ref, h_ref, lg_ref):
    mix = (jnp.dot(yh_ref[...].astype(bf16), wh_ref[...], preferred_element_type=f32)
           + jnp.dot(yr_ref[...].astype(bf16), wr_ref[...], preferred_element_type=f32)
           + jnp.dot(yn_ref[...].astype(bf16), wn_ref[...], preferred_element_type=f32))
    x = x_ref[...] + mix
    xo_ref[...] = x
    h = (_rms(x) * g_ref[...]).astype(bf16)
    h_ref[...] = h
    lg_ref[...] = jnp.dot(h, wrt_ref[...], preferred_element_type=f32)


def out_proj_norm_router(x2, y_hy, y_rw, y_na, w_out, g, w_router):
    N, D = x2.shape
    tm = PROJ_ROWS_PER_STEP
    d_hy, d_rw, d_na = y_hy.shape[1], y_rw.shape[1], y_na.shape[1]
    nr = w_router.shape[1]
    w_router = jnp.pad(w_router.astype(bf16), ((0, 0), (0, LANES - nr)))
    wb = w_out.astype(bf16)
    row = lambda c: pl.BlockSpec((tm, c), lambda i: (i, 0))
    full = lambda r, c: pl.BlockSpec((r, c), lambda i: (0, 0))
    return pl.pallas_call(
        _out_proj_kernel,
        grid=(N // tm,),
        in_specs=[row(D), row(d_hy), row(d_rw), row(d_na), full(d_hy, D), full(d_rw, D), full(d_na, D), full(1, D),
                  full(D, LANES)],
        out_specs=[row(D), row(D), row(LANES)],
        out_shape=[jax.ShapeDtypeStruct((N, D), f32), jax.ShapeDtypeStruct((N, D), bf16),
                   jax.ShapeDtypeStruct((N, LANES), f32)],
        compiler_params=pltpu.CompilerParams(dimension_semantics=("parallel",), vmem_limit_bytes=VMEM_LIMIT_BYTES),
        name="out_proj_norm_router",
    )(x2, y_hy, y_rw, y_na, wb[:d_hy], wb[d_hy:d_hy + d_rw], wb[d_hy + d_rw:], g.astype(f32).reshape(1, D), w_router)


def _final_norm_kernel(x_ref, g_ref, o_ref):
    x = x_ref[...]
    o_ref[...] = x * lax.rsqrt(jnp.mean(x * x, axis=-1, keepdims=True) + NORM_EPS) * g_ref[...]


def final_rmsnorm(x, g):
    Bn, L, D = x.shape
    tm = 1024
    out = pl.pallas_call(
        _final_norm_kernel,
        grid=(Bn * L // tm,),
        in_specs=[pl.BlockSpec((tm, D), lambda i: (i, 0)), pl.BlockSpec((1, D), lambda i: (0, 0))],
        out_specs=pl.BlockSpec((tm, D), lambda i: (i, 0)),
        out_shape=jax.ShapeDtypeStruct((Bn * L, D), x.dtype),
        name="final_rmsnorm",
    )(x.reshape(Bn * L, D), g.reshape(1, D))
    return out.reshape(Bn, L, D)


def hyena_positional_features(L):
    t = jnp.linspace(0.0, 1.0, L, dtype=f32)[:, None]
    w = (2.0 * math.pi / L) * jnp.arange(L, dtype=f32)[:, None]
    f = jnp.linspace(1e-4, HY_BANDS - 1, HY_BANDS, dtype=f32)[None, :]
    z = jnp.concatenate([t, jnp.cos(f * w), -jnp.sin(f * w)], axis=-1)
    return z, t


def hyena_filters(z, t, w1, b1, w2, b2, w3, b3, wout, freq):
    fr = freq.astype(f32)
    act = lambda u: jnp.sin(fr * u)
    h = act(z @ w1.astype(f32) + b1.astype(f32))
    h = act(h @ w2.astype(f32) + b2.astype(f32))
    h = act(h @ w3.astype(f32) + b3.astype(f32))
    h = (h @ wout.astype(f32)).reshape(-1, 2, HY_D)
    deltas = jnp.abs(jnp.linspace(HY_MIN_DECAY, HY_MAX_DECAY, HY_D, dtype=f32))
    h = h * jnp.exp(-t[:, :, None] * deltas)
    return h * lax.rsqrt(jnp.sum(h * h, axis=(0, 1), keepdims=True) + 1e-6)


def _split_bf16(x):
    hi = x.astype(bf16)
    return hi, (x - hi.astype(f32)).astype(bf16)


def _dot3(m_hi, m_lo, x_hi, x_lo):
    d = lambda a, b: lax.dot_general(a, b, _DN['nn'], preferred_element_type=f32)
    return d(m_hi, x_hi) + (d(m_hi, x_lo) + d(m_lo, x_hi))


def _cmatmul(mr, mi, xr, xi):
    xrh, xrl = _split_bf16(xr)
    rr = _dot3(mr[0], mr[1], xrh, xrl)
    ir = _dot3(mi[0], mi[1], xrh, xrl)
    if xi is None:
        return rr, ir
    xih, xil = _split_bf16(xi)
    ii = _dot3(mi[0], mi[1], xih, xil)
    ri = _dot3(mr[0], mr[1], xih, xil)
    return rr - ii, ri + ir


def _row_dft_kernel(mrh_ref, mrl_ref, mih_ref, mil_ref, *refs, real_input):
    if real_input:
        ur_ref, or_ref, oi_ref = refs
        ui = None
    else:
        ur_ref, ui_ref, or_ref, oi_ref = refs
        ui = ui_ref[0]
    o_r, o_i = _cmatmul((mrh_ref[...], mrl_ref[...]), (mih_ref[...], mil_ref[...]), ur_ref[0], ui)
    or_ref[0] = o_r
    oi_ref[0] = o_i


def row_dft(tabs, u, packed):
    rows, r_in, W = u.shape
    P = rows // 2 if packed else rows
    r_out = tabs[0].shape[0]
    wb = min(FFT_LANE_BLOCK, W)
    tab_spec = pl.BlockSpec((r_out, r_in), lambda p, j: (0, 0))
    re_spec = pl.BlockSpec((1, r_in, wb), lambda p, j: (p, 0, j))
    im_spec = pl.BlockSpec((1, r_in, wb), lambda p, j: (p + P, 0, j))
    out_spec = pl.BlockSpec((1, r_out, wb), lambda p, j: (p, 0, j))
    ins = [u, u] if packed else [u]
    return pl.pallas_call(
        functools.partial(_row_dft_kernel, real_input=not packed),
        grid=(P, W // wb),
        in_specs=[tab_spec] * 4 + ([re_spec, im_spec] if packed else [re_spec]),
        out_specs=[out_spec, out_spec],
        out_shape=[jax.ShapeDtypeStruct((P, r_out, W), f32)] * 2,
        compiler_params=pltpu.CompilerParams(dimension_semantics=("parallel", "parallel"),
                                             vmem_limit_bytes=VMEM_LIMIT_BYTES),
        name="hyena_row_dft",
    )(*tabs, *ins)


def _row_idft_gate_kernel(mrh_ref, mrl_ref, mih_ref, mil_ref, dr_ref, di_ref, z0_ref, z1_ref, x0_ref, x1_ref,
                          skip_ref, o0_ref, o1_ref):
    y_r, y_i = _cmatmul((mrh_ref[...], mrl_ref[...]), (mih_ref[...], mil_ref[...]), dr_ref[0], di_ref[0])
    skip = skip_ref[...]
    o0_ref[0] = x0_ref[0] * (y_r + z0_ref[0] * skip)
    o1_ref[0] = x1_ref[0] * (y_i + z1_ref[0] * skip)


def row_idft_gate(tabs, dr, di, z, x0, skip_row):
    P, r_in, W = dr.shape
    r_out = tabs[0].shape[0]
    wb = min(FFT_LANE_BLOCK, W)
    tab_spec = pl.BlockSpec((r_out, r_in), lambda p, j: (0, 0))
    d_spec = pl.BlockSpec((1, r_in, wb), lambda p, j: (p, 0, j))
    lo = pl.BlockSpec((1, r_out, wb), lambda p, j: (p, 0, j))
    hi = pl.BlockSpec((1, r_out, wb), lambda p, j: (p + P, 0, j))
    o0, o1 = pl.pallas_call(
        _row_idft_gate_kernel,
        grid=(P, W // wb),
        in_specs=[tab_spec] * 4 + [d_spec, d_spec, lo, hi, lo, hi, pl.BlockSpec((1, wb), lambda p, j: (0, j))],
        out_specs=[lo, lo],
        out_shape=[jax.ShapeDtypeStruct((P, r_out, W), f32)] * 2,
        compiler_params=pltpu.CompilerParams(dimension_semantics=("parallel", "parallel"),
                                             vmem_limit_bytes=VMEM_LIMIT_BYTES),
        name="hyena_row_idft_gate",
    )(*tabs, dr, di, z, z, x0, x0, skip_row)
    return jnp.concatenate([o0, o1], axis=0)


def _col_dft_kernel(grh_ref, grl_ref, gih_ref, gil_ref, ar_ref, ai_ref, br_ref, bi_ref, *, k1_per_step):
    for j in range(k1_per_step):
        b_r, b_i = _cmatmul((grh_ref[j], grl_ref[j]), (gih_ref[j], gil_ref[j]), ar_ref[0, j], ai_ref[0, j])
        br_ref[0, j] = b_r
        bi_ref[0, j] = b_i


def _col_conv_kernel(grh_ref, grl_ref, gih_ref, gil_ref, trh_ref, trl_ref, tih_ref, til_ref,
                     ar_ref, ai_ref, kr_ref, ki_ref, dr_ref, di_ref, *, k1_per_step):
    for j in range(k1_per_step):
        b_r, b_i = _cmatmul((grh_ref[j], grl_ref[j]), (gih_ref[j], gil_ref[j]), ar_ref[0, j], ai_ref[0, j])
        k_r = kr_ref[0, j]
        k_i = ki_ref[0, j]
        c_r = b_r * k_r - b_i * k_i
        c_i = b_r * k_i + b_i * k_r
        d_r, d_i = _cmatmul((trh_ref[j], trl_ref[j]), (tih_ref[j], til_ref[j]), c_r, c_i)
        dr_ref[0, j] = d_r
        di_ref[0, j] = d_i


def col_stage(g_tabs, gt_tabs, ar, ai, kr=None, ki=None):
    P, n1, n2, C = ar.shape
    kb = min(FFT_K1_PER_STEP, n1)
    g_spec = pl.BlockSpec((kb, n2, n2), lambda p, j: (j, 0, 0))
    a_spec = pl.BlockSpec((1, kb, n2, C), lambda p, j: (p, j, 0, 0))
    k_spec = pl.BlockSpec((1, kb, n2, C), lambda p, j: (0, j, 0, 0))
    common = dict(
        grid=(P, n1 // kb),
        out_specs=[a_spec, a_spec],
        out_shape=[jax.ShapeDtypeStruct((P, n1, n2, C), f32)] * 2,
        compiler_params=pltpu.CompilerParams(dimension_semantics=("parallel", "parallel"),
                                             vmem_limit_bytes=VMEM_LIMIT_BYTES),
    )
    if kr is None:
        return pl.pallas_call(functools.partial(_col_dft_kernel, k1_per_step=kb),
                              in_specs=[g_spec] * 4 + [a_spec] * 2, name="hyena_col_dft", **common)(*g_tabs, ar, ai)
    return pl.pallas_call(functools.partial(_col_conv_kernel, k1_per_step=kb),
                          in_specs=[g_spec] * 8 + [a_spec] * 2 + [k_spec] * 2, name="hyena_col_conv", **common)(
        *g_tabs, *gt_tabs, ar, ai, kr, ki)


def _bf16_tables(m):
    out = []
    for part in (np.real(m), np.imag(m)):
        x = jnp.asarray(part, f32)
        hi = x.astype(bf16)
        out += [hi, (x - hi.astype(f32)).astype(bf16)]
    return out


def hyena_dft_tables(L):
    n2 = FFT_N2
    n1 = 2 * L // n2
    N = n1 * n2
    a = np.arange(n1)
    fa = np.exp(-2j * np.pi * np.outer(a, a) / n1)
    k1 = np.arange(n1)[:, None, None]
    k2 = np.arange(n2)[None, :, None]
    nn = np.arange(n2)[None, None, :]
    g = np.exp(-2j * np.pi * nn * (k1 + n1 * k2) / N)
    gt = np.conj(np.transpose(g, (0, 2, 1)))
    fc = np.conj(fa).T / N
    return dict(fa=_bf16_tables(fa), fa_half=_bf16_tables(fa[:, :n1 // 2]), g=_bf16_tables(g), gt=_bf16_tables(gt),
                fc_half=_bf16_tables(fc[:n1 // 2]))


def hyena_long_conv_gate(tabs, z, x0, k2, skip):
    Bn, L, C = z.shape
    n2 = FFT_N2
    n1 = 2 * L // n2
    P = Bn // 2
    W = n2 * C
    ar, ai = row_dft(tabs['fa'], k2.reshape(1, n1, W), packed=False)
    kr, ki = col_stage(tabs['g'], tabs['gt'], ar.reshape(1, n1, n2, C), ai.reshape(1, n1, n2, C))
    zv = z.reshape(Bn, n1 // 2, W)
    ar, ai = row_dft(tabs['fa_half'], zv, packed=True)
    dr, di = col_stage(tabs['g'], tabs['gt'], ar.reshape(P, n1, n2, C), ai.reshape(P, n1, n2, C), kr, ki)
    out = row_idft_gate(tabs['fc_half'], dr.reshape(P, n1, W), di.reshape(P, n1, W), zv, x0.reshape(Bn, n1 // 2, W),
                        jnp.tile(skip.astype(f32), n2).reshape(1, W))
    return out.reshape(Bn, L, C)


def hyena_mixer(tabs, u, conv_w, conv_b, filt, skip):
    up = jnp.pad(u, ((0, 0), (1, 1), (0, 0)))
    uc = up[:, :-2] * conv_w[0] + up[:, 1:-1] * conv_w[1] + up[:, 2:] * conv_w[2] + conv_b
    x0, x1, v = jnp.split(uc, 3, axis=-1)
    z = (v * x1).astype(f32)
    k2 = jnp.concatenate([filt[:1, 0] + filt[:1, 1], filt[1:, 0],
                          jnp.zeros((1, HY_D), f32), filt[1:, 1][::-1]], axis=0)
    return hyena_long_conv_gate(tabs, z, x0.astype(f32), k2, skip)


def rwkv7_mixer(proj, lora_lane0, mu, w0, w2, a0, a2, k_k, k_a, r_k, g2, ln_w, ln_b):
    r, k, v, kk, lw2, a_both = rw_prep(proj, lora_lane0, mu, w0, w2, a0, a2, k_k)
    y2 = wkv7_chunked(r, k, v, kk, lw2, a_both, k_a.astype(f32).reshape(1, RW_D))
    return rw_post(y2, r, k, v, a_both, proj, lora_lane0, k_a, r_k, ln_w, ln_b, g2)


def hier_moe(h, logits, bg, be, w1, w3, w2):
    N, Dm = h.shape
    assert MOE_TOPK == 2
    g_logits = logits[:, :MOE_GROUPS] + bg.astype(f32)
    g_sel = jnp.argmax(g_logits, axis=-1)
    g_prob = jnp.take_along_axis(jax.nn.softmax(g_logits, axis=-1), g_sel[:, None], axis=-1)
    e_logits = (logits[:, MOE_GROUPS:MOE_GROUPS + MOE_EXPERTS] + be.astype(f32)).reshape(N, MOE_GROUPS, MOE_PER_GROUP)
    e_logits = jnp.take_along_axis(e_logits, g_sel[:, None, None], axis=1)[:, 0]
    top_val, top_idx = lax.top_k(e_logits, MOE_TOPK)
    gate = g_prob * jax.nn.softmax(top_val, axis=-1)
    expert = g_sel[:, None] * MOE_PER_GROUP + top_idx
    M = N * MOE_TOPK
    flat_e = expert.reshape(M)
    order = jnp.argsort(flat_e)
    e_sorted = flat_e[order]
    counts = jnp.bincount(flat_e, length=MOE_EXPERTS)
    padded = (counts + MOE_BLOCK - 1) // MOE_BLOCK * MOE_BLOCK
    pad_end = jnp.cumsum(padded)
    first = (jnp.cumsum(counts) - counts)[e_sorted]
    slot = (pad_end - padded)[e_sorted] + jnp.arange(M) - first
    n_blocks = -(-M // MOE_BLOCK) + MOE_EXPERTS
    tok_src = jnp.zeros((n_blocks * MOE_BLOCK,), jnp.int32).at[slot].set(
        (order // MOE_TOPK).astype(jnp.int32), indices_are_sorted=True, unique_indices=True)
    block_start = jnp.arange(n_blocks, dtype=pad_end.dtype) * MOE_BLOCK
    block_e = jnp.minimum(jnp.sum(pad_end[None, :] <= block_start[:, None], axis=1), MOE_EXPERTS - 1)
    y = moe_expert_ffn(h[tok_src], block_e, w1, w3, w2)
    pos = jnp.zeros((M,), jnp.int32).at[order].set(slot.astype(jnp.int32), unique_indices=True).reshape(N, MOE_TOPK)
    return gate[:, 0:1] * y[pos[:, 0]] + gate[:, 1:2] * y[pos[:, 1]]


def kernel(x, norm1_g, w_in, hy_conv_w, hy_conv_b, hy_w1, hy_b1, hy_w2, hy_b2, hy_w3, hy_b3, hy_wout, hy_freq, hy_skip, rw_mu, rw_w0, rw_w2, rw_a0, rw_a2, rw_kk, rw_ka, rw_rk, rw_g2, rw_ln_w, rw_ln_b, na_rpb, w_out, norm2_g, moe_wg, moe_bg, moe_we, moe_be, moe_w1, moe_w3, moe_w2, norm_f_g):
    Bn, L, _ = x.shape
    z_pos, t_pos = hyena_positional_features(L)
    dft_tabs = hyena_dft_tables(L)
    splits = np.cumsum(IN_SIZES)[:-1].tolist()
    hy_end, rkv_end = splits[0], splits[1]
    hy_lane0 = rkv_end - hy_end
    lora_lane0, na_lane0 = splits[1], splits[4]
    assert RW_D % LANES == 0 and lora_lane0 % LANES == 0 and na_lane0 % LANES == 0
    N = Bn * L
    x2 = x.reshape(N, D_MODEL)
    for l in range(DEPTH):
        w_in_l = jnp.concatenate([w_in[l][:, hy_end:rkv_end], w_in[l][:, :hy_end], w_in[l][:, rkv_end:]], axis=1)
        proj = norm_proj(x2, norm1_g[l], w_in_l).reshape(Bn, L, IN_D)
        hy_u = proj[..., hy_lane0:hy_lane0 + hy_end]
        filt = hyena_filters(z_pos, t_pos, hy_w1[l], hy_b1[l], hy_w2[l], hy_b2[l],
                             hy_w3[l], hy_b3[l], hy_wout[l], hy_freq[l])
        y_hy = hyena_mixer(dft_tabs, hy_u, hy_conv_w[l], hy_conv_b[l], filt, hy_skip[l])
        y_rw = rwkv7_mixer(proj, lora_lane0, rw_mu[l], rw_w0[l], rw_w2[l], rw_a0[l],
                           rw_a2[l], rw_kk[l], rw_ka[l], rw_rk[l].reshape(RW_D), rw_g2[l], rw_ln_w[l], rw_ln_b[l])
        y_na = neighborhood_attention_pallas(proj, na_rpb[l], na_lane0 // LANES)
        x2, h2, logits = out_proj_norm_router(
            x2, y_hy.reshape(N, HY_D), y_rw.reshape(N, RW_D), y_na.reshape(N, NA_D), w_out[l], norm2_g[l],
            jnp.concatenate([moe_wg[l], moe_we[l]], axis=1))
        x2 = x2 + hier_moe(h2, logits, moe_bg[l], moe_be[l], moe_w1[l], moe_w3[l], moe_w2[l])
    return final_rmsnorm(x2.reshape(Bn, L, D_MODEL), norm_f_g)
```

```python
import functools
import math

import jax
import jax.numpy as jnp
import numpy as np
from jax import lax
from jax.experimental import pallas as pl
from jax.experimental.pallas import tpu as pltpu

f32 = jnp.float32
bf16 = jnp.bfloat16

D_MODEL = 1024
DEPTH = 2
GRID_W = 64
NORM_EPS = 1e-6
NEG_INF = -1e30

HY_D = D_MODEL // 4
HY_EMB = 33
HY_BANDS = (HY_EMB - 1) // 2
HY_FFN = 64
HY_MIN_DECAY = math.log(1e-2) / 1.5
HY_MAX_DECAY = math.log(1e-2) / 0.3

RW_N = 64
RW_D = D_MODEL // 2
RW_H = RW_D // RW_N
RW_W_LORA = 64
RW_A_LORA = 64
RW_G_LORA = 128
RW_GN_EPS = 64e-5

NA_HD = 64
NA_D = D_MODEL // 4
NA_H = NA_D // NA_HD
NA_KR = 8
NA_KC = 16

MIX_D = HY_D + RW_D + NA_D
IN_SIZES = (3 * HY_D, 3 * RW_D, RW_G_LORA, 2 * RW_W_LORA, 2 * RW_A_LORA, 3 * NA_D)
IN_D = sum(IN_SIZES)

MOE_GROUPS = 4
MOE_PER_GROUP = 8
MOE_EXPERTS = MOE_GROUPS * MOE_PER_GROUP
MOE_TOPK = 2
MOE_FF = 512
MOE_BLOCK = 256

LANES = 128
WKV_CHUNK = 64
WKV_TIME_BLOCK = 256
NA_ROWS_PER_STEP = 4
RW_ROWS_PER_STEP = 512
SUBLANES = 8
PROJ_ROWS_PER_STEP = 512
FFT_N2 = 128
FFT_LANE_BLOCK = 4096
FFT_K1_PER_STEP = 4
VMEM_LIMIT_BYTES = 48 * 1024 * 1024

_DN = {'nn': (((1,), (0,)), ((), ())), 'nt': (((1,), (1,)), ((), ())), 'tn': (((0,), (0,)), ((), ()))}


def _mm(a, b, dims='nn'):
    return lax.dot_general(a.astype(bf16), b.astype(bf16), _DN[dims], preferred_element_type=f32)


def _wkv_kernel(r_ref, k_ref, v_ref, kk_ref, lw_ref, a_ref, ka_ref, y_ref, ht_ref, *, tb, batch):
    T = WKV_CHUNK
    H2 = 2 * T
    nc = tb // T
    npairs = r_ref.shape[-1] // LANES
    mm = _mm

    @pl.when(pl.program_id(1) == 0)
    def _():
        ht_ref[...] = jnp.zeros_like(ht_ref)

    d = pl.program_id(0) // batch
    sign = 1 - 2 * d
    row = lax.broadcasted_iota(jnp.int32, (H2, H2), 0)
    col = lax.broadcasted_iota(jnp.int32, (H2, H2), 1)
    same64 = (row // T) == (col // T)
    tdiff = (row % T - col % T) * sign
    strict = same64 & (tdiff > 0)
    incl = same64 & (tdiff >= 0)
    same16 = (row // 16) == (col // 16)
    same32 = (row // 32) == (col // 32)
    off32 = same32 & jnp.logical_not(same16)
    off64 = jnp.logical_not(same32)
    eye = jnp.where(row == col, 1.0, 0.0).astype(f32)
    trow = lax.broadcasted_iota(jnp.int32, (T, T), 0)
    tcol = lax.broadcasted_iota(jnp.int32, (T, T), 1)
    tri = jnp.where((trow - tcol) * sign >= 0, 1.0, 0.0).astype(bf16)
    lo_lane = lax.broadcasted_iota(jnp.int32, (T, LANES), 1) < T
    is_bwd = d == 1

    def stack(x):
        return jnp.concatenate([jnp.where(lo_lane, x, 0.0), jnp.where(lo_lane, 0.0, x)], axis=0)

    inst = [(s, p) for s in range(nc) for p in range(npairs)]
    offs = [pl.multiple_of((s + d * (nc - 1 - 2 * s)) * T, T) for s in range(nc)]

    def load(ref, s, p):
        return ref[0, pl.ds(offs[s], T), p * LANES:(p + 1) * LANES]

    cs_l = []
    for s, p in inst:
        lw = load(lw_ref, s, p)
        l1 = lw.astype(bf16)
        r1 = lw - l1.astype(f32)
        l2 = r1.astype(bf16)
        l3 = (r1 - l2.astype(f32)).astype(bf16)
        dd = lambda x: lax.dot_general(tri, x, _DN['nn'], preferred_element_type=f32)
        cs_l.append((dd(l1) + (dd(l2) + dd(l3)), lw))
    ops = []
    for (s, p), (cs, lw) in zip(inst, cs_l):
        r = load(r_ref, s, p)
        k = load(k_ref, s, p)
        v = load(v_ref, s, p)
        kk = load(kk_ref, s, p)
        a = load(a_ref, s, p)
        ka = ka_ref[:, p * LANES:(p + 1) * LANES]
        kd = k * (1.0 + (a - 1.0) * ka)
        b = kk * a
        cs_end = jnp.where(is_bwd, cs[0:1, :], cs[T - 1:T, :])
        em = jnp.exp(-cs)
        e_end = jnp.exp(cs_end - cs)
        ops.append(dict(
            As=stack(-kk * jnp.exp(cs - lw)), Rs=stack(r * jnp.exp(cs)), Bs=stack(b * em), Ks=stack(kd * em),
            Vs=stack(v), Bgs=stack(b * e_end), Kgs=stack(kd * e_end), g_end=jnp.exp(cs_end)))
    for o in ops:
        o['AR'] = jnp.concatenate([o['As'], o['Rs']], axis=0)
        S = mm(o['AR'], jnp.concatenate([o['Bs'], o['Ks']], axis=0), 'nt')
        N = jnp.where(strict, S[:H2, :H2], 0.0)
        o['Aak'] = jnp.where(strict, S[:H2, H2:], 0.0)
        o['Arb'] = jnp.where(incl, S[H2:, :H2], 0.0)
        o['Ark'] = jnp.where(incl, S[H2:, H2:], 0.0)
        o['Nd'] = jnp.where(same16, N, 0.0)
        o['N32'] = jnp.where(off32, N, 0.0)
        o['N64'] = jnp.where(off64, N, 0.0)
    for o in ops:
        o['X'] = eye + o['Nd']
        o['P'] = mm(o['Nd'], o['Nd'])
    for it in range(3):
        if it < 2:
            for o in ops:
                px = mm(o['P'], jnp.concatenate([o['X'], o['P']], axis=1))
                o['X'] = o['X'] + px[:, :H2]
                o['P'] = px[:, H2:]
        else:
            for o in ops:
                o['X'] = o['X'] + mm(o['P'], o['X'])
    for key in ('N32', 'N64'):
        for o in ops:
            o['Z'] = mm(o[key], o['X'])
        for o in ops:
            o['X'] = o['X'] + mm(o['X'], o['Z'])
    for o in ops:
        wy = mm(jnp.concatenate([o['Aak'], o['Ark']], axis=0), o['Vs'])
        o['W0'] = wy[:H2]
        o['Yv'] = wy[H2:]
    hts = [ht_ref[p] for p in range(npairs)]
    for s in range(nc):
        cur = [ops[s * npairs + p] for p in range(npairs)]
        arh = [mm(o['AR'], hts[p], 'nt') for p, o in enumerate(cur)]
        us = [mm(o['X'], o['W0'] + arh[p][:H2]) for p, o in enumerate(cur)]
        ys = [arh[p][H2:] + (mm(o['Arb'], us[p]) + o['Yv']) for p, o in enumerate(cur)]
        upd = [mm(jnp.concatenate([us[p], o['Vs']], axis=0), jnp.concatenate([o['Bgs'], o['Kgs']], axis=0), 'tn')
               for p, o in enumerate(cur)]
        hts = [jnp.where(same64, hts[p] * o['g_end'] + upd[p], 0.0) for p, o in enumerate(cur)]
        for p in range(npairs):
            y_ref[0, pl.ds(offs[s], T), p * LANES:(p + 1) * LANES] = ys[p][:T] + ys[p][T:]
    for p in range(npairs):
        ht_ref[p] = hts[p]


def wkv7_chunked(r, k, v, kk, lw2, a2, ka):
    Bn, L, C = r.shape
    tb = WKV_TIME_BLOCK
    nt = L // tb
    tmap = lambda i, t: t + (i // Bn) * (nt - 1 - 2 * t)
    shared = pl.BlockSpec((1, tb, C), lambda i, t: (i % Bn, tmap(i, t), 0))
    per_dir = pl.BlockSpec((1, tb, C), lambda i, t: (i % Bn, tmap(i, t), i // Bn))
    return pl.pallas_call(
        functools.partial(_wkv_kernel, tb=tb, batch=Bn),
        grid=(2 * Bn, nt),
        in_specs=[shared, shared, shared, shared, per_dir, per_dir, pl.BlockSpec((1, C), lambda i, t: (0, 0))],
        out_specs=pl.BlockSpec((1, tb, C), lambda i, t: (i, tmap(i, t), 0)),
        out_shape=jax.ShapeDtypeStruct((2 * Bn, L, C), f32),
        scratch_shapes=[pltpu.VMEM((C // LANES, LANES, LANES), f32)],
        compiler_params=pltpu.CompilerParams(dimension_semantics=("parallel", "arbitrary")),
        name="wkv7_chunked",
    )(r, k, v, kk, lw2, a2, ka)


def _mm_exact_rhs(a, b_bf16):
    ah = a.astype(bf16)
    al = (a - ah.astype(f32)).astype(bf16)
    d = lambda x: lax.dot_general(x, b_bf16, _DN['nn'], preferred_element_type=f32)
    return d(ah) + d(al)


def _head_sum_matrix(scale):
    i = lax.broadcasted_iota(jnp.int32, (RW_D, RW_D), 0) // RW_N
    j = lax.broadcasted_iota(jnp.int32, (RW_D, RW_D), 1) // RW_N
    return jnp.where(i == j, scale, 0.0).astype(bf16)


def _rw_prep_kernel(r_ref, k_ref, v_ref, rp_ref, kp_ref, vp_ref, rn_ref, kn_ref, vn_ref, w_ref, a_ref,
                    mu_ref, w0_ref, w2_ref, a0_ref, a2_ref, kkw_ref,
                    ro_ref, ko_ref, vo_ref, kko_ref, lw_ref, ao_ref, *, rows, nt):
    t = pl.program_id(1)
    has_prev = (t > 0).astype(f32)
    has_next = (t < nt - 1).astype(f32)
    ridx = lax.broadcasted_iota(jnp.int32, (rows, RW_D), 0)

    def tshift(cur_ref, prev_ref, next_ref, j):
        u = cur_ref[0]
        prev_row = prev_ref[0, SUBLANES - 1:SUBLANES, :] * has_prev
        next_row = next_ref[0, 0:1, :] * has_next
        up = jnp.where(ridx == 0, prev_row, pltpu.roll(u, 1, axis=0))
        un = jnp.where(ridx == rows - 1, next_row, pltpu.roll(u, rows - 1, axis=0))
        return u + mu_ref[j, 0:1, :] * (up - u) + mu_ref[j, 1:2, :] * (un - u)

    r = tshift(r_ref, rp_ref, rn_ref, 0)
    k = tshift(k_ref, kp_ref, kn_ref, 1)
    v = tshift(v_ref, vp_ref, vn_ref, 2)
    ro_ref[0] = r
    ko_ref[0] = k
    vo_ref[0] = v
    kk = k * kkw_ref[...]
    ss = _mm_exact_rhs(kk * kk, _head_sum_matrix(1.0))
    kko_ref[0] = kk * lax.rsqrt(jnp.maximum(ss, 1e-24))
    wl = jnp.tanh(w_ref[0])
    al = a_ref[0]
    for d in range(2):
        wlog = w0_ref[d:d + 1, :] + _mm(wl[:, d * RW_W_LORA:(d + 1) * RW_W_LORA], w2_ref[d])
        wlog = -jax.nn.softplus(-wlog) - 0.5
        lw_ref[0, :, d * RW_D:(d + 1) * RW_D] = -jnp.exp(wlog)
        av = a0_ref[d:d + 1, :] + _mm(al[:, d * RW_A_LORA:(d + 1) * RW_A_LORA], a2_ref[d])
        ao_ref[0, :, d * RW_D:(d + 1) * RW_D] = jax.nn.sigmoid(av)


def rw_prep(proj, lora_lane0, mu, w0, w2, a0, a2, k_k):
    Bn, L, _ = proj.shape
    C = RW_D
    rows = RW_ROWS_PER_STEP
    nt = L // rows
    hb = rows // SUBLANES
    lb = lora_lane0 // LANES
    cur = lambda j: pl.BlockSpec((1, rows, C), lambda b, t: (b, t, j))
    prev = lambda j: pl.BlockSpec((1, SUBLANES, C), lambda b, t: (b, jnp.maximum(t * hb - 1, 0), j))
    nxt = lambda j: pl.BlockSpec((1, SUBLANES, C), lambda b, t: (b, jnp.minimum((t + 1) * hb, L // SUBLANES - 1), j))
    lora_w = pl.BlockSpec((1, rows, LANES), lambda b, t: (b, t, lb + 1))
    lora_a = pl.BlockSpec((1, rows, LANES), lambda b, t: (b, t, lb + 2))
    full = lambda shp: pl.BlockSpec(shp, lambda b, t: (0,) * len(shp))
    out_c = pl.BlockSpec((1, rows, C), lambda b, t: (b, t, 0))
    out_2c = pl.BlockSpec((1, rows, 2 * C), lambda b, t: (b, t, 0))
    sds = lambda c: jax.ShapeDtypeStruct((Bn, L, c), f32)
    return pl.pallas_call(
        functools.partial(_rw_prep_kernel, rows=rows, nt=nt),
        grid=(Bn, nt),
        in_specs=[cur(0), cur(1), cur(2), prev(0), prev(1), prev(2), nxt(0), nxt(1), nxt(2), lora_w, lora_a,
                  full((3, 2, C)), full((2, C)), full((2, RW_W_LORA, C)), full((2, C)), full((2, RW_A_LORA, C)),
                  full((1, C))],
        out_specs=[out_c, out_c, out_c, out_c, out_2c, out_2c],
        out_shape=[sds(C), sds(C), sds(C), sds(C), sds(2 * C), sds(2 * C)],
        compiler_params=pltpu.CompilerParams(dimension_semantics=("parallel", "parallel"),
                                             vmem_limit_bytes=VMEM_LIMIT_BYTES),
        name="rwkv_prep",
    )(proj, proj, proj, proj, proj, proj, proj, proj, proj, proj, proj,
      mu.astype(f32), w0.astype(f32), w2.astype(bf16), a0.astype(f32), a2.astype(bf16), k_k.astype(f32).reshape(1, C))


def _rw_post_kernel(yf_ref, yb_ref, r_ref, k_ref, v_ref, a_ref, g_ref, ka_ref, rk_ref, lnw_ref, lnb_ref, g2_ref, o_ref):
    C = RW_D
    y = yf_ref[0] + yb_ref[0]
    avg = _head_sum_matrix(1.0 / RW_N)
    mean = _mm_exact_rhs(y, avg)
    yc = y - mean
    var = _mm_exact_rhs(yc * yc, avg)
    yn = yc * lax.rsqrt(var + RW_GN_EPS) * lnw_ref[...] + lnb_ref[...]
    a = a_ref[0]
    k = k_ref[0]
    ka = ka_ref[...]
    ksum = k * (1.0 + (a[:, :C] - 1.0) * ka) + k * (1.0 + (a[:, C:] - 1.0) * ka)
    coef = _mm_exact_rhs(r_ref[0] * ksum * rk_ref[...], _head_sum_matrix(1.0))
    gate = _mm(jax.nn.sigmoid(g_ref[0]), g2_ref[...])
    o_ref[0] = (yn + coef * v_ref[0]) * gate


def rw_post(y2, r, k, v, a2, proj, g_lane0, k_a, r_k, ln_w, ln_b, g2):
    Bn, L, C = r.shape
    rows = RW_ROWS_PER_STEP
    nt = L // rows
    gb = g_lane0 // LANES
    blk = lambda c: pl.BlockSpec((1, rows, c), lambda b, t: (b, t, 0))
    full = lambda shp: pl.BlockSpec(shp, lambda b, t: (0,) * len(shp))
    row = lambda x: x.astype(f32).reshape(1, C)
    return pl.pallas_call(
        _rw_post_kernel,
        grid=(Bn, nt),
        in_specs=[blk(C), pl.BlockSpec((1, rows, C), lambda b, t: (b + Bn, t, 0)), blk(C), blk(C), blk(C), blk(2 * C),
                  pl.BlockSpec((1, rows, LANES), lambda b, t: (b, t, gb)),
                  full((1, C)), full((1, C)), full((1, C)), full((1, C)), full((RW_G_LORA, C))],
        out_specs=blk(C),
        out_shape=jax.ShapeDtypeStruct((Bn, L, C), f32),
        compiler_params=pltpu.CompilerParams(dimension_semantics=("parallel", "parallel"),
                                             vmem_limit_bytes=VMEM_LIMIT_BYTES),
        name="rwkv_post",
    )(y2, y2, r, k, v, a2, proj, row(k_a), row(r_k), row(ln_w), row(ln_b), g2.astype(bf16))


def _na_kernel(q_ref, k_ref, v_ref, bias_ref, o_ref, kb_ref, vb_ref, *, rows_per_step, n_rows, kr):
    W = GRID_W
    rb = pl.program_id(2)

    @pl.when(rb == 0)
    def _():
        kb_ref[...] = k_ref[0].astype(bf16)
        vb_ref[...] = v_ref[0].astype(bf16)

    lo_lane = lax.broadcasted_iota(jnp.int32, (W, LANES), 1) < NA_HD
    scale = NA_HD ** -0.5
    rows = []
    for j in range(rows_per_step):
        r = rb * rows_per_step + j
        start = jnp.clip(r - kr // 2, 0, n_rows - kr)
        rows.append((start, start - r + (NA_KR - 1)))
    s_list = []
    for j, (start, didx) in enumerate(rows):
        q = q_ref[0, j * W:(j + 1) * W, :] * scale
        qs = jnp.concatenate([jnp.where(lo_lane, q, 0.0), jnp.where(lo_lane, 0.0, q)], axis=0)
        kw = kb_ref[pl.ds(pl.multiple_of(start * W, W), kr * W), :]
        s_list.append(_mm(qs, kw, 'nt') + bias_ref[didx, 0])
    p_list = []
    for s in s_list:
        m = jnp.max(s, axis=-1, keepdims=True)
        p = jnp.exp(s - m)
        p_list.append((p, jnp.sum(p, axis=-1, keepdims=True)))
    for j, ((start, _), (p, l)) in enumerate(zip(rows, p_list)):
        vw = vb_ref[pl.ds(pl.multiple_of(start * W, W), kr * W), :]
        o = _mm(p, vw) / l
        o_ref[0, j * W:(j + 1) * W, :] = jnp.where(lo_lane, o[:W], o[W:])


def na_bias_table(rpb, n_rows):
    W = GRID_W
    kr = min(NA_KR, n_rows)
    cols = jnp.arange(W)
    col_start = jnp.clip(cols - NA_KC // 2, 0, W - NA_KC)
    in_band = (cols[None, :] >= col_start[:, None]) & (cols[None, :] < col_start[:, None] + NA_KC)
    dc = jnp.clip(cols[None, :] - cols[:, None], -(NA_KC - 1), NA_KC - 1) + (NA_KC - 1)
    win = jnp.stack([rpb.astype(f32)[:, d:d + kr, :] for d in range(8)], axis=1)
    onehot = (dc[:, :, None] == jnp.arange(2 * NA_KC - 1)[None, None, :]).astype(f32)
    tab = jnp.einsum('hdic,qkc->hdqik', win, onehot, precision=lax.Precision.HIGHEST)
    tab = jnp.where(in_band[None, None, :, None, :], tab, NEG_INF)
    H = rpb.shape[0]
    tab = tab.reshape(H // 2, 2, 8, W, kr * W).transpose(2, 0, 1, 3, 4)
    return tab.reshape(8, H // 2, 2 * W, kr * W)


def neighborhood_attention_pallas(proj, rpb, lane_block0):
    Bn, L, _ = proj.shape
    W = GRID_W
    n_rows = L // W
    kr = min(NA_KR, n_rows)
    hp = NA_H * NA_HD // LANES
    rps = NA_ROWS_PER_STEP
    bias = na_bias_table(rpb, n_rows)
    kern = functools.partial(_na_kernel, rows_per_step=rps, n_rows=n_rows, kr=kr)
    return pl.pallas_call(
        kern,
        grid=(Bn, hp, n_rows // rps),
        in_specs=[pl.BlockSpec((1, rps * W, LANES), lambda b, h, r: (b, r, lane_block0 + h)),
                  pl.BlockSpec((1, L, LANES), lambda b, h, r: (b, 0, lane_block0 + hp + h)),
                  pl.BlockSpec((1, L, LANES), lambda b, h, r: (b, 0, lane_block0 + 2 * hp + h)),
                  pl.BlockSpec((8, 1, 2 * W, kr * W), lambda b, h, r: (0, h, 0, 0))],
        out_specs=pl.BlockSpec((1, rps * W, LANES), lambda b, h, r: (b, r, h)),
        out_shape=jax.ShapeDtypeStruct((Bn, L, NA_H * NA_HD), f32),
        scratch_shapes=[pltpu.VMEM((L, LANES), bf16), pltpu.VMEM((L, LANES), bf16)],
        compiler_params=pltpu.CompilerParams(dimension_semantics=("parallel", "parallel", "arbitrary"),
                                             vmem_limit_bytes=VMEM_LIMIT_BYTES),
        name="neighborhood_attention",
    )(proj, proj, proj, bias)


def _moe_ffn_kernel(be_ref, x_ref, w1_ref, w3_ref, w2_ref, o_ref, w1b_ref, w3b_ref, w2b_ref):
    i = pl.program_id(0)

    @pl.when((i == 0) | (be_ref[i] != be_ref[jnp.maximum(i - 1, 0)]))
    def _():
        w1b_ref[...] = w1_ref[0].astype(bf16)
        w3b_ref[...] = w3_ref[0].astype(bf16)
        w2b_ref[...] = w2_ref[0].astype(bf16)

    x = x_ref[...].astype(bf16)
    h1 = jnp.dot(x, w1b_ref[...], preferred_element_type=f32)
    h3 = jnp.dot(x, w3b_ref[...], preferred_element_type=f32)
    g = h1 * jax.nn.sigmoid(h1) * h3
    o_ref[...] = jnp.dot(g.astype(bf16), w2b_ref[...], preferred_element_type=f32)


def moe_expert_ffn(xs, block_e, w1, w3, w2):
    rows, Dm = xs.shape
    n_blocks = rows // MOE_BLOCK
    grid_spec = pltpu.PrefetchScalarGridSpec(
        num_scalar_prefetch=1,
        grid=(n_blocks,),
        in_specs=[pl.BlockSpec((MOE_BLOCK, Dm), lambda i, be: (i, 0)),
                  pl.BlockSpec((1, Dm, MOE_FF), lambda i, be: (be[i], 0, 0)),
                  pl.BlockSpec((1, Dm, MOE_FF), lambda i, be: (be[i], 0, 0)),
                  pl.BlockSpec((1, MOE_FF, Dm), lambda i, be: (be[i], 0, 0))],
        out_specs=pl.BlockSpec((MOE_BLOCK, Dm), lambda i, be: (i, 0)),
        scratch_shapes=[pltpu.VMEM((Dm, MOE_FF), bf16), pltpu.VMEM((Dm, MOE_FF), bf16), pltpu.VMEM((MOE_FF, Dm), bf16)],
    )
    return pl.pallas_call(
        _moe_ffn_kernel,
        grid_spec=grid_spec,
        out_shape=jax.ShapeDtypeStruct((rows, Dm), f32),
        compiler_params=pltpu.CompilerParams(dimension_semantics=("arbitrary",), vmem_limit_bytes=VMEM_LIMIT_BYTES),
        name="moe_expert_ffn",
    )(block_e.astype(jnp.int32), xs, w1.astype(f32), w3.astype(f32), w2.astype(f32))


def _rms(x):
    return x * lax.rsqrt(jnp.mean(x * x, axis=-1, keepdims=True) + NORM_EPS)


def _norm_proj_kernel(x_ref, g_ref, w_ref, o_ref):
    h = (_rms(x_ref[...]) * g_ref[...]).astype(bf16)
    o_ref[...] = jnp.dot(h, w_ref[...], preferred_element_type=f32)


def norm_proj(x2, g, w):
    N, D = x2.shape
    F = w.shape[1]
    tm = PROJ_ROWS_PER_STEP
    return pl.pallas_call(
        _norm_proj_kernel,
        grid=(N // tm,),
        in_specs=[pl.BlockSpec((tm, D), lambda i: (i, 0)), pl.BlockSpec((1, D), lambda i: (0, 0)),
                  pl.BlockSpec((D, F), lambda i: (0, 0))],
        out_specs=pl.BlockSpec((tm, F), lambda i: (i, 0)),
        out_shape=jax.ShapeDtypeStruct((N, F), f32),
        compiler_params=pltpu.CompilerParams(dimension_semantics=("parallel",), vmem_limit_bytes=VMEM_LIMIT_BYTES),
        name="norm_proj",
    )(x2, g.astype(f32).reshape(1, D), w.astype(bf16))


def _out_proj_kernel(x_ref, yh_ref, yr_ref, yn_ref, wh_ref, wr_ref, wn_ref, g_ref, wrt_ref, xo_ref, h_ref, lg_ref):
    mix = (jnp.dot(yh_ref[...].astype(bf16), wh_ref[...], preferred_element_type=f32)
           + jnp.dot(yr_ref[...].astype(bf16), wr_ref[...], preferred_element_type=f32)
           + jnp.dot(yn_ref[...].astype(bf16), wn_ref[...], preferred_element_type=f32))
    x = x_ref[...] + mix
    xo_ref[...] = x
    h = (_rms(x) * g_ref[...]).astype(bf16)
    h_ref[...] = h
    lg_ref[...] = jnp.dot(h, wrt_ref[...], preferred_element_type=f32)


def out_proj_norm_router(x2, y_hy, y_rw, y_na, w_out, g, w_router):
    N, D = x2.shape
    tm = PROJ_ROWS_PER_STEP
    d_hy, d_rw, d_na = y_hy.shape[1], y_rw.shape[1], y_na.shape[1]
    nr = w_router.shape[1]
    w_router = jnp.pad(w_router.astype(bf16), ((0, 0), (0, LANES - nr)))
    wb = w_out.astype(bf16)
    row = lambda c: pl.BlockSpec((tm, c), lambda i: (i, 0))
    full = lambda r, c: pl.BlockSpec((r, c), lambda i: (0, 0))
    return pl.pallas_call(
        _out_proj_kernel,
        grid=(N // tm,),
        in_specs=[row(D), row(d_hy), row(d_rw), row(d_na), full(d_hy, D), full(d_rw, D), full(d_na, D), full(1, D),
                  full(D, LANES)],
        out_specs=[row(D), row(D), row(LANES)],
        out_shape=[jax.ShapeDtypeStruct((N, D), f32), jax.ShapeDtypeStruct((N, D), bf16),
                   jax.ShapeDtypeStruct((N, LANES), f32)],
        compiler_params=pltpu.CompilerParams(dimension_semantics=("parallel",), vmem_limit_bytes=VMEM_LIMIT_BYTES),
        name="out_proj_norm_router",
    )(x2, y_hy, y_rw, y_na, wb[:d_hy], wb[d_hy:d_hy + d_rw], wb[d_hy + d_rw:], g.astype(f32).reshape(1, D), w_router)


def _final_norm_kernel(x_ref, g_ref, o_ref):
    x = x_ref[...]
    o_ref[...] = x * lax.rsqrt(jnp.mean(x * x, axis=-1, keepdims=True) + NORM_EPS) * g_ref[...]


def final_rmsnorm(x, g):
    Bn, L, D = x.shape
    tm = 1024
    out = pl.pallas_call(
        _final_norm_kernel,
        grid=(Bn * L // tm,),
        in_specs=[pl.BlockSpec((tm, D), lambda i: (i, 0)), pl.BlockSpec((1, D), lambda i: (0, 0))],
        out_specs=pl.BlockSpec((tm, D), lambda i: (i, 0)),
        out_shape=jax.ShapeDtypeStruct((Bn * L, D), x.dtype),
        name="final_rmsnorm",
    )(x.reshape(Bn * L, D), g.reshape(1, D))
    return out.reshape(Bn, L, D)


def hyena_positional_features(L):
    t = jnp.linspace(0.0, 1.0, L, dtype=f32)[:, None]
    w = (2.0 * math.pi / L) * jnp.arange(L, dtype=f32)[:, None]
    f = jnp.linspace(1e-4, HY_BANDS - 1, HY_BANDS, dtype=f32)[None, :]
    z = jnp.concatenate([t, jnp.cos(f * w), -jnp.sin(f * w)], axis=-1)
    return z, t


def hyena_filters(z, t, w1, b1, w2, b2, w3, b3, wout, freq):
    fr = freq.astype(f32)
    act = lambda u: jnp.sin(fr * u)
    h = act(z @ w1.astype(f32) + b1.astype(f32))
    h = act(h @ w2.astype(f32) + b2.astype(f32))
    h = act(h @ w3.astype(f32) + b3.astype(f32))
    h = (h @ wout.astype(f32)).reshape(-1, 2, HY_D)
    deltas = jnp.abs(jnp.linspace(HY_MIN_DECAY, HY_MAX_DECAY, HY_D, dtype=f32))
    h = h * jnp.exp(-t[:, :, None] * deltas)
    return h * lax.rsqrt(jnp.sum(h * h, axis=(0, 1), keepdims=True) + 1e-6)


def _split_bf16(x):
    hi = x.astype(bf16)
    return hi, (x - hi.astype(f32)).astype(bf16)


def _dot3(m_hi, m_lo, x_hi, x_lo):
    d = lambda a, b: lax.dot_general(a, b, _DN['nn'], preferred_element_type=f32)
    return d(m_hi, x_hi) + (d(m_hi, x_lo) + d(m_lo, x_hi))


def _cmatmul(mr, mi, xr, xi):
    xrh, xrl = _split_bf16(xr)
    rr = _dot3(mr[0], mr[1], xrh, xrl)
    ir = _dot3(mi[0], mi[1], xrh, xrl)
    if xi is None:
        return rr, ir
    xih, xil = _split_bf16(xi)
    ii = _dot3(mi[0], mi[1], xih, xil)
    ri = _dot3(mr[0], mr[1], xih, xil)
    return rr - ii, ri + ir


def _row_dft_kernel(mrh_ref, mrl_ref, mih_ref, mil_ref, *refs, real_input):
    if real_input:
        ur_ref, or_ref, oi_ref = refs
        ui = None
    else:
        ur_ref, ui_ref, or_ref, oi_ref = refs
        ui = ui_ref[0]
    o_r, o_i = _cmatmul((mrh_ref[...], mrl_ref[...]), (mih_ref[...], mil_ref[...]), ur_ref[0], ui)
    or_ref[0] = o_r
    oi_ref[0] = o_i


def row_dft(tabs, u, packed):
    rows, r_in, W = u.shape
    P = rows // 2 if packed else rows
    r_out = tabs[0].shape[0]
    wb = min(FFT_LANE_BLOCK, W)
    tab_spec = pl.BlockSpec((r_out, r_in), lambda p, j: (0, 0))
    re_spec = pl.BlockSpec((1, r_in, wb), lambda p, j: (p, 0, j))
    im_spec = pl.BlockSpec((1, r_in, wb), lambda p, j: (p + P, 0, j))
    out_spec = pl.BlockSpec((1, r_out, wb), lambda p, j: (p, 0, j))
    ins = [u, u] if packed else [u]
    return pl.pallas_call(
        functools.partial(_row_dft_kernel, real_input=not packed),
        grid=(P, W // wb),
        in_specs=[tab_spec] * 4 + ([re_spec, im_spec] if packed else [re_spec]),
        out_specs=[out_spec, out_spec],
        out_shape=[jax.ShapeDtypeStruct((P, r_out, W), f32)] * 2,
        compiler_params=pltpu.CompilerParams(dimension_semantics=("parallel", "parallel"),
                                             vmem_limit_bytes=VMEM_LIMIT_BYTES),
        name="hyena_row_dft",
    )(*tabs, *ins)


def _row_idft_gate_kernel(mrh_ref, mrl_ref, mih_ref, mil_ref, dr_ref, di_ref, z0_ref, z1_ref, x0_ref, x1_ref,
                          skip_ref, o0_ref, o1_ref):
    y_r, y_i = _cmatmul((mrh_ref[...], mrl_ref[...]), (mih_ref[...], mil_ref[...]), dr_ref[0], di_ref[0])
    skip = skip_ref[...]
    o0_ref[0] = x0_ref[0] * (y_r + z0_ref[0] * skip)
    o1_ref[0] = x1_ref[0] * (y_i + z1_ref[0] * skip)


def row_idft_gate(tabs, dr, di, z, x0, skip_row):
    P, r_in, W = dr.shape
    r_out = tabs[0].shape[0]
    wb = min(FFT_LANE_BLOCK, W)
    tab_spec = pl.BlockSpec((r_out, r_in), lambda p, j: (0, 0))
    d_spec = pl.BlockSpec((1, r_in, wb), lambda p, j: (p, 0, j))
    lo = pl.BlockSpec((1, r_out, wb), lambda p, j: (p, 0, j))
    hi = pl.BlockSpec((1, r_out, wb), lambda p, j: (p + P, 0, j))
    o0, o1 = pl.pallas_call(
        _row_idft_gate_kernel,
        grid=(P, W // wb),
        in_specs=[tab_spec] * 4 + [d_spec, d_spec, lo, hi, lo, hi, pl.BlockSpec((1, wb), lambda p, j: (0, j))],
        out_specs=[lo, lo],
        out_shape=[jax.ShapeDtypeStruct((P, r_out, W), f32)] * 2,
        compiler_params=pltpu.CompilerParams(dimension_semantics=("parallel", "parallel"),
                                             vmem_limit_bytes=VMEM_LIMIT_BYTES),
        name="hyena_row_idft_gate",
    )(*tabs, dr, di, z, z, x0, x0, skip_row)
    return jnp.concatenate([o0, o1], axis=0)


def _col_dft_kernel(grh_ref, grl_ref, gih_ref, gil_ref, ar_ref, ai_ref, br_ref, bi_ref, *, k1_per_step):
    for j in range(k1_per_step):
        b_r, b_i = _cmatmul((grh_ref[j], grl_ref[j]), (gih_ref[j], gil_ref[j]), ar_ref[0, j], ai_ref[0, j])
        br_ref[0, j] = b_r
        bi_ref[0, j] = b_i


def _col_conv_kernel(grh_ref, grl_ref, gih_ref, gil_ref, trh_ref, trl_ref, tih_ref, til_ref,
                     ar_ref, ai_ref, kr_ref, ki_ref, dr_ref, di_ref, *, k1_per_step):
    for j in range(k1_per_step):
        b_r, b_i = _cmatmul((grh_ref[j], grl_ref[j]), (gih_ref[j], gil_ref[j]), ar_ref[0, j], ai_ref[0, j])
        k_r = kr_ref[0, j]
        k_i = ki_ref[0, j]
        c_r = b_r * k_r - b_i * k_i
        c_i = b_r * k_i + b_i * k_r
        d_r, d_i = _cmatmul((trh_ref[j], trl_ref[j]), (tih_ref[j], til_ref[j]), c_r, c_i)
        dr_ref[0, j] = d_r
        di_ref[0, j] = d_i


def col_stage(g_tabs, gt_tabs, ar, ai, kr=None, ki=None):
    P, n1, n2, C = ar.shape
    kb = min(FFT_K1_PER_STEP, n1)
    g_spec = pl.BlockSpec((kb, n2, n2), lambda p, j: (j, 0, 0))
    a_spec = pl.BlockSpec((1, kb, n2, C), lambda p, j: (p, j, 0, 0))
    k_spec = pl.BlockSpec((1, kb, n2, C), lambda p, j: (0, j, 0, 0))
    common = dict(
        grid=(P, n1 // kb),
        out_specs=[a_spec, a_spec],
        out_shape=[jax.ShapeDtypeStruct((P, n1, n2, C), f32)] * 2,
        compiler_params=pltpu.CompilerParams(dimension_semantics=("parallel", "parallel"),
                                             vmem_limit_bytes=VMEM_LIMIT_BYTES),
    )
    if kr is None:
        return pl.pallas_call(functools.partial(_col_dft_kernel, k1_per_step=kb),
                              in_specs=[g_spec] * 4 + [a_spec] * 2, name="hyena_col_dft", **common)(*g_tabs, ar, ai)
    return pl.pallas_call(functools.partial(_col_conv_kernel, k1_per_step=kb),
                          in_specs=[g_spec] * 8 + [a_spec] * 2 + [k_spec] * 2, name="hyena_col_conv", **common)(
        *g_tabs, *gt_tabs, ar, ai, kr, ki)


def _bf16_tables(m):
    out = []
    for part in (np.real(m), np.imag(m)):
        x = jnp.asarray(part, f32)
        hi = x.astype(bf16)
        out += [hi, (x - hi.astype(f32)).astype(bf16)]
    return out


def hyena_dft_tables(L):
    n2 = FFT_N2
    n1 = 2 * L // n2
    N = n1 * n2
    a = np.arange(n1)
    fa = np.exp(-2j * np.pi * np.outer(a, a) / n1)
    k1 = np.arange(n1)[:, None, None]
    k2 = np.arange(n2)[None, :, None]
    nn = np.arange(n2)[None, None, :]
    g = np.exp(-2j * np.pi * nn * (k1 + n1 * k2) / N)
    gt = np.conj(np.transpose(g, (0, 2, 1)))
    fc = np.conj(fa).T / N
    return dict(fa=_bf16_tables(fa), fa_half=_bf16_tables(fa[:, :n1 // 2]), g=_bf16_tables(g), gt=_bf16_tables(gt),
                fc_half=_bf16_tables(fc[:n1 // 2]))


def hyena_long_conv_gate(tabs, z, x0, k2, skip):
    Bn, L, C = z.shape
    n2 = FFT_N2
    n1 = 2 * L // n2
    P = Bn // 2
    W = n2 * C
    ar, ai = row_dft(tabs['fa'], k2.reshape(1, n1, W), packed=False)
    kr, ki = col_stage(tabs['g'], tabs['gt'], ar.reshape(1, n1, n2, C), ai.reshape(1, n1, n2, C))
    zv = z.reshape(Bn, n1 // 2, W)
    ar, ai = row_dft(tabs['fa_half'], zv, packed=True)
    dr, di = col_stage(tabs['g'], tabs['gt'], ar.reshape(P, n1, n2, C), ai.reshape(P, n1, n2, C), kr, ki)
    out = row_idft_gate(tabs['fc_half'], dr.reshape(P, n1, W), di.reshape(P, n1, W), zv, x0.reshape(Bn, n1 // 2, W),
                        jnp.tile(skip.astype(f32), n2).reshape(1, W))
    return out.reshape(Bn, L, C)


def _hyena_prep_kernel(*refs, rows, nt):
    cur, prev, nxt = refs[0:3], refs[3:6], refs[6:9]
    w_ref, b_ref, z_ref, x0_ref = refs[9:]
    t = pl.program_id(1)
    has_prev = (t > 0).astype(f32)
    has_next = (t < nt - 1).astype(f32)
    ridx = lax.broadcasted_iota(jnp.int32, (rows, HY_D), 0)

    def conv(j):
        u = cur[j][0]
        prev_row = prev[j][0, SUBLANES - 1:SUBLANES, :] * has_prev
        next_row = nxt[j][0, 0:1, :] * has_next
        up = jnp.where(ridx == 0, prev_row, pltpu.roll(u, 1, axis=0))
        un = jnp.where(ridx == rows - 1, next_row, pltpu.roll(u, rows - 1, axis=0))
        sl = slice(j * HY_D, (j + 1) * HY_D)
        return up * w_ref[0:1, sl] + u * w_ref[1:2, sl] + un * w_ref[2:3, sl] + b_ref[:, sl]

    x0_ref[0] = conv(0)
    z_ref[0] = conv(2) * conv(1)


def hyena_prep(proj, lane0, conv_w, conv_b):
    Bn, L, _ = proj.shape
    rows = RW_ROWS_PER_STEP
    nt = L // rows
    hb = rows // SUBLANES
    b0 = lane0 // HY_D
    assert b0 * HY_D == lane0
    cur = lambda j: pl.BlockSpec((1, rows, HY_D), lambda b, t: (b, t, b0 + j))
    prev = lambda j: pl.BlockSpec((1, SUBLANES, HY_D), lambda b, t: (b, jnp.maximum(t * hb - 1, 0), b0 + j))
    nxt = lambda j: pl.BlockSpec((1, SUBLANES, HY_D), lambda b, t: (b, jnp.minimum((t + 1) * hb, L // SUBLANES - 1), b0 + j))
    full = lambda shp: pl.BlockSpec(shp, lambda b, t: (0,) * len(shp))
    out = pl.BlockSpec((1, rows, HY_D), lambda b, t: (b, t, 0))
    return pl.pallas_call(
        functools.partial(_hyena_prep_kernel, rows=rows, nt=nt),
        grid=(Bn, nt),
        in_specs=[cur(0), cur(1), cur(2), prev(0), prev(1), prev(2), nxt(0), nxt(1), nxt(2),
                  full((3, 3 * HY_D)), full((1, 3 * HY_D))],
        out_specs=[out, out],
        out_shape=[jax.ShapeDtypeStruct((Bn, L, HY_D), f32)] * 2,
        compiler_params=pltpu.CompilerParams(dimension_semantics=("parallel", "parallel")),
        name="hyena_prep",
    )(*([proj] * 9), conv_w.astype(f32), conv_b.astype(f32).reshape(1, 3 * HY_D))


def hyena_mixer(tabs, proj, lane0, conv_w, conv_b, filt, skip):
    z, x0 = hyena_prep(proj, lane0, conv_w, conv_b)
    k2 = jnp.concatenate([filt[:1, 0] + filt[:1, 1], filt[1:, 0],
                          jnp.zeros((1, HY_D), f32), filt[1:, 1][::-1]], axis=0)
    return hyena_long_conv_gate(tabs, z, x0, k2, skip)


def rwkv7_mixer(proj, lora_lane0, mu, w0, w2, a0, a2, k_k, k_a, r_k, g2, ln_w, ln_b):
    r, k, v, kk, lw2, a_both = rw_prep(proj, lora_lane0, mu, w0, w2, a0, a2, k_k)
    y2 = wkv7_chunked(r, k, v, kk, lw2, a_both, k_a.astype(f32).reshape(1, RW_D))
    return rw_post(y2, r, k, v, a_both, proj, lora_lane0, k_a, r_k, ln_w, ln_b, g2)


def hier_moe(h, logits, bg, be, w1, w3, w2):
    N, Dm = h.shape
    assert MOE_TOPK == 2
    g_logits = logits[:, :MOE_GROUPS] + bg.astype(f32)
    g_sel = jnp.argmax(g_logits, axis=-1)
    g_prob = jnp.take_along_axis(jax.nn.softmax(g_logits, axis=-1), g_sel[:, None], axis=-1)
    e_logits = (logits[:, MOE_GROUPS:MOE_GROUPS + MOE_EXPERTS] + be.astype(f32)).reshape(N, MOE_GROUPS, MOE_PER_GROUP)
    e_logits = jnp.take_along_axis(e_logits, g_sel[:, None, None], axis=1)[:, 0]
    top_val, top_idx = lax.top_k(e_logits, MOE_TOPK)
    gate = g_prob * jax.nn.softmax(top_val, axis=-1)
    expert = g_sel[:, None] * MOE_PER_GROUP + top_idx
    M = N * MOE_TOPK
    flat_e = expert.reshape(M).astype(jnp.int32)
    experts = jnp.arange(MOE_EXPERTS, dtype=jnp.int32)
    counts = jnp.sum((flat_e[:, None] == experts[None, :]).astype(jnp.int32), axis=0)
    padded = (counts + MOE_BLOCK - 1) // MOE_BLOCK * MOE_BLOCK
    pad_end = jnp.cumsum(padded)
    n_blocks = -(-M // MOE_BLOCK) + MOE_EXPERTS
    n_rows = n_blocks * MOE_BLOCK
    cum_need = jnp.cumsum(padded - counts)
    filler = jnp.arange(n_rows - M, dtype=jnp.int32)
    filler_e = jnp.sum((cum_need[None, :] <= filler[:, None]).astype(jnp.int32), axis=1)
    keys = jnp.concatenate([2 * flat_e, 2 * filler_e + 1])
    ids = jnp.concatenate([jnp.arange(M, dtype=jnp.int32), jnp.full((n_rows - M,), M, jnp.int32)])
    _, slot_src = lax.sort((keys, ids), num_keys=1)
    tok_src = jnp.where(slot_src < M, slot_src // MOE_TOPK, 0)
    block_start = jnp.arange(n_blocks, dtype=pad_end.dtype) * MOE_BLOCK
    block_e = jnp.minimum(jnp.sum(pad_end[None, :] <= block_start[:, None], axis=1), MOE_EXPERTS - 1)
    y = moe_expert_ffn(h[tok_src], block_e, w1, w3, w2)
    _, row_of = lax.sort((slot_src, jnp.arange(n_rows, dtype=jnp.int32)), num_keys=1)
    pos = row_of[:M].reshape(N, MOE_TOPK)
    return gate[:, 0:1] * y[pos[:, 0]] + gate[:, 1:2] * y[pos[:, 1]]


def kernel(x, norm1_g, w_in, hy_conv_w, hy_conv_b, hy_w1, hy_b1, hy_w2, hy_b2, hy_w3, hy_b3, hy_wout, hy_freq, hy_skip, rw_mu, rw_w0, rw_w2, rw_a0, rw_a2, rw_kk, rw_ka, rw_rk, rw_g2, rw_ln_w, rw_ln_b, na_rpb, w_out, norm2_g, moe_wg, moe_bg, moe_we, moe_be, moe_w1, moe_w3, moe_w2, norm_f_g):
    Bn, L, _ = x.shape
    z_pos, t_pos = hyena_positional_features(L)
    dft_tabs = hyena_dft_tables(L)
    splits = np.cumsum(IN_SIZES)[:-1].tolist()
    hy_end, rkv_end = splits[0], splits[1]
    hy_lane0 = rkv_end - hy_end
    lora_lane0, na_lane0 = splits[1], splits[4]
    assert RW_D % LANES == 0 and lora_lane0 % LANES == 0 and na_lane0 % LANES == 0
    N = Bn * L
    x2 = x.reshape(N, D_MODEL)
    for l in range(DEPTH):
        w_in_l = jnp.concatenate([w_in[l][:, hy_end:rkv_end], w_in[l][:, :hy_end], w_in[l][:, rkv_end:]], axis=1)
        proj = norm_proj(x2, norm1_g[l], w_in_l).reshape(Bn, L, IN_D)
        filt = hyena_filters(z_pos, t_pos, hy_w1[l], hy_b1[l], hy_w2[l], hy_b2[l],
                             hy_w3[l], hy_b3[l], hy_wout[l], hy_freq[l])
        y_hy = hyena_mixer(dft_tabs, proj, hy_lane0, hy_conv_w[l], hy_conv_b[l], filt, hy_skip[l])
        y_rw = rwkv7_mixer(proj, lora_lane0, rw_mu[l], rw_w0[l], rw_w2[l], rw_a0[l],
                           rw_a2[l], rw_kk[l], rw_ka[l], rw_rk[l].reshape(RW_D), rw_g2[l], rw_ln_w[l], rw_ln_b[l])
        y_na = neighborhood_attention_pallas(proj, na_rpb[l], na_lane0 // LANES)
        x2, h2, logits = out_proj_norm_router(
            x2, y_hy.reshape(N, HY_D), y_rw.reshape(N, RW_D), y_na.reshape(N, NA_D), w_out[l], norm2_g[l],
            jnp.concatenate([moe_wg[l], moe_we[l]], axis=1))
        x2 = x2 + hier_moe(h2, logits, moe_bg[l], moe_be[l], moe_w1[l], moe_w3[l], moe_w2[l])
    return final_rmsnorm(x2.reshape(Bn, L, D_MODEL), norm_f_g)
```

```python
import functools
import math

import jax
import jax.numpy as jnp
import numpy as np
from jax import lax
from jax.experimental import pallas as pl
from jax.experimental.pallas import tpu as pltpu

f32 = jnp.float32
bf16 = jnp.bfloat16

D_MODEL = 1024
DEPTH = 2
GRID_W = 64
NORM_EPS = 1e-6
NEG_INF = -1e30

HY_D = D_MODEL // 4
HY_EMB = 33
HY_BANDS = (HY_EMB - 1) // 2
HY_FFN = 64
HY_MIN_DECAY = math.log(1e-2) / 1.5
HY_MAX_DECAY = math.log(1e-2) / 0.3

RW_N = 64
RW_D = D_MODEL // 2
RW_H = RW_D // RW_N
RW_W_LORA = 64
RW_A_LORA = 64
RW_G_LORA = 128
RW_GN_EPS = 64e-5

NA_HD = 64
NA_D = D_MODEL // 4
NA_H = NA_D // NA_HD
NA_KR = 8
NA_KC = 16

MIX_D = HY_D + RW_D + NA_D
IN_SIZES = (3 * HY_D, 3 * RW_D, RW_G_LORA, 2 * RW_W_LORA, 2 * RW_A_LORA, 3 * NA_D)
IN_D = sum(IN_SIZES)

MOE_GROUPS = 4
MOE_PER_GROUP = 8
MOE_EXPERTS = MOE_GROUPS * MOE_PER_GROUP
MOE_TOPK = 2
MOE_FF = 512
MOE_BLOCK = 256

LANES = 128
WKV_CHUNK = 64
WKV_TIME_BLOCK = 512
NA_ROWS_PER_STEP = 4
RW_ROWS_PER_STEP = 512
SUBLANES = 8
PROJ_ROWS_PER_STEP = 512
FFT_N2 = 128
FFT_LANE_BLOCK = 4096
FFT_K1_PER_STEP = 4
VMEM_LIMIT_BYTES = 48 * 1024 * 1024

_DN = {'nn': (((1,), (0,)), ((), ())), 'nt': (((1,), (1,)), ((), ())), 'tn': (((0,), (0,)), ((), ()))}


def _mm(a, b, dims='nn'):
    return lax.dot_general(a.astype(bf16), b.astype(bf16), _DN[dims], preferred_element_type=f32)


def _wkv_kernel(r_ref, k_ref, v_ref, kk_ref, lw_ref, a_ref, ka_ref, y_ref, ht_ref, *, tb, batch):
    T = WKV_CHUNK
    H2 = 2 * T
    nc = tb // T
    npairs = r_ref.shape[-1] // LANES
    mm = _mm

    @pl.when(pl.program_id(1) == 0)
    def _():
        ht_ref[...] = jnp.zeros_like(ht_ref)

    d = pl.program_id(0) // batch
    sign = 1 - 2 * d
    row = lax.broadcasted_iota(jnp.int32, (H2, H2), 0)
    col = lax.broadcasted_iota(jnp.int32, (H2, H2), 1)
    same64 = (row // T) == (col // T)
    tdiff = (row % T - col % T) * sign
    strict = same64 & (tdiff > 0)
    incl = same64 & (tdiff >= 0)
    same16 = (row // 16) == (col // 16)
    same32 = (row // 32) == (col // 32)
    off32 = same32 & jnp.logical_not(same16)
    off64 = jnp.logical_not(same32)
    eye = jnp.where(row == col, 1.0, 0.0).astype(f32)
    trow = lax.broadcasted_iota(jnp.int32, (T, T), 0)
    tcol = lax.broadcasted_iota(jnp.int32, (T, T), 1)
    tri = jnp.where((trow - tcol) * sign >= 0, 1.0, 0.0).astype(bf16)
    lo_lane = lax.broadcasted_iota(jnp.int32, (T, LANES), 1) < T
    is_bwd = d == 1

    def stack(x):
        return jnp.concatenate([jnp.where(lo_lane, x, 0.0), jnp.where(lo_lane, 0.0, x)], axis=0)

    inst = [(s, p) for s in range(nc) for p in range(npairs)]
    offs = [pl.multiple_of((s + d * (nc - 1 - 2 * s)) * T, T) for s in range(nc)]

    def load(ref, s, p):
        return ref[0, pl.ds(offs[s], T), p * LANES:(p + 1) * LANES]

    cs_l = []
    for s, p in inst:
        lw = load(lw_ref, s, p)
        l1 = lw.astype(bf16)
        r1 = lw - l1.astype(f32)
        l2 = r1.astype(bf16)
        l3 = (r1 - l2.astype(f32)).astype(bf16)
        dd = lambda x: lax.dot_general(tri, x, _DN['nn'], preferred_element_type=f32)
        cs_l.append((dd(l1) + (dd(l2) + dd(l3)), lw))
    ops = []
    for (s, p), (cs, lw) in zip(inst, cs_l):
        r = load(r_ref, s, p)
        k = load(k_ref, s, p)
        v = load(v_ref, s, p)
        kk = load(kk_ref, s, p)
        a = load(a_ref, s, p)
        ka = ka_ref[:, p * LANES:(p + 1) * LANES]
        kd = k * (1.0 + (a - 1.0) * ka)
        b = kk * a
        cs_end = jnp.where(is_bwd, cs[0:1, :], cs[T - 1:T, :])
        em = jnp.exp(-cs)
        e_end = jnp.exp(cs_end - cs)
        ops.append(dict(
            As=stack(-kk * jnp.exp(cs - lw)), Rs=stack(r * jnp.exp(cs)), Bs=stack(b * em), Ks=stack(kd * em),
            Vs=stack(v), Bgs=stack(b * e_end), Kgs=stack(kd * e_end), g_end=jnp.exp(cs_end)))
    for o in ops:
        o['AR'] = jnp.concatenate([o['As'], o['Rs']], axis=0)
        S = mm(o['AR'], jnp.concatenate([o['Bs'], o['Ks']], axis=0), 'nt')
        N = jnp.where(strict, S[:H2, :H2], 0.0)
        o['Aak'] = jnp.where(strict, S[:H2, H2:], 0.0)
        o['Arb'] = jnp.where(incl, S[H2:, :H2], 0.0)
        o['Ark'] = jnp.where(incl, S[H2:, H2:], 0.0)
        o['Nd'] = jnp.where(same16, N, 0.0)
        o['N32'] = jnp.where(off32, N, 0.0)
        o['N64'] = jnp.where(off64, N, 0.0)
    for o in ops:
        o['X'] = eye + o['Nd']
        o['P'] = mm(o['Nd'], o['Nd'])
    for it in range(3):
        if it < 2:
            for o in ops:
                px = mm(o['P'], jnp.concatenate([o['X'], o['P']], axis=1))
                o['X'] = o['X'] + px[:, :H2]
                o['P'] = px[:, H2:]
        else:
            for o in ops:
                o['X'] = o['X'] + mm(o['P'], o['X'])
    for key in ('N32', 'N64'):
        for o in ops:
            o['Z'] = mm(o[key], o['X'])
        for o in ops:
            o['X'] = o['X'] + mm(o['X'], o['Z'])
    for o in ops:
        wy = mm(jnp.concatenate([o['Aak'], o['Ark']], axis=0), o['Vs'])
        o['W0'] = wy[:H2]
        o['Yv'] = wy[H2:]
    hts = [ht_ref[p] for p in range(npairs)]
    for s in range(nc):
        cur = [ops[s * npairs + p] for p in range(npairs)]
        arh = [mm(o['AR'], hts[p], 'nt') for p, o in enumerate(cur)]
        us = [mm(o['X'], o['W0'] + arh[p][:H2]) for p, o in enumerate(cur)]
        ys = [arh[p][H2:] + (mm(o['Arb'], us[p]) + o['Yv']) for p, o in enumerate(cur)]
        upd = [mm(jnp.concatenate([us[p], o['Vs']], axis=0), jnp.concatenate([o['Bgs'], o['Kgs']], axis=0), 'tn')
               for p, o in enumerate(cur)]
        hts = [jnp.where(same64, hts[p] * o['g_end'] + upd[p], 0.0) for p, o in enumerate(cur)]
        for p in range(npairs):
            y_ref[0, pl.ds(offs[s], T), p * LANES:(p + 1) * LANES] = ys[p][:T] + ys[p][T:]
    for p in range(npairs):
        ht_ref[p] = hts[p]


def wkv7_chunked(r, k, v, kk, lw2, a2, ka):
    Bn, L, C = r.shape
    tb = WKV_TIME_BLOCK
    nt = L // tb
    tmap = lambda i, t: t + (i // Bn) * (nt - 1 - 2 * t)
    shared = pl.BlockSpec((1, tb, C), lambda i, t: (i % Bn, tmap(i, t), 0))
    per_dir = pl.BlockSpec((1, tb, C), lambda i, t: (i % Bn, tmap(i, t), i // Bn))
    return pl.pallas_call(
        functools.partial(_wkv_kernel, tb=tb, batch=Bn),
        grid=(2 * Bn, nt),
        in_specs=[shared, shared, shared, shared, per_dir, per_dir, pl.BlockSpec((1, C), lambda i, t: (0, 0))],
        out_specs=pl.BlockSpec((1, tb, C), lambda i, t: (i, tmap(i, t), 0)),
        out_shape=jax.ShapeDtypeStruct((2 * Bn, L, C), f32),
        scratch_shapes=[pltpu.VMEM((C // LANES, LANES, LANES), f32)],
        compiler_params=pltpu.CompilerParams(dimension_semantics=("parallel", "arbitrary"),
                                             vmem_limit_bytes=VMEM_LIMIT_BYTES),
        name="wkv7_chunked",
    )(r, k, v, kk, lw2, a2, ka)


def _mm_exact_rhs(a, b_bf16):
    ah = a.astype(bf16)
    al = (a - ah.astype(f32)).astype(bf16)
    d = lambda x: lax.dot_general(x, b_bf16, _DN['nn'], preferred_element_type=f32)
    return d(ah) + d(al)


def _head_sum_matrix(scale):
    i = lax.broadcasted_iota(jnp.int32, (RW_D, RW_D), 0) // RW_N
    j = lax.broadcasted_iota(jnp.int32, (RW_D, RW_D), 1) // RW_N
    return jnp.where(i == j, scale, 0.0).astype(bf16)


def _rw_prep_kernel(r_ref, k_ref, v_ref, rp_ref, kp_ref, vp_ref, rn_ref, kn_ref, vn_ref, w_ref, a_ref,
                    mu_ref, w0_ref, w2_ref, a0_ref, a2_ref, kkw_ref,
                    ro_ref, ko_ref, vo_ref, kko_ref, lw_ref, ao_ref, *, rows, nt):
    t = pl.program_id(1)
    has_prev = (t > 0).astype(f32)
    has_next = (t < nt - 1).astype(f32)
    ridx = lax.broadcasted_iota(jnp.int32, (rows, RW_D), 0)

    def tshift(cur_ref, prev_ref, next_ref, j):
        u = cur_ref[0]
        prev_row = prev_ref[0, SUBLANES - 1:SUBLANES, :] * has_prev
        next_row = next_ref[0, 0:1, :] * has_next
        up = jnp.where(ridx == 0, prev_row, pltpu.roll(u, 1, axis=0))
        un = jnp.where(ridx == rows - 1, next_row, pltpu.roll(u, rows - 1, axis=0))
        return u + mu_ref[j, 0:1, :] * (up - u) + mu_ref[j, 1:2, :] * (un - u)

    r = tshift(r_ref, rp_ref, rn_ref, 0)
    k = tshift(k_ref, kp_ref, kn_ref, 1)
    v = tshift(v_ref, vp_ref, vn_ref, 2)
    ro_ref[0] = r
    ko_ref[0] = k
    vo_ref[0] = v
    kk = k * kkw_ref[...]
    ss = _mm_exact_rhs(kk * kk, _head_sum_matrix(1.0))
    kko_ref[0] = kk * lax.rsqrt(jnp.maximum(ss, 1e-24))
    wl = jnp.tanh(w_ref[0])
    al = a_ref[0]
    for d in range(2):
        wlog = w0_ref[d:d + 1, :] + _mm(wl[:, d * RW_W_LORA:(d + 1) * RW_W_LORA], w2_ref[d])
        wlog = -jax.nn.softplus(-wlog) - 0.5
        lw_ref[0, :, d * RW_D:(d + 1) * RW_D] = -jnp.exp(wlog)
        av = a0_ref[d:d + 1, :] + _mm(al[:, d * RW_A_LORA:(d + 1) * RW_A_LORA], a2_ref[d])
        ao_ref[0, :, d * RW_D:(d + 1) * RW_D] = jax.nn.sigmoid(av)


def rw_prep(proj, lora_lane0, mu, w0, w2, a0, a2, k_k):
    Bn, L, _ = proj.shape
    C = RW_D
    rows = RW_ROWS_PER_STEP
    nt = L // rows
    hb = rows // SUBLANES
    lb = lora_lane0 // LANES
    cur = lambda j: pl.BlockSpec((1, rows, C), lambda b, t: (b, t, j))
    prev = lambda j: pl.BlockSpec((1, SUBLANES, C), lambda b, t: (b, jnp.maximum(t * hb - 1, 0), j))
    nxt = lambda j: pl.BlockSpec((1, SUBLANES, C), lambda b, t: (b, jnp.minimum((t + 1) * hb, L // SUBLANES - 1), j))
    lora_w = pl.BlockSpec((1, rows, LANES), lambda b, t: (b, t, lb + 1))
    lora_a = pl.BlockSpec((1, rows, LANES), lambda b, t: (b, t, lb + 2))
    full = lambda shp: pl.BlockSpec(shp, lambda b, t: (0,) * len(shp))
    out_c = pl.BlockSpec((1, rows, C), lambda b, t: (b, t, 0))
    out_2c = pl.BlockSpec((1, rows, 2 * C), lambda b, t: (b, t, 0))
    sds = lambda c: jax.ShapeDtypeStruct((Bn, L, c), f32)
    return pl.pallas_call(
        functools.partial(_rw_prep_kernel, rows=rows, nt=nt),
        grid=(Bn, nt),
        in_specs=[cur(0), cur(1), cur(2), prev(0), prev(1), prev(2), nxt(0), nxt(1), nxt(2), lora_w, lora_a,
                  full((3, 2, C)), full((2, C)), full((2, RW_W_LORA, C)), full((2, C)), full((2, RW_A_LORA, C)),
                  full((1, C))],
        out_specs=[out_c, out_c, out_c, out_c, out_2c, out_2c],
        out_shape=[sds(C), sds(C), sds(C), sds(C), sds(2 * C), sds(2 * C)],
        compiler_params=pltpu.CompilerParams(dimension_semantics=("parallel", "parallel"),
                                             vmem_limit_bytes=VMEM_LIMIT_BYTES),
        name="rwkv_prep",
    )(proj, proj, proj, proj, proj, proj, proj, proj, proj, proj, proj,
      mu.astype(f32), w0.astype(f32), w2.astype(bf16), a0.astype(f32), a2.astype(bf16), k_k.astype(f32).reshape(1, C))


def _rw_post_kernel(yf_ref, yb_ref, r_ref, k_ref, v_ref, a_ref, g_ref, ka_ref, rk_ref, lnw_ref, lnb_ref, g2_ref, o_ref):
    C = RW_D
    y = yf_ref[0] + yb_ref[0]
    avg = _head_sum_matrix(1.0 / RW_N)
    mean = _mm_exact_rhs(y, avg)
    yc = y - mean
    var = _mm_exact_rhs(yc * yc, avg)
    yn = yc * lax.rsqrt(var + RW_GN_EPS) * lnw_ref[...] + lnb_ref[...]
    a = a_ref[0]
    k = k_ref[0]
    ka = ka_ref[...]
    ksum = k * (1.0 + (a[:, :C] - 1.0) * ka) + k * (1.0 + (a[:, C:] - 1.0) * ka)
    coef = _mm_exact_rhs(r_ref[0] * ksum * rk_ref[...], _head_sum_matrix(1.0))
    gate = _mm(jax.nn.sigmoid(g_ref[0]), g2_ref[...])
    o_ref[0] = (yn + coef * v_ref[0]) * gate


def rw_post(y2, r, k, v, a2, proj, g_lane0, k_a, r_k, ln_w, ln_b, g2):
    Bn, L, C = r.shape
    rows = RW_ROWS_PER_STEP
    nt = L // rows
    gb = g_lane0 // LANES
    blk = lambda c: pl.BlockSpec((1, rows, c), lambda b, t: (b, t, 0))
    full = lambda shp: pl.BlockSpec(shp, lambda b, t: (0,) * len(shp))
    row = lambda x: x.astype(f32).reshape(1, C)
    return pl.pallas_call(
        _rw_post_kernel,
        grid=(Bn, nt),
        in_specs=[blk(C), pl.BlockSpec((1, rows, C), lambda b, t: (b + Bn, t, 0)), blk(C), blk(C), blk(C), blk(2 * C),
                  pl.BlockSpec((1, rows, LANES), lambda b, t: (b, t, gb)),
                  full((1, C)), full((1, C)), full((1, C)), full((1, C)), full((RW_G_LORA, C))],
        out_specs=blk(C),
        out_shape=jax.ShapeDtypeStruct((Bn, L, C), f32),
        compiler_params=pltpu.CompilerParams(dimension_semantics=("parallel", "parallel"),
                                             vmem_limit_bytes=VMEM_LIMIT_BYTES),
        name="rwkv_post",
    )(y2, y2, r, k, v, a2, proj, row(k_a), row(r_k), row(ln_w), row(ln_b), g2.astype(bf16))


def _na_kernel(q_ref, k_ref, v_ref, bias_ref, o_ref, kb_ref, vb_ref, *, rows_per_step, n_rows, kr):
    W = GRID_W
    rb = pl.program_id(2)

    @pl.when(rb == 0)
    def _():
        kb_ref[...] = k_ref[0].astype(bf16)
        vb_ref[...] = v_ref[0].astype(bf16)

    lo_lane = lax.broadcasted_iota(jnp.int32, (W, LANES), 1) < NA_HD
    scale = NA_HD ** -0.5
    rows = []
    for j in range(rows_per_step):
        r = rb * rows_per_step + j
        start = jnp.clip(r - kr // 2, 0, n_rows - kr)
        rows.append((start, start - r + (NA_KR - 1)))
    s_list = []
    for j, (start, didx) in enumerate(rows):
        q = q_ref[0, j * W:(j + 1) * W, :] * scale
        qs = jnp.concatenate([jnp.where(lo_lane, q, 0.0), jnp.where(lo_lane, 0.0, q)], axis=0)
        kw = kb_ref[pl.ds(pl.multiple_of(start * W, W), kr * W), :]
        s_list.append(_mm(qs, kw, 'nt') + bias_ref[didx, 0])
    p_list = []
    for s in s_list:
        m = jnp.max(s, axis=-1, keepdims=True)
        p = jnp.exp(s - m)
        p_list.append((p, jnp.sum(p, axis=-1, keepdims=True)))
    for j, ((start, _), (p, l)) in enumerate(zip(rows, p_list)):
        vw = vb_ref[pl.ds(pl.multiple_of(start * W, W), kr * W), :]
        o = _mm(p, vw) / l
        o_ref[0, j * W:(j + 1) * W, :] = jnp.where(lo_lane, o[:W], o[W:])


def na_bias_table(rpb, n_rows):
    W = GRID_W
    kr = min(NA_KR, n_rows)
    cols = jnp.arange(W)
    col_start = jnp.clip(cols - NA_KC // 2, 0, W - NA_KC)
    in_band = (cols[None, :] >= col_start[:, None]) & (cols[None, :] < col_start[:, None] + NA_KC)
    dc = jnp.clip(cols[None, :] - cols[:, None], -(NA_KC - 1), NA_KC - 1) + (NA_KC - 1)
    win = jnp.stack([rpb.astype(f32)[:, d:d + kr, :] for d in range(8)], axis=1)
    onehot = (dc[:, :, None] == jnp.arange(2 * NA_KC - 1)[None, None, :]).astype(f32)
    tab = jnp.einsum('hdic,qkc->hdqik', win, onehot, precision=lax.Precision.HIGHEST)
    tab = jnp.where(in_band[None, None, :, None, :], tab, NEG_INF)
    H = rpb.shape[0]
    tab = tab.reshape(H // 2, 2, 8, W, kr * W).transpose(2, 0, 1, 3, 4)
    return tab.reshape(8, H // 2, 2 * W, kr * W)


def neighborhood_attention_pallas(proj, rpb, lane_block0):
    Bn, L, _ = proj.shape
    W = GRID_W
    n_rows = L // W
    kr = min(NA_KR, n_rows)
    hp = NA_H * NA_HD // LANES
    rps = NA_ROWS_PER_STEP
    bias = na_bias_table(rpb, n_rows)
    kern = functools.partial(_na_kernel, rows_per_step=rps, n_rows=n_rows, kr=kr)
    return pl.pallas_call(
        kern,
        grid=(Bn, hp, n_rows // rps),
        in_specs=[pl.BlockSpec((1, rps * W, LANES), lambda b, h, r: (b, r, lane_block0 + h)),
                  pl.BlockSpec((1, L, LANES), lambda b, h, r: (b, 0, lane_block0 + hp + h)),
                  pl.BlockSpec((1, L, LANES), lambda b, h, r: (b, 0, lane_block0 + 2 * hp + h)),
                  pl.BlockSpec((8, 1, 2 * W, kr * W), lambda b, h, r: (0, h, 0, 0))],
        out_specs=pl.BlockSpec((1, rps * W, LANES), lambda b, h, r: (b, r, h)),
        out_shape=jax.ShapeDtypeStruct((Bn, L, NA_H * NA_HD), f32),
        scratch_shapes=[pltpu.VMEM((L, LANES), bf16), pltpu.VMEM((L, LANES), bf16)],
        compiler_params=pltpu.CompilerParams(dimension_semantics=("parallel", "parallel", "arbitrary"),
                                             vmem_limit_bytes=VMEM_LIMIT_BYTES),
        name="neighborhood_attention",
    )(proj, proj, proj, bias)


def _moe_ffn_kernel(be_ref, x_ref, w1_ref, w3_ref, w2_ref, o_ref, w1b_ref, w3b_ref, w2b_ref):
    i = pl.program_id(0)

    @pl.when((i == 0) | (be_ref[i] != be_ref[jnp.maximum(i - 1, 0)]))
    def _():
        w1b_ref[...] = w1_ref[0, 0].astype(bf16)
        w3b_ref[...] = w3_ref[0, 0].astype(bf16)
        w2b_ref[...] = w2_ref[0, 0].astype(bf16)

    x = x_ref[...].astype(bf16)
    h1 = jnp.dot(x, w1b_ref[...], preferred_element_type=f32)
    h3 = jnp.dot(x, w3b_ref[...], preferred_element_type=f32)
    g = h1 * jax.nn.sigmoid(h1) * h3
    o_ref[...] = jnp.dot(g.astype(bf16), w2b_ref[...], preferred_element_type=f32)


def moe_expert_ffn(xs, block_e, w1, w3, w2, layer):
    rows, Dm = xs.shape
    n_blocks = rows // MOE_BLOCK
    grid_spec = pltpu.PrefetchScalarGridSpec(
        num_scalar_prefetch=1,
        grid=(n_blocks,),
        in_specs=[pl.BlockSpec((MOE_BLOCK, Dm), lambda i, be: (i, 0)),
                  pl.BlockSpec((1, 1, Dm, MOE_FF), lambda i, be: (layer, be[i], 0, 0)),
                  pl.BlockSpec((1, 1, Dm, MOE_FF), lambda i, be: (layer, be[i], 0, 0)),
                  pl.BlockSpec((1, 1, MOE_FF, Dm), lambda i, be: (layer, be[i], 0, 0))],
        out_specs=pl.BlockSpec((MOE_BLOCK, Dm), lambda i, be: (i, 0)),
        scratch_shapes=[pltpu.VMEM((Dm, MOE_FF), bf16), pltpu.VMEM((Dm, MOE_FF), bf16), pltpu.VMEM((MOE_FF, Dm), bf16)],
    )
    return pl.pallas_call(
        _moe_ffn_kernel,
        grid_spec=grid_spec,
        out_shape=jax.ShapeDtypeStruct((rows, Dm), f32),
        compiler_params=pltpu.CompilerParams(dimension_semantics=("arbitrary",), vmem_limit_bytes=VMEM_LIMIT_BYTES),
        name="moe_expert_ffn",
    )(block_e.astype(jnp.int32), xs, w1.astype(f32), w3.astype(f32), w2.astype(f32))


def _rms(x):
    return x * lax.rsqrt(jnp.mean(x * x, axis=-1, keepdims=True) + NORM_EPS)


def _norm_proj_kernel(x_ref, g_ref, w_ref, o_ref):
    h = (_rms(x_ref[...]) * g_ref[...]).astype(bf16)
    o_ref[...] = jnp.dot(h, w_ref[...], preferred_element_type=f32)


def norm_proj(x2, g, w):
    N, D = x2.shape
    F = w.shape[1]
    tm = PROJ_ROWS_PER_STEP
    return pl.pallas_call(
        _norm_proj_kernel,
        grid=(N // tm,),
        in_specs=[pl.BlockSpec((tm, D), lambda i: (i, 0)), pl.BlockSpec((1, D), lambda i: (0, 0)),
                  pl.BlockSpec((D, F), lambda i: (0, 0))],
        out_specs=pl.BlockSpec((tm, F), lambda i: (i, 0)),
        out_shape=jax.ShapeDtypeStruct((N, F), f32),
        compiler_params=pltpu.CompilerParams(dimension_semantics=("parallel",), vmem_limit_bytes=VMEM_LIMIT_BYTES),
        name="norm_proj",
    )(x2, g.astype(f32).reshape(1, D), w.astype(bf16))


def _out_proj_kernel(x_ref, yh_ref, yr_ref, yn_ref, wh_ref, wr_ref, wn_ref, g_ref, wrt_ref, xo_ref, h_ref, lg_ref):
    mix = (jnp.dot(yh_ref[...].astype(bf16), wh_ref[...], preferred_element_type=f32)
           + jnp.dot(yr_ref[...].astype(bf16), wr_ref[...], preferred_element_type=f32)
           + jnp.dot(yn_ref[...].astype(bf16), wn_ref[...], preferred_element_type=f32))
    x = x_ref[...] + mix
    xo_ref[...] = x
    h = (_rms(x) * g_ref[...]).astype(bf16)
    h_ref[...] = h
    lg_ref[...] = jnp.dot(h, wrt_ref[...], preferred_element_type=f32)


def out_proj_norm_router(x2, y_hy, y_rw, y_na, w_out, g, w_router):
    N, D = x2.shape
    tm = PROJ_ROWS_PER_STEP
    d_hy, d_rw, d_na = y_hy.shape[1], y_rw.shape[1], y_na.shape[1]
    nr = w_router.shape[1]
    w_router = jnp.pad(w_router.astype(bf16), ((0, 0), (0, LANES - nr)))
    wb = w_out.astype(bf16)
    row = lambda c: pl.BlockSpec((tm, c), lambda i: (i, 0))
    full = lambda r, c: pl.BlockSpec((r, c), lambda i: (0, 0))
    return pl.pallas_call(
        _out_proj_kernel,
        grid=(N // tm,),
        in_specs=[row(D), row(d_hy), row(d_rw), row(d_na), full(d_hy, D), full(d_rw, D), full(d_na, D), full(1, D),
                  full(D, LANES)],
        out_specs=[row(D), row(D), row(LANES)],
        out_shape=[jax.ShapeDtypeStruct((N, D), f32), jax.ShapeDtypeStruct((N, D), bf16),
                   jax.ShapeDtypeStruct((N, LANES), f32)],
        compiler_params=pltpu.CompilerParams(dimension_semantics=("parallel",), vmem_limit_bytes=VMEM_LIMIT_BYTES),
        name="out_proj_norm_router",
    )(x2, y_hy, y_rw, y_na, wb[:d_hy], wb[d_hy:d_hy + d_rw], wb[d_hy + d_rw:], g.astype(f32).reshape(1, D), w_router)


def _final_norm_kernel(x_ref, g_ref, o_ref):
    x = x_ref[...]
    o_ref[...] = x * lax.rsqrt(jnp.mean(x * x, axis=-1, keepdims=True) + NORM_EPS) * g_ref[...]


def final_rmsnorm(x, g):
    Bn, L, D = x.shape
    tm = 1024
    out = pl.pallas_call(
        _final_norm_kernel,
        grid=(Bn * L // tm,),
        in_specs=[pl.BlockSpec((tm, D), lambda i: (i, 0)), pl.BlockSpec((1, D), lambda i: (0, 0))],
        out_specs=pl.BlockSpec((tm, D), lambda i: (i, 0)),
        out_shape=jax.ShapeDtypeStruct((Bn * L, D), x.dtype),
        name="final_rmsnorm",
    )(x.reshape(Bn * L, D), g.reshape(1, D))
    return out.reshape(Bn, L, D)


def hyena_positional_features(L):
    t = jnp.linspace(0.0, 1.0, L, dtype=f32)[:, None]
    w = (2.0 * math.pi / L) * jnp.arange(L, dtype=f32)[:, None]
    f = jnp.linspace(1e-4, HY_BANDS - 1, HY_BANDS, dtype=f32)[None, :]
    z = jnp.concatenate([t, jnp.cos(f * w), -jnp.sin(f * w)], axis=-1)
    return z, t


def hyena_filters(z, t, w1, b1, w2, b2, w3, b3, wout, freq):
    fr = freq.astype(f32)
    act = lambda u: jnp.sin(fr * u)
    h = act(z @ w1.astype(f32) + b1.astype(f32))
    h = act(h @ w2.astype(f32) + b2.astype(f32))
    h = act(h @ w3.astype(f32) + b3.astype(f32))
    h = (h @ wout.astype(f32)).reshape(-1, 2, HY_D)
    deltas = jnp.abs(jnp.linspace(HY_MIN_DECAY, HY_MAX_DECAY, HY_D, dtype=f32))
    h = h * jnp.exp(-t[:, :, None] * deltas)
    return h * lax.rsqrt(jnp.sum(h * h, axis=(0, 1), keepdims=True) + 1e-6)


def _split_bf16(x):
    hi = x.astype(bf16)
    return hi, (x - hi.astype(f32)).astype(bf16)


def _dot3(m_hi, m_lo, x_hi, x_lo):
    d = lambda a, b: lax.dot_general(a, b, _DN['nn'], preferred_element_type=f32)
    return d(m_hi, x_hi) + (d(m_hi, x_lo) + d(m_lo, x_hi))


def _cmatmul(mr, mi, xr, xi):
    xrh, xrl = _split_bf16(xr)
    rr = _dot3(mr[0], mr[1], xrh, xrl)
    ir = _dot3(mi[0], mi[1], xrh, xrl)
    if xi is None:
        return rr, ir
    xih, xil = _split_bf16(xi)
    ii = _dot3(mi[0], mi[1], xih, xil)
    ri = _dot3(mr[0], mr[1], xih, xil)
    return rr - ii, ri + ir


def _row_dft_kernel(mrh_ref, mrl_ref, mih_ref, mil_ref, *refs, real_input):
    if real_input:
        ur_ref, or_ref, oi_ref = refs
        ui = None
    else:
        ur_ref, ui_ref, or_ref, oi_ref = refs
        ui = ui_ref[0]
    o_r, o_i = _cmatmul((mrh_ref[...], mrl_ref[...]), (mih_ref[...], mil_ref[...]), ur_ref[0], ui)
    or_ref[0] = o_r
    oi_ref[0] = o_i


def row_dft(tabs, u, packed):
    rows, r_in, W = u.shape
    P = rows // 2 if packed else rows
    r_out = tabs[0].shape[0]
    wb = min(FFT_LANE_BLOCK, W)
    tab_spec = pl.BlockSpec((r_out, r_in), lambda p, j: (0, 0))
    re_spec = pl.BlockSpec((1, r_in, wb), lambda p, j: (p, 0, j))
    im_spec = pl.BlockSpec((1, r_in, wb), lambda p, j: (p + P, 0, j))
    out_spec = pl.BlockSpec((1, r_out, wb), lambda p, j: (p, 0, j))
    ins = [u, u] if packed else [u]
    return pl.pallas_call(
        functools.partial(_row_dft_kernel, real_input=not packed),
        grid=(P, W // wb),
        in_specs=[tab_spec] * 4 + ([re_spec, im_spec] if packed else [re_spec]),
        out_specs=[out_spec, out_spec],
        out_shape=[jax.ShapeDtypeStruct((P, r_out, W), f32)] * 2,
        compiler_params=pltpu.CompilerParams(dimension_semantics=("parallel", "parallel"),
                                             vmem_limit_bytes=VMEM_LIMIT_BYTES),
        name="hyena_row_dft",
    )(*tabs, *ins)


def _row_idft_gate_kernel(mrh_ref, mrl_ref, mih_ref, mil_ref, dr_ref, di_ref, z0_ref, z1_ref, x0_ref, x1_ref,
                          skip_ref, o0_ref, o1_ref):
    y_r, y_i = _cmatmul((mrh_ref[...], mrl_ref[...]), (mih_ref[...], mil_ref[...]), dr_ref[0], di_ref[0])
    skip = skip_ref[...]
    o0_ref[0] = x0_ref[0] * (y_r + z0_ref[0] * skip)
    o1_ref[0] = x1_ref[0] * (y_i + z1_ref[0] * skip)


def row_idft_gate(tabs, dr, di, z, x0, skip_row):
    P, r_in, W = dr.shape
    r_out = tabs[0].shape[0]
    wb = min(FFT_LANE_BLOCK, W)
    tab_spec = pl.BlockSpec((r_out, r_in), lambda p, j: (0, 0))
    d_spec = pl.BlockSpec((1, r_in, wb), lambda p, j: (p, 0, j))
    lo = pl.BlockSpec((1, r_out, wb), lambda p, j: (p, 0, j))
    hi = pl.BlockSpec((1, r_out, wb), lambda p, j: (p + P, 0, j))
    o0, o1 = pl.pallas_call(
        _row_idft_gate_kernel,
        grid=(P, W // wb),
        in_specs=[tab_spec] * 4 + [d_spec, d_spec, lo, hi, lo, hi, pl.BlockSpec((1, wb), lambda p, j: (0, j))],
        out_specs=[lo, lo],
        out_shape=[jax.ShapeDtypeStruct((P, r_out, W), f32)] * 2,
        compiler_params=pltpu.CompilerParams(dimension_semantics=("parallel", "parallel"),
                                             vmem_limit_bytes=VMEM_LIMIT_BYTES),
        name="hyena_row_idft_gate",
    )(*tabs, dr, di, z, z, x0, x0, skip_row)
    return jnp.concatenate([o0, o1], axis=0)


def _col_dft_kernel(grh_ref, grl_ref, gih_ref, gil_ref, ar_ref, ai_ref, br_ref, bi_ref, *, k1_per_step):
    for j in range(k1_per_step):
        b_r, b_i = _cmatmul((grh_ref[j], grl_ref[j]), (gih_ref[j], gil_ref[j]), ar_ref[0, j], ai_ref[0, j])
        br_ref[0, j] = b_r
        bi_ref[0, j] = b_i


def _col_conv_kernel(grh_ref, grl_ref, gih_ref, gil_ref, trh_ref, trl_ref, tih_ref, til_ref,
                     ar_ref, ai_ref, kr_ref, ki_ref, dr_ref, di_ref, *, k1_per_step):
    for j in range(k1_per_step):
        b_r, b_i = _cmatmul((grh_ref[j], grl_ref[j]), (gih_ref[j], gil_ref[j]), ar_ref[0, j], ai_ref[0, j])
        k_r = kr_ref[0, j]
        k_i = ki_ref[0, j]
        c_r = b_r * k_r - b_i * k_i
        c_i = b_r * k_i + b_i * k_r
        d_r, d_i = _cmatmul((trh_ref[j], trl_ref[j]), (tih_ref[j], til_ref[j]), c_r, c_i)
        dr_ref[0, j] = d_r
        di_ref[0, j] = d_i


def col_stage(g_tabs, gt_tabs, ar, ai, kr=None, ki=None):
    P, n1, n2, C = ar.shape
    kb = min(FFT_K1_PER_STEP, n1)
    g_spec = pl.BlockSpec((kb, n2, n2), lambda p, j: (j, 0, 0))
    a_spec = pl.BlockSpec((1, kb, n2, C), lambda p, j: (p, j, 0, 0))
    k_spec = pl.BlockSpec((1, kb, n2, C), lambda p, j: (0, j, 0, 0))
    common = dict(
        grid=(P, n1 // kb),
        out_specs=[a_spec, a_spec],
        out_shape=[jax.ShapeDtypeStruct((P, n1, n2, C), f32)] * 2,
        compiler_params=pltpu.CompilerParams(dimension_semantics=("parallel", "parallel"),
                                             vmem_limit_bytes=VMEM_LIMIT_BYTES),
    )
    if kr is None:
        return pl.pallas_call(functools.partial(_col_dft_kernel, k1_per_step=kb),
                              in_specs=[g_spec] * 4 + [a_spec] * 2, name="hyena_col_dft", **common)(*g_tabs, ar, ai)
    return pl.pallas_call(functools.partial(_col_conv_kernel, k1_per_step=kb),
                          in_specs=[g_spec] * 8 + [a_spec] * 2 + [k_spec] * 2, name="hyena_col_conv", **common)(
        *g_tabs, *gt_tabs, ar, ai, kr, ki)


def _bf16_tables(m):
    out = []
    for part in (np.real(m), np.imag(m)):
        x = jnp.asarray(part, f32)
        hi = x.astype(bf16)
        out += [hi, (x - hi.astype(f32)).astype(bf16)]
    return out


def hyena_dft_tables(L):
    n2 = FFT_N2
    n1 = 2 * L // n2
    N = n1 * n2
    a = np.arange(n1)
    fa = np.exp(-2j * np.pi * np.outer(a, a) / n1)
    k1 = np.arange(n1)[:, None, None]
    k2 = np.arange(n2)[None, :, None]
    nn = np.arange(n2)[None, None, :]
    g = np.exp(-2j * np.pi * nn * (k1 + n1 * k2) / N)
    gt = np.conj(np.transpose(g, (0, 2, 1)))
    fc = np.conj(fa).T / N
    return dict(fa=_bf16_tables(fa), fa_half=_bf16_tables(fa[:, :n1 // 2]), g=_bf16_tables(g), gt=_bf16_tables(gt),
                fc_half=_bf16_tables(fc[:n1 // 2]))


def hyena_long_conv_gate(tabs, z, x0, k2, skip):
    Bn, L, C = z.shape
    n2 = FFT_N2
    n1 = 2 * L // n2
    P = Bn // 2
    W = n2 * C
    ar, ai = row_dft(tabs['fa'], k2.reshape(1, n1, W), packed=False)
    kr, ki = col_stage(tabs['g'], tabs['gt'], ar.reshape(1, n1, n2, C), ai.reshape(1, n1, n2, C))
    zv = z.reshape(Bn, n1 // 2, W)
    ar, ai = row_dft(tabs['fa_half'], zv, packed=True)
    dr, di = col_stage(tabs['g'], tabs['gt'], ar.reshape(P, n1, n2, C), ai.reshape(P, n1, n2, C), kr, ki)
    out = row_idft_gate(tabs['fc_half'], dr.reshape(P, n1, W), di.reshape(P, n1, W), zv, x0.reshape(Bn, n1 // 2, W),
                        jnp.tile(skip.astype(f32), n2).reshape(1, W))
    return out.reshape(Bn, L, C)


def _hyena_prep_kernel(*refs, rows, nt):
    cur, prev, nxt = refs[0:3], refs[3:6], refs[6:9]
    w_ref, b_ref, z_ref, x0_ref = refs[9:]
    t = pl.program_id(1)
    has_prev = (t > 0).astype(f32)
    has_next = (t < nt - 1).astype(f32)
    ridx = lax.broadcasted_iota(jnp.int32, (rows, HY_D), 0)

    def conv(j):
        u = cur[j][0]
        prev_row = prev[j][0, SUBLANES - 1:SUBLANES, :] * has_prev
        next_row = nxt[j][0, 0:1, :] * has_next
        up = jnp.where(ridx == 0, prev_row, pltpu.roll(u, 1, axis=0))
        un = jnp.where(ridx == rows - 1, next_row, pltpu.roll(u, rows - 1, axis=0))
        sl = slice(j * HY_D, (j + 1) * HY_D)
        return up * w_ref[0:1, sl] + u * w_ref[1:2, sl] + un * w_ref[2:3, sl] + b_ref[:, sl]

    x0_ref[0] = conv(0)
    z_ref[0] = conv(2) * conv(1)


def hyena_prep(proj, lane0, conv_w, conv_b):
    Bn, L, _ = proj.shape
    rows = RW_ROWS_PER_STEP
    nt = L // rows
    hb = rows // SUBLANES
    b0 = lane0 // HY_D
    assert b0 * HY_D == lane0
    cur = lambda j: pl.BlockSpec((1, rows, HY_D), lambda b, t: (b, t, b0 + j))
    prev = lambda j: pl.BlockSpec((1, SUBLANES, HY_D), lambda b, t: (b, jnp.maximum(t * hb - 1, 0), b0 + j))
    nxt = lambda j: pl.BlockSpec((1, SUBLANES, HY_D), lambda b, t: (b, jnp.minimum((t + 1) * hb, L // SUBLANES - 1), b0 + j))
    full = lambda shp: pl.BlockSpec(shp, lambda b, t: (0,) * len(shp))
    out = pl.BlockSpec((1, rows, HY_D), lambda b, t: (b, t, 0))
    return pl.pallas_call(
        functools.partial(_hyena_prep_kernel, rows=rows, nt=nt),
        grid=(Bn, nt),
        in_specs=[cur(0), cur(1), cur(2), prev(0), prev(1), prev(2), nxt(0), nxt(1), nxt(2),
                  full((3, 3 * HY_D)), full((1, 3 * HY_D))],
        out_specs=[out, out],
        out_shape=[jax.ShapeDtypeStruct((Bn, L, HY_D), f32)] * 2,
        compiler_params=pltpu.CompilerParams(dimension_semantics=("parallel", "parallel")),
        name="hyena_prep",
    )(*([proj] * 9), conv_w.astype(f32), conv_b.astype(f32).reshape(1, 3 * HY_D))


def hyena_mixer(tabs, proj, lane0, conv_w, conv_b, filt, skip):
    z, x0 = hyena_prep(proj, lane0, conv_w, conv_b)
    k2 = jnp.concatenate([filt[:1, 0] + filt[:1, 1], filt[1:, 0],
                          jnp.zeros((1, HY_D), f32), filt[1:, 1][::-1]], axis=0)
    return hyena_long_conv_gate(tabs, z, x0, k2, skip)


def rwkv7_mixer(proj, lora_lane0, mu, w0, w2, a0, a2, k_k, k_a, r_k, g2, ln_w, ln_b):
    r, k, v, kk, lw2, a_both = rw_prep(proj, lora_lane0, mu, w0, w2, a0, a2, k_k)
    y2 = wkv7_chunked(r, k, v, kk, lw2, a_both, k_a.astype(f32).reshape(1, RW_D))
    return rw_post(y2, r, k, v, a_both, proj, lora_lane0, k_a, r_k, ln_w, ln_b, g2)


def hier_moe(h, logits, bg, be, w1, w3, w2, layer):
    N, Dm = h.shape
    assert MOE_TOPK == 2
    g_logits = logits[:, :MOE_GROUPS] + bg.astype(f32)
    g_sel = jnp.argmax(g_logits, axis=-1)
    g_prob = jnp.take_along_axis(jax.nn.softmax(g_logits, axis=-1), g_sel[:, None], axis=-1)
    e_logits = (logits[:, MOE_GROUPS:MOE_GROUPS + MOE_EXPERTS] + be.astype(f32)).reshape(N, MOE_GROUPS, MOE_PER_GROUP)
    e_logits = jnp.take_along_axis(e_logits, g_sel[:, None, None], axis=1)[:, 0]
    top_val, top_idx = lax.top_k(e_logits, MOE_TOPK)
    gate = g_prob * jax.nn.softmax(top_val, axis=-1)
    expert = g_sel[:, None] * MOE_PER_GROUP + top_idx
    M = N * MOE_TOPK
    flat_e = expert.reshape(M).astype(jnp.int32)
    experts = jnp.arange(MOE_EXPERTS, dtype=jnp.int32)
    counts = jnp.sum((flat_e[:, None] == experts[None, :]).astype(jnp.int32), axis=0)
    padded = (counts + MOE_BLOCK - 1) // MOE_BLOCK * MOE_BLOCK
    pad_end = jnp.cumsum(padded)
    n_blocks = -(-M // MOE_BLOCK) + MOE_EXPERTS
    n_rows = n_blocks * MOE_BLOCK
    cum_need = jnp.cumsum(padded - counts)
    filler = jnp.arange(n_rows - M, dtype=jnp.int32)
    filler_e = jnp.sum((cum_need[None, :] <= filler[:, None]).astype(jnp.int32), axis=1)
    keys = jnp.concatenate([2 * flat_e, 2 * filler_e + 1])
    ids = jnp.concatenate([jnp.arange(M, dtype=jnp.int32), jnp.full((n_rows - M,), M, jnp.int32)])
    _, slot_src = lax.sort((keys, ids), num_keys=1)
    tok_src = jnp.where(slot_src < M, slot_src // MOE_TOPK, 0)
    block_start = jnp.arange(n_blocks, dtype=pad_end.dtype) * MOE_BLOCK
    block_e = jnp.minimum(jnp.sum(pad_end[None, :] <= block_start[:, None], axis=1), MOE_EXPERTS - 1)
    y = moe_expert_ffn(h[tok_src], block_e, w1, w3, w2, layer)
    _, row_of = lax.sort((slot_src, jnp.arange(n_rows, dtype=jnp.int32)), num_keys=1)
    pos = row_of[:M].reshape(N, MOE_TOPK)
    return gate[:, 0:1] * y[pos[:, 0]] + gate[:, 1:2] * y[pos[:, 1]]


def kernel(x, norm1_g, w_in, hy_conv_w, hy_conv_b, hy_w1, hy_b1, hy_w2, hy_b2, hy_w3, hy_b3, hy_wout, hy_freq, hy_skip, rw_mu, rw_w0, rw_w2, rw_a0, rw_a2, rw_kk, rw_ka, rw_rk, rw_g2, rw_ln_w, rw_ln_b, na_rpb, w_out, norm2_g, moe_wg, moe_bg, moe_we, moe_be, moe_w1, moe_w3, moe_w2, norm_f_g):
    Bn, L, _ = x.shape
    z_pos, t_pos = hyena_positional_features(L)
    dft_tabs = hyena_dft_tables(L)
    splits = np.cumsum(IN_SIZES)[:-1].tolist()
    hy_end, rkv_end = splits[0], splits[1]
    hy_lane0 = rkv_end - hy_end
    lora_lane0, na_lane0 = splits[1], splits[4]
    assert RW_D % LANES == 0 and lora_lane0 % LANES == 0 and na_lane0 % LANES == 0
    N = Bn * L
    x2 = x.reshape(N, D_MODEL)
    for l in range(DEPTH):
        w_in_l = jnp.concatenate([w_in[l][:, hy_end:rkv_end], w_in[l][:, :hy_end], w_in[l][:, rkv_end:]], axis=1)
        proj = norm_proj(x2, norm1_g[l], w_in_l).reshape(Bn, L, IN_D)
        filt = hyena_filters(z_pos, t_pos, hy_w1[l], hy_b1[l], hy_w2[l], hy_b2[l],
                             hy_w3[l], hy_b3[l], hy_wout[l], hy_freq[l])
        y_hy = hyena_mixer(dft_tabs, proj, hy_lane0, hy_conv_w[l], hy_conv_b[l], filt, hy_skip[l])
        y_rw = rwkv7_mixer(proj, lora_lane0, rw_mu[l], rw_w0[l], rw_w2[l], rw_a0[l],
                           rw_a2[l], rw_kk[l], rw_ka[l], rw_rk[l].reshape(RW_D), rw_g2[l], rw_ln_w[l], rw_ln_b[l])
        y_na = neighborhood_attention_pallas(proj, na_rpb[l], na_lane0 // LANES)
        x2, h2, logits = out_proj_norm_router(
            x2, y_hy.reshape(N, HY_D), y_rw.reshape(N, RW_D), y_na.reshape(N, NA_D), w_out[l], norm2_g[l],
            jnp.concatenate([moe_wg[l], moe_we[l]], axis=1))
        x2 = x2 + hier_moe(h2, logits, moe_bg[l], moe_be[l], moe_w1, moe_w3, moe_w2, l)
    return final_rmsnorm(x2.reshape(Bn, L, D_MODEL), norm_f_g)
```

```python
import functools
import math

import jax
import jax.numpy as jnp
import numpy as np
from jax import lax
from jax.experimental import pallas as pl
from jax.experimental.pallas import tpu as pltpu

f32 = jnp.float32
bf16 = jnp.bfloat16

D_MODEL = 1024
DEPTH = 2
GRID_W = 64
NORM_EPS = 1e-6
NEG_INF = -1e30

HY_D = D_MODEL // 4
HY_EMB = 33
HY_BANDS = (HY_EMB - 1) // 2
HY_FFN = 64
HY_MIN_DECAY = math.log(1e-2) / 1.5
HY_MAX_DECAY = math.log(1e-2) / 0.3

RW_N = 64
RW_D = D_MODEL // 2
RW_H = RW_D // RW_N
RW_W_LORA = 64
RW_A_LORA = 64
RW_G_LORA = 128
RW_GN_EPS = 64e-5

NA_HD = 64
NA_D = D_MODEL // 4
NA_H = NA_D // NA_HD
NA_KR = 8
NA_KC = 16

MIX_D = HY_D + RW_D + NA_D
IN_SIZES = (3 * HY_D, 3 * RW_D, RW_G_LORA, 2 * RW_W_LORA, 2 * RW_A_LORA, 3 * NA_D)
IN_D = sum(IN_SIZES)

MOE_GROUPS = 4
MOE_PER_GROUP = 8
MOE_EXPERTS = MOE_GROUPS * MOE_PER_GROUP
MOE_TOPK = 2
MOE_FF = 512
MOE_BLOCK = 256

LANES = 128
WKV_CHUNK = 64
WKV_TIME_BLOCK = 512
NA_ROWS_PER_STEP = 4
RW_ROWS_PER_STEP = 512
SUBLANES = 8
PROJ_ROWS_PER_STEP = 512
FFT_N2 = 128
FFT_LANE_BLOCK = 4096
FFT_K1_PER_STEP = 4
VMEM_LIMIT_BYTES = 48 * 1024 * 1024

_DN = {'nn': (((1,), (0,)), ((), ())), 'nt': (((1,), (1,)), ((), ())), 'tn': (((0,), (0,)), ((), ()))}


def _mm(a, b, dims='nn'):
    return lax.dot_general(a.astype(bf16), b.astype(bf16), _DN[dims], preferred_element_type=f32)


def _wkv_kernel(r_ref, k_ref, v_ref, kk_ref, lw_ref, a_ref, ka_ref, y_ref, ht_ref, *, tb, batch):
    T = WKV_CHUNK
    H2 = 2 * T
    nc = tb // T
    npairs = r_ref.shape[-1] // LANES
    mm = _mm

    @pl.when(pl.program_id(1) == 0)
    def _():
        ht_ref[...] = jnp.zeros_like(ht_ref)

    d = pl.program_id(0) // batch
    sign = 1 - 2 * d
    trow = lax.broadcasted_iota(jnp.int32, (T, H2), 0)
    lane = lax.broadcasted_iota(jnp.int32, (T, H2), 1)
    scol = lane % T
    tdiff = (trow - scol) * sign
    strict = tdiff > 0
    incl = tdiff >= 0
    same16 = (trow // 16) == (scol // 16)
    same32 = (trow // 32) == (scol // 32)
    off32 = same32 & jnp.logical_not(same16)
    off64 = jnp.logical_not(same32)
    eye2 = jnp.where(trow == scol, 1.0, 0.0).astype(f32)
    lo_lane = lane < T
    brow = lax.broadcasted_iota(jnp.int32, (H2, H2), 0)
    bcol = lax.broadcasted_iota(jnp.int32, (H2, H2), 1)
    same_head = (brow // T) == (bcol // T)
    crow = lax.broadcasted_iota(jnp.int32, (T, T), 0)
    ccol = lax.broadcasted_iota(jnp.int32, (T, T), 1)
    tri = jnp.where((crow - ccol) * sign >= 0, 1.0, 0.0).astype(bf16)
    is_bwd = d == 1

    def bd(x):
        return jnp.concatenate([jnp.where(lo_lane, x, 0.0), jnp.where(lo_lane, 0.0, x)], axis=0)

    inst = [(s, p) for s in range(nc) for p in range(npairs)]
    offs = [pl.multiple_of((s + d * (nc - 1 - 2 * s)) * T, T) for s in range(nc)]

    def load(ref, s, p):
        return ref[0, pl.ds(offs[s], T), p * LANES:(p + 1) * LANES]

    cs_l = []
    for s, p in inst:
        lw = load(lw_ref, s, p)
        l1 = lw.astype(bf16)
        r1 = lw - l1.astype(f32)
        l2 = r1.astype(bf16)
        l3 = (r1 - l2.astype(f32)).astype(bf16)
        dd = lambda x: lax.dot_general(tri, x, _DN['nn'], preferred_element_type=f32)
        cs_l.append((dd(l1) + (dd(l2) + dd(l3)), lw))
    ops = []
    for (s, p), (cs, lw) in zip(inst, cs_l):
        r = load(r_ref, s, p)
        k = load(k_ref, s, p)
        v = load(v_ref, s, p)
        kk = load(kk_ref, s, p)
        a = load(a_ref, s, p)
        ka = ka_ref[:, p * LANES:(p + 1) * LANES]
        kd = k * (1.0 + (a - 1.0) * ka)
        b = kk * a
        cs_end = jnp.where(is_bwd, cs[0:1, :], cs[T - 1:T, :])
        em = jnp.exp(-cs)
        e_end = jnp.exp(cs_end - cs)
        ops.append(dict(
            AR=jnp.concatenate([-kk * jnp.exp(cs - lw), r * jnp.exp(cs)], axis=0),
            BK=jnp.concatenate([bd(b * em), bd(kd * em)], axis=0),
            V=v, BKg=jnp.concatenate([b * e_end, kd * e_end], axis=0), g_end=jnp.exp(cs_end)))
    for o in ops:
        S = mm(o['AR'], o['BK'], 'nt')
        N = jnp.where(strict, S[:T, :H2], 0.0)
        o['AakArk'] = jnp.concatenate([jnp.where(strict, S[:T, H2:], 0.0), jnp.where(incl, S[T:, H2:], 0.0)], axis=0)
        o['Arb'] = jnp.where(incl, S[T:, :H2], 0.0)
        o['Nd'] = jnp.where(same16, N, 0.0)
        o['N32'] = jnp.where(off32, N, 0.0)
        o['N64'] = jnp.where(off64, N, 0.0)
    for o in ops:
        o['X'] = eye2 + o['Nd']
        o['P'] = mm(o['Nd'], bd(o['Nd']))
    for it in range(3):
        if it < 2:
            for o in ops:
                px = mm(o['P'], jnp.concatenate([bd(o['X']), bd(o['P'])], axis=1))
                o['X'] = o['X'] + px[:, :H2]
                o['P'] = px[:, H2:]
        else:
            for o in ops:
                o['X'] = o['X'] + mm(o['P'], bd(o['X']))
    for key in ('N32', 'N64'):
        for o in ops:
            o['Z'] = mm(o[key], bd(o['X']))
        for o in ops:
            o['X'] = o['X'] + mm(o['X'], bd(o['Z']))
    for o in ops:
        wy = mm(o['AakArk'], bd(o['V']))
        o['W0'] = wy[:T]
        o['Yv'] = wy[T:]
    hts = [ht_ref[p] for p in range(npairs)]
    for s in range(nc):
        cur = [ops[s * npairs + p] for p in range(npairs)]
        arh = [mm(o['AR'], hts[p], 'nt') for p, o in enumerate(cur)]
        us = [mm(o['X'], bd(o['W0'] + arh[p][:T])) for p, o in enumerate(cur)]
        ys = [arh[p][T:] + (mm(o['Arb'], bd(us[p])) + o['Yv']) for p, o in enumerate(cur)]
        upd = [mm(jnp.concatenate([us[p], o['V']], axis=0), o['BKg'], 'tn') for p, o in enumerate(cur)]
        hts = [jnp.where(same_head, hts[p] * o['g_end'] + upd[p], 0.0) for p, o in enumerate(cur)]
        for p in range(npairs):
            y_ref[0, pl.ds(offs[s], T), p * LANES:(p + 1) * LANES] = ys[p]
    for p in range(npairs):
        ht_ref[p] = hts[p]


def wkv7_chunked(r, k, v, kk, lw2, a2, ka):
    Bn, L, C = r.shape
    tb = WKV_TIME_BLOCK
    nt = L // tb
    tmap = lambda i, t: t + (i // Bn) * (nt - 1 - 2 * t)
    shared = pl.BlockSpec((1, tb, C), lambda i, t: (i % Bn, tmap(i, t), 0))
    per_dir = pl.BlockSpec((1, tb, C), lambda i, t: (i % Bn, tmap(i, t), i // Bn))
    return pl.pallas_call(
        functools.partial(_wkv_kernel, tb=tb, batch=Bn),
        grid=(2 * Bn, nt),
        in_specs=[shared, shared, shared, shared, per_dir, per_dir, pl.BlockSpec((1, C), lambda i, t: (0, 0))],
        out_specs=pl.BlockSpec((1, tb, C), lambda i, t: (i, tmap(i, t), 0)),
        out_shape=jax.ShapeDtypeStruct((2 * Bn, L, C), f32),
        scratch_shapes=[pltpu.VMEM((C // LANES, LANES, LANES), f32)],
        compiler_params=pltpu.CompilerParams(dimension_semantics=("parallel", "arbitrary"),
                                             vmem_limit_bytes=VMEM_LIMIT_BYTES),
        name="wkv7_chunked",
    )(r, k, v, kk, lw2, a2, ka)


def _mm_exact_rhs(a, b_bf16):
    ah = a.astype(bf16)
    al = (a - ah.astype(f32)).astype(bf16)
    d = lambda x: lax.dot_general(x, b_bf16, _DN['nn'], preferred_element_type=f32)
    return d(ah) + d(al)


def _head_sum_matrix(scale):
    i = lax.broadcasted_iota(jnp.int32, (RW_D, RW_D), 0) // RW_N
    j = lax.broadcasted_iota(jnp.int32, (RW_D, RW_D), 1) // RW_N
    return jnp.where(i == j, scale, 0.0).astype(bf16)


def _rw_prep_kernel(r_ref, k_ref, v_ref, rp_ref, kp_ref, vp_ref, rn_ref, kn_ref, vn_ref, w_ref, a_ref,
                    mu_ref, w0_ref, w2_ref, a0_ref, a2_ref, kkw_ref,
                    ro_ref, ko_ref, vo_ref, kko_ref, lw_ref, ao_ref, *, rows, nt):
    t = pl.program_id(1)
    has_prev = (t > 0).astype(f32)
    has_next = (t < nt - 1).astype(f32)
    ridx = lax.broadcasted_iota(jnp.int32, (rows, RW_D), 0)

    def tshift(cur_ref, prev_ref, next_ref, j):
        u = cur_ref[0]
        prev_row = prev_ref[0, SUBLANES - 1:SUBLANES, :] * has_prev
        next_row = next_ref[0, 0:1, :] * has_next
        up = jnp.where(ridx == 0, prev_row, pltpu.roll(u, 1, axis=0))
        un = jnp.where(ridx == rows - 1, next_row, pltpu.roll(u, rows - 1, axis=0))
        return u + mu_ref[j, 0:1, :] * (up - u) + mu_ref[j, 1:2, :] * (un - u)

    r = tshift(r_ref, rp_ref, rn_ref, 0)
    k = tshift(k_ref, kp_ref, kn_ref, 1)
    v = tshift(v_ref, vp_ref, vn_ref, 2)
    ro_ref[0] = r
    ko_ref[0] = k
    vo_ref[0] = v
    kk = k * kkw_ref[...]
    ss = _mm_exact_rhs(kk * kk, _head_sum_matrix(1.0))
    kko_ref[0] = kk * lax.rsqrt(jnp.maximum(ss, 1e-24))
    wl = jnp.tanh(w_ref[0])
    al = a_ref[0]
    for d in range(2):
        wlog = w0_ref[d:d + 1, :] + _mm(wl[:, d * RW_W_LORA:(d + 1) * RW_W_LORA], w2_ref[d])
        wlog = -jax.nn.softplus(-wlog) - 0.5
        lw_ref[0, :, d * RW_D:(d + 1) * RW_D] = -jnp.exp(wlog)
        av = a0_ref[d:d + 1, :] + _mm(al[:, d * RW_A_LORA:(d + 1) * RW_A_LORA], a2_ref[d])
        ao_ref[0, :, d * RW_D:(d + 1) * RW_D] = jax.nn.sigmoid(av)


def rw_prep(proj, lora_lane0, mu, w0, w2, a0, a2, k_k):
    Bn, L, _ = proj.shape
    C = RW_D
    rows = RW_ROWS_PER_STEP
    nt = L // rows
    hb = rows // SUBLANES
    lb = lora_lane0 // LANES
    cur = lambda j: pl.BlockSpec((1, rows, C), lambda b, t: (b, t, j))
    prev = lambda j: pl.BlockSpec((1, SUBLANES, C), lambda b, t: (b, jnp.maximum(t * hb - 1, 0), j))
    nxt = lambda j: pl.BlockSpec((1, SUBLANES, C), lambda b, t: (b, jnp.minimum((t + 1) * hb, L // SUBLANES - 1), j))
    lora_w = pl.BlockSpec((1, rows, LANES), lambda b, t: (b, t, lb + 1))
    lora_a = pl.BlockSpec((1, rows, LANES), lambda b, t: (b, t, lb + 2))
    full = lambda shp: pl.BlockSpec(shp, lambda b, t: (0,) * len(shp))
    out_c = pl.BlockSpec((1, rows, C), lambda b, t: (b, t, 0))
    out_2c = pl.BlockSpec((1, rows, 2 * C), lambda b, t: (b, t, 0))
    sds = lambda c: jax.ShapeDtypeStruct((Bn, L, c), f32)
    return pl.pallas_call(
        functools.partial(_rw_prep_kernel, rows=rows, nt=nt),
        grid=(Bn, nt),
        in_specs=[cur(0), cur(1), cur(2), prev(0), prev(1), prev(2), nxt(0), nxt(1), nxt(2), lora_w, lora_a,
                  full((3, 2, C)), full((2, C)), full((2, RW_W_LORA, C)), full((2, C)), full((2, RW_A_LORA, C)),
                  full((1, C))],
        out_specs=[out_c, out_c, out_c, out_c, out_2c, out_2c],
        out_shape=[sds(C), sds(C), sds(C), sds(C), sds(2 * C), sds(2 * C)],
        compiler_params=pltpu.CompilerParams(dimension_semantics=("parallel", "parallel"),
                                             vmem_limit_bytes=VMEM_LIMIT_BYTES),
        name="rwkv_prep",
    )(proj, proj, proj, proj, proj, proj, proj, proj, proj, proj, proj,
      mu.astype(f32), w0.astype(f32), w2.astype(bf16), a0.astype(f32), a2.astype(bf16), k_k.astype(f32).reshape(1, C))


def _rw_post_kernel(yf_ref, yb_ref, r_ref, k_ref, v_ref, a_ref, g_ref, ka_ref, rk_ref, lnw_ref, lnb_ref, g2_ref, o_ref):
    C = RW_D
    y = yf_ref[0] + yb_ref[0]
    avg = _head_sum_matrix(1.0 / RW_N)
    mean = _mm_exact_rhs(y, avg)
    yc = y - mean
    var = _mm_exact_rhs(yc * yc, avg)
    yn = yc * lax.rsqrt(var + RW_GN_EPS) * lnw_ref[...] + lnb_ref[...]
    a = a_ref[0]
    k = k_ref[0]
    ka = ka_ref[...]
    ksum = k * (1.0 + (a[:, :C] - 1.0) * ka) + k * (1.0 + (a[:, C:] - 1.0) * ka)
    coef = _mm_exact_rhs(r_ref[0] * ksum * rk_ref[...], _head_sum_matrix(1.0))
    gate = _mm(jax.nn.sigmoid(g_ref[0]), g2_ref[...])
    o_ref[0] = (yn + coef * v_ref[0]) * gate


def rw_post(y2, r, k, v, a2, proj, g_lane0, k_a, r_k, ln_w, ln_b, g2):
    Bn, L, C = r.shape
    rows = RW_ROWS_PER_STEP
    nt = L // rows
    gb = g_lane0 // LANES
    blk = lambda c: pl.BlockSpec((1, rows, c), lambda b, t: (b, t, 0))
    full = lambda shp: pl.BlockSpec(shp, lambda b, t: (0,) * len(shp))
    row = lambda x: x.astype(f32).reshape(1, C)
    return pl.pallas_call(
        _rw_post_kernel,
        grid=(Bn, nt),
        in_specs=[blk(C), pl.BlockSpec((1, rows, C), lambda b, t: (b + Bn, t, 0)), blk(C), blk(C), blk(C), blk(2 * C),
                  pl.BlockSpec((1, rows, LANES), lambda b, t: (b, t, gb)),
                  full((1, C)), full((1, C)), full((1, C)), full((1, C)), full((RW_G_LORA, C))],
        out_specs=blk(C),
        out_shape=jax.ShapeDtypeStruct((Bn, L, C), f32),
        compiler_params=pltpu.CompilerParams(dimension_semantics=("parallel", "parallel"),
                                             vmem_limit_bytes=VMEM_LIMIT_BYTES),
        name="rwkv_post",
    )(y2, y2, r, k, v, a2, proj, row(k_a), row(r_k), row(ln_w), row(ln_b), g2.astype(bf16))


def _na_kernel(q_ref, k_ref, v_ref, bias_ref, o_ref, kb_ref, vb_ref, *, rows_per_step, n_rows, kr):
    W = GRID_W
    rb = pl.program_id(2)

    @pl.when(rb == 0)
    def _():
        kb_ref[...] = k_ref[0].astype(bf16)
        vb_ref[...] = v_ref[0].astype(bf16)

    lo_lane = lax.broadcasted_iota(jnp.int32, (W, LANES), 1) < NA_HD
    scale = NA_HD ** -0.5
    rows = []
    for j in range(rows_per_step):
        r = rb * rows_per_step + j
        start = jnp.clip(r - kr // 2, 0, n_rows - kr)
        rows.append((start, start - r + (NA_KR - 1)))
    s_list = []
    for j, (start, didx) in enumerate(rows):
        q = q_ref[0, j * W:(j + 1) * W, :] * scale
        qs = jnp.concatenate([jnp.where(lo_lane, q, 0.0), jnp.where(lo_lane, 0.0, q)], axis=0)
        kw = kb_ref[pl.ds(pl.multiple_of(start * W, W), kr * W), :]
        s_list.append(_mm(qs, kw, 'nt') + bias_ref[didx, 0])
    p_list = []
    for s in s_list:
        m = jnp.max(s, axis=-1, keepdims=True)
        p = jnp.exp(s - m)
        p_list.append((p, jnp.sum(p, axis=-1, keepdims=True)))
    for j, ((start, _), (p, l)) in enumerate(zip(rows, p_list)):
        vw = vb_ref[pl.ds(pl.multiple_of(start * W, W), kr * W), :]
        o = _mm(p, vw) / l
        o_ref[0, j * W:(j + 1) * W, :] = jnp.where(lo_lane, o[:W], o[W:])


def na_bias_table(rpb, n_rows):
    W = GRID_W
    kr = min(NA_KR, n_rows)
    cols = jnp.arange(W)
    col_start = jnp.clip(cols - NA_KC // 2, 0, W - NA_KC)
    in_band = (cols[None, :] >= col_start[:, None]) & (cols[None, :] < col_start[:, None] + NA_KC)
    dc = jnp.clip(cols[None, :] - cols[:, None], -(NA_KC - 1), NA_KC - 1) + (NA_KC - 1)
    win = jnp.stack([rpb.astype(f32)[:, d:d + kr, :] for d in range(8)], axis=1)
    onehot = (dc[:, :, None] == jnp.arange(2 * NA_KC - 1)[None, None, :]).astype(f32)
    tab = jnp.einsum('hdic,qkc->hdqik', win, onehot, precision=lax.Precision.HIGHEST)
    tab = jnp.where(in_band[None, None, :, None, :], tab, NEG_INF)
    H = rpb.shape[0]
    tab = tab.reshape(H // 2, 2, 8, W, kr * W).transpose(2, 0, 1, 3, 4)
    return tab.reshape(8, H // 2, 2 * W, kr * W)


def neighborhood_attention_pallas(proj, rpb, lane_block0):
    Bn, L, _ = proj.shape
    W = GRID_W
    n_rows = L // W
    kr = min(NA_KR, n_rows)
    hp = NA_H * NA_HD // LANES
    rps = NA_ROWS_PER_STEP
    bias = na_bias_table(rpb, n_rows)
    kern = functools.partial(_na_kernel, rows_per_step=rps, n_rows=n_rows, kr=kr)
    return pl.pallas_call(
        kern,
        grid=(Bn, hp, n_rows // rps),
        in_specs=[pl.BlockSpec((1, rps * W, LANES), lambda b, h, r: (b, r, lane_block0 + h)),
                  pl.BlockSpec((1, L, LANES), lambda b, h, r: (b, 0, lane_block0 + hp + h)),
                  pl.BlockSpec((1, L, LANES), lambda b, h, r: (b, 0, lane_block0 + 2 * hp + h)),
                  pl.BlockSpec((8, 1, 2 * W, kr * W), lambda b, h, r: (0, h, 0, 0))],
        out_specs=pl.BlockSpec((1, rps * W, LANES), lambda b, h, r: (b, r, h)),
        out_shape=jax.ShapeDtypeStruct((Bn, L, NA_H * NA_HD), f32),
        scratch_shapes=[pltpu.VMEM((L, LANES), bf16), pltpu.VMEM((L, LANES), bf16)],
        compiler_params=pltpu.CompilerParams(dimension_semantics=("parallel", "parallel", "arbitrary"),
                                             vmem_limit_bytes=VMEM_LIMIT_BYTES),
        name="neighborhood_attention",
    )(proj, proj, proj, bias)


def _moe_ffn_kernel(be_ref, x_ref, w1_ref, w3_ref, w2_ref, o_ref, w1b_ref, w3b_ref, w2b_ref):
    i = pl.program_id(0)

    @pl.when((i == 0) | (be_ref[i] != be_ref[jnp.maximum(i - 1, 0)]))
    def _():
        w1b_ref[...] = w1_ref[0, 0].astype(bf16)
        w3b_ref[...] = w3_ref[0, 0].astype(bf16)
        w2b_ref[...] = w2_ref[0, 0].astype(bf16)

    x = x_ref[...].astype(bf16)
    h1 = jnp.dot(x, w1b_ref[...], preferred_element_type=f32)
    h3 = jnp.dot(x, w3b_ref[...], preferred_element_type=f32)
    g = h1 * jax.nn.sigmoid(h1) * h3
    o_ref[...] = jnp.dot(g.astype(bf16), w2b_ref[...], preferred_element_type=f32)


def moe_expert_ffn(xs, block_e, w1, w3, w2, layer):
    rows, Dm = xs.shape
    n_blocks = rows // MOE_BLOCK
    grid_spec = pltpu.PrefetchScalarGridSpec(
        num_scalar_prefetch=1,
        grid=(n_blocks,),
        in_specs=[pl.BlockSpec((MOE_BLOCK, Dm), lambda i, be: (i, 0)),
                  pl.BlockSpec((1, 1, Dm, MOE_FF), lambda i, be: (layer, be[i], 0, 0)),
                  pl.BlockSpec((1, 1, Dm, MOE_FF), lambda i, be: (layer, be[i], 0, 0)),
                  pl.BlockSpec((1, 1, MOE_FF, Dm), lambda i, be: (layer, be[i], 0, 0))],
        out_specs=pl.BlockSpec((MOE_BLOCK, Dm), lambda i, be: (i, 0)),
        scratch_shapes=[pltpu.VMEM((Dm, MOE_FF), bf16), pltpu.VMEM((Dm, MOE_FF), bf16), pltpu.VMEM((MOE_FF, Dm), bf16)],
    )
    return pl.pallas_call(
        _moe_ffn_kernel,
        grid_spec=grid_spec,
        out_shape=jax.ShapeDtypeStruct((rows, Dm), f32),
        compiler_params=pltpu.CompilerParams(dimension_semantics=("arbitrary",), vmem_limit_bytes=VMEM_LIMIT_BYTES),
        name="moe_expert_ffn",
    )(block_e.astype(jnp.int32), xs, w1.astype(f32), w3.astype(f32), w2.astype(f32))


def _rms(x):
    return x * lax.rsqrt(jnp.mean(x * x, axis=-1, keepdims=True) + NORM_EPS)


def _norm_proj_kernel(x_ref, g_ref, w_ref, o_ref):
    h = (_rms(x_ref[...]) * g_ref[...]).astype(bf16)
    o_ref[...] = jnp.dot(h, w_ref[...], preferred_element_type=f32)


def norm_proj(x2, g, w):
    N, D = x2.shape
    F = w.shape[1]
    tm = PROJ_ROWS_PER_STEP
    return pl.pallas_call(
        _norm_proj_kernel,
        grid=(N // tm,),
        in_specs=[pl.BlockSpec((tm, D), lambda i: (i, 0)), pl.BlockSpec((1, D), lambda i: (0, 0)),
                  pl.BlockSpec((D, F), lambda i: (0, 0))],
        out_specs=pl.BlockSpec((tm, F), lambda i: (i, 0)),
        out_shape=jax.ShapeDtypeStruct((N, F), f32),
        compiler_params=pltpu.CompilerParams(dimension_semantics=("parallel",), vmem_limit_bytes=VMEM_LIMIT_BYTES),
        name="norm_proj",
    )(x2, g.astype(f32).reshape(1, D), w.astype(bf16))


def _out_proj_kernel(x_ref, yh_ref, yr_ref, yn_ref, wh_ref, wr_ref, wn_ref, g_ref, wrt_ref, xo_ref, h_ref, lg_ref):
    mix = (jnp.dot(yh_ref[...].astype(bf16), wh_ref[...], preferred_element_type=f32)
           + jnp.dot(yr_ref[...].astype(bf16), wr_ref[...], preferred_element_type=f32)
           + jnp.dot(yn_ref[...].astype(bf16), wn_ref[...], preferred_element_type=f32))
    x = x_ref[...] + mix
    xo_ref[...] = x
    h = (_rms(x) * g_ref[...]).astype(bf16)
    h_ref[...] = h
    lg_ref[...] = jnp.dot(h, wrt_ref[...], preferred_element_type=f32)


def out_proj_norm_router(x2, y_hy, y_rw, y_na, w_out, g, w_router):
    N, D = x2.shape
    tm = PROJ_ROWS_PER_STEP
    d_hy, d_rw, d_na = y_hy.shape[1], y_rw.shape[1], y_na.shape[1]
    nr = w_router.shape[1]
    w_router = jnp.pad(w_router.astype(bf16), ((0, 0), (0, LANES - nr)))
    wb = w_out.astype(bf16)
    row = lambda c: pl.BlockSpec((tm, c), lambda i: (i, 0))
    full = lambda r, c: pl.BlockSpec((r, c), lambda i: (0, 0))
    return pl.pallas_call(
        _out_proj_kernel,
        grid=(N // tm,),
        in_specs=[row(D), row(d_hy), row(d_rw), row(d_na), full(d_hy, D), full(d_rw, D), full(d_na, D), full(1, D),
                  full(D, LANES)],
        out_specs=[row(D), row(D), row(LANES)],
        out_shape=[jax.ShapeDtypeStruct((N, D), f32), jax.ShapeDtypeStruct((N, D), bf16),
                   jax.ShapeDtypeStruct((N, LANES), f32)],
        compiler_params=pltpu.CompilerParams(dimension_semantics=("parallel",), vmem_limit_bytes=VMEM_LIMIT_BYTES),
        name="out_proj_norm_router",
    )(x2, y_hy, y_rw, y_na, wb[:d_hy], wb[d_hy:d_hy + d_rw], wb[d_hy + d_rw:], g.astype(f32).reshape(1, D), w_router)


def _final_norm_kernel(x_ref, g_ref, o_ref):
    x = x_ref[...]
    o_ref[...] = x * lax.rsqrt(jnp.mean(x * x, axis=-1, keepdims=True) + NORM_EPS) * g_ref[...]


def final_rmsnorm(x, g):
    Bn, L, D = x.shape
    tm = 1024
    out = pl.pallas_call(
        _final_norm_kernel,
        grid=(Bn * L // tm,),
        in_specs=[pl.BlockSpec((tm, D), lambda i: (i, 0)), pl.BlockSpec((1, D), lambda i: (0, 0))],
        out_specs=pl.BlockSpec((tm, D), lambda i: (i, 0)),
        out_shape=jax.ShapeDtypeStruct((Bn * L, D), x.dtype),
        name="final_rmsnorm",
    )(x.reshape(Bn * L, D), g.reshape(1, D))
    return out.reshape(Bn, L, D)


def hyena_positional_features(L):
    t = jnp.linspace(0.0, 1.0, L, dtype=f32)[:, None]
    w = (2.0 * math.pi / L) * jnp.arange(L, dtype=f32)[:, None]
    f = jnp.linspace(1e-4, HY_BANDS - 1, HY_BANDS, dtype=f32)[None, :]
    z = jnp.concatenate([t, jnp.cos(f * w), -jnp.sin(f * w)], axis=-1)
    return z, t


def hyena_filters(z, t, w1, b1, w2, b2, w3, b3, wout, freq):
    fr = freq.astype(f32)
    act = lambda u: jnp.sin(fr * u)
    h = act(z @ w1.astype(f32) + b1.astype(f32))
    h = act(h @ w2.astype(f32) + b2.astype(f32))
    h = act(h @ w3.astype(f32) + b3.astype(f32))
    h = (h @ wout.astype(f32)).reshape(-1, 2, HY_D)
    deltas = jnp.abs(jnp.linspace(HY_MIN_DECAY, HY_MAX_DECAY, HY_D, dtype=f32))
    h = h * jnp.exp(-t[:, :, None] * deltas)
    return h * lax.rsqrt(jnp.sum(h * h, axis=(0, 1), keepdims=True) + 1e-6)


def _split_bf16(x):
    hi = x.astype(bf16)
    return hi, (x - hi.astype(f32)).astype(bf16)


def _dot3(m_hi, m_lo, x_hi, x_lo):
    d = lambda a, b: lax.dot_general(a, b, _DN['nn'], preferred_element_type=f32)
    return d(m_hi, x_hi) + (d(m_hi, x_lo) + d(m_lo, x_hi))


def _cmatmul(mr, mi, xr, xi):
    xrh, xrl = _split_bf16(xr)
    rr = _dot3(mr[0], mr[1], xrh, xrl)
    ir = _dot3(mi[0], mi[1], xrh, xrl)
    if xi is None:
        return rr, ir
    xih, xil = _split_bf16(xi)
    ii = _dot3(mi[0], mi[1], xih, xil)
    ri = _dot3(mr[0], mr[1], xih, xil)
    return rr - ii, ri + ir


def _row_dft_kernel(mrh_ref, mrl_ref, mih_ref, mil_ref, *refs, real_input):
    if real_input:
        ur_ref, or_ref, oi_ref = refs
        ui = None
    else:
        ur_ref, ui_ref, or_ref, oi_ref = refs
        ui = ui_ref[0]
    o_r, o_i = _cmatmul((mrh_ref[...], mrl_ref[...]), (mih_ref[...], mil_ref[...]), ur_ref[0], ui)
    or_ref[0] = o_r
    oi_ref[0] = o_i


def row_dft(tabs, u, packed):
    rows, r_in, W = u.shape
    P = rows // 2 if packed else rows
    r_out = tabs[0].shape[0]
    wb = min(FFT_LANE_BLOCK, W)
    tab_spec = pl.BlockSpec((r_out, r_in), lambda p, j: (0, 0))
    re_spec = pl.BlockSpec((1, r_in, wb), lambda p, j: (p, 0, j))
    im_spec = pl.BlockSpec((1, r_in, wb), lambda p, j: (p + P, 0, j))
    out_spec = pl.BlockSpec((1, r_out, wb), lambda p, j: (p, 0, j))
    ins = [u, u] if packed else [u]
    return pl.pallas_call(
        functools.partial(_row_dft_kernel, real_input=not packed),
        grid=(P, W // wb),
        in_specs=[tab_spec] * 4 + ([re_spec, im_spec] if packed else [re_spec]),
        out_specs=[out_spec, out_spec],
        out_shape=[jax.ShapeDtypeStruct((P, r_out, W), f32)] * 2,
        compiler_params=pltpu.CompilerParams(dimension_semantics=("parallel", "parallel"),
                                             vmem_limit_bytes=VMEM_LIMIT_BYTES),
        name="hyena_row_dft",
    )(*tabs, *ins)


def _row_idft_gate_kernel(mrh_ref, mrl_ref, mih_ref, mil_ref, dr_ref, di_ref, z0_ref, z1_ref, x0_ref, x1_ref,
                          skip_ref, o0_ref, o1_ref):
    y_r, y_i = _cmatmul((mrh_ref[...], mrl_ref[...]), (mih_ref[...], mil_ref[...]), dr_ref[0], di_ref[0])
    skip = skip_ref[...]
    o0_ref[0] = x0_ref[0] * (y_r + z0_ref[0] * skip)
    o1_ref[0] = x1_ref[0] * (y_i + z1_ref[0] * skip)


def row_idft_gate(tabs, dr, di, z, x0, skip_row):
    P, r_in, W = dr.shape
    r_out = tabs[0].shape[0]
    wb = min(FFT_LANE_BLOCK, W)
    tab_spec = pl.BlockSpec((r_out, r_in), lambda p, j: (0, 0))
    d_spec = pl.BlockSpec((1, r_in, wb), lambda p, j: (p, 0, j))
    lo = pl.BlockSpec((1, r_out, wb), lambda p, j: (p, 0, j))
    hi = pl.BlockSpec((1, r_out, wb), lambda p, j: (p + P, 0, j))
    o0, o1 = pl.pallas_call(
        _row_idft_gate_kernel,
        grid=(P, W // wb),
        in_specs=[tab_spec] * 4 + [d_spec, d_spec, lo, hi, lo, hi, pl.BlockSpec((1, wb), lambda p, j: (0, j))],
        out_specs=[lo, lo],
        out_shape=[jax.ShapeDtypeStruct((P, r_out, W), f32)] * 2,
        compiler_params=pltpu.CompilerParams(dimension_semantics=("parallel", "parallel"),
                                             vmem_limit_bytes=VMEM_LIMIT_BYTES),
        name="hyena_row_idft_gate",
    )(*tabs, dr, di, z, z, x0, x0, skip_row)
    return jnp.concatenate([o0, o1], axis=0)


def _col_dft_kernel(grh_ref, grl_ref, gih_ref, gil_ref, ar_ref, ai_ref, br_ref, bi_ref, *, k1_per_step):
    for j in range(k1_per_step):
        b_r, b_i = _cmatmul((grh_ref[j], grl_ref[j]), (gih_ref[j], gil_ref[j]), ar_ref[0, j], ai_ref[0, j])
        br_ref[0, j] = b_r
        bi_ref[0, j] = b_i


def _col_conv_kernel(grh_ref, grl_ref, gih_ref, gil_ref, trh_ref, trl_ref, tih_ref, til_ref,
                     ar_ref, ai_ref, kr_ref, ki_ref, dr_ref, di_ref, *, k1_per_step):
    for j in range(k1_per_step):
        b_r, b_i = _cmatmul((grh_ref[j], grl_ref[j]), (gih_ref[j], gil_ref[j]), ar_ref[0, j], ai_ref[0, j])
        k_r = kr_ref[0, j]
        k_i = ki_ref[0, j]
        c_r = b_r * k_r - b_i * k_i
        c_i = b_r * k_i + b_i * k_r
        d_r, d_i = _cmatmul((trh_ref[j], trl_ref[j]), (tih_ref[j], til_ref[j]), c_r, c_i)
        dr_ref[0, j] = d_r
        di_ref[0, j] = d_i


def col_stage(g_tabs, gt_tabs, ar, ai, kr=None, ki=None):
    P, n1, n2, C = ar.shape
    kb = min(FFT_K1_PER_STEP, n1)
    g_spec = pl.BlockSpec((kb, n2, n2), lambda p, j: (j, 0, 0))
    a_spec = pl.BlockSpec((1, kb, n2, C), lambda p, j: (p, j, 0, 0))
    k_spec = pl.BlockSpec((1, kb, n2, C), lambda p, j: (0, j, 0, 0))
    common = dict(
        grid=(P, n1 // kb),
        out_specs=[a_spec, a_spec],
        out_shape=[jax.ShapeDtypeStruct((P, n1, n2, C), f32)] * 2,
        compiler_params=pltpu.CompilerParams(dimension_semantics=("parallel", "parallel"),
                                             vmem_limit_bytes=VMEM_LIMIT_BYTES),
    )
    if kr is None:
        return pl.pallas_call(functools.partial(_col_dft_kernel, k1_per_step=kb),
                              in_specs=[g_spec] * 4 + [a_spec] * 2, name="hyena_col_dft", **common)(*g_tabs, ar, ai)
    return pl.pallas_call(functools.partial(_col_conv_kernel, k1_per_step=kb),
                          in_specs=[g_spec] * 8 + [a_spec] * 2 + [k_spec] * 2, name="hyena_col_conv", **common)(
        *g_tabs, *gt_tabs, ar, ai, kr, ki)


def _bf16_tables(m):
    out = []
    for part in (np.real(m), np.imag(m)):
        x = jnp.asarray(part, f32)
        hi = x.astype(bf16)
        out += [hi, (x - hi.astype(f32)).astype(bf16)]
    return out


def hyena_dft_tables(L):
    n2 = FFT_N2
    n1 = 2 * L // n2
    N = n1 * n2
    a = np.arange(n1)
    fa = np.exp(-2j * np.pi * np.outer(a, a) / n1)
    k1 = np.arange(n1)[:, None, None]
    k2 = np.arange(n2)[None, :, None]
    nn = np.arange(n2)[None, None, :]
    g = np.exp(-2j * np.pi * nn * (k1 + n1 * k2) / N)
    gt = np.conj(np.transpose(g, (0, 2, 1)))
    fc = np.conj(fa).T / N
    return dict(fa=_bf16_tables(fa), fa_half=_bf16_tables(fa[:, :n1 // 2]), g=_bf16_tables(g), gt=_bf16_tables(gt),
                fc_half=_bf16_tables(fc[:n1 // 2]))


def hyena_long_conv_gate(tabs, z, x0, k2, skip):
    Bn, L, C = z.shape
    n2 = FFT_N2
    n1 = 2 * L // n2
    P = Bn // 2
    W = n2 * C
    ar, ai = row_dft(tabs['fa'], k2.reshape(1, n1, W), packed=False)
    kr, ki = col_stage(tabs['g'], tabs['gt'], ar.reshape(1, n1, n2, C), ai.reshape(1, n1, n2, C))
    zv = z.reshape(Bn, n1 // 2, W)
    ar, ai = row_dft(tabs['fa_half'], zv, packed=True)
    dr, di = col_stage(tabs['g'], tabs['gt'], ar.reshape(P, n1, n2, C), ai.reshape(P, n1, n2, C), kr, ki)
    out = row_idft_gate(tabs['fc_half'], dr.reshape(P, n1, W), di.reshape(P, n1, W), zv, x0.reshape(Bn, n1 // 2, W),
                        jnp.tile(skip.astype(f32), n2).reshape(1, W))
    return out.reshape(Bn, L, C)


def _hyena_prep_kernel(*refs, rows, nt):
    cur, prev, nxt = refs[0:3], refs[3:6], refs[6:9]
    w_ref, b_ref, z_ref, x0_ref = refs[9:]
    t = pl.program_id(1)
    has_prev = (t > 0).astype(f32)
    has_next = (t < nt - 1).astype(f32)
    ridx = lax.broadcasted_iota(jnp.int32, (rows, HY_D), 0)

    def conv(j):
        u = cur[j][0]
        prev_row = prev[j][0, SUBLANES - 1:SUBLANES, :] * has_prev
        next_row = nxt[j][0, 0:1, :] * has_next
        up = jnp.where(ridx == 0, prev_row, pltpu.roll(u, 1, axis=0))
        un = jnp.where(ridx == rows - 1, next_row, pltpu.roll(u, rows - 1, axis=0))
        sl = slice(j * HY_D, (j + 1) * HY_D)
        return up * w_ref[0:1, sl] + u * w_ref[1:2, sl] + un * w_ref[2:3, sl] + b_ref[:, sl]

    x0_ref[0] = conv(0)
    z_ref[0] = conv(2) * conv(1)


def hyena_prep(proj, lane0, conv_w, conv_b):
    Bn, L, _ = proj.shape
    rows = RW_ROWS_PER_STEP
    nt = L // rows
    hb = rows // SUBLANES
    b0 = lane0 // HY_D
    assert b0 * HY_D == lane0
    cur = lambda j: pl.BlockSpec((1, rows, HY_D), lambda b, t: (b, t, b0 + j))
    prev = lambda j: pl.BlockSpec((1, SUBLANES, HY_D), lambda b, t: (b, jnp.maximum(t * hb - 1, 0), b0 + j))
    nxt = lambda j: pl.BlockSpec((1, SUBLANES, HY_D), lambda b, t: (b, jnp.minimum((t + 1) * hb, L // SUBLANES - 1), b0 + j))
    full = lambda shp: pl.BlockSpec(shp, lambda b, t: (0,) * len(shp))
    out = pl.BlockSpec((1, rows, HY_D), lambda b, t: (b, t, 0))
    return pl.pallas_call(
        functools.partial(_hyena_prep_kernel, rows=rows, nt=nt),
        grid=(Bn, nt),
        in_specs=[cur(0), cur(1), cur(2), prev(0), prev(1), prev(2), nxt(0), nxt(1), nxt(2),
                  full((3, 3 * HY_D)), full((1, 3 * HY_D))],
        out_specs=[out, out],
        out_shape=[jax.ShapeDtypeStruct((Bn, L, HY_D), f32)] * 2,
        compiler_params=pltpu.CompilerParams(dimension_semantics=("parallel", "parallel")),
        name="hyena_prep",
    )(*([proj] * 9), conv_w.astype(f32), conv_b.astype(f32).reshape(1, 3 * HY_D))


def hyena_mixer(tabs, proj, lane0, conv_w, conv_b, filt, skip):
    z, x0 = hyena_prep(proj, lane0, conv_w, conv_b)
    k2 = jnp.concatenate([filt[:1, 0] + filt[:1, 1], filt[1:, 0],
                          jnp.zeros((1, HY_D), f32), filt[1:, 1][::-1]], axis=0)
    return hyena_long_conv_gate(tabs, z, x0, k2, skip)


def rwkv7_mixer(proj, lora_lane0, mu, w0, w2, a0, a2, k_k, k_a, r_k, g2, ln_w, ln_b):
    r, k, v, kk, lw2, a_both = rw_prep(proj, lora_lane0, mu, w0, w2, a0, a2, k_k)
    y2 = wkv7_chunked(r, k, v, kk, lw2, a_both, k_a.astype(f32).reshape(1, RW_D))
    return rw_post(y2, r, k, v, a_both, proj, lora_lane0, k_a, r_k, ln_w, ln_b, g2)


def hier_moe(h, logits, bg, be, w1, w3, w2, layer):
    N, Dm = h.shape
    assert MOE_TOPK == 2
    g_logits = logits[:, :MOE_GROUPS] + bg.astype(f32)
    g_sel = jnp.argmax(g_logits, axis=-1)
    g_prob = jnp.take_along_axis(jax.nn.softmax(g_logits, axis=-1), g_sel[:, None], axis=-1)
    e_logits = (logits[:, MOE_GROUPS:MOE_GROUPS + MOE_EXPERTS] + be.astype(f32)).reshape(N, MOE_GROUPS, MOE_PER_GROUP)
    e_logits = jnp.take_along_axis(e_logits, g_sel[:, None, None], axis=1)[:, 0]
    top_val, top_idx = lax.top_k(e_logits, MOE_TOPK)
    gate = g_prob * jax.nn.softmax(top_val, axis=-1)
    expert = g_sel[:, None] * MOE_PER_GROUP + top_idx
    M = N * MOE_TOPK
    flat_e = expert.reshape(M).astype(jnp.int32)
    experts = jnp.arange(MOE_EXPERTS, dtype=jnp.int32)
    counts = jnp.sum((flat_e[:, None] == experts[None, :]).astype(jnp.int32), axis=0)
    padded = (counts + MOE_BLOCK - 1) // MOE_BLOCK * MOE_BLOCK
    pad_end = jnp.cumsum(padded)
    n_blocks = -(-M // MOE_BLOCK) + MOE_EXPERTS
    n_rows = n_blocks * MOE_BLOCK
    cum_need = jnp.cumsum(padded - counts)
    filler = jnp.arange(n_rows - M, dtype=jnp.int32)
    filler_e = jnp.sum((cum_need[None, :] <= filler[:, None]).astype(jnp.int32), axis=1)
    keys = jnp.concatenate([2 * flat_e, 2 * filler_e + 1])
    ids = jnp.concatenate([jnp.arange(M, dtype=jnp.int32), jnp.full((n_rows - M,), M, jnp.int32)])
    _, slot_src = lax.sort((keys, ids), num_keys=1)
    tok_src = jnp.where(slot_src < M, slot_src // MOE_TOPK, 0)
    block_start = jnp.arange(n_blocks, dtype=pad_end.dtype) * MOE_BLOCK
    block_e = jnp.minimum(jnp.sum(pad_end[None, :] <= block_start[:, None], axis=1), MOE_EXPERTS - 1)
    y = moe_expert_ffn(h[tok_src], block_e, w1, w3, w2, layer)
    _, row_of = lax.sort((slot_src, jnp.arange(n_rows, dtype=jnp.int32)), num_keys=1)
    pos = row_of[:M].reshape(N, MOE_TOPK)
    return gate[:, 0:1] * y[pos[:, 0]] + gate[:, 1:2] * y[pos[:, 1]]


def kernel(x, norm1_g, w_in, hy_conv_w, hy_conv_b, hy_w1, hy_b1, hy_w2, hy_b2, hy_w3, hy_b3, hy_wout, hy_freq, hy_skip, rw_mu, rw_w0, rw_w2, rw_a0, rw_a2, rw_kk, rw_ka, rw_rk, rw_g2, rw_ln_w, rw_ln_b, na_rpb, w_out, norm2_g, moe_wg, moe_bg, moe_we, moe_be, moe_w1, moe_w3, moe_w2, norm_f_g):
    Bn, L, _ = x.shape
    z_pos, t_pos = hyena_positional_features(L)
    dft_tabs = hyena_dft_tables(L)
    splits = np.cumsum(IN_SIZES)[:-1].tolist()
    hy_end, rkv_end = splits[0], splits[1]
    hy_lane0 = rkv_end - hy_end
    lora_lane0, na_lane0 = splits[1], splits[4]
    assert RW_D % LANES == 0 and lora_lane0 % LANES == 0 and na_lane0 % LANES == 0
    N = Bn * L
    x2 = x.reshape(N, D_MODEL)
    for l in range(DEPTH):
        w_in_l = jnp.concatenate([w_in[l][:, hy_end:rkv_end], w_in[l][:, :hy_end], w_in[l][:, rkv_end:]], axis=1)
        proj = norm_proj(x2, norm1_g[l], w_in_l).reshape(Bn, L, IN_D)
        filt = hyena_filters(z_pos, t_pos, hy_w1[l], hy_b1[l], hy_w2[l], hy_b2[l],
                             hy_w3[l], hy_b3[l], hy_wout[l], hy_freq[l])
        y_hy = hyena_mixer(dft_tabs, proj, hy_lane0, hy_conv_w[l], hy_conv_b[l], filt, hy_skip[l])
        y_rw = rwkv7_mixer(proj, lora_lane0, rw_mu[l], rw_w0[l], rw_w2[l], rw_a0[l],
                           rw_a2[l], rw_kk[l], rw_ka[l], rw_rk[l].reshape(RW_D), rw_g2[l], rw_ln_w[l], rw_ln_b[l])
        y_na = neighborhood_attention_pallas(proj, na_rpb[l], na_lane0 // LANES)
        x2, h2, logits = out_proj_norm_router(
            x2, y_hy.reshape(N, HY_D), y_rw.reshape(N, RW_D), y_na.reshape(N, NA_D), w_out[l], norm2_g[l],
            jnp.concatenate([moe_wg[l], moe_we[l]], axis=1))
        x2 = x2 + hier_moe(h2, logits, moe_bg[l], moe_be[l], moe_w1, moe_w3, moe_w2, l)
    return final_rmsnorm(x2.reshape(Bn, L, D_MODEL), norm_f_g)
```

```python
import functools
import math

import jax
import jax.numpy as jnp
import numpy as np
from jax import lax
from jax.experimental import pallas as pl
from jax.experimental.pallas import tpu as pltpu

f32 = jnp.float32
bf16 = jnp.bfloat16

D_MODEL = 1024
DEPTH = 2
GRID_W = 64
NORM_EPS = 1e-6
NEG_INF = -1e30

HY_D = D_MODEL // 4
HY_EMB = 33
HY_BANDS = (HY_EMB - 1) // 2
HY_FFN = 64
HY_MIN_DECAY = math.log(1e-2) / 1.5
HY_MAX_DECAY = math.log(1e-2) / 0.3

RW_N = 64
RW_D = D_MODEL // 2
RW_H = RW_D // RW_N
RW_W_LORA = 64
RW_A_LORA = 64
RW_G_LORA = 128
RW_GN_EPS = 64e-5

NA_HD = 64
NA_D = D_MODEL // 4
NA_H = NA_D // NA_HD
NA_KR = 8
NA_KC = 16

MIX_D = HY_D + RW_D + NA_D
IN_SIZES = (3 * HY_D, 3 * RW_D, RW_G_LORA, 2 * RW_W_LORA, 2 * RW_A_LORA, 3 * NA_D)
IN_D = sum(IN_SIZES)

MOE_GROUPS = 4
MOE_PER_GROUP = 8
MOE_EXPERTS = MOE_GROUPS * MOE_PER_GROUP
MOE_TOPK = 2
MOE_FF = 512
MOE_BLOCK = 256

LANES = 128
WKV_CHUNK = 64
WKV_TIME_BLOCK = 512
NA_ROWS_PER_STEP = 4
RW_ROWS_PER_STEP = 512
SUBLANES = 8
PROJ_ROWS_PER_STEP = 512
FFT_N2 = 128
FFT_LANE_BLOCK = 4096
FFT_K1_PER_STEP = 4
VMEM_LIMIT_BYTES = 48 * 1024 * 1024

_DN = {'nn': (((1,), (0,)), ((), ())), 'nt': (((1,), (1,)), ((), ())), 'tn': (((0,), (0,)), ((), ()))}


def _mm(a, b, dims='nn'):
    return lax.dot_general(a.astype(bf16), b.astype(bf16), _DN[dims], preferred_element_type=f32)


def _wkv_kernel(r_ref, k_ref, v_ref, kk_ref, lw_ref, a_ref, ka_ref, y_ref, ht_ref, *, tb, batch):
    T = WKV_CHUNK
    H2 = 2 * T
    nc = tb // T
    npairs = r_ref.shape[-1] // LANES
    mm = _mm

    @pl.when(pl.program_id(1) == 0)
    def _():
        ht_ref[...] = jnp.zeros_like(ht_ref)

    d = pl.program_id(0) // batch
    sign = 1 - 2 * d
    trow = lax.broadcasted_iota(jnp.int32, (T, H2), 0)
    lane = lax.broadcasted_iota(jnp.int32, (T, H2), 1)
    scol = lane % T
    tdiff = (trow - scol) * sign
    strict = tdiff > 0
    incl = tdiff >= 0
    same16 = (trow // 16) == (scol // 16)
    same32 = (trow // 32) == (scol // 32)
    off32 = same32 & jnp.logical_not(same16)
    off64 = jnp.logical_not(same32)
    eye2 = jnp.where(trow == scol, 1.0, 0.0).astype(f32)
    lo_lane = lane < T
    brow = lax.broadcasted_iota(jnp.int32, (H2, H2), 0)
    bcol = lax.broadcasted_iota(jnp.int32, (H2, H2), 1)
    same_head = (brow // T) == (bcol // T)
    crow = lax.broadcasted_iota(jnp.int32, (T, T), 0)
    ccol = lax.broadcasted_iota(jnp.int32, (T, T), 1)
    tri = jnp.where((crow - ccol) * sign >= 0, 1.0, 0.0).astype(bf16)
    is_bwd = d == 1

    def bd(x):
        return jnp.concatenate([jnp.where(lo_lane, x, 0.0), jnp.where(lo_lane, 0.0, x)], axis=0)

    inst = [(s, p) for s in range(nc) for p in range(npairs)]
    offs = [pl.multiple_of((s + d * (nc - 1 - 2 * s)) * T, T) for s in range(nc)]

    def load(ref, s, p):
        return ref[0, pl.ds(offs[s], T), p * LANES:(p + 1) * LANES]

    cs_l = []
    for s, p in inst:
        lw = load(lw_ref, s, p)
        l1 = lw.astype(bf16)
        r1 = lw - l1.astype(f32)
        l2 = r1.astype(bf16)
        l3 = (r1 - l2.astype(f32)).astype(bf16)
        dd = lambda x: lax.dot_general(tri, x, _DN['nn'], preferred_element_type=f32)
        cs_l.append((dd(l1) + (dd(l2) + dd(l3)), lw))
    ops = []
    for (s, p), (cs, lw) in zip(inst, cs_l):
        r = load(r_ref, s, p)
        k = load(k_ref, s, p)
        v = load(v_ref, s, p)
        kk = load(kk_ref, s, p)
        a = load(a_ref, s, p)
        ka = ka_ref[:, p * LANES:(p + 1) * LANES]
        kd = k * (1.0 + (a - 1.0) * ka)
        b = kk * a
        cs_end = jnp.where(is_bwd, cs[0:1, :], cs[T - 1:T, :])
        em = jnp.exp(-cs)
        e_end = jnp.exp(cs_end - cs)
        ops.append(dict(
            AR=jnp.concatenate([-kk * jnp.exp(cs - lw), r * jnp.exp(cs)], axis=0),
            BK=jnp.concatenate([bd(b * em), bd(kd * em)], axis=0),
            V=v, BKg=jnp.concatenate([b * e_end, kd * e_end], axis=0), g_end=jnp.exp(cs_end)))
    for o in ops:
        S = mm(o['AR'], o['BK'], 'nt')
        N = jnp.where(strict, S[:T, :H2], 0.0)
        o['AakArk'] = jnp.concatenate([jnp.where(strict, S[:T, H2:], 0.0), jnp.where(incl, S[T:, H2:], 0.0)], axis=0)
        o['Arb'] = jnp.where(incl, S[T:, :H2], 0.0)
        o['Nd'] = jnp.where(same16, N, 0.0)
        o['N32'] = jnp.where(off32, N, 0.0)
        o['N64'] = jnp.where(off64, N, 0.0)
    for o in ops:
        o['X'] = eye2 + o['Nd']
        o['P'] = mm(o['Nd'], bd(o['Nd']))
    for it in range(3):
        if it < 2:
            for o in ops:
                px = mm(o['P'], jnp.concatenate([bd(o['X']), bd(o['P'])], axis=1))
                o['X'] = o['X'] + px[:, :H2]
                o['P'] = px[:, H2:]
        else:
            for o in ops:
                o['X'] = o['X'] + mm(o['P'], bd(o['X']))
    for key in ('N32', 'N64'):
        for o in ops:
            o['Z'] = mm(o[key], bd(o['X']))
        for o in ops:
            o['X'] = o['X'] + mm(o['X'], bd(o['Z']))
    for o in ops:
        wy = mm(o['AakArk'], bd(o['V']))
        o['W0'] = wy[:T]
        o['Yv'] = wy[T:]
    for o in ops:
        o['XAW'] = mm(o['X'], jnp.concatenate([bd(o['AR'][:T]), bd(o['W0'])], axis=1))
    for o in ops:
        ax = mm(o['Arb'], jnp.concatenate([bd(o['XAW'][:, :H2]), bd(o['XAW'][:, H2:])], axis=1))
        o['Rhat'] = o['AR'][T:] + ax[:, :H2]
        o['Yc'] = ax[:, H2:] + o['Yv']
    for o in ops:
        lhs = jnp.concatenate([o['XAW'], jnp.concatenate([jnp.zeros((T, H2), f32), o['V']], axis=1)], axis=0)
        pq = mm(lhs, o['BKg'], 'tn')
        o['Pc'] = jnp.where(same_head, pq[:H2], 0.0)
        o['Qc'] = jnp.where(same_head, pq[H2:], 0.0)
    hts = [ht_ref[p] for p in range(npairs)]
    for s in range(nc):
        cur = [ops[s * npairs + p] for p in range(npairs)]
        ys = [mm(o['Rhat'], hts[p], 'nt') + o['Yc'] for p, o in enumerate(cur)]
        hts = [hts[p] * o['g_end'] + (mm(hts[p], o['Pc']) + o['Qc']) for p, o in enumerate(cur)]
        for p in range(npairs):
            y_ref[0, pl.ds(offs[s], T), p * LANES:(p + 1) * LANES] = ys[p]
    for p in range(npairs):
        ht_ref[p] = hts[p]


def wkv7_chunked(r, k, v, kk, lw2, a2, ka):
    Bn, L, C = r.shape
    tb = WKV_TIME_BLOCK
    nt = L // tb
    tmap = lambda i, t: t + (i // Bn) * (nt - 1 - 2 * t)
    shared = pl.BlockSpec((1, tb, C), lambda i, t: (i % Bn, tmap(i, t), 0))
    per_dir = pl.BlockSpec((1, tb, C), lambda i, t: (i % Bn, tmap(i, t), i // Bn))
    return pl.pallas_call(
        functools.partial(_wkv_kernel, tb=tb, batch=Bn),
        grid=(2 * Bn, nt),
        in_specs=[shared, shared, shared, shared, per_dir, per_dir, pl.BlockSpec((1, C), lambda i, t: (0, 0))],
        out_specs=pl.BlockSpec((1, tb, C), lambda i, t: (i, tmap(i, t), 0)),
        out_shape=jax.ShapeDtypeStruct((2 * Bn, L, C), f32),
        scratch_shapes=[pltpu.VMEM((C // LANES, LANES, LANES), f32)],
        compiler_params=pltpu.CompilerParams(dimension_semantics=("parallel", "arbitrary"),
                                             vmem_limit_bytes=VMEM_LIMIT_BYTES),
        name="wkv7_chunked",
    )(r, k, v, kk, lw2, a2, ka)


def _mm_exact_rhs(a, b_bf16):
    ah = a.astype(bf16)
    al = (a - ah.astype(f32)).astype(bf16)
    d = lambda x: lax.dot_general(x, b_bf16, _DN['nn'], preferred_element_type=f32)
    return d(ah) + d(al)


def _head_sum_matrix(scale):
    i = lax.broadcasted_iota(jnp.int32, (RW_D, RW_D), 0) // RW_N
    j = lax.broadcasted_iota(jnp.int32, (RW_D, RW_D), 1) // RW_N
    return jnp.where(i == j, scale, 0.0).astype(bf16)


def _rw_prep_kernel(r_ref, k_ref, v_ref, rp_ref, kp_ref, vp_ref, rn_ref, kn_ref, vn_ref, w_ref, a_ref,
                    mu_ref, w0_ref, w2_ref, a0_ref, a2_ref, kkw_ref,
                    ro_ref, ko_ref, vo_ref, kko_ref, lw_ref, ao_ref, *, rows, nt):
    t = pl.program_id(1)
    has_prev = (t > 0).astype(f32)
    has_next = (t < nt - 1).astype(f32)
    ridx = lax.broadcasted_iota(jnp.int32, (rows, RW_D), 0)

    def tshift(cur_ref, prev_ref, next_ref, j):
        u = cur_ref[0]
        prev_row = prev_ref[0, SUBLANES - 1:SUBLANES, :] * has_prev
        next_row = next_ref[0, 0:1, :] * has_next
        up = jnp.where(ridx == 0, prev_row, pltpu.roll(u, 1, axis=0))
        un = jnp.where(ridx == rows - 1, next_row, pltpu.roll(u, rows - 1, axis=0))
        return u + mu_ref[j, 0:1, :] * (up - u) + mu_ref[j, 1:2, :] * (un - u)

    r = tshift(r_ref, rp_ref, rn_ref, 0)
    k = tshift(k_ref, kp_ref, kn_ref, 1)
    v = tshift(v_ref, vp_ref, vn_ref, 2)
    ro_ref[0] = r
    ko_ref[0] = k
    vo_ref[0] = v
    kk = k * kkw_ref[...]
    ss = _mm_exact_rhs(kk * kk, _head_sum_matrix(1.0))
    kko_ref[0] = kk * lax.rsqrt(jnp.maximum(ss, 1e-24))
    wl = jnp.tanh(w_ref[0])
    al = a_ref[0]
    for d in range(2):
        wpre = w0_ref[d:d + 1, :] + _mm(wl[:, d * RW_W_LORA:(d + 1) * RW_W_LORA], w2_ref[d])
        lw_ref[0, :, d * RW_D:(d + 1) * RW_D] = -math.exp(-0.5) * jax.nn.sigmoid(wpre)
        av = a0_ref[d:d + 1, :] + _mm(al[:, d * RW_A_LORA:(d + 1) * RW_A_LORA], a2_ref[d])
        ao_ref[0, :, d * RW_D:(d + 1) * RW_D] = jax.nn.sigmoid(av)


def rw_prep(proj, lora_lane0, mu, w0, w2, a0, a2, k_k):
    Bn, L, _ = proj.shape
    C = RW_D
    rows = RW_ROWS_PER_STEP
    nt = L // rows
    hb = rows // SUBLANES
    lb = lora_lane0 // LANES
    cur = lambda j: pl.BlockSpec((1, rows, C), lambda b, t: (b, t, j))
    prev = lambda j: pl.BlockSpec((1, SUBLANES, C), lambda b, t: (b, jnp.maximum(t * hb - 1, 0), j))
    nxt = lambda j: pl.BlockSpec((1, SUBLANES, C), lambda b, t: (b, jnp.minimum((t + 1) * hb, L // SUBLANES - 1), j))
    lora_w = pl.BlockSpec((1, rows, LANES), lambda b, t: (b, t, lb + 1))
    lora_a = pl.BlockSpec((1, rows, LANES), lambda b, t: (b, t, lb + 2))
    full = lambda shp: pl.BlockSpec(shp, lambda b, t: (0,) * len(shp))
    out_c = pl.BlockSpec((1, rows, C), lambda b, t: (b, t, 0))
    out_2c = pl.BlockSpec((1, rows, 2 * C), lambda b, t: (b, t, 0))
    sds = lambda c: jax.ShapeDtypeStruct((Bn, L, c), f32)
    return pl.pallas_call(
        functools.partial(_rw_prep_kernel, rows=rows, nt=nt),
        grid=(Bn, nt),
        in_specs=[cur(0), cur(1), cur(2), prev(0), prev(1), prev(2), nxt(0), nxt(1), nxt(2), lora_w, lora_a,
                  full((3, 2, C)), full((2, C)), full((2, RW_W_LORA, C)), full((2, C)), full((2, RW_A_LORA, C)),
                  full((1, C))],
        out_specs=[out_c, out_c, out_c, out_c, out_2c, out_2c],
        out_shape=[sds(C), sds(C), sds(C), sds(C), sds(2 * C), sds(2 * C)],
        compiler_params=pltpu.CompilerParams(dimension_semantics=("parallel", "parallel"),
                                             vmem_limit_bytes=VMEM_LIMIT_BYTES),
        name="rwkv_prep",
    )(proj, proj, proj, proj, proj, proj, proj, proj, proj, proj, proj,
      mu.astype(f32), w0.astype(f32), w2.astype(bf16), a0.astype(f32), a2.astype(bf16), k_k.astype(f32).reshape(1, C))


def _rw_post_kernel(yf_ref, yb_ref, r_ref, k_ref, v_ref, a_ref, g_ref, ka_ref, rk_ref, lnw_ref, lnb_ref, g2_ref, o_ref):
    C = RW_D
    y = yf_ref[0] + yb_ref[0]
    avg = _head_sum_matrix(1.0 / RW_N)
    mean = _mm_exact_rhs(y, avg)
    yc = y - mean
    var = _mm_exact_rhs(yc * yc, avg)
    yn = yc * lax.rsqrt(var + RW_GN_EPS) * lnw_ref[...] + lnb_ref[...]
    a = a_ref[0]
    k = k_ref[0]
    ka = ka_ref[...]
    ksum = k * (1.0 + (a[:, :C] - 1.0) * ka) + k * (1.0 + (a[:, C:] - 1.0) * ka)
    coef = _mm_exact_rhs(r_ref[0] * ksum * rk_ref[...], _head_sum_matrix(1.0))
    gate = _mm(jax.nn.sigmoid(g_ref[0]), g2_ref[...])
    o_ref[0] = (yn + coef * v_ref[0]) * gate


def rw_post(y2, r, k, v, a2, proj, g_lane0, k_a, r_k, ln_w, ln_b, g2):
    Bn, L, C = r.shape
    rows = RW_ROWS_PER_STEP
    nt = L // rows
    gb = g_lane0 // LANES
    blk = lambda c: pl.BlockSpec((1, rows, c), lambda b, t: (b, t, 0))
    full = lambda shp: pl.BlockSpec(shp, lambda b, t: (0,) * len(shp))
    row = lambda x: x.astype(f32).reshape(1, C)
    return pl.pallas_call(
        _rw_post_kernel,
        grid=(Bn, nt),
        in_specs=[blk(C), pl.BlockSpec((1, rows, C), lambda b, t: (b + Bn, t, 0)), blk(C), blk(C), blk(C), blk(2 * C),
                  pl.BlockSpec((1, rows, LANES), lambda b, t: (b, t, gb)),
                  full((1, C)), full((1, C)), full((1, C)), full((1, C)), full((RW_G_LORA, C))],
        out_specs=blk(C),
        out_shape=jax.ShapeDtypeStruct((Bn, L, C), f32),
        compiler_params=pltpu.CompilerParams(dimension_semantics=("parallel", "parallel"),
                                             vmem_limit_bytes=VMEM_LIMIT_BYTES),
        name="rwkv_post",
    )(y2, y2, r, k, v, a2, proj, row(k_a), row(r_k), row(ln_w), row(ln_b), g2.astype(bf16))


def _na_kernel(q_ref, k_ref, v_ref, bias_ref, o_ref, kb_ref, vb_ref, *, rows_per_step, n_rows, kr):
    W = GRID_W
    rb = pl.program_id(2)

    @pl.when(rb == 0)
    def _():
        kb_ref[...] = k_ref[0].astype(bf16)
        vb_ref[...] = v_ref[0].astype(bf16)

    lo_lane = lax.broadcasted_iota(jnp.int32, (W, LANES), 1) < NA_HD
    scale = NA_HD ** -0.5
    rows = []
    for j in range(rows_per_step):
        r = rb * rows_per_step + j
        start = jnp.clip(r - kr // 2, 0, n_rows - kr)
        rows.append((start, start - r + (NA_KR - 1)))
    s_list = []
    for j, (start, didx) in enumerate(rows):
        q = q_ref[0, j * W:(j + 1) * W, :] * scale
        qs = jnp.concatenate([jnp.where(lo_lane, q, 0.0), jnp.where(lo_lane, 0.0, q)], axis=0)
        kw = kb_ref[pl.ds(pl.multiple_of(start * W, W), kr * W), :]
        s_list.append(_mm(qs, kw, 'nt') + bias_ref[didx, 0])
    p_list = []
    for s in s_list:
        m = jnp.max(s, axis=-1, keepdims=True)
        p = jnp.exp(s - m)
        p_list.append((p, jnp.sum(p, axis=-1, keepdims=True)))
    for j, ((start, _), (p, l)) in enumerate(zip(rows, p_list)):
        vw = vb_ref[pl.ds(pl.multiple_of(start * W, W), kr * W), :]
        o = _mm(p, vw) / l
        o_ref[0, j * W:(j + 1) * W, :] = jnp.where(lo_lane, o[:W], o[W:])


def na_bias_table(rpb, n_rows):
    W = GRID_W
    kr = min(NA_KR, n_rows)
    cols = jnp.arange(W)
    col_start = jnp.clip(cols - NA_KC // 2, 0, W - NA_KC)
    in_band = (cols[None, :] >= col_start[:, None]) & (cols[None, :] < col_start[:, None] + NA_KC)
    dc = jnp.clip(cols[None, :] - cols[:, None], -(NA_KC - 1), NA_KC - 1) + (NA_KC - 1)
    win = jnp.stack([rpb.astype(f32)[:, d:d + kr, :] for d in range(8)], axis=1)
    onehot = (dc[:, :, None] == jnp.arange(2 * NA_KC - 1)[None, None, :]).astype(f32)
    tab = jnp.einsum('hdic,qkc->hdqik', win, onehot, precision=lax.Precision.HIGHEST)
    tab = jnp.where(in_band[None, None, :, None, :], tab, NEG_INF)
    H = rpb.shape[0]
    tab = tab.reshape(H // 2, 2, 8, W, kr * W).transpose(2, 0, 1, 3, 4)
    return tab.reshape(8, H // 2, 2 * W, kr * W)


def neighborhood_attention_pallas(proj, rpb, lane_block0):
    Bn, L, _ = proj.shape
    W = GRID_W
    n_rows = L // W
    kr = min(NA_KR, n_rows)
    hp = NA_H * NA_HD // LANES
    rps = NA_ROWS_PER_STEP
    bias = na_bias_table(rpb, n_rows)
    kern = functools.partial(_na_kernel, rows_per_step=rps, n_rows=n_rows, kr=kr)
    return pl.pallas_call(
        kern,
        grid=(Bn, hp, n_rows // rps),
        in_specs=[pl.BlockSpec((1, rps * W, LANES), lambda b, h, r: (b, r, lane_block0 + h)),
                  pl.BlockSpec((1, L, LANES), lambda b, h, r: (b, 0, lane_block0 + hp + h)),
                  pl.BlockSpec((1, L, LANES), lambda b, h, r: (b, 0, lane_block0 + 2 * hp + h)),
                  pl.BlockSpec((8, 1, 2 * W, kr * W), lambda b, h, r: (0, h, 0, 0))],
        out_specs=pl.BlockSpec((1, rps * W, LANES), lambda b, h, r: (b, r, h)),
        out_shape=jax.ShapeDtypeStruct((Bn, L, NA_H * NA_HD), f32),
        scratch_shapes=[pltpu.VMEM((L, LANES), bf16), pltpu.VMEM((L, LANES), bf16)],
        compiler_params=pltpu.CompilerParams(dimension_semantics=("parallel", "parallel", "arbitrary"),
                                             vmem_limit_bytes=VMEM_LIMIT_BYTES),
        name="neighborhood_attention",
    )(proj, proj, proj, bias)


def _moe_ffn_kernel(be_ref, x_ref, w1_ref, w3_ref, w2_ref, o_ref, w1b_ref, w3b_ref, w2b_ref):
    i = pl.program_id(0)

    @pl.when((i == 0) | (be_ref[i] != be_ref[jnp.maximum(i - 1, 0)]))
    def _():
        w1b_ref[...] = w1_ref[0, 0].astype(bf16)
        w3b_ref[...] = w3_ref[0, 0].astype(bf16)
        w2b_ref[...] = w2_ref[0, 0].astype(bf16)

    x = x_ref[...].astype(bf16)
    h1 = jnp.dot(x, w1b_ref[...], preferred_element_type=f32)
    h3 = jnp.dot(x, w3b_ref[...], preferred_element_type=f32)
    g = h1 * jax.nn.sigmoid(h1) * h3
    o_ref[...] = jnp.dot(g.astype(bf16), w2b_ref[...], preferred_element_type=f32)


def moe_expert_ffn(xs, block_e, w1, w3, w2, layer):
    rows, Dm = xs.shape
    n_blocks = rows // MOE_BLOCK
    grid_spec = pltpu.PrefetchScalarGridSpec(
        num_scalar_prefetch=1,
        grid=(n_blocks,),
        in_specs=[pl.BlockSpec((MOE_BLOCK, Dm), lambda i, be: (i, 0)),
                  pl.BlockSpec((1, 1, Dm, MOE_FF), lambda i, be: (layer, be[i], 0, 0)),
                  pl.BlockSpec((1, 1, Dm, MOE_FF), lambda i, be: (layer, be[i], 0, 0)),
                  pl.BlockSpec((1, 1, MOE_FF, Dm), lambda i, be: (layer, be[i], 0, 0))],
        out_specs=pl.BlockSpec((MOE_BLOCK, Dm), lambda i, be: (i, 0)),
        scratch_shapes=[pltpu.VMEM((Dm, MOE_FF), bf16), pltpu.VMEM((Dm, MOE_FF), bf16), pltpu.VMEM((MOE_FF, Dm), bf16)],
    )
    return pl.pallas_call(
        _moe_ffn_kernel,
        grid_spec=grid_spec,
        out_shape=jax.ShapeDtypeStruct((rows, Dm), f32),
        compiler_params=pltpu.CompilerParams(dimension_semantics=("arbitrary",), vmem_limit_bytes=VMEM_LIMIT_BYTES),
        name="moe_expert_ffn",
    )(block_e.astype(jnp.int32), xs, w1.astype(f32), w3.astype(f32), w2.astype(f32))


def _rms(x):
    return x * lax.rsqrt(jnp.mean(x * x, axis=-1, keepdims=True) + NORM_EPS)


def _norm_proj_kernel(x_ref, g_ref, w_ref, o_ref):
    h = (_rms(x_ref[...]) * g_ref[...]).astype(bf16)
    o_ref[...] = jnp.dot(h, w_ref[...], preferred_element_type=f32)


def norm_proj(x2, g, w):
    N, D = x2.shape
    F = w.shape[1]
    tm = PROJ_ROWS_PER_STEP
    return pl.pallas_call(
        _norm_proj_kernel,
        grid=(N // tm,),
        in_specs=[pl.BlockSpec((tm, D), lambda i: (i, 0)), pl.BlockSpec((1, D), lambda i: (0, 0)),
                  pl.BlockSpec((D, F), lambda i: (0, 0))],
        out_specs=pl.BlockSpec((tm, F), lambda i: (i, 0)),
        out_shape=jax.ShapeDtypeStruct((N, F), f32),
        compiler_params=pltpu.CompilerParams(dimension_semantics=("parallel",), vmem_limit_bytes=VMEM_LIMIT_BYTES),
        name="norm_proj",
    )(x2, g.astype(f32).reshape(1, D), w.astype(bf16))


def _out_proj_kernel(x_ref, yh_ref, yr_ref, yn_ref, wh_ref, wr_ref, wn_ref, g_ref, wrt_ref, xo_ref, h_ref, lg_ref):
    mix = (jnp.dot(yh_ref[...].astype(bf16), wh_ref[...], preferred_element_type=f32)
           + jnp.dot(yr_ref[...].astype(bf16), wr_ref[...], preferred_element_type=f32)
           + jnp.dot(yn_ref[...].astype(bf16), wn_ref[...], preferred_element_type=f32))
    x = x_ref[...] + mix
    xo_ref[...] = x
    h = (_rms(x) * g_ref[...]).astype(bf16)
    h_ref[...] = h
    lg_ref[...] = jnp.dot(h, wrt_ref[...], preferred_element_type=f32)


def out_proj_norm_router(x2, y_hy, y_rw, y_na, w_out, g, w_router):
    N, D = x2.shape
    tm = PROJ_ROWS_PER_STEP
    d_hy, d_rw, d_na = y_hy.shape[1], y_rw.shape[1], y_na.shape[1]
    nr = w_router.shape[1]
    w_router = jnp.pad(w_router.astype(bf16), ((0, 0), (0, LANES - nr)))
    wb = w_out.astype(bf16)
    row = lambda c: pl.BlockSpec((tm, c), lambda i: (i, 0))
    full = lambda r, c: pl.BlockSpec((r, c), lambda i: (0, 0))
    return pl.pallas_call(
        _out_proj_kernel,
        grid=(N // tm,),
        in_specs=[row(D), row(d_hy), row(d_rw), row(d_na), full(d_hy, D), full(d_rw, D), full(d_na, D), full(1, D),
                  full(D, LANES)],
        out_specs=[row(D), row(D), row(LANES)],
        out_shape=[jax.ShapeDtypeStruct((N, D), f32), jax.ShapeDtypeStruct((N, D), bf16),
                   jax.ShapeDtypeStruct((N, LANES), f32)],
        compiler_params=pltpu.CompilerParams(dimension_semantics=("parallel",), vmem_limit_bytes=VMEM_LIMIT_BYTES),
        name="out_proj_norm_router",
    )(x2, y_hy, y_rw, y_na, wb[:d_hy], wb[d_hy:d_hy + d_rw], wb[d_hy + d_rw:], g.astype(f32).reshape(1, D), w_router)


def _final_norm_kernel(x_ref, g_ref, o_ref):
    x = x_ref[...]
    o_ref[...] = x * lax.rsqrt(jnp.mean(x * x, axis=-1, keepdims=True) + NORM_EPS) * g_ref[...]


def final_rmsnorm(x, g):
    Bn, L, D = x.shape
    tm = 1024
    out = pl.pallas_call(
        _final_norm_kernel,
        grid=(Bn * L // tm,),
        in_specs=[pl.BlockSpec((tm, D), lambda i: (i, 0)), pl.BlockSpec((1, D), lambda i: (0, 0))],
        out_specs=pl.BlockSpec((tm, D), lambda i: (i, 0)),
        out_shape=jax.ShapeDtypeStruct((Bn * L, D), x.dtype),
        name="final_rmsnorm",
    )(x.reshape(Bn * L, D), g.reshape(1, D))
    return out.reshape(Bn, L, D)


def hyena_positional_features(L):
    t = jnp.linspace(0.0, 1.0, L, dtype=f32)[:, None]
    w = (2.0 * math.pi / L) * jnp.arange(L, dtype=f32)[:, None]
    f = jnp.linspace(1e-4, HY_BANDS - 1, HY_BANDS, dtype=f32)[None, :]
    z = jnp.concatenate([t, jnp.cos(f * w), -jnp.sin(f * w)], axis=-1)
    return z, t


def hyena_filters(z, t, w1, b1, w2, b2, w3, b3, wout, freq):
    fr = freq.astype(f32)
    act = lambda u: jnp.sin(fr * u)
    h = act(z @ w1.astype(f32) + b1.astype(f32))
    h = act(h @ w2.astype(f32) + b2.astype(f32))
    h = act(h @ w3.astype(f32) + b3.astype(f32))
    h = (h @ wout.astype(f32)).reshape(-1, 2, HY_D)
    deltas = jnp.abs(jnp.linspace(HY_MIN_DECAY, HY_MAX_DECAY, HY_D, dtype=f32))
    h = h * jnp.exp(-t[:, :, None] * deltas)
    return h * lax.rsqrt(jnp.sum(h * h, axis=(0, 1), keepdims=True) + 1e-6)


def _split_bf16(x):
    hi = x.astype(bf16)
    return hi, (x - hi.astype(f32)).astype(bf16)


def _dot3(m_hi, m_lo, x_hi, x_lo):
    d = lambda a, b: lax.dot_general(a, b, _DN['nn'], preferred_element_type=f32)
    return d(m_hi, x_hi) + (d(m_hi, x_lo) + d(m_lo, x_hi))


def _cmatmul(mr, mi, xr, xi):
    xrh, xrl = _split_bf16(xr)
    rr = _dot3(mr[0], mr[1], xrh, xrl)
    ir = _dot3(mi[0], mi[1], xrh, xrl)
    if xi is None:
        return rr, ir
    xih, xil = _split_bf16(xi)
    ii = _dot3(mi[0], mi[1], xih, xil)
    ri = _dot3(mr[0], mr[1], xih, xil)
    return rr - ii, ri + ir


def _row_dft_kernel(mrh_ref, mrl_ref, mih_ref, mil_ref, *refs, real_input):
    if real_input:
        ur_ref, or_ref, oi_ref = refs
        ui = None
    else:
        ur_ref, ui_ref, or_ref, oi_ref = refs
        ui = ui_ref[0]
    o_r, o_i = _cmatmul((mrh_ref[...], mrl_ref[...]), (mih_ref[...], mil_ref[...]), ur_ref[0], ui)
    or_ref[0] = o_r
    oi_ref[0] = o_i


def row_dft(tabs, u, packed):
    rows, r_in, W = u.shape
    P = rows // 2 if packed else rows
    r_out = tabs[0].shape[0]
    wb = min(FFT_LANE_BLOCK, W)
    tab_spec = pl.BlockSpec((r_out, r_in), lambda p, j: (0, 0))
    re_spec = pl.BlockSpec((1, r_in, wb), lambda p, j: (p, 0, j))
    im_spec = pl.BlockSpec((1, r_in, wb), lambda p, j: (p + P, 0, j))
    out_spec = pl.BlockSpec((1, r_out, wb), lambda p, j: (p, 0, j))
    ins = [u, u] if packed else [u]
    return pl.pallas_call(
        functools.partial(_row_dft_kernel, real_input=not packed),
        grid=(P, W // wb),
        in_specs=[tab_spec] * 4 + ([re_spec, im_spec] if packed else [re_spec]),
        out_specs=[out_spec, out_spec],
        out_shape=[jax.ShapeDtypeStruct((P, r_out, W), f32)] * 2,
        compiler_params=pltpu.CompilerParams(dimension_semantics=("parallel", "parallel"),
                                             vmem_limit_bytes=VMEM_LIMIT_BYTES),
        name="hyena_row_dft",
    )(*tabs, *ins)


def _row_idft_gate_kernel(mrh_ref, mrl_ref, mih_ref, mil_ref, dr_ref, di_ref, z0_ref, z1_ref, x0_ref, x1_ref,
                          skip_ref, o0_ref, o1_ref):
    y_r, y_i = _cmatmul((mrh_ref[...], mrl_ref[...]), (mih_ref[...], mil_ref[...]), dr_ref[0], di_ref[0])
    skip = skip_ref[...]
    o0_ref[0] = x0_ref[0] * (y_r + z0_ref[0] * skip)
    o1_ref[0] = x1_ref[0] * (y_i + z1_ref[0] * skip)


def row_idft_gate(tabs, dr, di, z, x0, skip_row):
    P, r_in, W = dr.shape
    r_out = tabs[0].shape[0]
    wb = min(FFT_LANE_BLOCK, W)
    tab_spec = pl.BlockSpec((r_out, r_in), lambda p, j: (0, 0))
    d_spec = pl.BlockSpec((1, r_in, wb), lambda p, j: (p, 0, j))
    lo = pl.BlockSpec((1, r_out, wb), lambda p, j: (p, 0, j))
    hi = pl.BlockSpec((1, r_out, wb), lambda p, j: (p + P, 0, j))
    o0, o1 = pl.pallas_call(
        _row_idft_gate_kernel,
        grid=(P, W // wb),
        in_specs=[tab_spec] * 4 + [d_spec, d_spec, lo, hi, lo, hi, pl.BlockSpec((1, wb), lambda p, j: (0, j))],
        out_specs=[lo, lo],
        out_shape=[jax.ShapeDtypeStruct((P, r_out, W), f32)] * 2,
        compiler_params=pltpu.CompilerParams(dimension_semantics=("parallel", "parallel"),
                                             vmem_limit_bytes=VMEM_LIMIT_BYTES),
        name="hyena_row_idft_gate",
    )(*tabs, dr, di, z, z, x0, x0, skip_row)
    return jnp.concatenate([o0, o1], axis=0)


def _col_dft_kernel(grh_ref, grl_ref, gih_ref, gil_ref, ar_ref, ai_ref, br_ref, bi_ref, *, k1_per_step):
    for j in range(k1_per_step):
        b_r, b_i = _cmatmul((grh_ref[j], grl_ref[j]), (gih_ref[j], gil_ref[j]), ar_ref[0, j], ai_ref[0, j])
        br_ref[0, j] = b_r
        bi_ref[0, j] = b_i


def _col_conv_kernel(grh_ref, grl_ref, gih_ref, gil_ref, trh_ref, trl_ref, tih_ref, til_ref,
                     ar_ref, ai_ref, kr_ref, ki_ref, dr_ref, di_ref, *, k1_per_step):
    for j in range(k1_per_step):
        b_r, b_i = _cmatmul((grh_ref[j], grl_ref[j]), (gih_ref[j], gil_ref[j]), ar_ref[0, j], ai_ref[0, j])
        k_r = kr_ref[0, j]
        k_i = ki_ref[0, j]
        c_r = b_r * k_r - b_i * k_i
        c_i = b_r * k_i + b_i * k_r
        d_r, d_i = _cmatmul((trh_ref[j], trl_ref[j]), (tih_ref[j], til_ref[j]), c_r, c_i)
        dr_ref[0, j] = d_r
        di_ref[0, j] = d_i


def col_stage(g_tabs, gt_tabs, ar, ai, kr=None, ki=None):
    P, n1, n2, C = ar.shape
    kb = min(FFT_K1_PER_STEP, n1)
    g_spec = pl.BlockSpec((kb, n2, n2), lambda p, j: (j, 0, 0))
    a_spec = pl.BlockSpec((1, kb, n2, C), lambda p, j: (p, j, 0, 0))
    k_spec = pl.BlockSpec((1, kb, n2, C), lambda p, j: (0, j, 0, 0))
    common = dict(
        grid=(P, n1 // kb),
        out_specs=[a_spec, a_spec],
        out_shape=[jax.ShapeDtypeStruct((P, n1, n2, C), f32)] * 2,
        compiler_params=pltpu.CompilerParams(dimension_semantics=("parallel", "parallel"),
                                             vmem_limit_bytes=VMEM_LIMIT_BYTES),
    )
    if kr is None:
        return pl.pallas_call(functools.partial(_col_dft_kernel, k1_per_step=kb),
                              in_specs=[g_spec] * 4 + [a_spec] * 2, name="hyena_col_dft", **common)(*g_tabs, ar, ai)
    return pl.pallas_call(functools.partial(_col_conv_kernel, k1_per_step=kb),
                          in_specs=[g_spec] * 8 + [a_spec] * 2 + [k_spec] * 2, name="hyena_col_conv", **common)(
        *g_tabs, *gt_tabs, ar, ai, kr, ki)


def _bf16_tables(m):
    out = []
    for part in (np.real(m), np.imag(m)):
        x = jnp.asarray(part, f32)
        hi = x.astype(bf16)
        out += [hi, (x - hi.astype(f32)).astype(bf16)]
    return out


def hyena_dft_tables(L):
    n2 = FFT_N2
    n1 = 2 * L // n2
    N = n1 * n2
    a = np.arange(n1)
    fa = np.exp(-2j * np.pi * np.outer(a, a) / n1)
    k1 = np.arange(n1)[:, None, None]
    k2 = np.arange(n2)[None, :, None]
    nn = np.arange(n2)[None, None, :]
    g = np.exp(-2j * np.pi * nn * (k1 + n1 * k2) / N)
    gt = np.conj(np.transpose(g, (0, 2, 1)))
    fc = np.conj(fa).T / N
    return dict(fa=_bf16_tables(fa), fa_half=_bf16_tables(fa[:, :n1 // 2]), g=_bf16_tables(g), gt=_bf16_tables(gt),
                fc_half=_bf16_tables(fc[:n1 // 2]))


def hyena_long_conv_gate(tabs, z, x0, k2, skip):
    Bn, L, C = z.shape
    n2 = FFT_N2
    n1 = 2 * L // n2
    P = Bn // 2
    W = n2 * C
    ar, ai = row_dft(tabs['fa'], k2.reshape(1, n1, W), packed=False)
    kr, ki = col_stage(tabs['g'], tabs['gt'], ar.reshape(1, n1, n2, C), ai.reshape(1, n1, n2, C))
    zv = z.reshape(Bn, n1 // 2, W)
    ar, ai = row_dft(tabs['fa_half'], zv, packed=True)
    dr, di = col_stage(tabs['g'], tabs['gt'], ar.reshape(P, n1, n2, C), ai.reshape(P, n1, n2, C), kr, ki)
    out = row_idft_gate(tabs['fc_half'], dr.reshape(P, n1, W), di.reshape(P, n1, W), zv, x0.reshape(Bn, n1 // 2, W),
                        jnp.tile(skip.astype(f32), n2).reshape(1, W))
    return out.reshape(Bn, L, C)


def _hyena_prep_kernel(*refs, rows, nt):
    cur, prev, nxt = refs[0:3], refs[3:6], refs[6:9]
    w_ref, b_ref, z_ref, x0_ref = refs[9:]
    t = pl.program_id(1)
    has_prev = (t > 0).astype(f32)
    has_next = (t < nt - 1).astype(f32)
    ridx = lax.broadcasted_iota(jnp.int32, (rows, HY_D), 0)

    def conv(j):
        u = cur[j][0]
        prev_row = prev[j][0, SUBLANES - 1:SUBLANES, :] * has_prev
        next_row = nxt[j][0, 0:1, :] * has_next
        up = jnp.where(ridx == 0, prev_row, pltpu.roll(u, 1, axis=0))
        un = jnp.where(ridx == rows - 1, next_row, pltpu.roll(u, rows - 1, axis=0))
        sl = slice(j * HY_D, (j + 1) * HY_D)
        return up * w_ref[0:1, sl] + u * w_ref[1:2, sl] + un * w_ref[2:3, sl] + b_ref[:, sl]

    x0_ref[0] = conv(0)
    z_ref[0] = conv(2) * conv(1)


def hyena_prep(proj, lane0, conv_w, conv_b):
    Bn, L, _ = proj.shape
    rows = RW_ROWS_PER_STEP
    nt = L // rows
    hb = rows // SUBLANES
    b0 = lane0 // HY_D
    assert b0 * HY_D == lane0
    cur = lambda j: pl.BlockSpec((1, rows, HY_D), lambda b, t: (b, t, b0 + j))
    prev = lambda j: pl.BlockSpec((1, SUBLANES, HY_D), lambda b, t: (b, jnp.maximum(t * hb - 1, 0), b0 + j))
    nxt = lambda j: pl.BlockSpec((1, SUBLANES, HY_D), lambda b, t: (b, jnp.minimum((t + 1) * hb, L // SUBLANES - 1), b0 + j))
    full = lambda shp: pl.BlockSpec(shp, lambda b, t: (0,) * len(shp))
    out = pl.BlockSpec((1, rows, HY_D), lambda b, t: (b, t, 0))
    return pl.pallas_call(
        functools.partial(_hyena_prep_kernel, rows=rows, nt=nt),
        grid=(Bn, nt),
        in_specs=[cur(0), cur(1), cur(2), prev(0), prev(1), prev(2), nxt(0), nxt(1), nxt(2),
                  full((3, 3 * HY_D)), full((1, 3 * HY_D))],
        out_specs=[out, out],
        out_shape=[jax.ShapeDtypeStruct((Bn, L, HY_D), f32)] * 2,
        compiler_params=pltpu.CompilerParams(dimension_semantics=("parallel", "parallel")),
        name="hyena_prep",
    )(*([proj] * 9), conv_w.astype(f32), conv_b.astype(f32).reshape(1, 3 * HY_D))


def hyena_mixer(tabs, proj, lane0, conv_w, conv_b, filt, skip):
    z, x0 = hyena_prep(proj, lane0, conv_w, conv_b)
    k2 = jnp.concatenate([filt[:1, 0] + filt[:1, 1], filt[1:, 0],
                          jnp.zeros((1, HY_D), f32), filt[1:, 1][::-1]], axis=0)
    return hyena_long_conv_gate(tabs, z, x0, k2, skip)


def rwkv7_mixer(proj, lora_lane0, mu, w0, w2, a0, a2, k_k, k_a, r_k, g2, ln_w, ln_b):
    r, k, v, kk, lw2, a_both = rw_prep(proj, lora_lane0, mu, w0, w2, a0, a2, k_k)
    y2 = wkv7_chunked(r, k, v, kk, lw2, a_both, k_a.astype(f32).reshape(1, RW_D))
    return rw_post(y2, r, k, v, a_both, proj, lora_lane0, k_a, r_k, ln_w, ln_b, g2)


def hier_moe(h, logits, bg, be, w1, w3, w2, layer):
    N, Dm = h.shape
    assert MOE_TOPK == 2
    g_logits = logits[:, :MOE_GROUPS] + bg.astype(f32)
    g_sel = jnp.argmax(g_logits, axis=-1)
    g_prob = jnp.take_along_axis(jax.nn.softmax(g_logits, axis=-1), g_sel[:, None], axis=-1)
    e_logits = (logits[:, MOE_GROUPS:MOE_GROUPS + MOE_EXPERTS] + be.astype(f32)).reshape(N, MOE_GROUPS, MOE_PER_GROUP)
    e_logits = jnp.take_along_axis(e_logits, g_sel[:, None, None], axis=1)[:, 0]
    top_val, top_idx = lax.top_k(e_logits, MOE_TOPK)
    gate = g_prob * jax.nn.softmax(top_val, axis=-1)
    expert = g_sel[:, None] * MOE_PER_GROUP + top_idx
    M = N * MOE_TOPK
    flat_e = expert.reshape(M).astype(jnp.int32)
    experts = jnp.arange(MOE_EXPERTS, dtype=jnp.int32)
    counts = jnp.sum((flat_e[:, None] == experts[None, :]).astype(jnp.int32), axis=0)
    padded = (counts + MOE_BLOCK - 1) // MOE_BLOCK * MOE_BLOCK
    pad_end = jnp.cumsum(padded)
    n_blocks = -(-M // MOE_BLOCK) + MOE_EXPERTS
    n_rows = n_blocks * MOE_BLOCK
    cum_need = jnp.cumsum(padded - counts)
    filler = jnp.arange(n_rows - M, dtype=jnp.int32)
    filler_e = jnp.sum((cum_need[None, :] <= filler[:, None]).astype(jnp.int32), axis=1)
    keys = jnp.concatenate([2 * flat_e, 2 * filler_e + 1])
    ids = jnp.concatenate([jnp.arange(M, dtype=jnp.int32), jnp.full((n_rows - M,), M, jnp.int32)])
    _, slot_src = lax.sort((keys, ids), num_keys=1)
    tok_src = jnp.where(slot_src < M, slot_src // MOE_TOPK, 0)
    block_start = jnp.arange(n_blocks, dtype=pad_end.dtype) * MOE_BLOCK
    block_e = jnp.minimum(jnp.sum(pad_end[None, :] <= block_start[:, None], axis=1), MOE_EXPERTS - 1)
    y = moe_expert_ffn(h[tok_src], block_e, w1, w3, w2, layer)
    _, row_of = lax.sort((slot_src, jnp.arange(n_rows, dtype=jnp.int32)), num_keys=1)
    pos = row_of[:M].reshape(N, MOE_TOPK)
    return gate[:, 0:1] * y[pos[:, 0]] + gate[:, 1:2] * y[pos[:, 1]]


def kernel(x, norm1_g, w_in, hy_conv_w, hy_conv_b, hy_w1, hy_b1, hy_w2, hy_b2, hy_w3, hy_b3, hy_wout, hy_freq, hy_skip, rw_mu, rw_w0, rw_w2, rw_a0, rw_a2, rw_kk, rw_ka, rw_rk, rw_g2, rw_ln_w, rw_ln_b, na_rpb, w_out, norm2_g, moe_wg, moe_bg, moe_we, moe_be, moe_w1, moe_w3, moe_w2, norm_f_g):
    Bn, L, _ = x.shape
    z_pos, t_pos = hyena_positional_features(L)
    dft_tabs = hyena_dft_tables(L)
    splits = np.cumsum(IN_SIZES)[:-1].tolist()
    hy_end, rkv_end = splits[0], splits[1]
    hy_lane0 = rkv_end - hy_end
    lora_lane0, na_lane0 = splits[1], splits[4]
    assert RW_D % LANES == 0 and lora_lane0 % LANES == 0 and na_lane0 % LANES == 0
    N = Bn * L
    x2 = x.reshape(N, D_MODEL)
    for l in range(DEPTH):
        w_in_l = jnp.concatenate([w_in[l][:, hy_end:rkv_end], w_in[l][:, :hy_end], w_in[l][:, rkv_end:]], axis=1)
        proj = norm_proj(x2, norm1_g[l], w_in_l).reshape(Bn, L, IN_D)
        filt = hyena_filters(z_pos, t_pos, hy_w1[l], hy_b1[l], hy_w2[l], hy_b2[l],
                             hy_w3[l], hy_b3[l], hy_wout[l], hy_freq[l])
        y_hy = hyena_mixer(dft_tabs, proj, hy_lane0, hy_conv_w[l], hy_conv_b[l], filt, hy_skip[l])
        y_rw = rwkv7_mixer(proj, lora_lane0, rw_mu[l], rw_w0[l], rw_w2[l], rw_a0[l],
                           rw_a2[l], rw_kk[l], rw_ka[l], rw_rk[l].reshape(RW_D), rw_g2[l], rw_ln_w[l], rw_ln_b[l])
        y_na = neighborhood_attention_pallas(proj, na_rpb[l], na_lane0 // LANES)
        x2, h2, logits = out_proj_norm_router(
            x2, y_hy.reshape(N, HY_D), y_rw.reshape(N, RW_D), y_na.reshape(N, NA_D), w_out[l], norm2_g[l],
            jnp.concatenate([moe_wg[l], moe_we[l]], axis=1))
        x2 = x2 + hier_moe(h2, logits, moe_bg[l], moe_be[l], moe_w1, moe_w3, moe_w2, l)
    return final_rmsnorm(x2.reshape(Bn, L, D_MODEL), norm_f_g)
```

```python
import functools
import math

import jax
import jax.numpy as jnp
import numpy as np
from jax import lax
from jax.experimental import pallas as pl
from jax.experimental.pallas import tpu as pltpu

f32 = jnp.float32
bf16 = jnp.bfloat16

D_MODEL = 1024
DEPTH = 2
GRID_W = 64
NORM_EPS = 1e-6
NEG_INF = -1e30

HY_D = D_MODEL // 4
HY_EMB = 33
HY_BANDS = (HY_EMB - 1) // 2
HY_FFN = 64
HY_MIN_DECAY = math.log(1e-2) / 1.5
HY_MAX_DECAY = math.log(1e-2) / 0.3

RW_N = 64
RW_D = D_MODEL // 2
RW_H = RW_D // RW_N
RW_W_LORA = 64
RW_A_LORA = 64
RW_G_LORA = 128
RW_GN_EPS = 64e-5

NA_HD = 64
NA_D = D_MODEL // 4
NA_H = NA_D // NA_HD
NA_KR = 8
NA_KC = 16

MIX_D = HY_D + RW_D + NA_D
IN_SIZES = (3 * HY_D, 3 * RW_D, RW_G_LORA, 2 * RW_W_LORA, 2 * RW_A_LORA, 3 * NA_D)
IN_D = sum(IN_SIZES)

MOE_GROUPS = 4
MOE_PER_GROUP = 8
MOE_EXPERTS = MOE_GROUPS * MOE_PER_GROUP
MOE_TOPK = 2
MOE_FF = 512
MOE_BLOCK = 512

LANES = 128
WKV_CHUNK = 64
WKV_TIME_BLOCK = 512
NA_ROWS_PER_STEP = 4
RW_ROWS_PER_STEP = 512
SUBLANES = 8
PROJ_ROWS_PER_STEP = 512
FFT_N2 = 128
FFT_N2_PER_STEP = 8
FFT_K1_PER_STEP = 4
VMEM_LIMIT_BYTES = 48 * 1024 * 1024

_DN = {'nn': (((1,), (0,)), ((), ())), 'nt': (((1,), (1,)), ((), ())), 'tn': (((0,), (0,)), ((), ()))}


def _mm(a, b, dims='nn'):
    return lax.dot_general(a.astype(bf16), b.astype(bf16), _DN[dims], preferred_element_type=f32)


def _wkv_kernel(r_ref, k_ref, v_ref, kk_ref, lw_ref, a_ref, ka_ref, y_ref, ht_ref, *, tb, batch):
    T = WKV_CHUNK
    H2 = 2 * T
    nc = tb // T
    npairs = r_ref.shape[-1] // LANES
    mm = _mm

    @pl.when(pl.program_id(1) == 0)
    def _():
        ht_ref[...] = jnp.zeros_like(ht_ref)

    d = pl.program_id(0) // batch
    sign = 1 - 2 * d
    trow = lax.broadcasted_iota(jnp.int32, (T, H2), 0)
    lane = lax.broadcasted_iota(jnp.int32, (T, H2), 1)
    scol = lane % T
    tdiff = (trow - scol) * sign
    strict = tdiff > 0
    incl = tdiff >= 0
    same16 = (trow // 16) == (scol // 16)
    same32 = (trow // 32) == (scol // 32)
    off32 = same32 & jnp.logical_not(same16)
    off64 = jnp.logical_not(same32)
    eye2 = jnp.where(trow == scol, 1.0, 0.0).astype(f32)
    lo_lane = lane < T
    brow = lax.broadcasted_iota(jnp.int32, (H2, H2), 0)
    bcol = lax.broadcasted_iota(jnp.int32, (H2, H2), 1)
    same_head = (brow // T) == (bcol // T)
    crow = lax.broadcasted_iota(jnp.int32, (T, T), 0)
    ccol = lax.broadcasted_iota(jnp.int32, (T, T), 1)
    tri = jnp.where((crow - ccol) * sign >= 0, 1.0, 0.0).astype(bf16)
    is_bwd = d == 1

    def bd(x):
        return jnp.concatenate([jnp.where(lo_lane, x, 0.0), jnp.where(lo_lane, 0.0, x)], axis=0)

    inst = [(s, p) for s in range(nc) for p in range(npairs)]
    offs = [pl.multiple_of((s + d * (nc - 1 - 2 * s)) * T, T) for s in range(nc)]

    def load(ref, s, p):
        return ref[0, pl.ds(offs[s], T), p * LANES:(p + 1) * LANES]

    cs_l = []
    for s, p in inst:
        lw = load(lw_ref, s, p)
        l1 = lw.astype(bf16)
        r1 = lw - l1.astype(f32)
        l2 = r1.astype(bf16)
        l3 = (r1 - l2.astype(f32)).astype(bf16)
        dd = lambda x: lax.dot_general(tri, x, _DN['nn'], preferred_element_type=f32)
        cs_l.append((dd(l1) + (dd(l2) + dd(l3)), lw))
    ops = []
    for (s, p), (cs, lw) in zip(inst, cs_l):
        r = load(r_ref, s, p)
        k = load(k_ref, s, p)
        v = load(v_ref, s, p)
        kk = load(kk_ref, s, p)
        a = load(a_ref, s, p)
        ka = ka_ref[:, p * LANES:(p + 1) * LANES]
        kd = k * (1.0 + (a - 1.0) * ka)
        b = kk * a
        cs_end = jnp.where(is_bwd, cs[0:1, :], cs[T - 1:T, :])
        em = jnp.exp(-cs)
        e_end = jnp.exp(cs_end - cs)
        ops.append(dict(
            AR=jnp.concatenate([-kk * jnp.exp(cs - lw), r * jnp.exp(cs)], axis=0),
            BK=jnp.concatenate([bd(b * em), bd(kd * em)], axis=0),
            V=v, BKg=jnp.concatenate([b * e_end, kd * e_end], axis=0), g_end=jnp.exp(cs_end)))
    for o in ops:
        S = mm(o['AR'], o['BK'], 'nt')
        N = jnp.where(strict, S[:T, :H2], 0.0)
        o['AakArk'] = jnp.concatenate([jnp.where(strict, S[:T, H2:], 0.0), jnp.where(incl, S[T:, H2:], 0.0)], axis=0)
        o['Arb'] = jnp.where(incl, S[T:, :H2], 0.0)
        o['Nd'] = jnp.where(same16, N, 0.0)
        o['N32'] = jnp.where(off32, N, 0.0)
        o['N64'] = jnp.where(off64, N, 0.0)
    for o in ops:
        o['X'] = eye2 + o['Nd']
        o['P'] = mm(o['Nd'], bd(o['Nd']))
    for it in range(3):
        if it < 2:
            for o in ops:
                px = mm(o['P'], jnp.concatenate([bd(o['X']), bd(o['P'])], axis=1))
                o['X'] = o['X'] + px[:, :H2]
                o['P'] = px[:, H2:]
        else:
            for o in ops:
                o['X'] = o['X'] + mm(o['P'], bd(o['X']))
    for key in ('N32', 'N64'):
        for o in ops:
            o['Z'] = mm(o[key], bd(o['X']))
        for o in ops:
            o['X'] = o['X'] + mm(o['X'], bd(o['Z']))
    for o in ops:
        wy = mm(o['AakArk'], bd(o['V']))
        o['W0'] = wy[:T]
        o['Yv'] = wy[T:]
    for o in ops:
        o['XAW'] = mm(o['X'], jnp.concatenate([bd(o['AR'][:T]), bd(o['W0'])], axis=1))
    for o in ops:
        ax = mm(o['Arb'], jnp.concatenate([bd(o['XAW'][:, :H2]), bd(o['XAW'][:, H2:])], axis=1))
        o['Rhat'] = o['AR'][T:] + ax[:, :H2]
        o['Yc'] = ax[:, H2:] + o['Yv']
    for o in ops:
        lhs = jnp.concatenate([o['XAW'], jnp.concatenate([jnp.zeros((T, H2), f32), o['V']], axis=1)], axis=0)
        pq = mm(lhs, o['BKg'], 'tn')
        o['Pc'] = jnp.where(same_head, pq[:H2], 0.0)
        o['Qc'] = jnp.where(same_head, pq[H2:], 0.0)
    hts = [ht_ref[p] for p in range(npairs)]
    for s in range(nc):
        cur = [ops[s * npairs + p] for p in range(npairs)]
        ys = [mm(o['Rhat'], hts[p], 'nt') + o['Yc'] for p, o in enumerate(cur)]
        hts = [hts[p] * o['g_end'] + (mm(hts[p], o['Pc']) + o['Qc']) for p, o in enumerate(cur)]
        for p in range(npairs):
            y_ref[0, pl.ds(offs[s], T), p * LANES:(p + 1) * LANES] = ys[p]
    for p in range(npairs):
        ht_ref[p] = hts[p]


def wkv7_chunked(r, k, v, kk, lw2, a2, ka):
    Bn, L, C = r.shape
    tb = WKV_TIME_BLOCK
    nt = L // tb
    tmap = lambda i, t: t + (i // Bn) * (nt - 1 - 2 * t)
    shared = pl.BlockSpec((1, tb, C), lambda i, t: (i % Bn, tmap(i, t), 0))
    per_dir = pl.BlockSpec((1, tb, C), lambda i, t: (i % Bn, tmap(i, t), i // Bn))
    return pl.pallas_call(
        functools.partial(_wkv_kernel, tb=tb, batch=Bn),
        grid=(2 * Bn, nt),
        in_specs=[shared, shared, shared, shared, per_dir, per_dir, pl.BlockSpec((1, C), lambda i, t: (0, 0))],
        out_specs=pl.BlockSpec((1, tb, C), lambda i, t: (i, tmap(i, t), 0)),
        out_shape=jax.ShapeDtypeStruct((2 * Bn, L, C), f32),
        scratch_shapes=[pltpu.VMEM((C // LANES, LANES, LANES), f32)],
        compiler_params=pltpu.CompilerParams(dimension_semantics=("parallel", "arbitrary"),
                                             vmem_limit_bytes=VMEM_LIMIT_BYTES),
        name="wkv7_chunked",
    )(r, k, v, kk, lw2, a2, ka)


def _mm_exact_rhs(a, b_bf16):
    ah = a.astype(bf16)
    al = (a - ah.astype(f32)).astype(bf16)
    d = lambda x: lax.dot_general(x, b_bf16, _DN['nn'], preferred_element_type=f32)
    return d(ah) + d(al)


def _head_sum_matrix(scale):
    i = lax.broadcasted_iota(jnp.int32, (RW_D, RW_D), 0) // RW_N
    j = lax.broadcasted_iota(jnp.int32, (RW_D, RW_D), 1) // RW_N
    return jnp.where(i == j, scale, 0.0).astype(bf16)


def _rw_prep_kernel(r_ref, k_ref, v_ref, rp_ref, kp_ref, vp_ref, rn_ref, kn_ref, vn_ref, w_ref, a_ref,
                    mu_ref, w0_ref, w2_ref, a0_ref, a2_ref, kkw_ref,
                    ro_ref, ko_ref, vo_ref, kko_ref, lw_ref, ao_ref, *, rows, nt):
    t = pl.program_id(1)
    has_prev = (t > 0).astype(f32)
    has_next = (t < nt - 1).astype(f32)
    ridx = lax.broadcasted_iota(jnp.int32, (rows, RW_D), 0)

    def tshift(cur_ref, prev_ref, next_ref, j):
        u = cur_ref[0]
        prev_row = prev_ref[0, SUBLANES - 1:SUBLANES, :] * has_prev
        next_row = next_ref[0, 0:1, :] * has_next
        up = jnp.where(ridx == 0, prev_row, pltpu.roll(u, 1, axis=0))
        un = jnp.where(ridx == rows - 1, next_row, pltpu.roll(u, rows - 1, axis=0))
        return u + mu_ref[j, 0:1, :] * (up - u) + mu_ref[j, 1:2, :] * (un - u)

    r = tshift(r_ref, rp_ref, rn_ref, 0)
    k = tshift(k_ref, kp_ref, kn_ref, 1)
    v = tshift(v_ref, vp_ref, vn_ref, 2)
    ro_ref[0] = r
    ko_ref[0] = k
    vo_ref[0] = v
    kk = k * kkw_ref[...]
    ss = _mm_exact_rhs(kk * kk, _head_sum_matrix(1.0))
    kko_ref[0] = kk * lax.rsqrt(jnp.maximum(ss, 1e-24))
    wl = jnp.tanh(w_ref[0])
    al = a_ref[0]
    for d in range(2):
        wpre = w0_ref[d:d + 1, :] + _mm(wl[:, d * RW_W_LORA:(d + 1) * RW_W_LORA], w2_ref[d])
        lw_ref[0, :, d * RW_D:(d + 1) * RW_D] = -math.exp(-0.5) * jax.nn.sigmoid(wpre)
        av = a0_ref[d:d + 1, :] + _mm(al[:, d * RW_A_LORA:(d + 1) * RW_A_LORA], a2_ref[d])
        ao_ref[0, :, d * RW_D:(d + 1) * RW_D] = jax.nn.sigmoid(av)


def rw_prep(proj, lora_lane0, mu, w0, w2, a0, a2, k_k):
    Bn, L, _ = proj.shape
    C = RW_D
    rows = RW_ROWS_PER_STEP
    nt = L // rows
    hb = rows // SUBLANES
    lb = lora_lane0 // LANES
    cur = lambda j: pl.BlockSpec((1, rows, C), lambda b, t: (b, t, j))
    prev = lambda j: pl.BlockSpec((1, SUBLANES, C), lambda b, t: (b, jnp.maximum(t * hb - 1, 0), j))
    nxt = lambda j: pl.BlockSpec((1, SUBLANES, C), lambda b, t: (b, jnp.minimum((t + 1) * hb, L // SUBLANES - 1), j))
    lora_w = pl.BlockSpec((1, rows, LANES), lambda b, t: (b, t, lb + 1))
    lora_a = pl.BlockSpec((1, rows, LANES), lambda b, t: (b, t, lb + 2))
    full = lambda shp: pl.BlockSpec(shp, lambda b, t: (0,) * len(shp))
    out_c = pl.BlockSpec((1, rows, C), lambda b, t: (b, t, 0))
    out_2c = pl.BlockSpec((1, rows, 2 * C), lambda b, t: (b, t, 0))
    sds = lambda c: jax.ShapeDtypeStruct((Bn, L, c), f32)
    return pl.pallas_call(
        functools.partial(_rw_prep_kernel, rows=rows, nt=nt),
        grid=(Bn, nt),
        in_specs=[cur(0), cur(1), cur(2), prev(0), prev(1), prev(2), nxt(0), nxt(1), nxt(2), lora_w, lora_a,
                  full((3, 2, C)), full((2, C)), full((2, RW_W_LORA, C)), full((2, C)), full((2, RW_A_LORA, C)),
                  full((1, C))],
        out_specs=[out_c, out_c, out_c, out_c, out_2c, out_2c],
        out_shape=[sds(C), sds(C), sds(C), sds(C), sds(2 * C), sds(2 * C)],
        compiler_params=pltpu.CompilerParams(dimension_semantics=("parallel", "parallel"),
                                             vmem_limit_bytes=VMEM_LIMIT_BYTES),
        name="rwkv_prep",
    )(proj, proj, proj, proj, proj, proj, proj, proj, proj, proj, proj,
      mu.astype(f32), w0.astype(f32), w2.astype(bf16), a0.astype(f32), a2.astype(bf16), k_k.astype(f32).reshape(1, C))


def _rw_post_kernel(yf_ref, yb_ref, r_ref, k_ref, v_ref, a_ref, g_ref, ka_ref, rk_ref, lnw_ref, lnb_ref, g2_ref, o_ref):
    C = RW_D
    y = yf_ref[0] + yb_ref[0]
    avg = _head_sum_matrix(1.0 / RW_N)
    mean = _mm_exact_rhs(y, avg)
    yc = y - mean
    var = _mm_exact_rhs(yc * yc, avg)
    yn = yc * lax.rsqrt(var + RW_GN_EPS) * lnw_ref[...] + lnb_ref[...]
    a = a_ref[0]
    k = k_ref[0]
    ka = ka_ref[...]
    ksum = k * (1.0 + (a[:, :C] - 1.0) * ka) + k * (1.0 + (a[:, C:] - 1.0) * ka)
    coef = _mm_exact_rhs(r_ref[0] * ksum * rk_ref[...], _head_sum_matrix(1.0))
    gate = _mm(jax.nn.sigmoid(g_ref[0]), g2_ref[...])
    o_ref[0] = (yn + coef * v_ref[0]) * gate


def rw_post(y2, r, k, v, a2, proj, g_lane0, k_a, r_k, ln_w, ln_b, g2):
    Bn, L, C = r.shape
    rows = RW_ROWS_PER_STEP
    nt = L // rows
    gb = g_lane0 // LANES
    blk = lambda c: pl.BlockSpec((1, rows, c), lambda b, t: (b, t, 0))
    full = lambda shp: pl.BlockSpec(shp, lambda b, t: (0,) * len(shp))
    row = lambda x: x.astype(f32).reshape(1, C)
    return pl.pallas_call(
        _rw_post_kernel,
        grid=(Bn, nt),
        in_specs=[blk(C), pl.BlockSpec((1, rows, C), lambda b, t: (b + Bn, t, 0)), blk(C), blk(C), blk(C), blk(2 * C),
                  pl.BlockSpec((1, rows, LANES), lambda b, t: (b, t, gb)),
                  full((1, C)), full((1, C)), full((1, C)), full((1, C)), full((RW_G_LORA, C))],
        out_specs=blk(C),
        out_shape=jax.ShapeDtypeStruct((Bn, L, C), f32),
        compiler_params=pltpu.CompilerParams(dimension_semantics=("parallel", "parallel"),
                                             vmem_limit_bytes=VMEM_LIMIT_BYTES),
        name="rwkv_post",
    )(y2, y2, r, k, v, a2, proj, row(k_a), row(r_k), row(ln_w), row(ln_b), g2.astype(bf16))


def _na_kernel(q_ref, k_ref, v_ref, bias_ref, o_ref, kb_ref, vb_ref, *, rows_per_step, n_rows, kr):
    W = GRID_W
    rb = pl.program_id(2)

    @pl.when(rb == 0)
    def _():
        kb_ref[...] = k_ref[0].astype(bf16)
        vb_ref[...] = v_ref[0].astype(bf16)

    lo_lane = lax.broadcasted_iota(jnp.int32, (W, LANES), 1) < NA_HD
    scale = NA_HD ** -0.5
    rows = []
    for j in range(rows_per_step):
        r = rb * rows_per_step + j
        start = jnp.clip(r - kr // 2, 0, n_rows - kr)
        rows.append((start, start - r + (NA_KR - 1)))
    s_list = []
    for j, (start, didx) in enumerate(rows):
        q = q_ref[0, j * W:(j + 1) * W, :] * scale
        qs = jnp.concatenate([jnp.where(lo_lane, q, 0.0), jnp.where(lo_lane, 0.0, q)], axis=0)
        kw = kb_ref[pl.ds(pl.multiple_of(start * W, W), kr * W), :]
        s_list.append(_mm(qs, kw, 'nt') + bias_ref[didx, 0])
    p_list = []
    for s in s_list:
        m = jnp.max(s, axis=-1, keepdims=True)
        p = jnp.exp(s - m)
        p_list.append((p, jnp.sum(p, axis=-1, keepdims=True)))
    for j, ((start, _), (p, l)) in enumerate(zip(rows, p_list)):
        vw = vb_ref[pl.ds(pl.multiple_of(start * W, W), kr * W), :]
        o = _mm(p, vw) / l
        o_ref[0, j * W:(j + 1) * W, :] = jnp.where(lo_lane, o[:W], o[W:])


def na_bias_table(rpb, n_rows):
    W = GRID_W
    kr = min(NA_KR, n_rows)
    cols = jnp.arange(W)
    col_start = jnp.clip(cols - NA_KC // 2, 0, W - NA_KC)
    in_band = (cols[None, :] >= col_start[:, None]) & (cols[None, :] < col_start[:, None] + NA_KC)
    dc = jnp.clip(cols[None, :] - cols[:, None], -(NA_KC - 1), NA_KC - 1) + (NA_KC - 1)
    win = jnp.stack([rpb.astype(f32)[:, d:d + kr, :] for d in range(8)], axis=1)
    onehot = (dc[:, :, None] == jnp.arange(2 * NA_KC - 1)[None, None, :]).astype(f32)
    tab = jnp.einsum('hdic,qkc->hdqik', win, onehot, precision=lax.Precision.HIGHEST)
    tab = jnp.where(in_band[None, None, :, None, :], tab, NEG_INF)
    H = rpb.shape[0]
    tab = tab.reshape(H // 2, 2, 8, W, kr * W).transpose(2, 0, 1, 3, 4)
    return tab.reshape(8, H // 2, 2 * W, kr * W)


def neighborhood_attention_pallas(proj, rpb, lane_block0):
    Bn, L, _ = proj.shape
    W = GRID_W
    n_rows = L // W
    kr = min(NA_KR, n_rows)
    hp = NA_H * NA_HD // LANES
    rps = NA_ROWS_PER_STEP
    bias = na_bias_table(rpb, n_rows)
    kern = functools.partial(_na_kernel, rows_per_step=rps, n_rows=n_rows, kr=kr)
    return pl.pallas_call(
        kern,
        grid=(Bn, hp, n_rows // rps),
        in_specs=[pl.BlockSpec((1, rps * W, LANES), lambda b, h, r: (b, r, lane_block0 + h)),
                  pl.BlockSpec((1, L, LANES), lambda b, h, r: (b, 0, lane_block0 + hp + h)),
                  pl.BlockSpec((1, L, LANES), lambda b, h, r: (b, 0, lane_block0 + 2 * hp + h)),
                  pl.BlockSpec((8, 1, 2 * W, kr * W), lambda b, h, r: (0, h, 0, 0))],
        out_specs=pl.BlockSpec((1, rps * W, LANES), lambda b, h, r: (b, r, h)),
        out_shape=jax.ShapeDtypeStruct((Bn, L, NA_H * NA_HD), f32),
        scratch_shapes=[pltpu.VMEM((L, LANES), bf16), pltpu.VMEM((L, LANES), bf16)],
        compiler_params=pltpu.CompilerParams(dimension_semantics=("parallel", "parallel", "arbitrary"),
                                             vmem_limit_bytes=VMEM_LIMIT_BYTES),
        name="neighborhood_attention",
    )(proj, proj, proj, bias)


def _moe_ffn_kernel(be_ref, x_ref, w1_ref, w3_ref, w2_ref, o_ref, w1b_ref, w3b_ref, w2b_ref):
    i = pl.program_id(0)

    @pl.when((i == 0) | (be_ref[i] != be_ref[jnp.maximum(i - 1, 0)]))
    def _():
        w1b_ref[...] = w1_ref[0, 0].astype(bf16)
        w3b_ref[...] = w3_ref[0, 0].astype(bf16)
        w2b_ref[...] = w2_ref[0, 0].astype(bf16)

    x = x_ref[...].astype(bf16)
    h1 = jnp.dot(x, w1b_ref[...], preferred_element_type=f32)
    h3 = jnp.dot(x, w3b_ref[...], preferred_element_type=f32)
    g = h1 * jax.nn.sigmoid(h1) * h3
    o_ref[...] = jnp.dot(g.astype(bf16), w2b_ref[...], preferred_element_type=f32)


def moe_expert_ffn(xs, block_e, w1, w3, w2, layer):
    rows, Dm = xs.shape
    n_blocks = rows // MOE_BLOCK
    grid_spec = pltpu.PrefetchScalarGridSpec(
        num_scalar_prefetch=1,
        grid=(n_blocks,),
        in_specs=[pl.BlockSpec((MOE_BLOCK, Dm), lambda i, be: (i, 0)),
                  pl.BlockSpec((1, 1, Dm, MOE_FF), lambda i, be: (layer, be[i], 0, 0)),
                  pl.BlockSpec((1, 1, Dm, MOE_FF), lambda i, be: (layer, be[i], 0, 0)),
                  pl.BlockSpec((1, 1, MOE_FF, Dm), lambda i, be: (layer, be[i], 0, 0))],
        out_specs=pl.BlockSpec((MOE_BLOCK, Dm), lambda i, be: (i, 0)),
        scratch_shapes=[pltpu.VMEM((Dm, MOE_FF), bf16), pltpu.VMEM((Dm, MOE_FF), bf16), pltpu.VMEM((MOE_FF, Dm), bf16)],
    )
    return pl.pallas_call(
        _moe_ffn_kernel,
        grid_spec=grid_spec,
        out_shape=jax.ShapeDtypeStruct((rows, Dm), f32),
        compiler_params=pltpu.CompilerParams(dimension_semantics=("arbitrary",), vmem_limit_bytes=VMEM_LIMIT_BYTES),
        name="moe_expert_ffn",
    )(block_e.astype(jnp.int32), xs, w1.astype(f32), w3.astype(f32), w2.astype(f32))


def _rms(x):
    return x * lax.rsqrt(jnp.mean(x * x, axis=-1, keepdims=True) + NORM_EPS)


def _norm_proj_kernel(x_ref, g_ref, w_ref, o_ref):
    h = (_rms(x_ref[...]) * g_ref[...]).astype(bf16)
    o_ref[...] = jnp.dot(h, w_ref[...], preferred_element_type=f32)


def norm_proj(x2, g, w):
    N, D = x2.shape
    F = w.shape[1]
    tm = PROJ_ROWS_PER_STEP
    return pl.pallas_call(
        _norm_proj_kernel,
        grid=(N // tm,),
        in_specs=[pl.BlockSpec((tm, D), lambda i: (i, 0)), pl.BlockSpec((1, D), lambda i: (0, 0)),
                  pl.BlockSpec((D, F), lambda i: (0, 0))],
        out_specs=pl.BlockSpec((tm, F), lambda i: (i, 0)),
        out_shape=jax.ShapeDtypeStruct((N, F), f32),
        compiler_params=pltpu.CompilerParams(dimension_semantics=("parallel",), vmem_limit_bytes=VMEM_LIMIT_BYTES),
        name="norm_proj",
    )(x2, g.astype(f32).reshape(1, D), w.astype(bf16))


def _out_proj_kernel(x_ref, yh_ref, yr_ref, yn_ref, wh_ref, wr_ref, wn_ref, g_ref, wrt_ref, xo_ref, h_ref, lg_ref):
    mix = (jnp.dot(yh_ref[...].astype(bf16), wh_ref[...], preferred_element_type=f32)
           + jnp.dot(yr_ref[...].astype(bf16), wr_ref[...], preferred_element_type=f32)
           + jnp.dot(yn_ref[...].astype(bf16), wn_ref[...], preferred_element_type=f32))
    x = x_ref[...] + mix
    xo_ref[...] = x
    h = (_rms(x) * g_ref[...]).astype(bf16)
    h_ref[...] = h
    lg_ref[...] = jnp.dot(h, wrt_ref[...], preferred_element_type=f32)


def out_proj_norm_router(x2, y_hy, y_rw, y_na, w_out, g, w_router):
    N, D = x2.shape
    tm = PROJ_ROWS_PER_STEP
    d_hy, d_rw, d_na = y_hy.shape[1], y_rw.shape[1], y_na.shape[1]
    nr = w_router.shape[1]
    w_router = jnp.pad(w_router.astype(bf16), ((0, 0), (0, LANES - nr)))
    wb = w_out.astype(bf16)
    row = lambda c: pl.BlockSpec((tm, c), lambda i: (i, 0))
    full = lambda r, c: pl.BlockSpec((r, c), lambda i: (0, 0))
    return pl.pallas_call(
        _out_proj_kernel,
        grid=(N // tm,),
        in_specs=[row(D), row(d_hy), row(d_rw), row(d_na), full(d_hy, D), full(d_rw, D), full(d_na, D), full(1, D),
                  full(D, LANES)],
        out_specs=[row(D), row(D), row(LANES)],
        out_shape=[jax.ShapeDtypeStruct((N, D), f32), jax.ShapeDtypeStruct((N, D), bf16),
                   jax.ShapeDtypeStruct((N, LANES), f32)],
        compiler_params=pltpu.CompilerParams(dimension_semantics=("parallel",), vmem_limit_bytes=VMEM_LIMIT_BYTES),
        name="out_proj_norm_router",
    )(x2, y_hy, y_rw, y_na, wb[:d_hy], wb[d_hy:d_hy + d_rw], wb[d_hy + d_rw:], g.astype(f32).reshape(1, D), w_router)


def _final_norm_kernel(x_ref, g_ref, o_ref):
    x = x_ref[...]
    o_ref[...] = x * lax.rsqrt(jnp.mean(x * x, axis=-1, keepdims=True) + NORM_EPS) * g_ref[...]


def final_rmsnorm(x, g):
    Bn, L, D = x.shape
    tm = 1024
    out = pl.pallas_call(
        _final_norm_kernel,
        grid=(Bn * L // tm,),
        in_specs=[pl.BlockSpec((tm, D), lambda i: (i, 0)), pl.BlockSpec((1, D), lambda i: (0, 0))],
        out_specs=pl.BlockSpec((tm, D), lambda i: (i, 0)),
        out_shape=jax.ShapeDtypeStruct((Bn * L, D), x.dtype),
        name="final_rmsnorm",
    )(x.reshape(Bn * L, D), g.reshape(1, D))
    return out.reshape(Bn, L, D)


def hyena_positional_features(L):
    t = jnp.linspace(0.0, 1.0, L, dtype=f32)[:, None]
    w = (2.0 * math.pi / L) * jnp.arange(L, dtype=f32)[:, None]
    f = jnp.linspace(1e-4, HY_BANDS - 1, HY_BANDS, dtype=f32)[None, :]
    z = jnp.concatenate([t, jnp.cos(f * w), -jnp.sin(f * w)], axis=-1)
    return z, t


def hyena_filters(z, t, w1, b1, w2, b2, w3, b3, wout, freq):
    fr = freq.astype(f32)
    act = lambda u: jnp.sin(fr * u)
    h = act(z @ w1.astype(f32) + b1.astype(f32))
    h = act(h @ w2.astype(f32) + b2.astype(f32))
    h = act(h @ w3.astype(f32) + b3.astype(f32))
    h = (h @ wout.astype(f32)).reshape(-1, 2, HY_D)
    deltas = jnp.abs(jnp.linspace(HY_MIN_DECAY, HY_MAX_DECAY, HY_D, dtype=f32))
    h = h * jnp.exp(-t[:, :, None] * deltas)
    return h * lax.rsqrt(jnp.sum(h * h, axis=(0, 1), keepdims=True) + 1e-6)


def _split_bf16(x):
    hi = x.astype(bf16)
    return hi, (x - hi.astype(f32)).astype(bf16)


def _dot3(m_hi, m_lo, x_hi, x_lo):
    d = lambda a, b: lax.dot_general(a, b, _DN['nn'], preferred_element_type=f32)
    return d(m_hi, x_hi) + (d(m_hi, x_lo) + d(m_lo, x_hi))


def _cmatmul(mr, mi, xr, xi):
    xrh, xrl = _split_bf16(xr)
    rr = _dot3(mr[0], mr[1], xrh, xrl)
    ir = _dot3(mi[0], mi[1], xrh, xrl)
    if xi is None:
        return rr, ir
    xih, xil = _split_bf16(xi)
    ii = _dot3(mi[0], mi[1], xih, xil)
    ri = _dot3(mr[0], mr[1], xih, xil)
    return rr - ii, ri + ir


def _row_dft_kernel(mrh_ref, mrl_ref, mih_ref, mil_ref, *refs, real_input, nb):
    if real_input:
        ur_ref, or_ref, oi_ref = refs
    else:
        ur_ref, ui_ref, or_ref, oi_ref = refs
    mr = (mrh_ref[...], mrl_ref[...])
    mi = (mih_ref[...], mil_ref[...])
    for j in range(nb):
        ui = None if real_input else ui_ref[0, :, j, :]
        o_r, o_i = _cmatmul(mr, mi, ur_ref[0, :, j, :], ui)
        or_ref[0, :, j, :] = o_r
        oi_ref[0, :, j, :] = o_i


def row_dft(tabs, u, packed):
    rows, r_in, n2, C = u.shape
    P = rows // 2 if packed else rows
    r_out = tabs[0].shape[0]
    nb = FFT_N2_PER_STEP
    tab_spec = pl.BlockSpec((r_out, r_in), lambda p, j: (0, 0))
    re_spec = pl.BlockSpec((1, r_in, nb, C), lambda p, j: (p, 0, j, 0))
    im_spec = pl.BlockSpec((1, r_in, nb, C), lambda p, j: (p + P, 0, j, 0))
    out_spec = pl.BlockSpec((1, r_out, nb, C), lambda p, j: (p, 0, j, 0))
    ins = [u, u] if packed else [u]
    return pl.pallas_call(
        functools.partial(_row_dft_kernel, real_input=not packed, nb=nb),
        grid=(P, n2 // nb),
        in_specs=[tab_spec] * 4 + ([re_spec, im_spec] if packed else [re_spec]),
        out_specs=[out_spec, out_spec],
        out_shape=[jax.ShapeDtypeStruct((P, r_out, n2, C), f32)] * 2,
        compiler_params=pltpu.CompilerParams(dimension_semantics=("parallel", "parallel"),
                                             vmem_limit_bytes=VMEM_LIMIT_BYTES),
        name="hyena_row_dft",
    )(*tabs, *ins)


def _row_idft_gate_kernel(mrh_ref, mrl_ref, mih_ref, mil_ref, dr_ref, di_ref, z0_ref, z1_ref, x0_ref, x1_ref,
                          skip_ref, o0_ref, o1_ref, *, nb):
    mr = (mrh_ref[...], mrl_ref[...])
    mi = (mih_ref[...], mil_ref[...])
    skip = skip_ref[...]
    for j in range(nb):
        y_r, y_i = _cmatmul(mr, mi, dr_ref[0, :, j, :], di_ref[0, :, j, :])
        o0_ref[0, :, j, :] = x0_ref[0, :, j, :] * (y_r + z0_ref[0, :, j, :] * skip)
        o1_ref[0, :, j, :] = x1_ref[0, :, j, :] * (y_i + z1_ref[0, :, j, :] * skip)


def row_idft_gate(tabs, dr, di, z, x0, skip_row):
    P, r_in, n2, C = dr.shape
    r_out = tabs[0].shape[0]
    nb = FFT_N2_PER_STEP
    tab_spec = pl.BlockSpec((r_out, r_in), lambda p, j: (0, 0))
    d_spec = pl.BlockSpec((1, r_in, nb, C), lambda p, j: (p, 0, j, 0))
    lo = pl.BlockSpec((1, r_out, nb, C), lambda p, j: (p, 0, j, 0))
    hi = pl.BlockSpec((1, r_out, nb, C), lambda p, j: (p + P, 0, j, 0))
    o0, o1 = pl.pallas_call(
        functools.partial(_row_idft_gate_kernel, nb=nb),
        grid=(P, n2 // nb),
        in_specs=[tab_spec] * 4 + [d_spec, d_spec, lo, hi, lo, hi, pl.BlockSpec((1, C), lambda p, j: (0, 0))],
        out_specs=[lo, lo],
        out_shape=[jax.ShapeDtypeStruct((P, r_out, n2, C), f32)] * 2,
        compiler_params=pltpu.CompilerParams(dimension_semantics=("parallel", "parallel"),
                                             vmem_limit_bytes=VMEM_LIMIT_BYTES),
        name="hyena_row_idft_gate",
    )(*tabs, dr, di, z, z, x0, x0, skip_row)
    return jnp.concatenate([o0, o1], axis=0)


def _col_dft_kernel(grh_ref, grl_ref, gih_ref, gil_ref, ar_ref, ai_ref, br_ref, bi_ref, *, k1_per_step):
    for j in range(k1_per_step):
        b_r, b_i = _cmatmul((grh_ref[j], grl_ref[j]), (gih_ref[j], gil_ref[j]), ar_ref[0, j], ai_ref[0, j])
        br_ref[0, j] = b_r
        bi_ref[0, j] = b_i


def _col_conv_kernel(grh_ref, grl_ref, gih_ref, gil_ref, trh_ref, trl_ref, tih_ref, til_ref,
                     ar_ref, ai_ref, kr_ref, ki_ref, dr_ref, di_ref, *, k1_per_step):
    for j in range(k1_per_step):
        b_r, b_i = _cmatmul((grh_ref[j], grl_ref[j]), (gih_ref[j], gil_ref[j]), ar_ref[0, j], ai_ref[0, j])
        k_r = kr_ref[0, j]
        k_i = ki_ref[0, j]
        c_r = b_r * k_r - b_i * k_i
        c_i = b_r * k_i + b_i * k_r
        d_r, d_i = _cmatmul((trh_ref[j], trl_ref[j]), (tih_ref[j], til_ref[j]), c_r, c_i)
        dr_ref[0, j] = d_r
        di_ref[0, j] = d_i


def col_stage(g_tabs, gt_tabs, ar, ai, kr=None, ki=None):
    P, n1, n2, C = ar.shape
    kb = min(FFT_K1_PER_STEP, n1)
    g_spec = pl.BlockSpec((kb, n2, n2), lambda p, j: (j, 0, 0))
    a_spec = pl.BlockSpec((1, kb, n2, C), lambda p, j: (p, j, 0, 0))
    k_spec = pl.BlockSpec((1, kb, n2, C), lambda p, j: (0, j, 0, 0))
    common = dict(
        grid=(P, n1 // kb),
        out_specs=[a_spec, a_spec],
        out_shape=[jax.ShapeDtypeStruct((P, n1, n2, C), f32)] * 2,
        compiler_params=pltpu.CompilerParams(dimension_semantics=("parallel", "parallel"),
                                             vmem_limit_bytes=VMEM_LIMIT_BYTES),
    )
    if kr is None:
        return pl.pallas_call(functools.partial(_col_dft_kernel, k1_per_step=kb),
                              in_specs=[g_spec] * 4 + [a_spec] * 2, name="hyena_col_dft", **common)(*g_tabs, ar, ai)
    return pl.pallas_call(functools.partial(_col_conv_kernel, k1_per_step=kb),
                          in_specs=[g_spec] * 8 + [a_spec] * 2 + [k_spec] * 2, name="hyena_col_conv", **common)(
        *g_tabs, *gt_tabs, ar, ai, kr, ki)


def _bf16_tables(m):
    out = []
    for part in (np.real(m), np.imag(m)):
        x = jnp.asarray(part, f32)
        hi = x.astype(bf16)
        out += [hi, (x - hi.astype(f32)).astype(bf16)]
    return out


def hyena_dft_tables(L):
    n2 = FFT_N2
    n1 = 2 * L // n2
    N = n1 * n2
    a = np.arange(n1)
    fa = np.exp(-2j * np.pi * np.outer(a, a) / n1)
    k1 = np.arange(n1)[:, None, None]
    k2 = np.arange(n2)[None, :, None]
    nn = np.arange(n2)[None, None, :]
    g = np.exp(-2j * np.pi * nn * (k1 + n1 * k2) / N)
    gt = np.conj(np.transpose(g, (0, 2, 1)))
    fc = np.conj(fa).T / N
    return dict(fa=_bf16_tables(fa), fa_half=_bf16_tables(fa[:, :n1 // 2]), g=_bf16_tables(g), gt=_bf16_tables(gt),
                fc_half=_bf16_tables(fc[:n1 // 2]))


def hyena_long_conv_gate(tabs, z, x0, k2, skip):
    Bn, L, C = z.shape
    n2 = FFT_N2
    n1 = 2 * L // n2
    ar, ai = row_dft(tabs['fa'], k2.reshape(1, n1, n2, C), packed=False)
    kr, ki = col_stage(tabs['g'], tabs['gt'], ar, ai)
    zv = z.reshape(Bn, n1 // 2, n2, C)
    ar, ai = row_dft(tabs['fa_half'], zv, packed=True)
    dr, di = col_stage(tabs['g'], tabs['gt'], ar, ai, kr, ki)
    out = row_idft_gate(tabs['fc_half'], dr, di, zv, x0.reshape(Bn, n1 // 2, n2, C), skip.astype(f32).reshape(1, C))
    return out.reshape(Bn, L, C)


def _hyena_prep_kernel(*refs, rows, nt):
    cur, prev, nxt = refs[0:3], refs[3:6], refs[6:9]
    w_ref, b_ref, z_ref, x0_ref = refs[9:]
    t = pl.program_id(1)
    has_prev = (t > 0).astype(f32)
    has_next = (t < nt - 1).astype(f32)
    ridx = lax.broadcasted_iota(jnp.int32, (rows, HY_D), 0)

    def conv(j):
        u = cur[j][0]
        prev_row = prev[j][0, SUBLANES - 1:SUBLANES, :] * has_prev
        next_row = nxt[j][0, 0:1, :] * has_next
        up = jnp.where(ridx == 0, prev_row, pltpu.roll(u, 1, axis=0))
        un = jnp.where(ridx == rows - 1, next_row, pltpu.roll(u, rows - 1, axis=0))
        sl = slice(j * HY_D, (j + 1) * HY_D)
        return up * w_ref[0:1, sl] + u * w_ref[1:2, sl] + un * w_ref[2:3, sl] + b_ref[:, sl]

    x0_ref[0] = conv(0)
    z_ref[0] = conv(2) * conv(1)


def hyena_prep(proj, lane0, conv_w, conv_b):
    Bn, L, _ = proj.shape
    rows = RW_ROWS_PER_STEP
    nt = L // rows
    hb = rows // SUBLANES
    b0 = lane0 // HY_D
    assert b0 * HY_D == lane0
    cur = lambda j: pl.BlockSpec((1, rows, HY_D), lambda b, t: (b, t, b0 + j))
    prev = lambda j: pl.BlockSpec((1, SUBLANES, HY_D), lambda b, t: (b, jnp.maximum(t * hb - 1, 0), b0 + j))
    nxt = lambda j: pl.BlockSpec((1, SUBLANES, HY_D), lambda b, t: (b, jnp.minimum((t + 1) * hb, L // SUBLANES - 1), b0 + j))
    full = lambda shp: pl.BlockSpec(shp, lambda b, t: (0,) * len(shp))
    out = pl.BlockSpec((1, rows, HY_D), lambda b, t: (b, t, 0))
    return pl.pallas_call(
        functools.partial(_hyena_prep_kernel, rows=rows, nt=nt),
        grid=(Bn, nt),
        in_specs=[cur(0), cur(1), cur(2), prev(0), prev(1), prev(2), nxt(0), nxt(1), nxt(2),
                  full((3, 3 * HY_D)), full((1, 3 * HY_D))],
        out_specs=[out, out],
        out_shape=[jax.ShapeDtypeStruct((Bn, L, HY_D), f32)] * 2,
        compiler_params=pltpu.CompilerParams(dimension_semantics=("parallel", "parallel")),
        name="hyena_prep",
    )(*([proj] * 9), conv_w.astype(f32), conv_b.astype(f32).reshape(1, 3 * HY_D))


def hyena_mixer(tabs, proj, lane0, conv_w, conv_b, filt, skip):
    z, x0 = hyena_prep(proj, lane0, conv_w, conv_b)
    k2 = jnp.concatenate([filt[:1, 0] + filt[:1, 1], filt[1:, 0],
                          jnp.zeros((1, HY_D), f32), filt[1:, 1][::-1]], axis=0)
    return hyena_long_conv_gate(tabs, z, x0, k2, skip)


def rwkv7_mixer(proj, lora_lane0, mu, w0, w2, a0, a2, k_k, k_a, r_k, g2, ln_w, ln_b):
    r, k, v, kk, lw2, a_both = rw_prep(proj, lora_lane0, mu, w0, w2, a0, a2, k_k)
    y2 = wkv7_chunked(r, k, v, kk, lw2, a_both, k_a.astype(f32).reshape(1, RW_D))
    return rw_post(y2, r, k, v, a_both, proj, lora_lane0, k_a, r_k, ln_w, ln_b, g2)


def hier_moe(h, logits, bg, be, w1, w3, w2, layer):
    N, Dm = h.shape
    assert MOE_TOPK == 2
    g_logits = logits[:, :MOE_GROUPS] + bg.astype(f32)
    g_sel = jnp.argmax(g_logits, axis=-1)
    g_prob = jnp.take_along_axis(jax.nn.softmax(g_logits, axis=-1), g_sel[:, None], axis=-1)
    e_logits = (logits[:, MOE_GROUPS:MOE_GROUPS + MOE_EXPERTS] + be.astype(f32)).reshape(N, MOE_GROUPS, MOE_PER_GROUP)
    e_logits = jnp.take_along_axis(e_logits, g_sel[:, None, None], axis=1)[:, 0]
    top_val, top_idx = lax.top_k(e_logits, MOE_TOPK)
    gate = g_prob * jax.nn.softmax(top_val, axis=-1)
    expert = g_sel[:, None] * MOE_PER_GROUP + top_idx
    M = N * MOE_TOPK
    flat_e = expert.reshape(M).astype(jnp.int32)
    experts = jnp.arange(MOE_EXPERTS, dtype=jnp.int32)
    counts = jnp.sum((flat_e[:, None] == experts[None, :]).astype(jnp.int32), axis=0)
    padded = (counts + MOE_BLOCK - 1) // MOE_BLOCK * MOE_BLOCK
    pad_end = jnp.cumsum(padded)
    n_blocks = -(-M // MOE_BLOCK) + MOE_EXPERTS
    n_rows = n_blocks * MOE_BLOCK
    cum_need = jnp.cumsum(padded - counts)
    filler = jnp.arange(n_rows - M, dtype=jnp.int32)
    filler_e = jnp.sum((cum_need[None, :] <= filler[:, None]).astype(jnp.int32), axis=1)
    keys = jnp.concatenate([2 * flat_e, 2 * filler_e + 1])
    ids = jnp.concatenate([jnp.arange(M, dtype=jnp.int32), jnp.full((n_rows - M,), M, jnp.int32)])
    _, slot_src = lax.sort((keys, ids), num_keys=1)
    tok_src = jnp.where(slot_src < M, slot_src // MOE_TOPK, 0)
    block_start = jnp.arange(n_blocks, dtype=pad_end.dtype) * MOE_BLOCK
    block_e = jnp.minimum(jnp.sum(pad_end[None, :] <= block_start[:, None], axis=1), MOE_EXPERTS - 1)
    y = moe_expert_ffn(h[tok_src], block_e, w1, w3, w2, layer)
    _, row_of = lax.sort((slot_src, jnp.arange(n_rows, dtype=jnp.int32)), num_keys=1)
    pos = row_of[:M].reshape(N, MOE_TOPK)
    return gate[:, 0:1] * y[pos[:, 0]] + gate[:, 1:2] * y[pos[:, 1]]


def kernel(x, norm1_g, w_in, hy_conv_w, hy_conv_b, hy_w1, hy_b1, hy_w2, hy_b2, hy_w3, hy_b3, hy_wout, hy_freq, hy_skip, rw_mu, rw_w0, rw_w2, rw_a0, rw_a2, rw_kk, rw_ka, rw_rk, rw_g2, rw_ln_w, rw_ln_b, na_rpb, w_out, norm2_g, moe_wg, moe_bg, moe_we, moe_be, moe_w1, moe_w3, moe_w2, norm_f_g):
    Bn, L, _ = x.shape
    z_pos, t_pos = hyena_positional_features(L)
    dft_tabs = hyena_dft_tables(L)
    splits = np.cumsum(IN_SIZES)[:-1].tolist()
    hy_end, rkv_end = splits[0], splits[1]
    hy_lane0 = rkv_end - hy_end
    lora_lane0, na_lane0 = splits[1], splits[4]
    assert RW_D % LANES == 0 and lora_lane0 % LANES == 0 and na_lane0 % LANES == 0
    N = Bn * L
    x2 = x.reshape(N, D_MODEL)
    for l in range(DEPTH):
        w_in_l = jnp.concatenate([w_in[l][:, hy_end:rkv_end], w_in[l][:, :hy_end], w_in[l][:, rkv_end:]], axis=1)
        proj = norm_proj(x2, norm1_g[l], w_in_l).reshape(Bn, L, IN_D)
        filt = hyena_filters(z_pos, t_pos, hy_w1[l], hy_b1[l], hy_w2[l], hy_b2[l],
                             hy_w3[l], hy_b3[l], hy_wout[l], hy_freq[l])
        y_hy = hyena_mixer(dft_tabs, proj, hy_lane0, hy_conv_w[l], hy_conv_b[l], filt, hy_skip[l])
        y_rw = rwkv7_mixer(proj, lora_lane0, rw_mu[l], rw_w0[l], rw_w2[l], rw_a0[l],
                           rw_a2[l], rw_kk[l], rw_ka[l], rw_rk[l].reshape(RW_D), rw_g2[l], rw_ln_w[l], rw_ln_b[l])
        y_na = neighborhood_attention_pallas(proj, na_rpb[l], na_lane0 // LANES)
        x2, h2, logits = out_proj_norm_router(
            x2, y_hy.reshape(N, HY_D), y_rw.reshape(N, RW_D), y_na.reshape(N, NA_D), w_out[l], norm2_g[l],
            jnp.concatenate([moe_wg[l], moe_we[l]], axis=1))
        x2 = x2 + hier_moe(h2, logits, moe_bg[l], moe_be[l], moe_w1, moe_w3, moe_w2, l)
    return final_rmsnorm(x2.reshape(Bn, L, D_MODEL), norm_f_g)
```

```python
import functools
import math

import jax
import jax.numpy as jnp
import numpy as np
from jax import lax
from jax.experimental import pallas as pl
from jax.experimental.pallas import tpu as pltpu

f32 = jnp.float32
bf16 = jnp.bfloat16

D_MODEL = 1024
DEPTH = 2
GRID_W = 64
NORM_EPS = 1e-6
NEG_INF = -1e30

HY_D = D_MODEL // 4
HY_EMB = 33
HY_BANDS = (HY_EMB - 1) // 2
HY_FFN = 64
HY_MIN_DECAY = math.log(1e-2) / 1.5
HY_MAX_DECAY = math.log(1e-2) / 0.3

RW_N = 64
RW_D = D_MODEL // 2
RW_H = RW_D // RW_N
RW_W_LORA = 64
RW_A_LORA = 64
RW_G_LORA = 128
RW_GN_EPS = 64e-5

NA_HD = 64
NA_D = D_MODEL // 4
NA_H = NA_D // NA_HD
NA_KR = 8
NA_KC = 16

MIX_D = HY_D + RW_D + NA_D
IN_SIZES = (3 * HY_D, 3 * RW_D, RW_G_LORA, 2 * RW_W_LORA, 2 * RW_A_LORA, 3 * NA_D)
IN_D = sum(IN_SIZES)

MOE_GROUPS = 4
MOE_PER_GROUP = 8
MOE_EXPERTS = MOE_GROUPS * MOE_PER_GROUP
MOE_TOPK = 2
MOE_FF = 512
MOE_BLOCK = 512

LANES = 128
WKV_CHUNK = 64
WKV_TIME_BLOCK = 512
NA_ROWS_PER_STEP = 4
RW_ROWS_PER_STEP = 512
SUBLANES = 8
PROJ_ROWS_PER_STEP = 512
FFT_N2 = 128
FFT_LANE_BLOCK = 4096
FFT_K1_PER_STEP = 4
VMEM_LIMIT_BYTES = 48 * 1024 * 1024

_DN = {'nn': (((1,), (0,)), ((), ())), 'nt': (((1,), (1,)), ((), ())), 'tn': (((0,), (0,)), ((), ()))}


def _mm(a, b, dims='nn'):
    return lax.dot_general(a.astype(bf16), b.astype(bf16), _DN[dims], preferred_element_type=f32)


def _wkv_kernel(r_ref, k_ref, v_ref, kk_ref, lw_ref, a_ref, ka_ref, y_ref, ht_ref, *, tb, batch):
    T = WKV_CHUNK
    H2 = 2 * T
    nc = tb // T
    npairs = r_ref.shape[-1] // LANES
    mm = _mm

    @pl.when(pl.program_id(1) == 0)
    def _():
        ht_ref[...] = jnp.zeros_like(ht_ref)

    d = pl.program_id(0) // batch
    sign = 1 - 2 * d
    trow = lax.broadcasted_iota(jnp.int32, (T, H2), 0)
    lane = lax.broadcasted_iota(jnp.int32, (T, H2), 1)
    scol = lane % T
    tdiff = (trow - scol) * sign
    strict = tdiff > 0
    incl = tdiff >= 0
    same16 = (trow // 16) == (scol // 16)
    same32 = (trow // 32) == (scol // 32)
    off32 = same32 & jnp.logical_not(same16)
    off64 = jnp.logical_not(same32)
    eye2 = jnp.where(trow == scol, 1.0, 0.0).astype(f32)
    lo_lane = lane < T
    brow = lax.broadcasted_iota(jnp.int32, (H2, H2), 0)
    bcol = lax.broadcasted_iota(jnp.int32, (H2, H2), 1)
    same_head = (brow // T) == (bcol // T)
    crow = lax.broadcasted_iota(jnp.int32, (T, T), 0)
    ccol = lax.broadcasted_iota(jnp.int32, (T, T), 1)
    tri = jnp.where((crow - ccol) * sign >= 0, 1.0, 0.0).astype(bf16)
    is_bwd = d == 1

    def bd(x):
        return jnp.concatenate([jnp.where(lo_lane, x, 0.0), jnp.where(lo_lane, 0.0, x)], axis=0)

    inst = [(s, p) for s in range(nc) for p in range(npairs)]
    offs = [pl.multiple_of((s + d * (nc - 1 - 2 * s)) * T, T) for s in range(nc)]

    def load(ref, s, p):
        return ref[0, pl.ds(offs[s], T), p * LANES:(p + 1) * LANES]

    cs_l = []
    for s, p in inst:
        lw = load(lw_ref, s, p)
        l1 = lw.astype(bf16)
        r1 = lw - l1.astype(f32)
        l2 = r1.astype(bf16)
        l3 = (r1 - l2.astype(f32)).astype(bf16)
        dd = lambda x: lax.dot_general(tri, x, _DN['nn'], preferred_element_type=f32)
        cs_l.append((dd(l1) + (dd(l2) + dd(l3)), lw))
    ops = []
    for (s, p), (cs, lw) in zip(inst, cs_l):
        r = load(r_ref, s, p)
        k = load(k_ref, s, p)
        v = load(v_ref, s, p)
        kk = load(kk_ref, s, p)
        a = load(a_ref, s, p)
        ka = ka_ref[:, p * LANES:(p + 1) * LANES]
        kd = k * (1.0 + (a - 1.0) * ka)
        b = kk * a
        cs_end = jnp.where(is_bwd, cs[0:1, :], cs[T - 1:T, :])
        em = jnp.exp(-cs)
        e_end = jnp.exp(cs_end - cs)
        ops.append(dict(
            AR=jnp.concatenate([-kk * jnp.exp(cs - lw), r * jnp.exp(cs)], axis=0),
            BK=jnp.concatenate([bd(b * em), bd(kd * em)], axis=0),
            V=v, BKg=jnp.concatenate([b * e_end, kd * e_end], axis=0), g_end=jnp.exp(cs_end)))
    for o in ops:
        S = mm(o['AR'], o['BK'], 'nt')
        N = jnp.where(strict, S[:T, :H2], 0.0)
        o['AakArk'] = jnp.concatenate([jnp.where(strict, S[:T, H2:], 0.0), jnp.where(incl, S[T:, H2:], 0.0)], axis=0)
        o['Arb'] = jnp.where(incl, S[T:, :H2], 0.0)
        o['Nd'] = jnp.where(same16, N, 0.0)
        o['N32'] = jnp.where(off32, N, 0.0)
        o['N64'] = jnp.where(off64, N, 0.0)
    for o in ops:
        o['X'] = eye2 + o['Nd']
        o['P'] = mm(o['Nd'], bd(o['Nd']))
    for it in range(3):
        if it < 2:
            for o in ops:
                px = mm(o['P'], jnp.concatenate([bd(o['X']), bd(o['P'])], axis=1))
                o['X'] = o['X'] + px[:, :H2]
                o['P'] = px[:, H2:]
        else:
            for o in ops:
                o['X'] = o['X'] + mm(o['P'], bd(o['X']))
    for key in ('N32', 'N64'):
        for o in ops:
            o['Z'] = mm(o[key], bd(o['X']))
        for o in ops:
            o['X'] = o['X'] + mm(o['X'], bd(o['Z']))
    for o in ops:
        wy = mm(o['AakArk'], bd(o['V']))
        o['W0'] = wy[:T]
        o['Yv'] = wy[T:]
    for o in ops:
        o['XAW'] = mm(o['X'], jnp.concatenate([bd(o['AR'][:T]), bd(o['W0'])], axis=1))
    for o in ops:
        ax = mm(o['Arb'], jnp.concatenate([bd(o['XAW'][:, :H2]), bd(o['XAW'][:, H2:])], axis=1))
        o['Rhat'] = o['AR'][T:] + ax[:, :H2]
        o['Yc'] = ax[:, H2:] + o['Yv']
    for o in ops:
        lhs = jnp.concatenate([o['XAW'], jnp.concatenate([jnp.zeros((T, H2), f32), o['V']], axis=1)], axis=0)
        pq = mm(lhs, o['BKg'], 'tn')
        o['Pc'] = jnp.where(same_head, pq[:H2], 0.0)
        o['Qc'] = jnp.where(same_head, pq[H2:], 0.0)
    hts = [ht_ref[p] for p in range(npairs)]
    for s in range(nc):
        cur = [ops[s * npairs + p] for p in range(npairs)]
        ys = [mm(o['Rhat'], hts[p], 'nt') + o['Yc'] for p, o in enumerate(cur)]
        hts = [hts[p] * o['g_end'] + (mm(hts[p], o['Pc']) + o['Qc']) for p, o in enumerate(cur)]
        for p in range(npairs):
            y_ref[0, pl.ds(offs[s], T), p * LANES:(p + 1) * LANES] = ys[p]
    for p in range(npairs):
        ht_ref[p] = hts[p]


def wkv7_chunked(r, k, v, kk, lw2, a2, ka):
    Bn, L, C = r.shape
    tb = WKV_TIME_BLOCK
    nt = L // tb
    tmap = lambda i, t: t + (i // Bn) * (nt - 1 - 2 * t)
    shared = pl.BlockSpec((1, tb, C), lambda i, t: (i % Bn, tmap(i, t), 0))
    per_dir = pl.BlockSpec((1, tb, C), lambda i, t: (i % Bn, tmap(i, t), i // Bn))
    return pl.pallas_call(
        functools.partial(_wkv_kernel, tb=tb, batch=Bn),
        grid=(2 * Bn, nt),
        in_specs=[shared, shared, shared, shared, per_dir, per_dir, pl.BlockSpec((1, C), lambda i, t: (0, 0))],
        out_specs=pl.BlockSpec((1, tb, C), lambda i, t: (i, tmap(i, t), 0)),
        out_shape=jax.ShapeDtypeStruct((2 * Bn, L, C), f32),
        scratch_shapes=[pltpu.VMEM((C // LANES, LANES, LANES), f32)],
        compiler_params=pltpu.CompilerParams(dimension_semantics=("parallel", "arbitrary"),
                                             vmem_limit_bytes=VMEM_LIMIT_BYTES),
        name="wkv7_chunked",
    )(r, k, v, kk, lw2, a2, ka)


def _mm_exact_rhs(a, b_bf16):
    ah = a.astype(bf16)
    al = (a - ah.astype(f32)).astype(bf16)
    d = lambda x: lax.dot_general(x, b_bf16, _DN['nn'], preferred_element_type=f32)
    return d(ah) + d(al)


def _head_sum_matrix(scale):
    i = lax.broadcasted_iota(jnp.int32, (RW_D, RW_D), 0) // RW_N
    j = lax.broadcasted_iota(jnp.int32, (RW_D, RW_D), 1) // RW_N
    return jnp.where(i == j, scale, 0.0).astype(bf16)


def _rw_prep_kernel(r_ref, k_ref, v_ref, rp_ref, kp_ref, vp_ref, rn_ref, kn_ref, vn_ref, w_ref, a_ref,
                    mu_ref, w0_ref, w2_ref, a0_ref, a2_ref, kkw_ref,
                    ro_ref, ko_ref, vo_ref, kko_ref, lw_ref, ao_ref, *, rows, nt):
    t = pl.program_id(1)
    has_prev = (t > 0).astype(f32)
    has_next = (t < nt - 1).astype(f32)
    ridx = lax.broadcasted_iota(jnp.int32, (rows, RW_D), 0)

    def tshift(cur_ref, prev_ref, next_ref, j):
        u = cur_ref[0]
        prev_row = prev_ref[0, SUBLANES - 1:SUBLANES, :] * has_prev
        next_row = next_ref[0, 0:1, :] * has_next
        up = jnp.where(ridx == 0, prev_row, pltpu.roll(u, 1, axis=0))
        un = jnp.where(ridx == rows - 1, next_row, pltpu.roll(u, rows - 1, axis=0))
        return u + mu_ref[j, 0:1, :] * (up - u) + mu_ref[j, 1:2, :] * (un - u)

    r = tshift(r_ref, rp_ref, rn_ref, 0)
    k = tshift(k_ref, kp_ref, kn_ref, 1)
    v = tshift(v_ref, vp_ref, vn_ref, 2)
    ro_ref[0] = r
    ko_ref[0] = k
    vo_ref[0] = v
    kk = k * kkw_ref[...]
    ss = _mm_exact_rhs(kk * kk, _head_sum_matrix(1.0))
    kko_ref[0] = kk * lax.rsqrt(jnp.maximum(ss, 1e-24))
    wl = jnp.tanh(w_ref[0])
    al = a_ref[0]
    for d in range(2):
        wpre = w0_ref[d:d + 1, :] + _mm(wl[:, d * RW_W_LORA:(d + 1) * RW_W_LORA], w2_ref[d])
        lw_ref[0, :, d * RW_D:(d + 1) * RW_D] = -math.exp(-0.5) * jax.nn.sigmoid(wpre)
        av = a0_ref[d:d + 1, :] + _mm(al[:, d * RW_A_LORA:(d + 1) * RW_A_LORA], a2_ref[d])
        ao_ref[0, :, d * RW_D:(d + 1) * RW_D] = jax.nn.sigmoid(av)


def rw_prep(proj, lora_lane0, mu, w0, w2, a0, a2, k_k):
    Bn, L, _ = proj.shape
    C = RW_D
    rows = RW_ROWS_PER_STEP
    nt = L // rows
    hb = rows // SUBLANES
    lb = lora_lane0 // LANES
    cur = lambda j: pl.BlockSpec((1, rows, C), lambda b, t: (b, t, j))
    prev = lambda j: pl.BlockSpec((1, SUBLANES, C), lambda b, t: (b, jnp.maximum(t * hb - 1, 0), j))
    nxt = lambda j: pl.BlockSpec((1, SUBLANES, C), lambda b, t: (b, jnp.minimum((t + 1) * hb, L // SUBLANES - 1), j))
    lora_w = pl.BlockSpec((1, rows, LANES), lambda b, t: (b, t, lb + 1))
    lora_a = pl.BlockSpec((1, rows, LANES), lambda b, t: (b, t, lb + 2))
    full = lambda shp: pl.BlockSpec(shp, lambda b, t: (0,) * len(shp))
    out_c = pl.BlockSpec((1, rows, C), lambda b, t: (b, t, 0))
    out_2c = pl.BlockSpec((1, rows, 2 * C), lambda b, t: (b, t, 0))
    sds = lambda c: jax.ShapeDtypeStruct((Bn, L, c), f32)
    return pl.pallas_call(
        functools.partial(_rw_prep_kernel, rows=rows, nt=nt),
        grid=(Bn, nt),
        in_specs=[cur(0), cur(1), cur(2), prev(0), prev(1), prev(2), nxt(0), nxt(1), nxt(2), lora_w, lora_a,
                  full((3, 2, C)), full((2, C)), full((2, RW_W_LORA, C)), full((2, C)), full((2, RW_A_LORA, C)),
                  full((1, C))],
        out_specs=[out_c, out_c, out_c, out_c, out_2c, out_2c],
        out_shape=[sds(C), sds(C), sds(C), sds(C), sds(2 * C), sds(2 * C)],
        compiler_params=pltpu.CompilerParams(dimension_semantics=("parallel", "parallel"),
                                             vmem_limit_bytes=VMEM_LIMIT_BYTES),
        name="rwkv_prep",
    )(proj, proj, proj, proj, proj, proj, proj, proj, proj, proj, proj,
      mu.astype(f32), w0.astype(f32), w2.astype(bf16), a0.astype(f32), a2.astype(bf16), k_k.astype(f32).reshape(1, C))


def _rw_post_kernel(yf_ref, yb_ref, r_ref, k_ref, v_ref, a_ref, g_ref, ka_ref, rk_ref, lnw_ref, lnb_ref, g2_ref, o_ref):
    C = RW_D
    y = yf_ref[0] + yb_ref[0]
    avg = _head_sum_matrix(1.0 / RW_N)
    mean = _mm_exact_rhs(y, avg)
    yc = y - mean
    var = _mm_exact_rhs(yc * yc, avg)
    yn = yc * lax.rsqrt(var + RW_GN_EPS) * lnw_ref[...] + lnb_ref[...]
    a = a_ref[0]
    k = k_ref[0]
    ka = ka_ref[...]
    ksum = k * (1.0 + (a[:, :C] - 1.0) * ka) + k * (1.0 + (a[:, C:] - 1.0) * ka)
    coef = _mm_exact_rhs(r_ref[0] * ksum * rk_ref[...], _head_sum_matrix(1.0))
    gate = _mm(jax.nn.sigmoid(g_ref[0]), g2_ref[...])
    o_ref[0] = (yn + coef * v_ref[0]) * gate


def rw_post(y2, r, k, v, a2, proj, g_lane0, k_a, r_k, ln_w, ln_b, g2):
    Bn, L, C = r.shape
    rows = RW_ROWS_PER_STEP
    nt = L // rows
    gb = g_lane0 // LANES
    blk = lambda c: pl.BlockSpec((1, rows, c), lambda b, t: (b, t, 0))
    full = lambda shp: pl.BlockSpec(shp, lambda b, t: (0,) * len(shp))
    row = lambda x: x.astype(f32).reshape(1, C)
    return pl.pallas_call(
        _rw_post_kernel,
        grid=(Bn, nt),
        in_specs=[blk(C), pl.BlockSpec((1, rows, C), lambda b, t: (b + Bn, t, 0)), blk(C), blk(C), blk(C), blk(2 * C),
                  pl.BlockSpec((1, rows, LANES), lambda b, t: (b, t, gb)),
                  full((1, C)), full((1, C)), full((1, C)), full((1, C)), full((RW_G_LORA, C))],
        out_specs=blk(C),
        out_shape=jax.ShapeDtypeStruct((Bn, L, C), f32),
        compiler_params=pltpu.CompilerParams(dimension_semantics=("parallel", "parallel"),
                                             vmem_limit_bytes=VMEM_LIMIT_BYTES),
        name="rwkv_post",
    )(y2, y2, r, k, v, a2, proj, row(k_a), row(r_k), row(ln_w), row(ln_b), g2.astype(bf16))


def _na_kernel(q_ref, k_ref, v_ref, bias_ref, o_ref, kb_ref, vb_ref, *, rows_per_step, n_rows, kr):
    W = GRID_W
    rb = pl.program_id(2)

    @pl.when(rb == 0)
    def _():
        kb_ref[...] = k_ref[0].astype(bf16)
        vb_ref[...] = v_ref[0].astype(bf16)

    lo_lane = lax.broadcasted_iota(jnp.int32, (W, LANES), 1) < NA_HD
    scale = NA_HD ** -0.5
    rows = []
    for j in range(rows_per_step):
        r = rb * rows_per_step + j
        start = jnp.clip(r - kr // 2, 0, n_rows - kr)
        rows.append((start, start - r + (NA_KR - 1)))
    s_list = []
    for j, (start, didx) in enumerate(rows):
        q = q_ref[0, j * W:(j + 1) * W, :] * scale
        qs = jnp.concatenate([jnp.where(lo_lane, q, 0.0), jnp.where(lo_lane, 0.0, q)], axis=0)
        kw = kb_ref[pl.ds(pl.multiple_of(start * W, W), kr * W), :]
        s_list.append(_mm(qs, kw, 'nt') + bias_ref[didx, 0])
    p_list = []
    for s in s_list:
        m = jnp.max(s, axis=-1, keepdims=True)
        p = jnp.exp(s - m)
        p_list.append((p, jnp.sum(p, axis=-1, keepdims=True)))
    for j, ((start, _), (p, l)) in enumerate(zip(rows, p_list)):
        vw = vb_ref[pl.ds(pl.multiple_of(start * W, W), kr * W), :]
        o = _mm(p, vw) / l
        o_ref[0, j * W:(j + 1) * W, :] = jnp.where(lo_lane, o[:W], o[W:])


def na_bias_table(rpb, n_rows):
    W = GRID_W
    kr = min(NA_KR, n_rows)
    cols = jnp.arange(W)
    col_start = jnp.clip(cols - NA_KC // 2, 0, W - NA_KC)
    in_band = (cols[None, :] >= col_start[:, None]) & (cols[None, :] < col_start[:, None] + NA_KC)
    dc = jnp.clip(cols[None, :] - cols[:, None], -(NA_KC - 1), NA_KC - 1) + (NA_KC - 1)
    win = jnp.stack([rpb.astype(f32)[:, d:d + kr, :] for d in range(8)], axis=1)
    onehot = (dc[:, :, None] == jnp.arange(2 * NA_KC - 1)[None, None, :]).astype(f32)
    tab = jnp.einsum('hdic,qkc->hdqik', win, onehot, precision=lax.Precision.HIGHEST)
    tab = jnp.where(in_band[None, None, :, None, :], tab, NEG_INF)
    H = rpb.shape[0]
    tab = tab.reshape(H // 2, 2, 8, W, kr * W).transpose(2, 0, 1, 3, 4)
    return tab.reshape(8, H // 2, 2 * W, kr * W)


def neighborhood_attention_pallas(proj, rpb, lane_block0):
    Bn, L, _ = proj.shape
    W = GRID_W
    n_rows = L // W
    kr = min(NA_KR, n_rows)
    hp = NA_H * NA_HD // LANES
    rps = NA_ROWS_PER_STEP
    bias = na_bias_table(rpb, n_rows)
    kern = functools.partial(_na_kernel, rows_per_step=rps, n_rows=n_rows, kr=kr)
    return pl.pallas_call(
        kern,
        grid=(Bn, hp, n_rows // rps),
        in_specs=[pl.BlockSpec((1, rps * W, LANES), lambda b, h, r: (b, r, lane_block0 + h)),
                  pl.BlockSpec((1, L, LANES), lambda b, h, r: (b, 0, lane_block0 + hp + h)),
                  pl.BlockSpec((1, L, LANES), lambda b, h, r: (b, 0, lane_block0 + 2 * hp + h)),
                  pl.BlockSpec((8, 1, 2 * W, kr * W), lambda b, h, r: (0, h, 0, 0))],
        out_specs=pl.BlockSpec((1, rps * W, LANES), lambda b, h, r: (b, r, h)),
        out_shape=jax.ShapeDtypeStruct((Bn, L, NA_H * NA_HD), f32),
        scratch_shapes=[pltpu.VMEM((L, LANES), bf16), pltpu.VMEM((L, LANES), bf16)],
        compiler_params=pltpu.CompilerParams(dimension_semantics=("parallel", "parallel", "arbitrary"),
                                             vmem_limit_bytes=VMEM_LIMIT_BYTES),
        name="neighborhood_attention",
    )(proj, proj, proj, bias)


def _moe_ffn_kernel(be_ref, x_ref, w1_ref, w3_ref, w2_ref, o_ref, w1b_ref, w3b_ref, w2b_ref):
    i = pl.program_id(0)

    @pl.when((i == 0) | (be_ref[i] != be_ref[jnp.maximum(i - 1, 0)]))
    def _():
        w1b_ref[...] = w1_ref[0, 0].astype(bf16)
        w3b_ref[...] = w3_ref[0, 0].astype(bf16)
        w2b_ref[...] = w2_ref[0, 0].astype(bf16)

    x = x_ref[...].astype(bf16)
    h1 = jnp.dot(x, w1b_ref[...], preferred_element_type=f32)
    h3 = jnp.dot(x, w3b_ref[...], preferred_element_type=f32)
    g = h1 * jax.nn.sigmoid(h1) * h3
    o_ref[...] = jnp.dot(g.astype(bf16), w2b_ref[...], preferred_element_type=f32)


def moe_expert_ffn(xs, block_e, w1, w3, w2, layer):
    rows, Dm = xs.shape
    n_blocks = rows // MOE_BLOCK
    grid_spec = pltpu.PrefetchScalarGridSpec(
        num_scalar_prefetch=1,
        grid=(n_blocks,),
        in_specs=[pl.BlockSpec((MOE_BLOCK, Dm), lambda i, be: (i, 0)),
                  pl.BlockSpec((1, 1, Dm, MOE_FF), lambda i, be: (layer, be[i], 0, 0)),
                  pl.BlockSpec((1, 1, Dm, MOE_FF), lambda i, be: (layer, be[i], 0, 0)),
                  pl.BlockSpec((1, 1, MOE_FF, Dm), lambda i, be: (layer, be[i], 0, 0))],
        out_specs=pl.BlockSpec((MOE_BLOCK, Dm), lambda i, be: (i, 0)),
        scratch_shapes=[pltpu.VMEM((Dm, MOE_FF), bf16), pltpu.VMEM((Dm, MOE_FF), bf16), pltpu.VMEM((MOE_FF, Dm), bf16)],
    )
    return pl.pallas_call(
        _moe_ffn_kernel,
        grid_spec=grid_spec,
        out_shape=jax.ShapeDtypeStruct((rows, Dm), f32),
        compiler_params=pltpu.CompilerParams(dimension_semantics=("arbitrary",), vmem_limit_bytes=VMEM_LIMIT_BYTES),
        name="moe_expert_ffn",
    )(block_e.astype(jnp.int32), xs, w1.astype(f32), w3.astype(f32), w2.astype(f32))


def _rms(x):
    return x * lax.rsqrt(jnp.mean(x * x, axis=-1, keepdims=True) + NORM_EPS)


def _norm_proj_kernel(x_ref, g_ref, w_ref, o_ref):
    h = (_rms(x_ref[...]) * g_ref[...]).astype(bf16)
    o_ref[...] = jnp.dot(h, w_ref[...], preferred_element_type=f32)


def norm_proj(x2, g, w):
    N, D = x2.shape
    F = w.shape[1]
    tm = PROJ_ROWS_PER_STEP
    return pl.pallas_call(
        _norm_proj_kernel,
        grid=(N // tm,),
        in_specs=[pl.BlockSpec((tm, D), lambda i: (i, 0)), pl.BlockSpec((1, D), lambda i: (0, 0)),
                  pl.BlockSpec((D, F), lambda i: (0, 0))],
        out_specs=pl.BlockSpec((tm, F), lambda i: (i, 0)),
        out_shape=jax.ShapeDtypeStruct((N, F), f32),
        compiler_params=pltpu.CompilerParams(dimension_semantics=("parallel",), vmem_limit_bytes=VMEM_LIMIT_BYTES),
        name="norm_proj",
    )(x2, g.astype(f32).reshape(1, D), w.astype(bf16))


def _out_proj_kernel(x_ref, yh_ref, yr_ref, yn_ref, wh_ref, wr_ref, wn_ref, g_ref, wrt_ref, xo_ref, h_ref, lg_ref):
    mix = (jnp.dot(yh_ref[...].astype(bf16), wh_ref[...], preferred_element_type=f32)
           + jnp.dot(yr_ref[...].astype(bf16), wr_ref[...], preferred_element_type=f32)
           + jnp.dot(yn_ref[...].astype(bf16), wn_ref[...], preferred_element_type=f32))
    x = x_ref[...] + mix
    xo_ref[...] = x
    h = (_rms(x) * g_ref[...]).astype(bf16)
    h_ref[...] = h
    lg_ref[...] = jnp.dot(h, wrt_ref[...], preferred_element_type=f32)


def out_proj_norm_router(x2, y_hy, y_rw, y_na, w_out, g, w_router):
    N, D = x2.shape
    tm = PROJ_ROWS_PER_STEP
    d_hy, d_rw, d_na = y_hy.shape[1], y_rw.shape[1], y_na.shape[1]
    nr = w_router.shape[1]
    w_router = jnp.pad(w_router.astype(bf16), ((0, 0), (0, LANES - nr)))
    wb = w_out.astype(bf16)
    row = lambda c: pl.BlockSpec((tm, c), lambda i: (i, 0))
    full = lambda r, c: pl.BlockSpec((r, c), lambda i: (0, 0))
    return pl.pallas_call(
        _out_proj_kernel,
        grid=(N // tm,),
        in_specs=[row(D), row(d_hy), row(d_rw), row(d_na), full(d_hy, D), full(d_rw, D), full(d_na, D), full(1, D),
                  full(D, LANES)],
        out_specs=[row(D), row(D), row(LANES)],
        out_shape=[jax.ShapeDtypeStruct((N, D), f32), jax.ShapeDtypeStruct((N, D), bf16),
                   jax.ShapeDtypeStruct((N, LANES), f32)],
        compiler_params=pltpu.CompilerParams(dimension_semantics=("parallel",), vmem_limit_bytes=VMEM_LIMIT_BYTES),
        name="out_proj_norm_router",
    )(x2, y_hy, y_rw, y_na, wb[:d_hy], wb[d_hy:d_hy + d_rw], wb[d_hy + d_rw:], g.astype(f32).reshape(1, D), w_router)


def _final_norm_kernel(x_ref, g_ref, o_ref):
    x = x_ref[...]
    o_ref[...] = x * lax.rsqrt(jnp.mean(x * x, axis=-1, keepdims=True) + NORM_EPS) * g_ref[...]


def final_rmsnorm(x, g):
    Bn, L, D = x.shape
    tm = 1024
    out = pl.pallas_call(
        _final_norm_kernel,
        grid=(Bn * L // tm,),
        in_specs=[pl.BlockSpec((tm, D), lambda i: (i, 0)), pl.BlockSpec((1, D), lambda i: (0, 0))],
        out_specs=pl.BlockSpec((tm, D), lambda i: (i, 0)),
        out_shape=jax.ShapeDtypeStruct((Bn * L, D), x.dtype),
        name="final_rmsnorm",
    )(x.reshape(Bn * L, D), g.reshape(1, D))
    return out.reshape(Bn, L, D)


def hyena_positional_features(L):
    t = jnp.linspace(0.0, 1.0, L, dtype=f32)[:, None]
    w = (2.0 * math.pi / L) * jnp.arange(L, dtype=f32)[:, None]
    f = jnp.linspace(1e-4, HY_BANDS - 1, HY_BANDS, dtype=f32)[None, :]
    z = jnp.concatenate([t, jnp.cos(f * w), -jnp.sin(f * w)], axis=-1)
    return z, t


def hyena_filters(z, t, w1, b1, w2, b2, w3, b3, wout, freq):
    fr = freq.astype(f32)
    act = lambda u: jnp.sin(fr * u)
    h = act(z @ w1.astype(f32) + b1.astype(f32))
    h = act(h @ w2.astype(f32) + b2.astype(f32))
    h = act(h @ w3.astype(f32) + b3.astype(f32))
    h = (h @ wout.astype(f32)).reshape(-1, 2, HY_D)
    deltas = jnp.abs(jnp.linspace(HY_MIN_DECAY, HY_MAX_DECAY, HY_D, dtype=f32))
    h = h * jnp.exp(-t[:, :, None] * deltas)
    return h * lax.rsqrt(jnp.sum(h * h, axis=(0, 1), keepdims=True) + 1e-6)


def _split_bf16(x):
    hi = x.astype(bf16)
    return hi, (x - hi.astype(f32)).astype(bf16)


def _dot3(m_hi, m_lo, x_hi, x_lo):
    d = lambda a, b: lax.dot_general(a, b, _DN['nn'], preferred_element_type=f32)
    return d(m_hi, x_hi) + (d(m_hi, x_lo) + d(m_lo, x_hi))


def _cmatmul(mr, mi, xr, xi):
    xrh, xrl = _split_bf16(xr)
    rr = _dot3(mr[0], mr[1], xrh, xrl)
    ir = _dot3(mi[0], mi[1], xrh, xrl)
    if xi is None:
        return rr, ir
    xih, xil = _split_bf16(xi)
    ii = _dot3(mi[0], mi[1], xih, xil)
    ri = _dot3(mr[0], mr[1], xih, xil)
    return rr - ii, ri + ir


def _row_dft_kernel(mrh_ref, mrl_ref, mih_ref, mil_ref, *refs, real_input):
    if real_input:
        ur_ref, or_ref, oi_ref = refs
        ui = None
    else:
        ur_ref, ui_ref, or_ref, oi_ref = refs
        ui = ui_ref[0]
    o_r, o_i = _cmatmul((mrh_ref[...], mrl_ref[...]), (mih_ref[...], mil_ref[...]), ur_ref[0], ui)
    or_ref[0] = o_r
    oi_ref[0] = o_i


def row_dft(tabs, u, packed):
    rows, r_in, W = u.shape
    P = rows // 2 if packed else rows
    r_out = tabs[0].shape[0]
    wb = min(FFT_LANE_BLOCK, W)
    tab_spec = pl.BlockSpec((r_out, r_in), lambda p, j: (0, 0))
    re_spec = pl.BlockSpec((1, r_in, wb), lambda p, j: (p, 0, j))
    im_spec = pl.BlockSpec((1, r_in, wb), lambda p, j: (p + P, 0, j))
    out_spec = pl.BlockSpec((1, r_out, wb), lambda p, j: (p, 0, j))
    ins = [u, u] if packed else [u]
    return pl.pallas_call(
        functools.partial(_row_dft_kernel, real_input=not packed),
        grid=(P, W // wb),
        in_specs=[tab_spec] * 4 + ([re_spec, im_spec] if packed else [re_spec]),
        out_specs=[out_spec, out_spec],
        out_shape=[jax.ShapeDtypeStruct((P, r_out, W), f32)] * 2,
        compiler_params=pltpu.CompilerParams(dimension_semantics=("parallel", "parallel"),
                                             vmem_limit_bytes=VMEM_LIMIT_BYTES),
        name="hyena_row_dft",
    )(*tabs, *ins)


def _row_idft_gate_kernel(mrh_ref, mrl_ref, mih_ref, mil_ref, dr_ref, di_ref, z0_ref, z1_ref, x0_ref, x1_ref,
                          skip_ref, o0_ref, o1_ref):
    y_r, y_i = _cmatmul((mrh_ref[...], mrl_ref[...]), (mih_ref[...], mil_ref[...]), dr_ref[0], di_ref[0])
    skip = skip_ref[...]
    o0_ref[0] = x0_ref[0] * (y_r + z0_ref[0] * skip)
    o1_ref[0] = x1_ref[0] * (y_i + z1_ref[0] * skip)


def row_idft_gate(tabs, dr, di, z, x0, skip_row):
    P, r_in, W = dr.shape
    r_out = tabs[0].shape[0]
    wb = min(FFT_LANE_BLOCK, W)
    tab_spec = pl.BlockSpec((r_out, r_in), lambda p, j: (0, 0))
    d_spec = pl.BlockSpec((1, r_in, wb), lambda p, j: (p, 0, j))
    lo = pl.BlockSpec((1, r_out, wb), lambda p, j: (p, 0, j))
    hi = pl.BlockSpec((1, r_out, wb), lambda p, j: (p + P, 0, j))
    o0, o1 = pl.pallas_call(
        _row_idft_gate_kernel,
        grid=(P, W // wb),
        in_specs=[tab_spec] * 4 + [d_spec, d_spec, lo, hi, lo, hi, pl.BlockSpec((1, wb), lambda p, j: (0, j))],
        out_specs=[lo, lo],
        out_shape=[jax.ShapeDtypeStruct((P, r_out, W), f32)] * 2,
        compiler_params=pltpu.CompilerParams(dimension_semantics=("parallel", "parallel"),
                                             vmem_limit_bytes=VMEM_LIMIT_BYTES),
        name="hyena_row_idft_gate",
    )(*tabs, dr, di, z, z, x0, x0, skip_row)
    return jnp.concatenate([o0, o1], axis=0)


def _col_dft_kernel(grh_ref, grl_ref, gih_ref, gil_ref, ar_ref, ai_ref, br_ref, bi_ref, *, k1_per_step):
    for j in range(k1_per_step):
        b_r, b_i = _cmatmul((grh_ref[j], grl_ref[j]), (gih_ref[j], gil_ref[j]), ar_ref[0, j], ai_ref[0, j])
        br_ref[0, j] = b_r
        bi_ref[0, j] = b_i


def _col_conv_kernel(grh_ref, grl_ref, gih_ref, gil_ref, trh_ref, trl_ref, tih_ref, til_ref,
                     ar_ref, ai_ref, kr_ref, ki_ref, dr_ref, di_ref, *, k1_per_step):
    for j in range(k1_per_step):
        b_r, b_i = _cmatmul((grh_ref[j], grl_ref[j]), (gih_ref[j], gil_ref[j]), ar_ref[0, j], ai_ref[0, j])
        k_r = kr_ref[0, j]
        k_i = ki_ref[0, j]
        c_r = b_r * k_r - b_i * k_i
        c_i = b_r * k_i + b_i * k_r
        d_r, d_i = _cmatmul((trh_ref[j], trl_ref[j]), (tih_ref[j], til_ref[j]), c_r, c_i)
        dr_ref[0, j] = d_r
        di_ref[0, j] = d_i


def col_stage(g_tabs, gt_tabs, ar, ai, kr=None, ki=None):
    P, n1, n2, C = ar.shape
    kb = min(FFT_K1_PER_STEP, n1)
    g_spec = pl.BlockSpec((kb, n2, n2), lambda p, j: (j, 0, 0))
    a_spec = pl.BlockSpec((1, kb, n2, C), lambda p, j: (p, j, 0, 0))
    k_spec = pl.BlockSpec((1, kb, n2, C), lambda p, j: (0, j, 0, 0))
    common = dict(
        grid=(P, n1 // kb),
        out_specs=[a_spec, a_spec],
        out_shape=[jax.ShapeDtypeStruct((P, n1, n2, C), f32)] * 2,
        compiler_params=pltpu.CompilerParams(dimension_semantics=("parallel", "parallel"),
                                             vmem_limit_bytes=VMEM_LIMIT_BYTES),
    )
    if kr is None:
        return pl.pallas_call(functools.partial(_col_dft_kernel, k1_per_step=kb),
                              in_specs=[g_spec] * 4 + [a_spec] * 2, name="hyena_col_dft", **common)(*g_tabs, ar, ai)
    return pl.pallas_call(functools.partial(_col_conv_kernel, k1_per_step=kb),
                          in_specs=[g_spec] * 8 + [a_spec] * 2 + [k_spec] * 2, name="hyena_col_conv", **common)(
        *g_tabs, *gt_tabs, ar, ai, kr, ki)


def _bf16_tables(m):
    out = []
    for part in (np.real(m), np.imag(m)):
        x = jnp.asarray(part, f32)
        hi = x.astype(bf16)
        out += [hi, (x - hi.astype(f32)).astype(bf16)]
    return out


def hyena_dft_tables(L):
    n2 = FFT_N2
    n1 = 2 * L // n2
    N = n1 * n2
    a = np.arange(n1)
    fa = np.exp(-2j * np.pi * np.outer(a, a) / n1)
    k1 = np.arange(n1)[:, None, None]
    k2 = np.arange(n2)[None, :, None]
    nn = np.arange(n2)[None, None, :]
    g = np.exp(-2j * np.pi * nn * (k1 + n1 * k2) / N)
    gt = np.conj(np.transpose(g, (0, 2, 1)))
    fc = np.conj(fa).T / N
    return dict(fa=_bf16_tables(fa), fa_half=_bf16_tables(fa[:, :n1 // 2]), g=_bf16_tables(g), gt=_bf16_tables(gt),
                fc_half=_bf16_tables(fc[:n1 // 2]))


def hyena_long_conv_gate(tabs, z, x0, k2, skip):
    Bn, L, C = z.shape
    n2 = FFT_N2
    n1 = 2 * L // n2
    P = Bn // 2
    W = n2 * C
    ar, ai = row_dft(tabs['fa'], k2.reshape(1, n1, W), packed=False)
    kr, ki = col_stage(tabs['g'], tabs['gt'], ar.reshape(1, n1, n2, C), ai.reshape(1, n1, n2, C))
    zv = z.reshape(Bn, n1 // 2, W)
    ar, ai = row_dft(tabs['fa_half'], zv, packed=True)
    dr, di = col_stage(tabs['g'], tabs['gt'], ar.reshape(P, n1, n2, C), ai.reshape(P, n1, n2, C), kr, ki)
    out = row_idft_gate(tabs['fc_half'], dr.reshape(P, n1, W), di.reshape(P, n1, W), zv, x0.reshape(Bn, n1 // 2, W),
                        jnp.tile(skip.astype(f32), n2).reshape(1, W))
    return out.reshape(Bn, L, C)


def _hyena_prep_kernel(*refs, rows, nt):
    cur, prev, nxt = refs[0:3], refs[3:6], refs[6:9]
    w_ref, b_ref, z_ref, x0_ref = refs[9:]
    t = pl.program_id(1)
    has_prev = (t > 0).astype(f32)
    has_next = (t < nt - 1).astype(f32)
    ridx = lax.broadcasted_iota(jnp.int32, (rows, HY_D), 0)

    def conv(j):
        u = cur[j][0]
        prev_row = prev[j][0, SUBLANES - 1:SUBLANES, :] * has_prev
        next_row = nxt[j][0, 0:1, :] * has_next
        up = jnp.where(ridx == 0, prev_row, pltpu.roll(u, 1, axis=0))
        un = jnp.where(ridx == rows - 1, next_row, pltpu.roll(u, rows - 1, axis=0))
        sl = slice(j * HY_D, (j + 1) * HY_D)
        return up * w_ref[0:1, sl] + u * w_ref[1:2, sl] + un * w_ref[2:3, sl] + b_ref[:, sl]

    x0_ref[0] = conv(0)
    z_ref[0] = conv(2) * conv(1)


def hyena_prep(proj, lane0, conv_w, conv_b):
    Bn, L, _ = proj.shape
    rows = RW_ROWS_PER_STEP
    nt = L // rows
    hb = rows // SUBLANES
    b0 = lane0 // HY_D
    assert b0 * HY_D == lane0
    cur = lambda j: pl.BlockSpec((1, rows, HY_D), lambda b, t: (b, t, b0 + j))
    prev = lambda j: pl.BlockSpec((1, SUBLANES, HY_D), lambda b, t: (b, jnp.maximum(t * hb - 1, 0), b0 + j))
    nxt = lambda j: pl.BlockSpec((1, SUBLANES, HY_D), lambda b, t: (b, jnp.minimum((t + 1) * hb, L // SUBLANES - 1), b0 + j))
    full = lambda shp: pl.BlockSpec(shp, lambda b, t: (0,) * len(shp))
    out = pl.BlockSpec((1, rows, HY_D), lambda b, t: (b, t, 0))
    return pl.pallas_call(
        functools.partial(_hyena_prep_kernel, rows=rows, nt=nt),
        grid=(Bn, nt),
        in_specs=[cur(0), cur(1), cur(2), prev(0), prev(1), prev(2), nxt(0), nxt(1), nxt(2),
                  full((3, 3 * HY_D)), full((1, 3 * HY_D))],
        out_specs=[out, out],
        out_shape=[jax.ShapeDtypeStruct((Bn, L, HY_D), f32)] * 2,
        compiler_params=pltpu.CompilerParams(dimension_semantics=("parallel", "parallel")),
        name="hyena_prep",
    )(*([proj] * 9), conv_w.astype(f32), conv_b.astype(f32).reshape(1, 3 * HY_D))


def hyena_mixer(tabs, proj, lane0, conv_w, conv_b, filt, skip):
    z, x0 = hyena_prep(proj, lane0, conv_w, conv_b)
    k2 = jnp.concatenate([filt[:1, 0] + filt[:1, 1], filt[1:, 0],
                          jnp.zeros((1, HY_D), f32), filt[1:, 1][::-1]], axis=0)
    return hyena_long_conv_gate(tabs, z, x0, k2, skip)


def rwkv7_mixer(proj, lora_lane0, mu, w0, w2, a0, a2, k_k, k_a, r_k, g2, ln_w, ln_b):
    r, k, v, kk, lw2, a_both = rw_prep(proj, lora_lane0, mu, w0, w2, a0, a2, k_k)
    y2 = wkv7_chunked(r, k, v, kk, lw2, a_both, k_a.astype(f32).reshape(1, RW_D))
    return rw_post(y2, r, k, v, a_both, proj, lora_lane0, k_a, r_k, ln_w, ln_b, g2)


def hier_moe(h, logits, bg, be, w1, w3, w2, layer):
    N, Dm = h.shape
    assert MOE_TOPK == 2
    g_logits = logits[:, :MOE_GROUPS] + bg.astype(f32)
    g_sel = jnp.argmax(g_logits, axis=-1)
    g_prob = jnp.take_along_axis(jax.nn.softmax(g_logits, axis=-1), g_sel[:, None], axis=-1)
    e_logits = (logits[:, MOE_GROUPS:MOE_GROUPS + MOE_EXPERTS] + be.astype(f32)).reshape(N, MOE_GROUPS, MOE_PER_GROUP)
    e_logits = jnp.take_along_axis(e_logits, g_sel[:, None, None], axis=1)[:, 0]
    top_val, top_idx = lax.top_k(e_logits, MOE_TOPK)
    gate = g_prob * jax.nn.softmax(top_val, axis=-1)
    expert = g_sel[:, None] * MOE_PER_GROUP + top_idx
    M = N * MOE_TOPK
    flat_e = expert.reshape(M).astype(jnp.int32)
    experts = jnp.arange(MOE_EXPERTS, dtype=jnp.int32)
    counts = jnp.sum((flat_e[:, None] == experts[None, :]).astype(jnp.int32), axis=0)
    padded = (counts + MOE_BLOCK - 1) // MOE_BLOCK * MOE_BLOCK
    pad_end = jnp.cumsum(padded)
    n_blocks = -(-M // MOE_BLOCK) + MOE_EXPERTS
    n_rows = n_blocks * MOE_BLOCK
    cum_need = jnp.cumsum(padded - counts)
    filler = jnp.arange(n_rows - M, dtype=jnp.int32)
    filler_e = jnp.sum((cum_need[None, :] <= filler[:, None]).astype(jnp.int32), axis=1)
    keys = jnp.concatenate([2 * flat_e, 2 * filler_e + 1])
    ids = jnp.concatenate([jnp.arange(M, dtype=jnp.int32), jnp.full((n_rows - M,), M, jnp.int32)])
    _, slot_src = lax.sort((keys, ids), num_keys=1)
    tok_src = jnp.where(slot_src < M, slot_src // MOE_TOPK, jnp.arange(n_rows, dtype=jnp.int32) % N)
    block_start = jnp.arange(n_blocks, dtype=pad_end.dtype) * MOE_BLOCK
    block_e = jnp.minimum(jnp.sum(pad_end[None, :] <= block_start[:, None], axis=1), MOE_EXPERTS - 1)
    y = moe_expert_ffn(h[tok_src], block_e, w1, w3, w2, layer)
    _, row_of = lax.sort((slot_src, jnp.arange(n_rows, dtype=jnp.int32)), num_keys=1)
    pos = row_of[:M].reshape(N, MOE_TOPK)
    return gate[:, 0:1] * y[pos[:, 0]] + gate[:, 1:2] * y[pos[:, 1]]


def kernel(x, norm1_g, w_in, hy_conv_w, hy_conv_b, hy_w1, hy_b1, hy_w2, hy_b2, hy_w3, hy_b3, hy_wout, hy_freq, hy_skip, rw_mu, rw_w0, rw_w2, rw_a0, rw_a2, rw_kk, rw_ka, rw_rk, rw_g2, rw_ln_w, rw_ln_b, na_rpb, w_out, norm2_g, moe_wg, moe_bg, moe_we, moe_be, moe_w1, moe_w3, moe_w2, norm_f_g):
    Bn, L, _ = x.shape
    z_pos, t_pos = hyena_positional_features(L)
    dft_tabs = hyena_dft_tables(L)
    splits = np.cumsum(IN_SIZES)[:-1].tolist()
    hy_end, rkv_end = splits[0], splits[1]
    hy_lane0 = rkv_end - hy_end
    lora_lane0, na_lane0 = splits[1], splits[4]
    assert RW_D % LANES == 0 and lora_lane0 % LANES == 0 and na_lane0 % LANES == 0
    N = Bn * L
    x2 = x.reshape(N, D_MODEL)
    for l in range(DEPTH):
        w_in_l = jnp.concatenate([w_in[l][:, hy_end:rkv_end], w_in[l][:, :hy_end], w_in[l][:, rkv_end:]], axis=1)
        proj = norm_proj(x2, norm1_g[l], w_in_l).reshape(Bn, L, IN_D)
        filt = hyena_filters(z_pos, t_pos, hy_w1[l], hy_b1[l], hy_w2[l], hy_b2[l],
                             hy_w3[l], hy_b3[l], hy_wout[l], hy_freq[l])
        y_hy = hyena_mixer(dft_tabs, proj, hy_lane0, hy_conv_w[l], hy_conv_b[l], filt, hy_skip[l])
        y_rw = rwkv7_mixer(proj, lora_lane0, rw_mu[l], rw_w0[l], rw_w2[l], rw_a0[l],
                           rw_a2[l], rw_kk[l], rw_ka[l], rw_rk[l].reshape(RW_D), rw_g2[l], rw_ln_w[l], rw_ln_b[l])
        y_na = neighborhood_attention_pallas(proj, na_rpb[l], na_lane0 // LANES)
        x2, h2, logits = out_proj_norm_router(
            x2, y_hy.reshape(N, HY_D), y_rw.reshape(N, RW_D), y_na.reshape(N, NA_D), w_out[l], norm2_g[l],
            jnp.concatenate([moe_wg[l], moe_we[l]], axis=1))
        x2 = x2 + hier_moe(h2, logits, moe_bg[l], moe_be[l], moe_w1, moe_w3, moe_w2, l)
    return final_rmsnorm(x2.reshape(Bn, L, D_MODEL), norm_f_g)
```

```python
import functools
import math

import jax
import jax.numpy as jnp
import numpy as np
from jax import lax
from jax.experimental import pallas as pl
from jax.experimental.pallas import tpu as pltpu

f32 = jnp.float32
bf16 = jnp.bfloat16

D_MODEL = 1024
DEPTH = 2
GRID_W = 64
NORM_EPS = 1e-6
NEG_INF = -1e30

HY_D = D_MODEL // 4
HY_EMB = 33
HY_BANDS = (HY_EMB - 1) // 2
HY_FFN = 64
HY_MIN_DECAY = math.log(1e-2) / 1.5
HY_MAX_DECAY = math.log(1e-2) / 0.3

RW_N = 64
RW_D = D_MODEL // 2
RW_H = RW_D // RW_N
RW_W_LORA = 64
RW_A_LORA = 64
RW_G_LORA = 128
RW_GN_EPS = 64e-5

NA_HD = 64
NA_D = D_MODEL // 4
NA_H = NA_D // NA_HD
NA_KR = 8
NA_KC = 16

MIX_D = HY_D + RW_D + NA_D
IN_SIZES = (3 * HY_D, 3 * RW_D, RW_G_LORA, 2 * RW_W_LORA, 2 * RW_A_LORA, 3 * NA_D)
IN_D = sum(IN_SIZES)

MOE_GROUPS = 4
MOE_PER_GROUP = 8
MOE_EXPERTS = MOE_GROUPS * MOE_PER_GROUP
MOE_TOPK = 2
MOE_FF = 512
MOE_BLOCK = 512

LANES = 128
WKV_CHUNK = 64
WKV_TIME_BLOCK = 512
NA_ROWS_PER_STEP = 4
RW_ROWS_PER_STEP = 512
SUBLANES = 8
PROJ_ROWS_PER_STEP = 512
FFT_N2 = 128
FFT_LANE_BLOCK = 4096
FFT_K1_PER_STEP = 4
VMEM_LIMIT_BYTES = 48 * 1024 * 1024

_DN = {'nn': (((1,), (0,)), ((), ())), 'nt': (((1,), (1,)), ((), ())), 'tn': (((0,), (0,)), ((), ()))}


def _mm(a, b, dims='nn'):
    return lax.dot_general(a.astype(bf16), b.astype(bf16), _DN[dims], preferred_element_type=f32)


def _wkv_kernel(r_ref, k_ref, v_ref, kk_ref, lw_ref, a_ref, ka_ref, y_ref, ht_ref, *, tb, batch):
    T = WKV_CHUNK
    H2 = 2 * T
    nc = tb // T
    npairs = r_ref.shape[-1] // LANES
    mm = _mm

    @pl.when(pl.program_id(1) == 0)
    def _():
        ht_ref[...] = jnp.zeros_like(ht_ref)

    d = pl.program_id(0) // batch
    sign = 1 - 2 * d
    trow = lax.broadcasted_iota(jnp.int32, (T, H2), 0)
    lane = lax.broadcasted_iota(jnp.int32, (T, H2), 1)
    scol = lane % T
    tdiff = (trow - scol) * sign
    strict = tdiff > 0
    incl = tdiff >= 0
    same16 = (trow // 16) == (scol // 16)
    same32 = (trow // 32) == (scol // 32)
    off32 = same32 & jnp.logical_not(same16)
    off64 = jnp.logical_not(same32)
    eye2 = jnp.where(trow == scol, 1.0, 0.0).astype(f32)
    lo_lane = lane < T
    brow = lax.broadcasted_iota(jnp.int32, (H2, H2), 0)
    bcol = lax.broadcasted_iota(jnp.int32, (H2, H2), 1)
    same_head = (brow // T) == (bcol // T)
    crow = lax.broadcasted_iota(jnp.int32, (T, T), 0)
    ccol = lax.broadcasted_iota(jnp.int32, (T, T), 1)
    tri = jnp.where((crow - ccol) * sign >= 0, 1.0, 0.0).astype(bf16)
    is_bwd = d == 1

    def bd(x):
        return jnp.concatenate([jnp.where(lo_lane, x, 0.0), jnp.where(lo_lane, 0.0, x)], axis=0)

    inst = [(s, p) for s in range(nc) for p in range(npairs)]
    offs = [pl.multiple_of((s + d * (nc - 1 - 2 * s)) * T, T) for s in range(nc)]

    def load(ref, s, p):
        return ref[0, pl.ds(offs[s], T), p * LANES:(p + 1) * LANES]

    cs_l = []
    for s, p in inst:
        lw = load(lw_ref, s, p)
        l1 = lw.astype(bf16)
        l2 = (lw - l1.astype(f32)).astype(bf16)
        dd = lambda x: lax.dot_general(tri, x, _DN['nn'], preferred_element_type=f32)
        cs_l.append((dd(l1) + dd(l2), lw))
    ops = []
    for (s, p), (cs, lw) in zip(inst, cs_l):
        r = load(r_ref, s, p)
        k = load(k_ref, s, p)
        v = load(v_ref, s, p)
        kk = load(kk_ref, s, p)
        a = load(a_ref, s, p)
        ka = ka_ref[:, p * LANES:(p + 1) * LANES]
        kd = k * (1.0 + (a - 1.0) * ka)
        b = kk * a
        cs_end = jnp.where(is_bwd, cs[0:1, :], cs[T - 1:T, :])
        em = jnp.exp(-cs)
        e_end = jnp.exp(cs_end - cs)
        ops.append(dict(
            AR=jnp.concatenate([-kk * jnp.exp(cs - lw), r * jnp.exp(cs)], axis=0),
            BK=jnp.concatenate([bd(b * em), bd(kd * em)], axis=0),
            V=v, BKg=jnp.concatenate([b * e_end, kd * e_end], axis=0), g_end=jnp.exp(cs_end)))
    for o in ops:
        S = mm(o['AR'], o['BK'], 'nt')
        N = jnp.where(strict, S[:T, :H2], 0.0)
        o['AakArk'] = jnp.concatenate([jnp.where(strict, S[:T, H2:], 0.0), jnp.where(incl, S[T:, H2:], 0.0)], axis=0)
        o['Arb'] = jnp.where(incl, S[T:, :H2], 0.0)
        o['Nd'] = jnp.where(same16, N, 0.0)
        o['N32'] = jnp.where(off32, N, 0.0)
        o['N64'] = jnp.where(off64, N, 0.0)
    for o in ops:
        o['X'] = eye2 + o['Nd']
        o['P'] = mm(o['Nd'], bd(o['Nd']))
    for it in range(3):
        if it < 2:
            for o in ops:
                px = mm(o['P'], jnp.concatenate([bd(o['X']), bd(o['P'])], axis=1))
                o['X'] = o['X'] + px[:, :H2]
                o['P'] = px[:, H2:]
        else:
            for o in ops:
                o['X'] = o['X'] + mm(o['P'], bd(o['X']))
    for key in ('N32', 'N64'):
        for o in ops:
            o['Z'] = mm(o[key], bd(o['X']))
        for o in ops:
            o['X'] = o['X'] + mm(o['X'], bd(o['Z']))
    for o in ops:
        wy = mm(o['AakArk'], bd(o['V']))
        o['W0'] = wy[:T]
        o['Yv'] = wy[T:]
    for o in ops:
        o['XAW'] = mm(o['X'], jnp.concatenate([bd(o['AR'][:T]), bd(o['W0'])], axis=1))
    for o in ops:
        ax = mm(o['Arb'], jnp.concatenate([bd(o['XAW'][:, :H2]), bd(o['XAW'][:, H2:])], axis=1))
        o['Rhat'] = o['AR'][T:] + ax[:, :H2]
        o['Yc'] = ax[:, H2:] + o['Yv']
    for o in ops:
        lhs = jnp.concatenate([o['XAW'], jnp.concatenate([jnp.zeros((T, H2), f32), o['V']], axis=1)], axis=0)
        pq = mm(lhs, o['BKg'], 'tn')
        o['Pc'] = jnp.where(same_head, pq[:H2], 0.0)
        o['Qc'] = jnp.where(same_head, pq[H2:], 0.0)
    hts = [ht_ref[p] for p in range(npairs)]
    for s in range(nc):
        cur = [ops[s * npairs + p] for p in range(npairs)]
        ys = [mm(o['Rhat'], hts[p], 'nt') + o['Yc'] for p, o in enumerate(cur)]
        hts = [hts[p] * o['g_end'] + (mm(hts[p], o['Pc']) + o['Qc']) for p, o in enumerate(cur)]
        for p in range(npairs):
            y_ref[0, pl.ds(offs[s], T), p * LANES:(p + 1) * LANES] = ys[p]
    for p in range(npairs):
        ht_ref[p] = hts[p]


def wkv7_chunked(r, k, v, kk, lw2, a2, ka):
    Bn, L, C = r.shape
    tb = WKV_TIME_BLOCK
    nt = L // tb
    tmap = lambda i, t: t + (i // Bn) * (nt - 1 - 2 * t)
    shared = pl.BlockSpec((1, tb, C), lambda i, t: (i % Bn, tmap(i, t), 0))
    per_dir = pl.BlockSpec((1, tb, C), lambda i, t: (i % Bn, tmap(i, t), i // Bn))
    return pl.pallas_call(
        functools.partial(_wkv_kernel, tb=tb, batch=Bn),
        grid=(2 * Bn, nt),
        in_specs=[shared, shared, shared, shared, per_dir, per_dir, pl.BlockSpec((1, C), lambda i, t: (0, 0))],
        out_specs=pl.BlockSpec((1, tb, C), lambda i, t: (i, tmap(i, t), 0)),
        out_shape=jax.ShapeDtypeStruct((2 * Bn, L, C), f32),
        scratch_shapes=[pltpu.VMEM((C // LANES, LANES, LANES), f32)],
        compiler_params=pltpu.CompilerParams(dimension_semantics=("parallel", "arbitrary"),
                                             vmem_limit_bytes=VMEM_LIMIT_BYTES),
        name="wkv7_chunked",
    )(r, k, v, kk, lw2, a2, ka)


def _mm_exact_rhs(a, b_bf16):
    ah = a.astype(bf16)
    al = (a - ah.astype(f32)).astype(bf16)
    d = lambda x: lax.dot_general(x, b_bf16, _DN['nn'], preferred_element_type=f32)
    return d(ah) + d(al)


def _head_sum_matrix(scale):
    i = lax.broadcasted_iota(jnp.int32, (RW_D, RW_D), 0) // RW_N
    j = lax.broadcasted_iota(jnp.int32, (RW_D, RW_D), 1) // RW_N
    return jnp.where(i == j, scale, 0.0).astype(bf16)


def _rw_prep_kernel(r_ref, k_ref, v_ref, rp_ref, kp_ref, vp_ref, rn_ref, kn_ref, vn_ref, w_ref, a_ref,
                    mu_ref, w0_ref, w2_ref, a0_ref, a2_ref, kkw_ref,
                    ro_ref, ko_ref, vo_ref, kko_ref, lw_ref, ao_ref, *, rows, nt):
    t = pl.program_id(1)
    has_prev = (t > 0).astype(f32)
    has_next = (t < nt - 1).astype(f32)
    ridx = lax.broadcasted_iota(jnp.int32, (rows, RW_D), 0)

    def tshift(cur_ref, prev_ref, next_ref, j):
        u = cur_ref[0]
        prev_row = prev_ref[0, SUBLANES - 1:SUBLANES, :] * has_prev
        next_row = next_ref[0, 0:1, :] * has_next
        up = jnp.where(ridx == 0, prev_row, pltpu.roll(u, 1, axis=0))
        un = jnp.where(ridx == rows - 1, next_row, pltpu.roll(u, rows - 1, axis=0))
        return u + mu_ref[j, 0:1, :] * (up - u) + mu_ref[j, 1:2, :] * (un - u)

    r = tshift(r_ref, rp_ref, rn_ref, 0)
    k = tshift(k_ref, kp_ref, kn_ref, 1)
    v = tshift(v_ref, vp_ref, vn_ref, 2)
    ro_ref[0] = r
    ko_ref[0] = k
    vo_ref[0] = v
    kk = k * kkw_ref[...]
    ss = _mm_exact_rhs(kk * kk, _head_sum_matrix(1.0))
    kko_ref[0] = kk * lax.rsqrt(jnp.maximum(ss, 1e-24))
    wl = jnp.tanh(w_ref[0])
    al = a_ref[0]
    for d in range(2):
        wpre = w0_ref[d:d + 1, :] + _mm(wl[:, d * RW_W_LORA:(d + 1) * RW_W_LORA], w2_ref[d])
        lw_ref[0, :, d * RW_D:(d + 1) * RW_D] = -math.exp(-0.5) * jax.nn.sigmoid(wpre)
        av = a0_ref[d:d + 1, :] + _mm(al[:, d * RW_A_LORA:(d + 1) * RW_A_LORA], a2_ref[d])
        ao_ref[0, :, d * RW_D:(d + 1) * RW_D] = jax.nn.sigmoid(av)


def rw_prep(proj, lora_lane0, mu, w0, w2, a0, a2, k_k):
    Bn, L, _ = proj.shape
    C = RW_D
    rows = RW_ROWS_PER_STEP
    nt = L // rows
    hb = rows // SUBLANES
    lb = lora_lane0 // LANES
    cur = lambda j: pl.BlockSpec((1, rows, C), lambda b, t: (b, t, j))
    prev = lambda j: pl.BlockSpec((1, SUBLANES, C), lambda b, t: (b, jnp.maximum(t * hb - 1, 0), j))
    nxt = lambda j: pl.BlockSpec((1, SUBLANES, C), lambda b, t: (b, jnp.minimum((t + 1) * hb, L // SUBLANES - 1), j))
    lora_w = pl.BlockSpec((1, rows, LANES), lambda b, t: (b, t, lb + 1))
    lora_a = pl.BlockSpec((1, rows, LANES), lambda b, t: (b, t, lb + 2))
    full = lambda shp: pl.BlockSpec(shp, lambda b, t: (0,) * len(shp))
    out_c = pl.BlockSpec((1, rows, C), lambda b, t: (b, t, 0))
    out_2c = pl.BlockSpec((1, rows, 2 * C), lambda b, t: (b, t, 0))
    sds = lambda c: jax.ShapeDtypeStruct((Bn, L, c), f32)
    return pl.pallas_call(
        functools.partial(_rw_prep_kernel, rows=rows, nt=nt),
        grid=(Bn, nt),
        in_specs=[cur(0), cur(1), cur(2), prev(0), prev(1), prev(2), nxt(0), nxt(1), nxt(2), lora_w, lora_a,
                  full((3, 2, C)), full((2, C)), full((2, RW_W_LORA, C)), full((2, C)), full((2, RW_A_LORA, C)),
                  full((1, C))],
        out_specs=[out_c, out_c, out_c, out_c, out_2c, out_2c],
        out_shape=[sds(C), sds(C), sds(C), sds(C), sds(2 * C), sds(2 * C)],
        compiler_params=pltpu.CompilerParams(dimension_semantics=("parallel", "parallel"),
                                             vmem_limit_bytes=VMEM_LIMIT_BYTES),
        name="rwkv_prep",
    )(proj, proj, proj, proj, proj, proj, proj, proj, proj, proj, proj,
      mu.astype(f32), w0.astype(f32), w2.astype(bf16), a0.astype(f32), a2.astype(bf16), k_k.astype(f32).reshape(1, C))


def _rw_post_kernel(yf_ref, yb_ref, r_ref, k_ref, v_ref, a_ref, g_ref, ka_ref, rk_ref, lnw_ref, lnb_ref, g2_ref, o_ref):
    C = RW_D
    y = yf_ref[0] + yb_ref[0]
    avg = _head_sum_matrix(1.0 / RW_N)
    mean = _mm_exact_rhs(y, avg)
    yc = y - mean
    var = _mm_exact_rhs(yc * yc, avg)
    yn = yc * lax.rsqrt(var + RW_GN_EPS) * lnw_ref[...] + lnb_ref[...]
    a = a_ref[0]
    k = k_ref[0]
    ka = ka_ref[...]
    ksum = k * (1.0 + (a[:, :C] - 1.0) * ka) + k * (1.0 + (a[:, C:] - 1.0) * ka)
    coef = _mm_exact_rhs(r_ref[0] * ksum * rk_ref[...], _head_sum_matrix(1.0))
    gate = _mm(jax.nn.sigmoid(g_ref[0]), g2_ref[...])
    o_ref[0] = ((yn + coef * v_ref[0]) * gate).astype(o_ref.dtype)


def rw_post(y2, r, k, v, a2, proj, g_lane0, k_a, r_k, ln_w, ln_b, g2):
    Bn, L, C = r.shape
    rows = RW_ROWS_PER_STEP
    nt = L // rows
    gb = g_lane0 // LANES
    blk = lambda c: pl.BlockSpec((1, rows, c), lambda b, t: (b, t, 0))
    full = lambda shp: pl.BlockSpec(shp, lambda b, t: (0,) * len(shp))
    row = lambda x: x.astype(f32).reshape(1, C)
    return pl.pallas_call(
        _rw_post_kernel,
        grid=(Bn, nt),
        in_specs=[blk(C), pl.BlockSpec((1, rows, C), lambda b, t: (b + Bn, t, 0)), blk(C), blk(C), blk(C), blk(2 * C),
                  pl.BlockSpec((1, rows, LANES), lambda b, t: (b, t, gb)),
                  full((1, C)), full((1, C)), full((1, C)), full((1, C)), full((RW_G_LORA, C))],
        out_specs=blk(C),
        out_shape=jax.ShapeDtypeStruct((Bn, L, C), bf16),
        compiler_params=pltpu.CompilerParams(dimension_semantics=("parallel", "parallel"),
                                             vmem_limit_bytes=VMEM_LIMIT_BYTES),
        name="rwkv_post",
    )(y2, y2, r, k, v, a2, proj, row(k_a), row(r_k), row(ln_w), row(ln_b), g2.astype(bf16))


def _na_kernel(q_ref, k_ref, v_ref, bias_ref, o_ref, kb_ref, vb_ref, *, rows_per_step, n_rows, kr):
    W = GRID_W
    rb = pl.program_id(2)

    @pl.when(rb == 0)
    def _():
        kb_ref[...] = k_ref[0].astype(bf16)
        vb_ref[...] = v_ref[0].astype(bf16)

    lo_lane = lax.broadcasted_iota(jnp.int32, (W, LANES), 1) < NA_HD
    scale = NA_HD ** -0.5
    rows = []
    for j in range(rows_per_step):
        r = rb * rows_per_step + j
        start = jnp.clip(r - kr // 2, 0, n_rows - kr)
        rows.append((start, start - r + (NA_KR - 1)))
    s_list = []
    for j, (start, didx) in enumerate(rows):
        q = q_ref[0, j * W:(j + 1) * W, :] * scale
        qs = jnp.concatenate([jnp.where(lo_lane, q, 0.0), jnp.where(lo_lane, 0.0, q)], axis=0)
        kw = kb_ref[pl.ds(pl.multiple_of(start * W, W), kr * W), :]
        s_list.append(_mm(qs, kw, 'nt') + bias_ref[didx, 0])
    p_list = []
    for s in s_list:
        m = jnp.max(s, axis=-1, keepdims=True)
        p = jnp.exp(s - m)
        p_list.append((p, jnp.sum(p, axis=-1, keepdims=True)))
    for j, ((start, _), (p, l)) in enumerate(zip(rows, p_list)):
        vw = vb_ref[pl.ds(pl.multiple_of(start * W, W), kr * W), :]
        o = _mm(p, vw) / l
        o_ref[0, j * W:(j + 1) * W, :] = jnp.where(lo_lane, o[:W], o[W:]).astype(o_ref.dtype)


def na_bias_table(rpb, n_rows):
    W = GRID_W
    kr = min(NA_KR, n_rows)
    cols = jnp.arange(W)
    col_start = jnp.clip(cols - NA_KC // 2, 0, W - NA_KC)
    in_band = (cols[None, :] >= col_start[:, None]) & (cols[None, :] < col_start[:, None] + NA_KC)
    dc = jnp.clip(cols[None, :] - cols[:, None], -(NA_KC - 1), NA_KC - 1) + (NA_KC - 1)
    win = jnp.stack([rpb.astype(f32)[:, d:d + kr, :] for d in range(8)], axis=1)
    onehot = (dc[:, :, None] == jnp.arange(2 * NA_KC - 1)[None, None, :]).astype(f32)
    tab = jnp.einsum('hdic,qkc->hdqik', win, onehot, precision=lax.Precision.HIGHEST)
    tab = jnp.where(in_band[None, None, :, None, :], tab, NEG_INF)
    H = rpb.shape[0]
    tab = tab.reshape(H // 2, 2, 8, W, kr * W).transpose(2, 0, 1, 3, 4)
    return tab.reshape(8, H // 2, 2 * W, kr * W)


def neighborhood_attention_pallas(proj, rpb, lane_block0):
    Bn, L, _ = proj.shape
    W = GRID_W
    n_rows = L // W
    kr = min(NA_KR, n_rows)
    hp = NA_H * NA_HD // LANES
    rps = NA_ROWS_PER_STEP
    bias = na_bias_table(rpb, n_rows)
    kern = functools.partial(_na_kernel, rows_per_step=rps, n_rows=n_rows, kr=kr)
    return pl.pallas_call(
        kern,
        grid=(Bn, hp, n_rows // rps),
        in_specs=[pl.BlockSpec((1, rps * W, LANES), lambda b, h, r: (b, r, lane_block0 + h)),
                  pl.BlockSpec((1, L, LANES), lambda b, h, r: (b, 0, lane_block0 + hp + h)),
                  pl.BlockSpec((1, L, LANES), lambda b, h, r: (b, 0, lane_block0 + 2 * hp + h)),
                  pl.BlockSpec((8, 1, 2 * W, kr * W), lambda b, h, r: (0, h, 0, 0))],
        out_specs=pl.BlockSpec((1, rps * W, LANES), lambda b, h, r: (b, r, h)),
        out_shape=jax.ShapeDtypeStruct((Bn, L, NA_H * NA_HD), bf16),
        scratch_shapes=[pltpu.VMEM((L, LANES), bf16), pltpu.VMEM((L, LANES), bf16)],
        compiler_params=pltpu.CompilerParams(dimension_semantics=("parallel", "parallel", "arbitrary"),
                                             vmem_limit_bytes=VMEM_LIMIT_BYTES),
        name="neighborhood_attention",
    )(proj, proj, proj, bias)


def _moe_ffn_kernel(be_ref, x_ref, w1_ref, w3_ref, w2_ref, o_ref, w1b_ref, w3b_ref, w2b_ref):
    i = pl.program_id(0)

    @pl.when((i == 0) | (be_ref[i] != be_ref[jnp.maximum(i - 1, 0)]))
    def _():
        w1b_ref[...] = w1_ref[0, 0].astype(bf16)
        w3b_ref[...] = w3_ref[0, 0].astype(bf16)
        w2b_ref[...] = w2_ref[0, 0].astype(bf16)

    x = x_ref[...].astype(bf16)
    h1 = jnp.dot(x, w1b_ref[...], preferred_element_type=f32)
    h3 = jnp.dot(x, w3b_ref[...], preferred_element_type=f32)
    g = h1 * jax.nn.sigmoid(h1) * h3
    o_ref[...] = jnp.dot(g.astype(bf16), w2b_ref[...], preferred_element_type=f32).astype(o_ref.dtype)


def moe_expert_ffn(xs, block_e, w1, w3, w2, layer):
    rows, Dm = xs.shape
    n_blocks = rows // MOE_BLOCK
    grid_spec = pltpu.PrefetchScalarGridSpec(
        num_scalar_prefetch=1,
        grid=(n_blocks,),
        in_specs=[pl.BlockSpec((MOE_BLOCK, Dm), lambda i, be: (i, 0)),
                  pl.BlockSpec((1, 1, Dm, MOE_FF), lambda i, be: (layer, be[i], 0, 0)),
                  pl.BlockSpec((1, 1, Dm, MOE_FF), lambda i, be: (layer, be[i], 0, 0)),
                  pl.BlockSpec((1, 1, MOE_FF, Dm), lambda i, be: (layer, be[i], 0, 0))],
        out_specs=pl.BlockSpec((MOE_BLOCK, Dm), lambda i, be: (i, 0)),
        scratch_shapes=[pltpu.VMEM((Dm, MOE_FF), bf16), pltpu.VMEM((Dm, MOE_FF), bf16), pltpu.VMEM((MOE_FF, Dm), bf16)],
    )
    return pl.pallas_call(
        _moe_ffn_kernel,
        grid_spec=grid_spec,
        out_shape=jax.ShapeDtypeStruct((rows, Dm), bf16),
        compiler_params=pltpu.CompilerParams(dimension_semantics=("arbitrary",), vmem_limit_bytes=VMEM_LIMIT_BYTES),
        name="moe_expert_ffn",
    )(block_e.astype(jnp.int32), xs, w1.astype(f32), w3.astype(f32), w2.astype(f32))


def _rms(x):
    return x * lax.rsqrt(jnp.mean(x * x, axis=-1, keepdims=True) + NORM_EPS)


def _norm_proj_kernel(x_ref, g_ref, w_ref, o_ref):
    h = (_rms(x_ref[...]) * g_ref[...]).astype(bf16)
    o_ref[...] = jnp.dot(h, w_ref[...], preferred_element_type=f32)


def norm_proj(x2, g, w):
    N, D = x2.shape
    F = w.shape[1]
    tm = PROJ_ROWS_PER_STEP
    return pl.pallas_call(
        _norm_proj_kernel,
        grid=(N // tm,),
        in_specs=[pl.BlockSpec((tm, D), lambda i: (i, 0)), pl.BlockSpec((1, D), lambda i: (0, 0)),
                  pl.BlockSpec((D, F), lambda i: (0, 0))],
        out_specs=pl.BlockSpec((tm, F), lambda i: (i, 0)),
        out_shape=jax.ShapeDtypeStruct((N, F), f32),
        compiler_params=pltpu.CompilerParams(dimension_semantics=("parallel",), vmem_limit_bytes=VMEM_LIMIT_BYTES),
        name="norm_proj",
    )(x2, g.astype(f32).reshape(1, D), w.astype(bf16))


def _out_proj_kernel(x_ref, yh_ref, yr_ref, yn_ref, wh_ref, wr_ref, wn_ref, g_ref, wrt_ref, xo_ref, h_ref, lg_ref):
    mix = (jnp.dot(yh_ref[...].astype(bf16), wh_ref[...], preferred_element_type=f32)
           + jnp.dot(yr_ref[...].astype(bf16), wr_ref[...], preferred_element_type=f32)
           + jnp.dot(yn_ref[...].astype(bf16), wn_ref[...], preferred_element_type=f32))
    x = x_ref[...] + mix
    xo_ref[...] = x
    h = (_rms(x) * g_ref[...]).astype(bf16)
    h_ref[...] = h
    lg_ref[...] = jnp.dot(h, wrt_ref[...], preferred_element_type=f32)


def out_proj_norm_router(x2, y_hy, y_rw, y_na, w_out, g, w_router):
    N, D = x2.shape
    tm = PROJ_ROWS_PER_STEP
    d_hy, d_rw, d_na = y_hy.shape[1], y_rw.shape[1], y_na.shape[1]
    nr = w_router.shape[1]
    w_router = jnp.pad(w_router.astype(bf16), ((0, 0), (0, LANES - nr)))
    wb = w_out.astype(bf16)
    row = lambda c: pl.BlockSpec((tm, c), lambda i: (i, 0))
    full = lambda r, c: pl.BlockSpec((r, c), lambda i: (0, 0))
    return pl.pallas_call(
        _out_proj_kernel,
        grid=(N // tm,),
        in_specs=[row(D), row(d_hy), row(d_rw), row(d_na), full(d_hy, D), full(d_rw, D), full(d_na, D), full(1, D),
                  full(D, LANES)],
        out_specs=[row(D), row(D), row(LANES)],
        out_shape=[jax.ShapeDtypeStruct((N, D), f32), jax.ShapeDtypeStruct((N, D), bf16),
                   jax.ShapeDtypeStruct((N, LANES), f32)],
        compiler_params=pltpu.CompilerParams(dimension_semantics=("parallel",), vmem_limit_bytes=VMEM_LIMIT_BYTES),
        name="out_proj_norm_router",
    )(x2, y_hy, y_rw, y_na, wb[:d_hy], wb[d_hy:d_hy + d_rw], wb[d_hy + d_rw:], g.astype(f32).reshape(1, D), w_router)


def _final_norm_kernel(x_ref, g_ref, o_ref):
    x = x_ref[...]
    o_ref[...] = x * lax.rsqrt(jnp.mean(x * x, axis=-1, keepdims=True) + NORM_EPS) * g_ref[...]


def final_rmsnorm(x, g):
    Bn, L, D = x.shape
    tm = 1024
    out = pl.pallas_call(
        _final_norm_kernel,
        grid=(Bn * L // tm,),
        in_specs=[pl.BlockSpec((tm, D), lambda i: (i, 0)), pl.BlockSpec((1, D), lambda i: (0, 0))],
        out_specs=pl.BlockSpec((tm, D), lambda i: (i, 0)),
        out_shape=jax.ShapeDtypeStruct((Bn * L, D), x.dtype),
        name="final_rmsnorm",
    )(x.reshape(Bn * L, D), g.reshape(1, D))
    return out.reshape(Bn, L, D)


def hyena_positional_features(L):
    t = jnp.linspace(0.0, 1.0, L, dtype=f32)[:, None]
    w = (2.0 * math.pi / L) * jnp.arange(L, dtype=f32)[:, None]
    f = jnp.linspace(1e-4, HY_BANDS - 1, HY_BANDS, dtype=f32)[None, :]
    z = jnp.concatenate([t, jnp.cos(f * w), -jnp.sin(f * w)], axis=-1)
    return z, t


def hyena_filters(z, t, w1, b1, w2, b2, w3, b3, wout, freq):
    fr = freq.astype(f32)
    act = lambda u: jnp.sin(fr * u)
    h = act(z @ w1.astype(f32) + b1.astype(f32))
    h = act(h @ w2.astype(f32) + b2.astype(f32))
    h = act(h @ w3.astype(f32) + b3.astype(f32))
    h = (h @ wout.astype(f32)).reshape(-1, 2, HY_D)
    deltas = jnp.abs(jnp.linspace(HY_MIN_DECAY, HY_MAX_DECAY, HY_D, dtype=f32))
    h = h * jnp.exp(-t[:, :, None] * deltas)
    return h * lax.rsqrt(jnp.sum(h * h, axis=(0, 1), keepdims=True) + 1e-6)


def _split_bf16(x):
    hi = x.astype(bf16)
    return hi, (x - hi.astype(f32)).astype(bf16)


def _dot3(m_hi, m_lo, x_hi, x_lo):
    d = lambda a, b: lax.dot_general(a, b, _DN['nn'], preferred_element_type=f32)
    return d(m_hi, x_hi) + (d(m_hi, x_lo) + d(m_lo, x_hi))


def _cmatmul(mr, mi, xr, xi):
    xrh, xrl = _split_bf16(xr)
    rr = _dot3(mr[0], mr[1], xrh, xrl)
    ir = _dot3(mi[0], mi[1], xrh, xrl)
    if xi is None:
        return rr, ir
    xih, xil = _split_bf16(xi)
    ii = _dot3(mi[0], mi[1], xih, xil)
    ri = _dot3(mr[0], mr[1], xih, xil)
    return rr - ii, ri + ir


def _row_dft_kernel(mrh_ref, mrl_ref, mih_ref, mil_ref, *refs, real_input):
    if real_input:
        ur_ref, or_ref, oi_ref = refs
        ui = None
    else:
        ur_ref, ui_ref, or_ref, oi_ref = refs
        ui = ui_ref[0]
    o_r, o_i = _cmatmul((mrh_ref[...], mrl_ref[...]), (mih_ref[...], mil_ref[...]), ur_ref[0], ui)
    or_ref[0] = o_r
    oi_ref[0] = o_i


def row_dft(tabs, u, packed):
    rows, r_in, W = u.shape
    P = rows // 2 if packed else rows
    r_out = tabs[0].shape[0]
    wb = min(FFT_LANE_BLOCK, W)
    tab_spec = pl.BlockSpec((r_out, r_in), lambda p, j: (0, 0))
    re_spec = pl.BlockSpec((1, r_in, wb), lambda p, j: (p, 0, j))
    im_spec = pl.BlockSpec((1, r_in, wb), lambda p, j: (p + P, 0, j))
    out_spec = pl.BlockSpec((1, r_out, wb), lambda p, j: (p, 0, j))
    ins = [u, u] if packed else [u]
    return pl.pallas_call(
        functools.partial(_row_dft_kernel, real_input=not packed),
        grid=(P, W // wb),
        in_specs=[tab_spec] * 4 + ([re_spec, im_spec] if packed else [re_spec]),
        out_specs=[out_spec, out_spec],
        out_shape=[jax.ShapeDtypeStruct((P, r_out, W), f32)] * 2,
        compiler_params=pltpu.CompilerParams(dimension_semantics=("parallel", "parallel"),
                                             vmem_limit_bytes=VMEM_LIMIT_BYTES),
        name="hyena_row_dft",
    )(*tabs, *ins)


def _row_idft_gate_kernel(mrh_ref, mrl_ref, mih_ref, mil_ref, dr_ref, di_ref, z0_ref, z1_ref, x0_ref, x1_ref,
                          skip_ref, o0_ref, o1_ref):
    y_r, y_i = _cmatmul((mrh_ref[...], mrl_ref[...]), (mih_ref[...], mil_ref[...]), dr_ref[0], di_ref[0])
    skip = skip_ref[...]
    o0_ref[0] = (x0_ref[0] * (y_r + z0_ref[0] * skip)).astype(o0_ref.dtype)
    o1_ref[0] = (x1_ref[0] * (y_i + z1_ref[0] * skip)).astype(o1_ref.dtype)


def row_idft_gate(tabs, dr, di, z, x0, skip_row):
    P, r_in, W = dr.shape
    r_out = tabs[0].shape[0]
    wb = min(FFT_LANE_BLOCK, W)
    tab_spec = pl.BlockSpec((r_out, r_in), lambda p, j: (0, 0))
    d_spec = pl.BlockSpec((1, r_in, wb), lambda p, j: (p, 0, j))
    lo = pl.BlockSpec((1, r_out, wb), lambda p, j: (p, 0, j))
    hi = pl.BlockSpec((1, r_out, wb), lambda p, j: (p + P, 0, j))
    o0, o1 = pl.pallas_call(
        _row_idft_gate_kernel,
        grid=(P, W // wb),
        in_specs=[tab_spec] * 4 + [d_spec, d_spec, lo, hi, lo, hi, pl.BlockSpec((1, wb), lambda p, j: (0, j))],
        out_specs=[lo, lo],
        out_shape=[jax.ShapeDtypeStruct((P, r_out, W), bf16)] * 2,
        compiler_params=pltpu.CompilerParams(dimension_semantics=("parallel", "parallel"),
                                             vmem_limit_bytes=VMEM_LIMIT_BYTES),
        name="hyena_row_idft_gate",
    )(*tabs, dr, di, z, z, x0, x0, skip_row)
    return jnp.concatenate([o0, o1], axis=0)


def _col_dft_kernel(grh_ref, grl_ref, gih_ref, gil_ref, ar_ref, ai_ref, br_ref, bi_ref, *, k1_per_step):
    for j in range(k1_per_step):
        b_r, b_i = _cmatmul((grh_ref[j], grl_ref[j]), (gih_ref[j], gil_ref[j]), ar_ref[0, j], ai_ref[0, j])
        br_ref[0, j] = b_r
        bi_ref[0, j] = b_i


def _col_conv_kernel(grh_ref, grl_ref, gih_ref, gil_ref, trh_ref, trl_ref, tih_ref, til_ref,
                     ar_ref, ai_ref, kr_ref, ki_ref, dr_ref, di_ref, *, k1_per_step):
    for j in range(k1_per_step):
        b_r, b_i = _cmatmul((grh_ref[j], grl_ref[j]), (gih_ref[j], gil_ref[j]), ar_ref[0, j], ai_ref[0, j])
        k_r = kr_ref[0, j]
        k_i = ki_ref[0, j]
        c_r = b_r * k_r - b_i * k_i
        c_i = b_r * k_i + b_i * k_r
        d_r, d_i = _cmatmul((trh_ref[j], trl_ref[j]), (tih_ref[j], til_ref[j]), c_r, c_i)
        dr_ref[0, j] = d_r
        di_ref[0, j] = d_i


def col_stage(g_tabs, gt_tabs, ar, ai, kr=None, ki=None):
    P, n1, n2, C = ar.shape
    kb = min(FFT_K1_PER_STEP, n1)
    g_spec = pl.BlockSpec((kb, n2, n2), lambda p, j: (j, 0, 0))
    a_spec = pl.BlockSpec((1, kb, n2, C), lambda p, j: (p, j, 0, 0))
    k_spec = pl.BlockSpec((1, kb, n2, C), lambda p, j: (0, j, 0, 0))
    common = dict(
        grid=(P, n1 // kb),
        out_specs=[a_spec, a_spec],
        out_shape=[jax.ShapeDtypeStruct((P, n1, n2, C), f32)] * 2,
        compiler_params=pltpu.CompilerParams(dimension_semantics=("parallel", "parallel"),
                                             vmem_limit_bytes=VMEM_LIMIT_BYTES),
    )
    if kr is None:
        return pl.pallas_call(functools.partial(_col_dft_kernel, k1_per_step=kb),
                              in_specs=[g_spec] * 4 + [a_spec] * 2, name="hyena_col_dft", **common)(*g_tabs, ar, ai)
    return pl.pallas_call(functools.partial(_col_conv_kernel, k1_per_step=kb),
                          in_specs=[g_spec] * 8 + [a_spec] * 2 + [k_spec] * 2, name="hyena_col_conv", **common)(
        *g_tabs, *gt_tabs, ar, ai, kr, ki)


def _bf16_tables(m):
    out = []
    for part in (np.real(m), np.imag(m)):
        x = jnp.asarray(part, f32)
        hi = x.astype(bf16)
        out += [hi, (x - hi.astype(f32)).astype(bf16)]
    return out


def hyena_dft_tables(L):
    n2 = FFT_N2
    n1 = 2 * L // n2
    N = n1 * n2
    a = np.arange(n1)
    fa = np.exp(-2j * np.pi * np.outer(a, a) / n1)
    k1 = np.arange(n1)[:, None, None]
    k2 = np.arange(n2)[None, :, None]
    nn = np.arange(n2)[None, None, :]
    g = np.exp(-2j * np.pi * nn * (k1 + n1 * k2) / N)
    gt = np.conj(np.transpose(g, (0, 2, 1)))
    fc = np.conj(fa).T / N
    return dict(fa=_bf16_tables(fa), fa_half=_bf16_tables(fa[:, :n1 // 2]), g=_bf16_tables(g), gt=_bf16_tables(gt),
                fc_half=_bf16_tables(fc[:n1 // 2]))


def hyena_long_conv_gate(tabs, z, x0, k2, skip):
    Bn, L, C = z.shape
    n2 = FFT_N2
    n1 = 2 * L // n2
    P = Bn // 2
    W = n2 * C
    ar, ai = row_dft(tabs['fa'], k2.reshape(1, n1, W), packed=False)
    kr, ki = col_stage(tabs['g'], tabs['gt'], ar.reshape(1, n1, n2, C), ai.reshape(1, n1, n2, C))
    zv = z.reshape(Bn, n1 // 2, W)
    ar, ai = row_dft(tabs['fa_half'], zv, packed=True)
    dr, di = col_stage(tabs['g'], tabs['gt'], ar.reshape(P, n1, n2, C), ai.reshape(P, n1, n2, C), kr, ki)
    out = row_idft_gate(tabs['fc_half'], dr.reshape(P, n1, W), di.reshape(P, n1, W), zv, x0.reshape(Bn, n1 // 2, W),
                        jnp.tile(skip.astype(f32), n2).reshape(1, W))
    return out.reshape(Bn, L, C)


def _hyena_prep_kernel(*refs, rows, nt):
    cur, prev, nxt = refs[0:3], refs[3:6], refs[6:9]
    w_ref, b_ref, z_ref, x0_ref = refs[9:]
    t = pl.program_id(1)
    has_prev = (t > 0).astype(f32)
    has_next = (t < nt - 1).astype(f32)
    ridx = lax.broadcasted_iota(jnp.int32, (rows, HY_D), 0)

    def conv(j):
        u = cur[j][0]
        prev_row = prev[j][0, SUBLANES - 1:SUBLANES, :] * has_prev
        next_row = nxt[j][0, 0:1, :] * has_next
        up = jnp.where(ridx == 0, prev_row, pltpu.roll(u, 1, axis=0))
        un = jnp.where(ridx == rows - 1, next_row, pltpu.roll(u, rows - 1, axis=0))
        sl = slice(j * HY_D, (j + 1) * HY_D)
        return up * w_ref[0:1, sl] + u * w_ref[1:2, sl] + un * w_ref[2:3, sl] + b_ref[:, sl]

    x0_ref[0] = conv(0)
    z_ref[0] = conv(2) * conv(1)


def hyena_prep(proj, lane0, conv_w, conv_b):
    Bn, L, _ = proj.shape
    rows = RW_ROWS_PER_STEP
    nt = L // rows
    hb = rows // SUBLANES
    b0 = lane0 // HY_D
    assert b0 * HY_D == lane0
    cur = lambda j: pl.BlockSpec((1, rows, HY_D), lambda b, t: (b, t, b0 + j))
    prev = lambda j: pl.BlockSpec((1, SUBLANES, HY_D), lambda b, t: (b, jnp.maximum(t * hb - 1, 0), b0 + j))
    nxt = lambda j: pl.BlockSpec((1, SUBLANES, HY_D), lambda b, t: (b, jnp.minimum((t + 1) * hb, L // SUBLANES - 1), b0 + j))
    full = lambda shp: pl.BlockSpec(shp, lambda b, t: (0,) * len(shp))
    out = pl.BlockSpec((1, rows, HY_D), lambda b, t: (b, t, 0))
    return pl.pallas_call(
        functools.partial(_hyena_prep_kernel, rows=rows, nt=nt),
        grid=(Bn, nt),
        in_specs=[cur(0), cur(1), cur(2), prev(0), prev(1), prev(2), nxt(0), nxt(1), nxt(2),
                  full((3, 3 * HY_D)), full((1, 3 * HY_D))],
        out_specs=[out, out],
        out_shape=[jax.ShapeDtypeStruct((Bn, L, HY_D), f32)] * 2,
        compiler_params=pltpu.CompilerParams(dimension_semantics=("parallel", "parallel")),
        name="hyena_prep",
    )(*([proj] * 9), conv_w.astype(f32), conv_b.astype(f32).reshape(1, 3 * HY_D))


def hyena_mixer(tabs, proj, lane0, conv_w, conv_b, filt, skip):
    z, x0 = hyena_prep(proj, lane0, conv_w, conv_b)
    k2 = jnp.concatenate([filt[:1, 0] + filt[:1, 1], filt[1:, 0],
                          jnp.zeros((1, HY_D), f32), filt[1:, 1][::-1]], axis=0)
    return hyena_long_conv_gate(tabs, z, x0, k2, skip)


def rwkv7_mixer(proj, lora_lane0, mu, w0, w2, a0, a2, k_k, k_a, r_k, g2, ln_w, ln_b):
    r, k, v, kk, lw2, a_both = rw_prep(proj, lora_lane0, mu, w0, w2, a0, a2, k_k)
    y2 = wkv7_chunked(r, k, v, kk, lw2, a_both, k_a.astype(f32).reshape(1, RW_D))
    return rw_post(y2, r, k, v, a_both, proj, lora_lane0, k_a, r_k, ln_w, ln_b, g2)


def hier_moe(h, logits, bg, be, w1, w3, w2, layer):
    N, Dm = h.shape
    assert MOE_TOPK == 2
    g_logits = logits[:, :MOE_GROUPS] + bg.astype(f32)
    g_sel = jnp.argmax(g_logits, axis=-1)
    g_prob = jnp.take_along_axis(jax.nn.softmax(g_logits, axis=-1), g_sel[:, None], axis=-1)
    e_logits = (logits[:, MOE_GROUPS:MOE_GROUPS + MOE_EXPERTS] + be.astype(f32)).reshape(N, MOE_GROUPS, MOE_PER_GROUP)
    e_logits = jnp.take_along_axis(e_logits, g_sel[:, None, None], axis=1)[:, 0]
    top_val, top_idx = lax.top_k(e_logits, MOE_TOPK)
    gate = g_prob * jax.nn.softmax(top_val, axis=-1)
    expert = g_sel[:, None] * MOE_PER_GROUP + top_idx
    M = N * MOE_TOPK
    flat_e = expert.reshape(M).astype(jnp.int32)
    experts = jnp.arange(MOE_EXPERTS, dtype=jnp.int32)
    counts = jnp.sum((flat_e[:, None] == experts[None, :]).astype(jnp.int32), axis=0)
    padded = (counts + MOE_BLOCK - 1) // MOE_BLOCK * MOE_BLOCK
    pad_end = jnp.cumsum(padded)
    n_blocks = -(-M // MOE_BLOCK) + MOE_EXPERTS
    n_rows = n_blocks * MOE_BLOCK
    cum_need = jnp.cumsum(padded - counts)
    filler = jnp.arange(n_rows - M, dtype=jnp.int32)
    filler_e = jnp.sum((cum_need[None, :] <= filler[:, None]).astype(jnp.int32), axis=1)
    keys = jnp.concatenate([2 * flat_e, 2 * filler_e + 1])
    ids = jnp.concatenate([jnp.arange(M, dtype=jnp.int32), jnp.full((n_rows - M,), M, jnp.int32)])
    _, slot_src = lax.sort((keys, ids), num_keys=1)
    tok_src = jnp.where(slot_src < M, slot_src // MOE_TOPK, jnp.arange(n_rows, dtype=jnp.int32) % N)
    block_start = jnp.arange(n_blocks, dtype=pad_end.dtype) * MOE_BLOCK
    block_e = jnp.minimum(jnp.sum(pad_end[None, :] <= block_start[:, None], axis=1), MOE_EXPERTS - 1)
    y = moe_expert_ffn(h[tok_src], block_e, w1, w3, w2, layer)
    _, row_of = lax.sort((slot_src, jnp.arange(n_rows, dtype=jnp.int32)), num_keys=1)
    pos = row_of[:M].reshape(N, MOE_TOPK)
    return gate[:, 0:1] * y[pos[:, 0]].astype(f32) + gate[:, 1:2] * y[pos[:, 1]].astype(f32)


def kernel(x, norm1_g, w_in, hy_conv_w, hy_conv_b, hy_w1, hy_b1, hy_w2, hy_b2, hy_w3, hy_b3, hy_wout, hy_freq, hy_skip, rw_mu, rw_w0, rw_w2, rw_a0, rw_a2, rw_kk, rw_ka, rw_rk, rw_g2, rw_ln_w, rw_ln_b, na_rpb, w_out, norm2_g, moe_wg, moe_bg, moe_we, moe_be, moe_w1, moe_w3, moe_w2, norm_f_g):
    Bn, L, _ = x.shape
    z_pos, t_pos = hyena_positional_features(L)
    dft_tabs = hyena_dft_tables(L)
    splits = np.cumsum(IN_SIZES)[:-1].tolist()
    hy_end, rkv_end = splits[0], splits[1]
    hy_lane0 = rkv_end - hy_end
    lora_lane0, na_lane0 = splits[1], splits[4]
    assert RW_D % LANES == 0 and lora_lane0 % LANES == 0 and na_lane0 % LANES == 0
    N = Bn * L
    x2 = x.reshape(N, D_MODEL)
    for l in range(DEPTH):
        w_in_l = jnp.concatenate([w_in[l][:, hy_end:rkv_end], w_in[l][:, :hy_end], w_in[l][:, rkv_end:]], axis=1)
        proj = norm_proj(x2, norm1_g[l], w_in_l).reshape(Bn, L, IN_D)
        filt = hyena_filters(z_pos, t_pos, hy_w1[l], hy_b1[l], hy_w2[l], hy_b2[l],
                             hy_w3[l], hy_b3[l], hy_wout[l], hy_freq[l])
        y_hy = hyena_mixer(dft_tabs, proj, hy_lane0, hy_conv_w[l], hy_conv_b[l], filt, hy_skip[l])
        y_rw = rwkv7_mixer(proj, lora_lane0, rw_mu[l], rw_w0[l], rw_w2[l], rw_a0[l],
                           rw_a2[l], rw_kk[l], rw_ka[l], rw_rk[l].reshape(RW_D), rw_g2[l], rw_ln_w[l], rw_ln_b[l])
        y_na = neighborhood_attention_pallas(proj, na_rpb[l], na_lane0 // LANES)
        x2, h2, logits = out_proj_norm_router(
            x2, y_hy.reshape(N, HY_D), y_rw.reshape(N, RW_D), y_na.reshape(N, NA_D), w_out[l], norm2_g[l],
            jnp.concatenate([moe_wg[l], moe_we[l]], axis=1))
        x2 = x2 + hier_moe(h2, logits, moe_bg[l], moe_be[l], moe_w1, moe_w3, moe_w2, l)
    return final_rmsnorm(x2.reshape(Bn, L, D_MODEL), norm_f_g)
```

```python
import functools
import math

import jax
import jax.numpy as jnp
import numpy as np
from jax import lax
from jax.experimental import pallas as pl
from jax.experimental.pallas import tpu as pltpu

f32 = jnp.float32
bf16 = jnp.bfloat16

D_MODEL = 1024
DEPTH = 2
GRID_W = 64
NORM_EPS = 1e-6
NEG_INF = -1e30

HY_D = D_MODEL // 4
HY_EMB = 33
HY_BANDS = (HY_EMB - 1) // 2
HY_FFN = 64
HY_MIN_DECAY = math.log(1e-2) / 1.5
HY_MAX_DECAY = math.log(1e-2) / 0.3

RW_N = 64
RW_D = D_MODEL // 2
RW_H = RW_D // RW_N
RW_W_LORA = 64
RW_A_LORA = 64
RW_G_LORA = 128
RW_GN_EPS = 64e-5

NA_HD = 64
NA_D = D_MODEL // 4
NA_H = NA_D // NA_HD
NA_KR = 8
NA_KC = 16

MIX_D = HY_D + RW_D + NA_D
IN_SIZES = (3 * HY_D, 3 * RW_D, RW_G_LORA, 2 * RW_W_LORA, 2 * RW_A_LORA, 3 * NA_D)
IN_D = sum(IN_SIZES)

MOE_GROUPS = 4
MOE_PER_GROUP = 8
MOE_EXPERTS = MOE_GROUPS * MOE_PER_GROUP
MOE_TOPK = 2
MOE_FF = 512
MOE_BLOCK = 512

LANES = 128
WKV_CHUNK = 64
WKV_TIME_BLOCK = 512
NA_ROWS_PER_STEP = 4
RW_ROWS_PER_STEP = 512
SUBLANES = 8
PROJ_ROWS_PER_STEP = 512
FFT_N2 = 128
FFT_LANE_BLOCK = 4096
FFT_K1_PER_STEP = 4
VMEM_LIMIT_BYTES = 48 * 1024 * 1024

_DN = {'nn': (((1,), (0,)), ((), ())), 'nt': (((1,), (1,)), ((), ())), 'tn': (((0,), (0,)), ((), ()))}


def _mm(a, b, dims='nn'):
    return lax.dot_general(a.astype(bf16), b.astype(bf16), _DN[dims], preferred_element_type=f32)


def _wkv_kernel(r_ref, k_ref, v_ref, kk_ref, lw_ref, a_ref, ka_ref, y_ref, ht_ref, *, tb, batch):
    T = WKV_CHUNK
    H2 = 2 * T
    nc = tb // T
    npairs = r_ref.shape[-1] // LANES
    mm = _mm

    @pl.when(pl.program_id(1) == 0)
    def _():
        ht_ref[...] = jnp.zeros_like(ht_ref)

    d = pl.program_id(0) // batch
    sign = 1 - 2 * d
    trow = lax.broadcasted_iota(jnp.int32, (T, H2), 0)
    lane = lax.broadcasted_iota(jnp.int32, (T, H2), 1)
    scol = lane % T
    tdiff = (trow - scol) * sign
    strict = tdiff > 0
    incl = tdiff >= 0
    same16 = (trow // 16) == (scol // 16)
    same32 = (trow // 32) == (scol // 32)
    off32 = same32 & jnp.logical_not(same16)
    off64 = jnp.logical_not(same32)
    eye2 = jnp.where(trow == scol, 1.0, 0.0).astype(f32)
    lo_lane = lane < T
    brow = lax.broadcasted_iota(jnp.int32, (H2, H2), 0)
    bcol = lax.broadcasted_iota(jnp.int32, (H2, H2), 1)
    same_head = (brow // T) == (bcol // T)
    crow = lax.broadcasted_iota(jnp.int32, (T, T), 0)
    ccol = lax.broadcasted_iota(jnp.int32, (T, T), 1)
    tri = jnp.where((crow - ccol) * sign >= 0, 1.0, 0.0).astype(bf16)
    is_bwd = d == 1

    def bd(x):
        return jnp.concatenate([jnp.where(lo_lane, x, 0.0), jnp.where(lo_lane, 0.0, x)], axis=0)

    inst = [(s, p) for s in range(nc) for p in range(npairs)]
    offs = [pl.multiple_of((s + d * (nc - 1 - 2 * s)) * T, T) for s in range(nc)]

    def load(ref, s, p):
        return ref[0, pl.ds(offs[s], T), p * LANES:(p + 1) * LANES]

    cs_l = []
    for s, p in inst:
        lw = load(lw_ref, s, p)
        l1 = lw.astype(bf16)
        l2 = (lw - l1.astype(f32)).astype(bf16)
        dd = lambda x: lax.dot_general(tri, x, _DN['nn'], preferred_element_type=f32)
        cs_l.append((dd(l1) + dd(l2), lw))
    ops = []
    for (s, p), (cs, lw) in zip(inst, cs_l):
        r = load(r_ref, s, p)
        k = load(k_ref, s, p)
        v = load(v_ref, s, p)
        kk = load(kk_ref, s, p)
        a = load(a_ref, s, p)
        ka = ka_ref[:, p * LANES:(p + 1) * LANES]
        kd = k * (1.0 + (a - 1.0) * ka)
        b = kk * a
        cs_end = jnp.where(is_bwd, cs[0:1, :], cs[T - 1:T, :])
        em = jnp.exp(-cs)
        e_end = jnp.exp(cs_end - cs)
        ops.append(dict(
            AR=jnp.concatenate([-kk * jnp.exp(cs - lw), r * jnp.exp(cs)], axis=0),
            BK=jnp.concatenate([bd(b * em), bd(kd * em)], axis=0),
            V=v, BKg=jnp.concatenate([b * e_end, kd * e_end], axis=0), g_end=jnp.exp(cs_end)))
    for o in ops:
        S = mm(o['AR'], o['BK'], 'nt')
        N = jnp.where(strict, S[:T, :H2], 0.0)
        o['AakArk'] = jnp.concatenate([jnp.where(strict, S[:T, H2:], 0.0), jnp.where(incl, S[T:, H2:], 0.0)], axis=0)
        o['Arb'] = jnp.where(incl, S[T:, :H2], 0.0)
        o['Nd'] = jnp.where(same16, N, 0.0)
        o['N32'] = jnp.where(off32, N, 0.0)
        o['N64'] = jnp.where(off64, N, 0.0)
    for o in ops:
        o['X'] = eye2 + o['Nd']
        o['P'] = mm(o['Nd'], bd(o['Nd']))
    for it in range(3):
        if it < 2:
            for o in ops:
                px = mm(o['P'], jnp.concatenate([bd(o['X']), bd(o['P'])], axis=1))
                o['X'] = o['X'] + px[:, :H2]
                o['P'] = px[:, H2:]
        else:
            for o in ops:
                o['X'] = o['X'] + mm(o['P'], bd(o['X']))
    for key in ('N32', 'N64'):
        for o in ops:
            o['Z'] = mm(o[key], bd(o['X']))
        for o in ops:
            o['X'] = o['X'] + mm(o['X'], bd(o['Z']))
    for o in ops:
        wy = mm(o['AakArk'], bd(o['V']))
        o['W0'] = wy[:T]
        o['Yv'] = wy[T:]
    for o in ops:
        o['XAW'] = mm(o['X'], jnp.concatenate([bd(o['AR'][:T]), bd(o['W0'])], axis=1))
    for o in ops:
        ax = mm(o['Arb'], jnp.concatenate([bd(o['XAW'][:, :H2]), bd(o['XAW'][:, H2:])], axis=1))
        o['Rhat'] = o['AR'][T:] + ax[:, :H2]
        o['Yc'] = ax[:, H2:] + o['Yv']
    for o in ops:
        lhs = jnp.concatenate([o['XAW'], jnp.concatenate([jnp.zeros((T, H2), f32), o['V']], axis=1)], axis=0)
        pq = mm(lhs, o['BKg'], 'tn')
        o['Pc'] = jnp.where(same_head, pq[:H2], 0.0)
        o['Qc'] = jnp.where(same_head, pq[H2:], 0.0)
    hts = [ht_ref[p] for p in range(npairs)]
    for s in range(nc):
        cur = [ops[s * npairs + p] for p in range(npairs)]
        ys = [mm(o['Rhat'], hts[p], 'nt') + o['Yc'] for p, o in enumerate(cur)]
        hts = [hts[p] * o['g_end'] + (mm(hts[p], o['Pc']) + o['Qc']) for p, o in enumerate(cur)]
        for p in range(npairs):
            y_ref[0, pl.ds(offs[s], T), p * LANES:(p + 1) * LANES] = ys[p]
    for p in range(npairs):
        ht_ref[p] = hts[p]


def wkv7_chunked(r, k, v, kk, lw2, a2, ka):
    Bn, L, C = r.shape
    tb = WKV_TIME_BLOCK
    nt = L // tb
    tmap = lambda i, t: t + (i // Bn) * (nt - 1 - 2 * t)
    shared = pl.BlockSpec((1, tb, C), lambda i, t: (i % Bn, tmap(i, t), 0))
    per_dir = pl.BlockSpec((1, tb, C), lambda i, t: (i % Bn, tmap(i, t), i // Bn))
    return pl.pallas_call(
        functools.partial(_wkv_kernel, tb=tb, batch=Bn),
        grid=(2 * Bn, nt),
        in_specs=[shared, shared, shared, shared, per_dir, per_dir, pl.BlockSpec((1, C), lambda i, t: (0, 0))],
        out_specs=pl.BlockSpec((1, tb, C), lambda i, t: (i, tmap(i, t), 0)),
        out_shape=jax.ShapeDtypeStruct((2 * Bn, L, C), f32),
        scratch_shapes=[pltpu.VMEM((C // LANES, LANES, LANES), f32)],
        compiler_params=pltpu.CompilerParams(dimension_semantics=("parallel", "arbitrary"),
                                             vmem_limit_bytes=VMEM_LIMIT_BYTES),
        name="wkv7_chunked",
    )(r, k, v, kk, lw2, a2, ka)


def _mm_exact_rhs(a, b_bf16):
    ah = a.astype(bf16)
    al = (a - ah.astype(f32)).astype(bf16)
    d = lambda x: lax.dot_general(x, b_bf16, _DN['nn'], preferred_element_type=f32)
    return d(ah) + d(al)


def _head_sum_matrix(scale):
    i = lax.broadcasted_iota(jnp.int32, (RW_D, RW_D), 0) // RW_N
    j = lax.broadcasted_iota(jnp.int32, (RW_D, RW_D), 1) // RW_N
    return jnp.where(i == j, scale, 0.0).astype(bf16)


def _rw_prep_kernel(r_ref, k_ref, v_ref, rp_ref, kp_ref, vp_ref, rn_ref, kn_ref, vn_ref, w_ref, a_ref,
                    mu_ref, w0_ref, w2_ref, a0_ref, a2_ref, kkw_ref,
                    ro_ref, ko_ref, vo_ref, kko_ref, lw_ref, ao_ref, *, rows, nt):
    t = pl.program_id(1)
    has_prev = (t > 0).astype(f32)
    has_next = (t < nt - 1).astype(f32)
    ridx = lax.broadcasted_iota(jnp.int32, (rows, RW_D), 0)

    def tshift(cur_ref, prev_ref, next_ref, j):
        u = cur_ref[0]
        prev_row = prev_ref[0, SUBLANES - 1:SUBLANES, :] * has_prev
        next_row = next_ref[0, 0:1, :] * has_next
        up = jnp.where(ridx == 0, prev_row, pltpu.roll(u, 1, axis=0))
        un = jnp.where(ridx == rows - 1, next_row, pltpu.roll(u, rows - 1, axis=0))
        return u + mu_ref[j, 0:1, :] * (up - u) + mu_ref[j, 1:2, :] * (un - u)

    r = tshift(r_ref, rp_ref, rn_ref, 0)
    k = tshift(k_ref, kp_ref, kn_ref, 1)
    v = tshift(v_ref, vp_ref, vn_ref, 2)
    ro_ref[0] = r
    ko_ref[0] = k
    vo_ref[0] = v
    kk = k * kkw_ref[...]
    ss = _mm_exact_rhs(kk * kk, _head_sum_matrix(1.0))
    kko_ref[0] = kk * lax.rsqrt(jnp.maximum(ss, 1e-24))
    wl = jnp.tanh(w_ref[0])
    al = a_ref[0]
    for d in range(2):
        wpre = w0_ref[d:d + 1, :] + _mm(wl[:, d * RW_W_LORA:(d + 1) * RW_W_LORA], w2_ref[d])
        lw_ref[0, :, d * RW_D:(d + 1) * RW_D] = -math.exp(-0.5) * jax.nn.sigmoid(wpre)
        av = a0_ref[d:d + 1, :] + _mm(al[:, d * RW_A_LORA:(d + 1) * RW_A_LORA], a2_ref[d])
        ao_ref[0, :, d * RW_D:(d + 1) * RW_D] = jax.nn.sigmoid(av)


def rw_prep(proj, lora_lane0, mu, w0, w2, a0, a2, k_k):
    Bn, L, _ = proj.shape
    C = RW_D
    rows = RW_ROWS_PER_STEP
    nt = L // rows
    hb = rows // SUBLANES
    lb = lora_lane0 // LANES
    cur = lambda j: pl.BlockSpec((1, rows, C), lambda b, t: (b, t, j))
    prev = lambda j: pl.BlockSpec((1, SUBLANES, C), lambda b, t: (b, jnp.maximum(t * hb - 1, 0), j))
    nxt = lambda j: pl.BlockSpec((1, SUBLANES, C), lambda b, t: (b, jnp.minimum((t + 1) * hb, L // SUBLANES - 1), j))
    lora_w = pl.BlockSpec((1, rows, LANES), lambda b, t: (b, t, lb + 1))
    lora_a = pl.BlockSpec((1, rows, LANES), lambda b, t: (b, t, lb + 2))
    full = lambda shp: pl.BlockSpec(shp, lambda b, t: (0,) * len(shp))
    out_c = pl.BlockSpec((1, rows, C), lambda b, t: (b, t, 0))
    out_2c = pl.BlockSpec((1, rows, 2 * C), lambda b, t: (b, t, 0))
    sds = lambda c: jax.ShapeDtypeStruct((Bn, L, c), f32)
    return pl.pallas_call(
        functools.partial(_rw_prep_kernel, rows=rows, nt=nt),
        grid=(Bn, nt),
        in_specs=[cur(0), cur(1), cur(2), prev(0), prev(1), prev(2), nxt(0), nxt(1), nxt(2), lora_w, lora_a,
                  full((3, 2, C)), full((2, C)), full((2, RW_W_LORA, C)), full((2, C)), full((2, RW_A_LORA, C)),
                  full((1, C))],
        out_specs=[out_c, out_c, out_c, out_c, out_2c, out_2c],
        out_shape=[sds(C), sds(C), sds(C), sds(C), sds(2 * C), sds(2 * C)],
        compiler_params=pltpu.CompilerParams(dimension_semantics=("parallel", "parallel"),
                                             vmem_limit_bytes=VMEM_LIMIT_BYTES),
        name="rwkv_prep",
    )(proj, proj, proj, proj, proj, proj, proj, proj, proj, proj, proj,
      mu.astype(f32), w0.astype(f32), w2.astype(bf16), a0.astype(f32), a2.astype(bf16), k_k.astype(f32).reshape(1, C))


def _rw_post_kernel(yf_ref, yb_ref, r_ref, k_ref, v_ref, a_ref, g_ref, ka_ref, rk_ref, lnw_ref, lnb_ref, g2_ref, o_ref):
    C = RW_D
    y = yf_ref[0] + yb_ref[0]
    avg = _head_sum_matrix(1.0 / RW_N)
    mean = _mm_exact_rhs(y, avg)
    yc = y - mean
    var = _mm_exact_rhs(yc * yc, avg)
    yn = yc * lax.rsqrt(var + RW_GN_EPS) * lnw_ref[...] + lnb_ref[...]
    a = a_ref[0]
    k = k_ref[0]
    ka = ka_ref[...]
    ksum = k * (1.0 + (a[:, :C] - 1.0) * ka) + k * (1.0 + (a[:, C:] - 1.0) * ka)
    coef = _mm_exact_rhs(r_ref[0] * ksum * rk_ref[...], _head_sum_matrix(1.0))
    gate = _mm(jax.nn.sigmoid(g_ref[0]), g2_ref[...])
    o_ref[0] = ((yn + coef * v_ref[0]) * gate).astype(o_ref.dtype)


def rw_post(y2, r, k, v, a2, proj, g_lane0, k_a, r_k, ln_w, ln_b, g2):
    Bn, L, C = r.shape
    rows = RW_ROWS_PER_STEP
    nt = L // rows
    gb = g_lane0 // LANES
    blk = lambda c: pl.BlockSpec((1, rows, c), lambda b, t: (b, t, 0))
    full = lambda shp: pl.BlockSpec(shp, lambda b, t: (0,) * len(shp))
    row = lambda x: x.astype(f32).reshape(1, C)
    return pl.pallas_call(
        _rw_post_kernel,
        grid=(Bn, nt),
        in_specs=[blk(C), pl.BlockSpec((1, rows, C), lambda b, t: (b + Bn, t, 0)), blk(C), blk(C), blk(C), blk(2 * C),
                  pl.BlockSpec((1, rows, LANES), lambda b, t: (b, t, gb)),
                  full((1, C)), full((1, C)), full((1, C)), full((1, C)), full((RW_G_LORA, C))],
        out_specs=blk(C),
        out_shape=jax.ShapeDtypeStruct((Bn, L, C), bf16),
        compiler_params=pltpu.CompilerParams(dimension_semantics=("parallel", "parallel"),
                                             vmem_limit_bytes=VMEM_LIMIT_BYTES),
        name="rwkv_post",
    )(y2, y2, r, k, v, a2, proj, row(k_a), row(r_k), row(ln_w), row(ln_b), g2.astype(bf16))


def _na_kernel(q_ref, k_ref, v_ref, bias_ref, o_ref, kb_ref, vb_ref, *, rows_per_step, n_rows, kr):
    W = GRID_W
    rb = pl.program_id(2)

    @pl.when(rb == 0)
    def _():
        kb_ref[...] = k_ref[0].astype(bf16)
        vb_ref[...] = v_ref[0].astype(bf16)

    lo_lane = lax.broadcasted_iota(jnp.int32, (W, LANES), 1) < NA_HD
    scale = NA_HD ** -0.5
    rows = []
    for j in range(rows_per_step):
        r = rb * rows_per_step + j
        start = jnp.clip(r - kr // 2, 0, n_rows - kr)
        rows.append((start, start - r + (NA_KR - 1)))
    s_list = []
    for j, (start, didx) in enumerate(rows):
        q = q_ref[0, j * W:(j + 1) * W, :] * scale
        qs = jnp.concatenate([jnp.where(lo_lane, q, 0.0), jnp.where(lo_lane, 0.0, q)], axis=0)
        kw = kb_ref[pl.ds(pl.multiple_of(start * W, W), kr * W), :]
        s_list.append(_mm(qs, kw, 'nt') + bias_ref[didx, 0])
    p_list = []
    for s in s_list:
        m = jnp.max(s, axis=-1, keepdims=True)
        p = jnp.exp(s - m)
        p_list.append((p, jnp.sum(p, axis=-1, keepdims=True)))
    for j, ((start, _), (p, l)) in enumerate(zip(rows, p_list)):
        vw = vb_ref[pl.ds(pl.multiple_of(start * W, W), kr * W), :]
        o = _mm(p, vw) / l
        o_ref[0, j * W:(j + 1) * W, :] = jnp.where(lo_lane, o[:W], o[W:]).astype(o_ref.dtype)


def na_bias_table(rpb, n_rows):
    W = GRID_W
    kr = min(NA_KR, n_rows)
    cols = jnp.arange(W)
    col_start = jnp.clip(cols - NA_KC // 2, 0, W - NA_KC)
    in_band = (cols[None, :] >= col_start[:, None]) & (cols[None, :] < col_start[:, None] + NA_KC)
    dc = jnp.clip(cols[None, :] - cols[:, None], -(NA_KC - 1), NA_KC - 1) + (NA_KC - 1)
    win = jnp.stack([rpb.astype(f32)[:, d:d + kr, :] for d in range(8)], axis=1)
    onehot = (dc[:, :, None] == jnp.arange(2 * NA_KC - 1)[None, None, :]).astype(f32)
    tab = jnp.einsum('hdic,qkc->hdqik', win, onehot, precision=lax.Precision.HIGHEST)
    tab = jnp.where(in_band[None, None, :, None, :], tab, NEG_INF)
    H = rpb.shape[0]
    tab = tab.reshape(H // 2, 2, 8, W, kr * W).transpose(2, 0, 1, 3, 4)
    return tab.reshape(8, H // 2, 2 * W, kr * W)


def neighborhood_attention_pallas(proj, rpb, lane_block0):
    Bn, L, _ = proj.shape
    W = GRID_W
    n_rows = L // W
    kr = min(NA_KR, n_rows)
    hp = NA_H * NA_HD // LANES
    rps = NA_ROWS_PER_STEP
    bias = na_bias_table(rpb, n_rows)
    kern = functools.partial(_na_kernel, rows_per_step=rps, n_rows=n_rows, kr=kr)
    return pl.pallas_call(
        kern,
        grid=(Bn, hp, n_rows // rps),
        in_specs=[pl.BlockSpec((1, rps * W, LANES), lambda b, h, r: (b, r, lane_block0 + h)),
                  pl.BlockSpec((1, L, LANES), lambda b, h, r: (b, 0, lane_block0 + hp + h)),
                  pl.BlockSpec((1, L, LANES), lambda b, h, r: (b, 0, lane_block0 + 2 * hp + h)),
                  pl.BlockSpec((8, 1, 2 * W, kr * W), lambda b, h, r: (0, h, 0, 0))],
        out_specs=pl.BlockSpec((1, rps * W, LANES), lambda b, h, r: (b, r, h)),
        out_shape=jax.ShapeDtypeStruct((Bn, L, NA_H * NA_HD), bf16),
        scratch_shapes=[pltpu.VMEM((L, LANES), bf16), pltpu.VMEM((L, LANES), bf16)],
        compiler_params=pltpu.CompilerParams(dimension_semantics=("parallel", "parallel", "arbitrary"),
                                             vmem_limit_bytes=VMEM_LIMIT_BYTES),
        name="neighborhood_attention",
    )(proj, proj, proj, bias)


def _moe_ffn_kernel(be_ref, x_ref, w1_ref, w3_ref, w2_ref, o_ref, w1b_ref, w3b_ref, w2b_ref):
    i = pl.program_id(0)

    @pl.when((i == 0) | (be_ref[i] != be_ref[jnp.maximum(i - 1, 0)]))
    def _():
        w1b_ref[...] = w1_ref[0, 0].astype(bf16)
        w3b_ref[...] = w3_ref[0, 0].astype(bf16)
        w2b_ref[...] = w2_ref[0, 0].astype(bf16)

    x = x_ref[...].astype(bf16)
    h1 = jnp.dot(x, w1b_ref[...], preferred_element_type=f32)
    h3 = jnp.dot(x, w3b_ref[...], preferred_element_type=f32)
    g = h1 * jax.nn.sigmoid(h1) * h3
    o_ref[...] = jnp.dot(g.astype(bf16), w2b_ref[...], preferred_element_type=f32).astype(o_ref.dtype)


def moe_expert_ffn(xs, block_e, w1, w3, w2, layer):
    rows, Dm = xs.shape
    n_blocks = rows // MOE_BLOCK
    grid_spec = pltpu.PrefetchScalarGridSpec(
        num_scalar_prefetch=1,
        grid=(n_blocks,),
        in_specs=[pl.BlockSpec((MOE_BLOCK, Dm), lambda i, be: (i, 0)),
                  pl.BlockSpec((1, 1, Dm, MOE_FF), lambda i, be: (layer, be[i], 0, 0)),
                  pl.BlockSpec((1, 1, Dm, MOE_FF), lambda i, be: (layer, be[i], 0, 0)),
                  pl.BlockSpec((1, 1, MOE_FF, Dm), lambda i, be: (layer, be[i], 0, 0))],
        out_specs=pl.BlockSpec((MOE_BLOCK, Dm), lambda i, be: (i, 0)),
        scratch_shapes=[pltpu.VMEM((Dm, MOE_FF), bf16), pltpu.VMEM((Dm, MOE_FF), bf16), pltpu.VMEM((MOE_FF, Dm), bf16)],
    )
    return pl.pallas_call(
        _moe_ffn_kernel,
        grid_spec=grid_spec,
        out_shape=jax.ShapeDtypeStruct((rows, Dm), bf16),
        compiler_params=pltpu.CompilerParams(dimension_semantics=("arbitrary",), vmem_limit_bytes=VMEM_LIMIT_BYTES),
        name="moe_expert_ffn",
    )(block_e.astype(jnp.int32), xs, w1.astype(f32), w3.astype(f32), w2.astype(f32))


def _rms(x):
    return x * lax.rsqrt(jnp.mean(x * x, axis=-1, keepdims=True) + NORM_EPS)


def _moe_combine(x_ref, y0_ref, y1_ref, g0_ref, g1_ref):
    return x_ref[...] + (g0_ref[...] * y0_ref[...].astype(f32) + g1_ref[...] * y1_ref[...].astype(f32))


def _moe_specs(tm, D):
    row = pl.BlockSpec((tm, D), lambda i: (i, 0))
    col = pl.BlockSpec((tm, 1), lambda i: (i, 0))
    return [row, row, col, col]


def _norm_proj_kernel(*refs, combine):
    if combine:
        x_ref, y0_ref, y1_ref, g0_ref, g1_ref, g_ref, w_ref, o_ref, xo_ref = refs
        x = _moe_combine(x_ref, y0_ref, y1_ref, g0_ref, g1_ref)
        xo_ref[...] = x
    else:
        x_ref, g_ref, w_ref, o_ref = refs
        x = x_ref[...]
    h = (_rms(x) * g_ref[...]).astype(bf16)
    o_ref[...] = jnp.dot(h, w_ref[...], preferred_element_type=f32)


def norm_proj(x2, g, w, moe=None):
    N, D = x2.shape
    F = w.shape[1]
    tm = PROJ_ROWS_PER_STEP
    row = pl.BlockSpec((tm, D), lambda i: (i, 0))
    out_row = pl.BlockSpec((tm, F), lambda i: (i, 0))
    par = [pl.BlockSpec((1, D), lambda i: (0, 0)), pl.BlockSpec((D, F), lambda i: (0, 0))]
    params = dict(compiler_params=pltpu.CompilerParams(dimension_semantics=("parallel",),
                                                       vmem_limit_bytes=VMEM_LIMIT_BYTES), name="norm_proj")
    gw = (g.astype(f32).reshape(1, D), w.astype(bf16))
    if moe is None:
        proj = pl.pallas_call(functools.partial(_norm_proj_kernel, combine=False), grid=(N // tm,),
                              in_specs=[row] + par, out_specs=out_row,
                              out_shape=jax.ShapeDtypeStruct((N, F), f32), **params)(x2, *gw)
        return x2, proj
    proj, x_new = pl.pallas_call(functools.partial(_norm_proj_kernel, combine=True), grid=(N // tm,),
                                 in_specs=[row] + _moe_specs(tm, D) + par, out_specs=[out_row, row],
                                 out_shape=[jax.ShapeDtypeStruct((N, F), f32), jax.ShapeDtypeStruct((N, D), f32)],
                                 **params)(x2, *moe, *gw)
    return x_new, proj


def _out_proj_kernel(x_ref, yh_ref, yr_ref, yn_ref, wh_ref, wr_ref, wn_ref, g_ref, wrt_ref, xo_ref, h_ref, lg_ref):
    mix = (jnp.dot(yh_ref[...].astype(bf16), wh_ref[...], preferred_element_type=f32)
           + jnp.dot(yr_ref[...].astype(bf16), wr_ref[...], preferred_element_type=f32)
           + jnp.dot(yn_ref[...].astype(bf16), wn_ref[...], preferred_element_type=f32))
    x = x_ref[...] + mix
    xo_ref[...] = x
    h = (_rms(x) * g_ref[...]).astype(bf16)
    h_ref[...] = h
    lg_ref[...] = jnp.dot(h, wrt_ref[...], preferred_element_type=f32)


def out_proj_norm_router(x2, y_hy, y_rw, y_na, w_out, g, w_router):
    N, D = x2.shape
    tm = PROJ_ROWS_PER_STEP
    d_hy, d_rw, d_na = y_hy.shape[1], y_rw.shape[1], y_na.shape[1]
    nr = w_router.shape[1]
    w_router = jnp.pad(w_router.astype(bf16), ((0, 0), (0, LANES - nr)))
    wb = w_out.astype(bf16)
    row = lambda c: pl.BlockSpec((tm, c), lambda i: (i, 0))
    full = lambda r, c: pl.BlockSpec((r, c), lambda i: (0, 0))
    return pl.pallas_call(
        _out_proj_kernel,
        grid=(N // tm,),
        in_specs=[row(D), row(d_hy), row(d_rw), row(d_na), full(d_hy, D), full(d_rw, D), full(d_na, D), full(1, D),
                  full(D, LANES)],
        out_specs=[row(D), row(D), row(LANES)],
        out_shape=[jax.ShapeDtypeStruct((N, D), f32), jax.ShapeDtypeStruct((N, D), bf16),
                   jax.ShapeDtypeStruct((N, LANES), f32)],
        compiler_params=pltpu.CompilerParams(dimension_semantics=("parallel",), vmem_limit_bytes=VMEM_LIMIT_BYTES),
        name="out_proj_norm_router",
    )(x2, y_hy, y_rw, y_na, wb[:d_hy], wb[d_hy:d_hy + d_rw], wb[d_hy + d_rw:], g.astype(f32).reshape(1, D), w_router)


def _final_norm_kernel(x_ref, y0_ref, y1_ref, g0_ref, g1_ref, g_ref, o_ref):
    o_ref[...] = _rms(_moe_combine(x_ref, y0_ref, y1_ref, g0_ref, g1_ref)) * g_ref[...]


def final_rmsnorm(x2, g, moe):
    N, D = x2.shape
    tm = PROJ_ROWS_PER_STEP
    row = pl.BlockSpec((tm, D), lambda i: (i, 0))
    return pl.pallas_call(
        _final_norm_kernel,
        grid=(N // tm,),
        in_specs=[row] + _moe_specs(tm, D) + [pl.BlockSpec((1, D), lambda i: (0, 0))],
        out_specs=row,
        out_shape=jax.ShapeDtypeStruct((N, D), f32),
        compiler_params=pltpu.CompilerParams(dimension_semantics=("parallel",), vmem_limit_bytes=VMEM_LIMIT_BYTES),
        name="final_rmsnorm",
    )(x2, *moe, g.astype(f32).reshape(1, D))


def hyena_positional_features(L):
    t = jnp.linspace(0.0, 1.0, L, dtype=f32)[:, None]
    w = (2.0 * math.pi / L) * jnp.arange(L, dtype=f32)[:, None]
    f = jnp.linspace(1e-4, HY_BANDS - 1, HY_BANDS, dtype=f32)[None, :]
    z = jnp.concatenate([t, jnp.cos(f * w), -jnp.sin(f * w)], axis=-1)
    return z, t


def hyena_filters(z, t, w1, b1, w2, b2, w3, b3, wout, freq):
    fr = freq.astype(f32)
    act = lambda u: jnp.sin(fr * u)
    h = act(z @ w1.astype(f32) + b1.astype(f32))
    h = act(h @ w2.astype(f32) + b2.astype(f32))
    h = act(h @ w3.astype(f32) + b3.astype(f32))
    h = (h @ wout.astype(f32)).reshape(-1, 2, HY_D)
    deltas = jnp.abs(jnp.linspace(HY_MIN_DECAY, HY_MAX_DECAY, HY_D, dtype=f32))
    h = h * jnp.exp(-t[:, :, None] * deltas)
    return h * lax.rsqrt(jnp.sum(h * h, axis=(0, 1), keepdims=True) + 1e-6)


def _split_bf16(x):
    hi = x.astype(bf16)
    return hi, (x - hi.astype(f32)).astype(bf16)


def _dot3(m_hi, m_lo, x_hi, x_lo):
    d = lambda a, b: lax.dot_general(a, b, _DN['nn'], preferred_element_type=f32)
    return d(m_hi, x_hi) + (d(m_hi, x_lo) + d(m_lo, x_hi))


def _cmatmul(mr, mi, xr, xi):
    xrh, xrl = _split_bf16(xr)
    rr = _dot3(mr[0], mr[1], xrh, xrl)
    ir = _dot3(mi[0], mi[1], xrh, xrl)
    if xi is None:
        return rr, ir
    xih, xil = _split_bf16(xi)
    ii = _dot3(mi[0], mi[1], xih, xil)
    ri = _dot3(mr[0], mr[1], xih, xil)
    return rr - ii, ri + ir


def _row_dft_kernel(mrh_ref, mrl_ref, mih_ref, mil_ref, *refs, real_input):
    if real_input:
        ur_ref, or_ref, oi_ref = refs
        ui = None
    else:
        ur_ref, ui_ref, or_ref, oi_ref = refs
        ui = ui_ref[0]
    o_r, o_i = _cmatmul((mrh_ref[...], mrl_ref[...]), (mih_ref[...], mil_ref[...]), ur_ref[0], ui)
    or_ref[0] = o_r
    oi_ref[0] = o_i


def row_dft(tabs, u, packed):
    rows, r_in, W = u.shape
    P = rows // 2 if packed else rows
    r_out = tabs[0].shape[0]
    wb = min(FFT_LANE_BLOCK, W)
    tab_spec = pl.BlockSpec((r_out, r_in), lambda p, j: (0, 0))
    re_spec = pl.BlockSpec((1, r_in, wb), lambda p, j: (p, 0, j))
    im_spec = pl.BlockSpec((1, r_in, wb), lambda p, j: (p + P, 0, j))
    out_spec = pl.BlockSpec((1, r_out, wb), lambda p, j: (p, 0, j))
    ins = [u, u] if packed else [u]
    return pl.pallas_call(
        functools.partial(_row_dft_kernel, real_input=not packed),
        grid=(P, W // wb),
        in_specs=[tab_spec] * 4 + ([re_spec, im_spec] if packed else [re_spec]),
        out_specs=[out_spec, out_spec],
        out_shape=[jax.ShapeDtypeStruct((P, r_out, W), f32)] * 2,
        compiler_params=pltpu.CompilerParams(dimension_semantics=("parallel", "parallel"),
                                             vmem_limit_bytes=VMEM_LIMIT_BYTES),
        name="hyena_row_dft",
    )(*tabs, *ins)


def _row_idft_gate_kernel(mrh_ref, mrl_ref, mih_ref, mil_ref, dr_ref, di_ref, z0_ref, z1_ref, x0_ref, x1_ref,
                          skip_ref, o0_ref, o1_ref):
    y_r, y_i = _cmatmul((mrh_ref[...], mrl_ref[...]), (mih_ref[...], mil_ref[...]), dr_ref[0], di_ref[0])
    skip = skip_ref[...]
    o0_ref[0] = (x0_ref[0] * (y_r + z0_ref[0] * skip)).astype(o0_ref.dtype)
    o1_ref[0] = (x1_ref[0] * (y_i + z1_ref[0] * skip)).astype(o1_ref.dtype)


def row_idft_gate(tabs, dr, di, z, x0, skip_row):
    P, r_in, W = dr.shape
    r_out = tabs[0].shape[0]
    wb = min(FFT_LANE_BLOCK, W)
    tab_spec = pl.BlockSpec((r_out, r_in), lambda p, j: (0, 0))
    d_spec = pl.BlockSpec((1, r_in, wb), lambda p, j: (p, 0, j))
    lo = pl.BlockSpec((1, r_out, wb), lambda p, j: (p, 0, j))
    hi = pl.BlockSpec((1, r_out, wb), lambda p, j: (p + P, 0, j))
    o0, o1 = pl.pallas_call(
        _row_idft_gate_kernel,
        grid=(P, W // wb),
        in_specs=[tab_spec] * 4 + [d_spec, d_spec, lo, hi, lo, hi, pl.BlockSpec((1, wb), lambda p, j: (0, j))],
        out_specs=[lo, lo],
        out_shape=[jax.ShapeDtypeStruct((P, r_out, W), bf16)] * 2,
        compiler_params=pltpu.CompilerParams(dimension_semantics=("parallel", "parallel"),
                                             vmem_limit_bytes=VMEM_LIMIT_BYTES),
        name="hyena_row_idft_gate",
    )(*tabs, dr, di, z, z, x0, x0, skip_row)
    return jnp.concatenate([o0, o1], axis=0)


def _col_dft_kernel(grh_ref, grl_ref, gih_ref, gil_ref, ar_ref, ai_ref, br_ref, bi_ref, *, k1_per_step):
    for j in range(k1_per_step):
        b_r, b_i = _cmatmul((grh_ref[j], grl_ref[j]), (gih_ref[j], gil_ref[j]), ar_ref[0, j], ai_ref[0, j])
        br_ref[0, j] = b_r
        bi_ref[0, j] = b_i


def _col_conv_kernel(grh_ref, grl_ref, gih_ref, gil_ref, trh_ref, trl_ref, tih_ref, til_ref,
                     ar_ref, ai_ref, kr_ref, ki_ref, dr_ref, di_ref, *, k1_per_step):
    for j in range(k1_per_step):
        b_r, b_i = _cmatmul((grh_ref[j], grl_ref[j]), (gih_ref[j], gil_ref[j]), ar_ref[0, j], ai_ref[0, j])
        k_r = kr_ref[0, j]
        k_i = ki_ref[0, j]
        c_r = b_r * k_r - b_i * k_i
        c_i = b_r * k_i + b_i * k_r
        d_r, d_i = _cmatmul((trh_ref[j], trl_ref[j]), (tih_ref[j], til_ref[j]), c_r, c_i)
        dr_ref[0, j] = d_r
        di_ref[0, j] = d_i


def col_stage(g_tabs, gt_tabs, ar, ai, kr=None, ki=None):
    P, n1, n2, C = ar.shape
    kb = min(FFT_K1_PER_STEP, n1)
    g_spec = pl.BlockSpec((kb, n2, n2), lambda p, j: (j, 0, 0))
    a_spec = pl.BlockSpec((1, kb, n2, C), lambda p, j: (p, j, 0, 0))
    k_spec = pl.BlockSpec((1, kb, n2, C), lambda p, j: (0, j, 0, 0))
    common = dict(
        grid=(P, n1 // kb),
        out_specs=[a_spec, a_spec],
        out_shape=[jax.ShapeDtypeStruct((P, n1, n2, C), f32)] * 2,
        compiler_params=pltpu.CompilerParams(dimension_semantics=("parallel", "parallel"),
                                             vmem_limit_bytes=VMEM_LIMIT_BYTES),
    )
    if kr is None:
        return pl.pallas_call(functools.partial(_col_dft_kernel, k1_per_step=kb),
                              in_specs=[g_spec] * 4 + [a_spec] * 2, name="hyena_col_dft", **common)(*g_tabs, ar, ai)
    return pl.pallas_call(functools.partial(_col_conv_kernel, k1_per_step=kb),
                          in_specs=[g_spec] * 8 + [a_spec] * 2 + [k_spec] * 2, name="hyena_col_conv", **common)(
        *g_tabs, *gt_tabs, ar, ai, kr, ki)


def _bf16_tables(m):
    out = []
    for part in (np.real(m), np.imag(m)):
        x = jnp.asarray(part, f32)
        hi = x.astype(bf16)
        out += [hi, (x - hi.astype(f32)).astype(bf16)]
    return out


def hyena_dft_tables(L):
    n2 = FFT_N2
    n1 = 2 * L // n2
    N = n1 * n2
    a = np.arange(n1)
    fa = np.exp(-2j * np.pi * np.outer(a, a) / n1)
    k1 = np.arange(n1)[:, None, None]
    k2 = np.arange(n2)[None, :, None]
    nn = np.arange(n2)[None, None, :]
    g = np.exp(-2j * np.pi * nn * (k1 + n1 * k2) / N)
    gt = np.conj(np.transpose(g, (0, 2, 1)))
    fc = np.conj(fa).T / N
    return dict(fa=_bf16_tables(fa), fa_half=_bf16_tables(fa[:, :n1 // 2]), g=_bf16_tables(g), gt=_bf16_tables(gt),
                fc_half=_bf16_tables(fc[:n1 // 2]))


def hyena_long_conv_gate(tabs, z, x0, k2, skip):
    Bn, L, C = z.shape
    n2 = FFT_N2
    n1 = 2 * L // n2
    P = Bn // 2
    W = n2 * C
    ar, ai = row_dft(tabs['fa'], k2.reshape(1, n1, W), packed=False)
    kr, ki = col_stage(tabs['g'], tabs['gt'], ar.reshape(1, n1, n2, C), ai.reshape(1, n1, n2, C))
    zv = z.reshape(Bn, n1 // 2, W)
    ar, ai = row_dft(tabs['fa_half'], zv, packed=True)
    dr, di = col_stage(tabs['g'], tabs['gt'], ar.reshape(P, n1, n2, C), ai.reshape(P, n1, n2, C), kr, ki)
    out = row_idft_gate(tabs['fc_half'], dr.reshape(P, n1, W), di.reshape(P, n1, W), zv, x0.reshape(Bn, n1 // 2, W),
                        jnp.tile(skip.astype(f32), n2).reshape(1, W))
    return out.reshape(Bn, L, C)


def _hyena_prep_kernel(*refs, rows, nt):
    cur, prev, nxt = refs[0:3], refs[3:6], refs[6:9]
    w_ref, b_ref, z_ref, x0_ref = refs[9:]
    t = pl.program_id(1)
    has_prev = (t > 0).astype(f32)
    has_next = (t < nt - 1).astype(f32)
    ridx = lax.broadcasted_iota(jnp.int32, (rows, HY_D), 0)

    def conv(j):
        u = cur[j][0]
        prev_row = prev[j][0, SUBLANES - 1:SUBLANES, :] * has_prev
        next_row = nxt[j][0, 0:1, :] * has_next
        up = jnp.where(ridx == 0, prev_row, pltpu.roll(u, 1, axis=0))
        un = jnp.where(ridx == rows - 1, next_row, pltpu.roll(u, rows - 1, axis=0))
        sl = slice(j * HY_D, (j + 1) * HY_D)
        return up * w_ref[0:1, sl] + u * w_ref[1:2, sl] + un * w_ref[2:3, sl] + b_ref[:, sl]

    x0_ref[0] = conv(0)
    z_ref[0] = conv(2) * conv(1)


def hyena_prep(proj, lane0, conv_w, conv_b):
    Bn, L, _ = proj.shape
    rows = RW_ROWS_PER_STEP
    nt = L // rows
    hb = rows // SUBLANES
    b0 = lane0 // HY_D
    assert b0 * HY_D == lane0
    cur = lambda j: pl.BlockSpec((1, rows, HY_D), lambda b, t: (b, t, b0 + j))
    prev = lambda j: pl.BlockSpec((1, SUBLANES, HY_D), lambda b, t: (b, jnp.maximum(t * hb - 1, 0), b0 + j))
    nxt = lambda j: pl.BlockSpec((1, SUBLANES, HY_D), lambda b, t: (b, jnp.minimum((t + 1) * hb, L // SUBLANES - 1), b0 + j))
    full = lambda shp: pl.BlockSpec(shp, lambda b, t: (0,) * len(shp))
    out = pl.BlockSpec((1, rows, HY_D), lambda b, t: (b, t, 0))
    return pl.pallas_call(
        functools.partial(_hyena_prep_kernel, rows=rows, nt=nt),
        grid=(Bn, nt),
        in_specs=[cur(0), cur(1), cur(2), prev(0), prev(1), prev(2), nxt(0), nxt(1), nxt(2),
                  full((3, 3 * HY_D)), full((1, 3 * HY_D))],
        out_specs=[out, out],
        out_shape=[jax.ShapeDtypeStruct((Bn, L, HY_D), f32)] * 2,
        compiler_params=pltpu.CompilerParams(dimension_semantics=("parallel", "parallel")),
        name="hyena_prep",
    )(*([proj] * 9), conv_w.astype(f32), conv_b.astype(f32).reshape(1, 3 * HY_D))


def hyena_mixer(tabs, proj, lane0, conv_w, conv_b, filt, skip):
    z, x0 = hyena_prep(proj, lane0, conv_w, conv_b)
    k2 = jnp.concatenate([filt[:1, 0] + filt[:1, 1], filt[1:, 0],
                          jnp.zeros((1, HY_D), f32), filt[1:, 1][::-1]], axis=0)
    return hyena_long_conv_gate(tabs, z, x0, k2, skip)


def rwkv7_mixer(proj, lora_lane0, mu, w0, w2, a0, a2, k_k, k_a, r_k, g2, ln_w, ln_b):
    r, k, v, kk, lw2, a_both = rw_prep(proj, lora_lane0, mu, w0, w2, a0, a2, k_k)
    y2 = wkv7_chunked(r, k, v, kk, lw2, a_both, k_a.astype(f32).reshape(1, RW_D))
    return rw_post(y2, r, k, v, a_both, proj, lora_lane0, k_a, r_k, ln_w, ln_b, g2)


def hier_moe(h, logits, bg, be, w1, w3, w2, layer):
    N, Dm = h.shape
    assert MOE_TOPK == 2
    g_logits = logits[:, :MOE_GROUPS] + bg.astype(f32)
    g_sel = jnp.argmax(g_logits, axis=-1)
    g_prob = jnp.take_along_axis(jax.nn.softmax(g_logits, axis=-1), g_sel[:, None], axis=-1)
    e_logits = (logits[:, MOE_GROUPS:MOE_GROUPS + MOE_EXPERTS] + be.astype(f32)).reshape(N, MOE_GROUPS, MOE_PER_GROUP)
    e_logits = jnp.take_along_axis(e_logits, g_sel[:, None, None], axis=1)[:, 0]
    top_val, top_idx = lax.top_k(e_logits, MOE_TOPK)
    gate = g_prob * jax.nn.softmax(top_val, axis=-1)
    expert = g_sel[:, None] * MOE_PER_GROUP + top_idx
    M = N * MOE_TOPK
    flat_e = expert.reshape(M).astype(jnp.int32)
    experts = jnp.arange(MOE_EXPERTS, dtype=jnp.int32)
    counts = jnp.sum((flat_e[:, None] == experts[None, :]).astype(jnp.int32), axis=0)
    padded = (counts + MOE_BLOCK - 1) // MOE_BLOCK * MOE_BLOCK
    pad_end = jnp.cumsum(padded)
    n_blocks = -(-M // MOE_BLOCK) + MOE_EXPERTS
    n_rows = n_blocks * MOE_BLOCK
    cum_need = jnp.cumsum(padded - counts)
    filler = jnp.arange(n_rows - M, dtype=jnp.int32)
    filler_e = jnp.sum((cum_need[None, :] <= filler[:, None]).astype(jnp.int32), axis=1)
    keys = jnp.concatenate([2 * flat_e, 2 * filler_e + 1])
    ids = jnp.concatenate([jnp.arange(M, dtype=jnp.int32), jnp.full((n_rows - M,), M, jnp.int32)])
    _, slot_src = lax.sort((keys, ids), num_keys=1)
    tok_src = jnp.where(slot_src < M, slot_src // MOE_TOPK, jnp.arange(n_rows, dtype=jnp.int32) % N)
    block_start = jnp.arange(n_blocks, dtype=pad_end.dtype) * MOE_BLOCK
    block_e = jnp.minimum(jnp.sum(pad_end[None, :] <= block_start[:, None], axis=1), MOE_EXPERTS - 1)
    y = moe_expert_ffn(h[tok_src], block_e, w1, w3, w2, layer)
    _, row_of = lax.sort((slot_src, jnp.arange(n_rows, dtype=jnp.int32)), num_keys=1)
    pos = row_of[:M].reshape(N, MOE_TOPK)
    return y[pos[:, 0]], y[pos[:, 1]], gate[:, 0:1], gate[:, 1:2]


def kernel(x, norm1_g, w_in, hy_conv_w, hy_conv_b, hy_w1, hy_b1, hy_w2, hy_b2, hy_w3, hy_b3, hy_wout, hy_freq, hy_skip, rw_mu, rw_w0, rw_w2, rw_a0, rw_a2, rw_kk, rw_ka, rw_rk, rw_g2, rw_ln_w, rw_ln_b, na_rpb, w_out, norm2_g, moe_wg, moe_bg, moe_we, moe_be, moe_w1, moe_w3, moe_w2, norm_f_g):
    Bn, L, _ = x.shape
    z_pos, t_pos = hyena_positional_features(L)
    dft_tabs = hyena_dft_tables(L)
    splits = np.cumsum(IN_SIZES)[:-1].tolist()
    hy_end, rkv_end = splits[0], splits[1]
    hy_lane0 = rkv_end - hy_end
    lora_lane0, na_lane0 = splits[1], splits[4]
    assert RW_D % LANES == 0 and lora_lane0 % LANES == 0 and na_lane0 % LANES == 0
    N = Bn * L
    x2 = x.reshape(N, D_MODEL)
    moe = None
    for l in range(DEPTH):
        w_in_l = jnp.concatenate([w_in[l][:, hy_end:rkv_end], w_in[l][:, :hy_end], w_in[l][:, rkv_end:]], axis=1)
        x2, proj = norm_proj(x2, norm1_g[l], w_in_l, moe)
        proj = proj.reshape(Bn, L, IN_D)
        filt = hyena_filters(z_pos, t_pos, hy_w1[l], hy_b1[l], hy_w2[l], hy_b2[l],
                             hy_w3[l], hy_b3[l], hy_wout[l], hy_freq[l])
        y_hy = hyena_mixer(dft_tabs, proj, hy_lane0, hy_conv_w[l], hy_conv_b[l], filt, hy_skip[l])
        y_rw = rwkv7_mixer(proj, lora_lane0, rw_mu[l], rw_w0[l], rw_w2[l], rw_a0[l],
                           rw_a2[l], rw_kk[l], rw_ka[l], rw_rk[l].reshape(RW_D), rw_g2[l], rw_ln_w[l], rw_ln_b[l])
        y_na = neighborhood_attention_pallas(proj, na_rpb[l], na_lane0 // LANES)
        x2, h2, logits = out_proj_norm_router(
            x2, y_hy.reshape(N, HY_D), y_rw.reshape(N, RW_D), y_na.reshape(N, NA_D), w_out[l], norm2_g[l],
            jnp.concatenate([moe_wg[l], moe_we[l]], axis=1))
        moe = hier_moe(h2, logits, moe_bg[l], moe_be[l], moe_w1, moe_w3, moe_w2, l)
    return final_rmsnorm(x2, norm_f_g, moe).reshape(Bn, L, D_MODEL)
```

```python
import functools
import math

import jax
import jax.numpy as jnp
import numpy as np
from jax import lax
from jax.experimental import pallas as pl
from jax.experimental.pallas import tpu as pltpu

f32 = jnp.float32
bf16 = jnp.bfloat16

D_MODEL = 1024
DEPTH = 2
GRID_W = 64
NORM_EPS = 1e-6
NEG_INF = -1e30

HY_D = D_MODEL // 4
HY_EMB = 33
HY_BANDS = (HY_EMB - 1) // 2
HY_FFN = 64
HY_MIN_DECAY = math.log(1e-2) / 1.5
HY_MAX_DECAY = math.log(1e-2) / 0.3

RW_N = 64
RW_D = D_MODEL // 2
RW_H = RW_D // RW_N
RW_W_LORA = 64
RW_A_LORA = 64
RW_G_LORA = 128
RW_GN_EPS = 64e-5

NA_HD = 64
NA_D = D_MODEL // 4
NA_H = NA_D // NA_HD
NA_KR = 8
NA_KC = 16

MIX_D = HY_D + RW_D + NA_D
IN_SIZES = (3 * HY_D, 3 * RW_D, RW_G_LORA, 2 * RW_W_LORA, 2 * RW_A_LORA, 3 * NA_D)
IN_D = sum(IN_SIZES)

MOE_GROUPS = 4
MOE_PER_GROUP = 8
MOE_EXPERTS = MOE_GROUPS * MOE_PER_GROUP
MOE_TOPK = 2
MOE_FF = 512
MOE_BLOCK = 512

LANES = 128
WKV_CHUNK = 64
WKV_TIME_BLOCK = 512
NA_ROWS_PER_STEP = 8
RW_ROWS_PER_STEP = 512
SUBLANES = 8
PROJ_ROWS_PER_STEP = 512
FFT_N2 = 128
FFT_LANE_BLOCK = 4096
FFT_K1_PER_STEP = 4
VMEM_LIMIT_BYTES = 48 * 1024 * 1024

_DN = {'nn': (((1,), (0,)), ((), ())), 'nt': (((1,), (1,)), ((), ())), 'tn': (((0,), (0,)), ((), ()))}


def _mm(a, b, dims='nn'):
    return lax.dot_general(a.astype(bf16), b.astype(bf16), _DN[dims], preferred_element_type=f32)


def _wkv_kernel(r_ref, k_ref, v_ref, kk_ref, lw_ref, a_ref, ka_ref, y_ref, ht_ref, *, tb, batch):
    T = WKV_CHUNK
    H2 = 2 * T
    nc = tb // T
    npairs = r_ref.shape[-1] // LANES
    mm = _mm

    @pl.when(pl.program_id(1) == 0)
    def _():
        ht_ref[...] = jnp.zeros_like(ht_ref)

    d = pl.program_id(0) // batch
    sign = 1 - 2 * d
    trow = lax.broadcasted_iota(jnp.int32, (T, H2), 0)
    lane = lax.broadcasted_iota(jnp.int32, (T, H2), 1)
    scol = lane % T
    tdiff = (trow - scol) * sign
    strict = tdiff > 0
    incl = tdiff >= 0
    same16 = (trow // 16) == (scol // 16)
    same32 = (trow // 32) == (scol // 32)
    off32 = same32 & jnp.logical_not(same16)
    off64 = jnp.logical_not(same32)
    eye2 = jnp.where(trow == scol, 1.0, 0.0).astype(f32)
    lo_lane = lane < T
    brow = lax.broadcasted_iota(jnp.int32, (H2, H2), 0)
    bcol = lax.broadcasted_iota(jnp.int32, (H2, H2), 1)
    same_head = (brow // T) == (bcol // T)
    crow = lax.broadcasted_iota(jnp.int32, (T, T), 0)
    ccol = lax.broadcasted_iota(jnp.int32, (T, T), 1)
    tri = jnp.where((crow - ccol) * sign >= 0, 1.0, 0.0).astype(bf16)
    is_bwd = d == 1

    def bd(x):
        return jnp.concatenate([jnp.where(lo_lane, x, 0.0), jnp.where(lo_lane, 0.0, x)], axis=0)

    inst = [(s, p) for s in range(nc) for p in range(npairs)]
    offs = [pl.multiple_of((s + d * (nc - 1 - 2 * s)) * T, T) for s in range(nc)]

    def load(ref, s, p):
        return ref[0, pl.ds(offs[s], T), p * LANES:(p + 1) * LANES]

    cs_l = []
    for s, p in inst:
        lw = load(lw_ref, s, p)
        l1 = lw.astype(bf16)
        l2 = (lw - l1.astype(f32)).astype(bf16)
        dd = lambda x: lax.dot_general(tri, x, _DN['nn'], preferred_element_type=f32)
        cs_l.append((dd(l1) + dd(l2), lw))
    ops = []
    for (s, p), (cs, lw) in zip(inst, cs_l):
        r = load(r_ref, s, p)
        k = load(k_ref, s, p)
        v = load(v_ref, s, p)
        kk = load(kk_ref, s, p)
        a = load(a_ref, s, p)
        ka = ka_ref[:, p * LANES:(p + 1) * LANES]
        kd = k * (1.0 + (a - 1.0) * ka)
        b = kk * a
        cs_end = jnp.where(is_bwd, cs[0:1, :], cs[T - 1:T, :])
        em = jnp.exp(-cs)
        e_end = jnp.exp(cs_end - cs)
        ops.append(dict(
            AR=jnp.concatenate([-kk * jnp.exp(cs - lw), r * jnp.exp(cs)], axis=0),
            BK=jnp.concatenate([bd(b * em), bd(kd * em)], axis=0),
            V=v, BKg=jnp.concatenate([b * e_end, kd * e_end], axis=0), g_end=jnp.exp(cs_end)))
    for o in ops:
        S = mm(o['AR'], o['BK'], 'nt')
        N = jnp.where(strict, S[:T, :H2], 0.0)
        o['AakArk'] = jnp.concatenate([jnp.where(strict, S[:T, H2:], 0.0), jnp.where(incl, S[T:, H2:], 0.0)], axis=0)
        o['Arb'] = jnp.where(incl, S[T:, :H2], 0.0)
        o['Nd'] = jnp.where(same16, N, 0.0)
        o['N32'] = jnp.where(off32, N, 0.0)
        o['N64'] = jnp.where(off64, N, 0.0)
    for o in ops:
        o['X'] = eye2 + o['Nd']
        o['P'] = mm(o['Nd'], bd(o['Nd']))
    for it in range(3):
        if it < 2:
            for o in ops:
                px = mm(o['P'], jnp.concatenate([bd(o['X']), bd(o['P'])], axis=1))
                o['X'] = o['X'] + px[:, :H2]
                o['P'] = px[:, H2:]
        else:
            for o in ops:
                o['X'] = o['X'] + mm(o['P'], bd(o['X']))
    for key in ('N32', 'N64'):
        for o in ops:
            o['Z'] = mm(o[key], bd(o['X']))
        for o in ops:
            o['X'] = o['X'] + mm(o['X'], bd(o['Z']))
    for o in ops:
        wy = mm(o['AakArk'], bd(o['V']))
        o['W0'] = wy[:T]
        o['Yv'] = wy[T:]
    for o in ops:
        o['XAW'] = mm(o['X'], jnp.concatenate([bd(o['AR'][:T]), bd(o['W0'])], axis=1))
    for o in ops:
        ax = mm(o['Arb'], jnp.concatenate([bd(o['XAW'][:, :H2]), bd(o['XAW'][:, H2:])], axis=1))
        o['Rhat'] = o['AR'][T:] + ax[:, :H2]
        o['Yc'] = ax[:, H2:] + o['Yv']
    for o in ops:
        lhs = jnp.concatenate([o['XAW'], jnp.concatenate([jnp.zeros((T, H2), f32), o['V']], axis=1)], axis=0)
        pq = mm(lhs, o['BKg'], 'tn')
        o['Pc'] = jnp.where(same_head, pq[:H2], 0.0)
        o['Qc'] = jnp.where(same_head, pq[H2:], 0.0)
    hts = [ht_ref[p] for p in range(npairs)]
    for s in range(nc):
        cur = [ops[s * npairs + p] for p in range(npairs)]
        ys = [mm(o['Rhat'], hts[p], 'nt') + o['Yc'] for p, o in enumerate(cur)]
        hts = [hts[p] * o['g_end'] + (mm(hts[p], o['Pc']) + o['Qc']) for p, o in enumerate(cur)]
        for p in range(npairs):
            y_ref[0, pl.ds(offs[s], T), p * LANES:(p + 1) * LANES] = ys[p]
    for p in range(npairs):
        ht_ref[p] = hts[p]


def wkv7_chunked(r, k, v, kk, lw2, a2, ka):
    Bn, L, C = r.shape
    tb = WKV_TIME_BLOCK
    nt = L // tb
    tmap = lambda i, t: t + (i // Bn) * (nt - 1 - 2 * t)
    shared = pl.BlockSpec((1, tb, C), lambda i, t: (i % Bn, tmap(i, t), 0))
    per_dir = pl.BlockSpec((1, tb, C), lambda i, t: (i % Bn, tmap(i, t), i // Bn))
    return pl.pallas_call(
        functools.partial(_wkv_kernel, tb=tb, batch=Bn),
        grid=(2 * Bn, nt),
        in_specs=[shared, shared, shared, shared, per_dir, per_dir, pl.BlockSpec((1, C), lambda i, t: (0, 0))],
        out_specs=pl.BlockSpec((1, tb, C), lambda i, t: (i, tmap(i, t), 0)),
        out_shape=jax.ShapeDtypeStruct((2 * Bn, L, C), f32),
        scratch_shapes=[pltpu.VMEM((C // LANES, LANES, LANES), f32)],
        compiler_params=pltpu.CompilerParams(dimension_semantics=("parallel", "arbitrary"),
                                             vmem_limit_bytes=VMEM_LIMIT_BYTES),
        name="wkv7_chunked",
    )(r, k, v, kk, lw2, a2, ka)


def _mm_exact_rhs(a, b_bf16):
    ah = a.astype(bf16)
    al = (a - ah.astype(f32)).astype(bf16)
    d = lambda x: lax.dot_general(x, b_bf16, _DN['nn'], preferred_element_type=f32)
    return d(ah) + d(al)


def _head_sum_matrix(scale):
    i = lax.broadcasted_iota(jnp.int32, (RW_D, RW_D), 0) // RW_N
    j = lax.broadcasted_iota(jnp.int32, (RW_D, RW_D), 1) // RW_N
    return jnp.where(i == j, scale, 0.0).astype(bf16)


def _rw_prep_kernel(r_ref, k_ref, v_ref, rp_ref, kp_ref, vp_ref, rn_ref, kn_ref, vn_ref, w_ref, a_ref,
                    mu_ref, w0_ref, w2_ref, a0_ref, a2_ref, kkw_ref,
                    ro_ref, ko_ref, vo_ref, kko_ref, lw_ref, ao_ref, *, rows, nt):
    t = pl.program_id(1)
    has_prev = (t > 0).astype(f32)
    has_next = (t < nt - 1).astype(f32)
    ridx = lax.broadcasted_iota(jnp.int32, (rows, RW_D), 0)

    def tshift(cur_ref, prev_ref, next_ref, j):
        u = cur_ref[0]
        prev_row = prev_ref[0, SUBLANES - 1:SUBLANES, :] * has_prev
        next_row = next_ref[0, 0:1, :] * has_next
        up = jnp.where(ridx == 0, prev_row, pltpu.roll(u, 1, axis=0))
        un = jnp.where(ridx == rows - 1, next_row, pltpu.roll(u, rows - 1, axis=0))
        return u + mu_ref[j, 0:1, :] * (up - u) + mu_ref[j, 1:2, :] * (un - u)

    r = tshift(r_ref, rp_ref, rn_ref, 0)
    k = tshift(k_ref, kp_ref, kn_ref, 1)
    v = tshift(v_ref, vp_ref, vn_ref, 2)
    ro_ref[0] = r
    ko_ref[0] = k
    vo_ref[0] = v
    kk = k * kkw_ref[...]
    ss = _mm_exact_rhs(kk * kk, _head_sum_matrix(1.0))
    kko_ref[0] = kk * lax.rsqrt(jnp.maximum(ss, 1e-24))
    wl = jnp.tanh(w_ref[0])
    al = a_ref[0]
    for d in range(2):
        wpre = w0_ref[d:d + 1, :] + _mm(wl[:, d * RW_W_LORA:(d + 1) * RW_W_LORA], w2_ref[d])
        lw_ref[0, :, d * RW_D:(d + 1) * RW_D] = -math.exp(-0.5) * jax.nn.sigmoid(wpre)
        av = a0_ref[d:d + 1, :] + _mm(al[:, d * RW_A_LORA:(d + 1) * RW_A_LORA], a2_ref[d])
        ao_ref[0, :, d * RW_D:(d + 1) * RW_D] = jax.nn.sigmoid(av)


def rw_prep(proj, lora_lane0, mu, w0, w2, a0, a2, k_k):
    Bn, L, _ = proj.shape
    C = RW_D
    rows = RW_ROWS_PER_STEP
    nt = L // rows
    hb = rows // SUBLANES
    lb = lora_lane0 // LANES
    cur = lambda j: pl.BlockSpec((1, rows, C), lambda b, t: (b, t, j))
    prev = lambda j: pl.BlockSpec((1, SUBLANES, C), lambda b, t: (b, jnp.maximum(t * hb - 1, 0), j))
    nxt = lambda j: pl.BlockSpec((1, SUBLANES, C), lambda b, t: (b, jnp.minimum((t + 1) * hb, L // SUBLANES - 1), j))
    lora_w = pl.BlockSpec((1, rows, LANES), lambda b, t: (b, t, lb + 1))
    lora_a = pl.BlockSpec((1, rows, LANES), lambda b, t: (b, t, lb + 2))
    full = lambda shp: pl.BlockSpec(shp, lambda b, t: (0,) * len(shp))
    out_c = pl.BlockSpec((1, rows, C), lambda b, t: (b, t, 0))
    out_2c = pl.BlockSpec((1, rows, 2 * C), lambda b, t: (b, t, 0))
    sds = lambda c: jax.ShapeDtypeStruct((Bn, L, c), f32)
    return pl.pallas_call(
        functools.partial(_rw_prep_kernel, rows=rows, nt=nt),
        grid=(Bn, nt),
        in_specs=[cur(0), cur(1), cur(2), prev(0), prev(1), prev(2), nxt(0), nxt(1), nxt(2), lora_w, lora_a,
                  full((3, 2, C)), full((2, C)), full((2, RW_W_LORA, C)), full((2, C)), full((2, RW_A_LORA, C)),
                  full((1, C))],
        out_specs=[out_c, out_c, out_c, out_c, out_2c, out_2c],
        out_shape=[sds(C), sds(C), sds(C), sds(C), sds(2 * C), sds(2 * C)],
        compiler_params=pltpu.CompilerParams(dimension_semantics=("parallel", "parallel"),
                                             vmem_limit_bytes=VMEM_LIMIT_BYTES),
        name="rwkv_prep",
    )(proj, proj, proj, proj, proj, proj, proj, proj, proj, proj, proj,
      mu.astype(f32), w0.astype(f32), w2.astype(bf16), a0.astype(f32), a2.astype(bf16), k_k.astype(f32).reshape(1, C))


def _rw_post_kernel(yf_ref, yb_ref, r_ref, k_ref, v_ref, a_ref, g_ref, ka_ref, rk_ref, lnw_ref, lnb_ref, g2_ref, o_ref):
    C = RW_D
    y = yf_ref[0] + yb_ref[0]
    avg = _head_sum_matrix(1.0 / RW_N)
    mean = _mm_exact_rhs(y, avg)
    yc = y - mean
    var = _mm_exact_rhs(yc * yc, avg)
    yn = yc * lax.rsqrt(var + RW_GN_EPS) * lnw_ref[...] + lnb_ref[...]
    a = a_ref[0]
    k = k_ref[0]
    ka = ka_ref[...]
    ksum = k * (1.0 + (a[:, :C] - 1.0) * ka) + k * (1.0 + (a[:, C:] - 1.0) * ka)
    coef = _mm_exact_rhs(r_ref[0] * ksum * rk_ref[...], _head_sum_matrix(1.0))
    gate = _mm(jax.nn.sigmoid(g_ref[0]), g2_ref[...])
    o_ref[0] = ((yn + coef * v_ref[0]) * gate).astype(o_ref.dtype)


def rw_post(y2, r, k, v, a2, proj, g_lane0, k_a, r_k, ln_w, ln_b, g2):
    Bn, L, C = r.shape
    rows = RW_ROWS_PER_STEP
    nt = L // rows
    gb = g_lane0 // LANES
    blk = lambda c: pl.BlockSpec((1, rows, c), lambda b, t: (b, t, 0))
    full = lambda shp: pl.BlockSpec(shp, lambda b, t: (0,) * len(shp))
    row = lambda x: x.astype(f32).reshape(1, C)
    return pl.pallas_call(
        _rw_post_kernel,
        grid=(Bn, nt),
        in_specs=[blk(C), pl.BlockSpec((1, rows, C), lambda b, t: (b + Bn, t, 0)), blk(C), blk(C), blk(C), blk(2 * C),
                  pl.BlockSpec((1, rows, LANES), lambda b, t: (b, t, gb)),
                  full((1, C)), full((1, C)), full((1, C)), full((1, C)), full((RW_G_LORA, C))],
        out_specs=blk(C),
        out_shape=jax.ShapeDtypeStruct((Bn, L, C), bf16),
        compiler_params=pltpu.CompilerParams(dimension_semantics=("parallel", "parallel"),
                                             vmem_limit_bytes=VMEM_LIMIT_BYTES),
        name="rwkv_post",
    )(y2, y2, r, k, v, a2, proj, row(k_a), row(r_k), row(ln_w), row(ln_b), g2.astype(bf16))


def _na_kernel(q_ref, k_ref, v_ref, bias_ref, o_ref, kb_ref, vb_ref, *, rows_per_step, n_rows, kr):
    W = GRID_W
    rb = pl.program_id(2)

    @pl.when(rb == 0)
    def _():
        kb_ref[...] = k_ref[0].astype(bf16)
        vb_ref[...] = v_ref[0].astype(bf16)

    lo_lane = lax.broadcasted_iota(jnp.int32, (W, LANES), 1) < NA_HD
    scale = NA_HD ** -0.5
    rows = []
    for j in range(rows_per_step):
        r = rb * rows_per_step + j
        start = jnp.clip(r - kr // 2, 0, n_rows - kr)
        rows.append((start, start - r + (NA_KR - 1)))
    s_list = []
    for j, (start, didx) in enumerate(rows):
        q = q_ref[0, j * W:(j + 1) * W, :] * scale
        qs = jnp.concatenate([jnp.where(lo_lane, q, 0.0), jnp.where(lo_lane, 0.0, q)], axis=0)
        kw = kb_ref[pl.ds(pl.multiple_of(start * W, W), kr * W), :]
        s_list.append(_mm(qs, kw, 'nt') + bias_ref[didx, 0])
    p_list = []
    for s in s_list:
        m = jnp.max(s, axis=-1, keepdims=True)
        p = jnp.exp(s - m)
        p_list.append((p, jnp.sum(p, axis=-1, keepdims=True)))
    for j, ((start, _), (p, l)) in enumerate(zip(rows, p_list)):
        vw = vb_ref[pl.ds(pl.multiple_of(start * W, W), kr * W), :]
        o = _mm(p, vw) / l
        o_ref[0, j * W:(j + 1) * W, :] = jnp.where(lo_lane, o[:W], o[W:]).astype(o_ref.dtype)


def na_bias_table(rpb, n_rows):
    W = GRID_W
    kr = min(NA_KR, n_rows)
    cols = jnp.arange(W)
    col_start = jnp.clip(cols - NA_KC // 2, 0, W - NA_KC)
    in_band = (cols[None, :] >= col_start[:, None]) & (cols[None, :] < col_start[:, None] + NA_KC)
    dc = jnp.clip(cols[None, :] - cols[:, None], -(NA_KC - 1), NA_KC - 1) + (NA_KC - 1)
    win = jnp.stack([rpb.astype(f32)[:, d:d + kr, :] for d in range(8)], axis=1)
    onehot = (dc[:, :, None] == jnp.arange(2 * NA_KC - 1)[None, None, :]).astype(f32)
    tab = jnp.einsum('hdic,qkc->hdqik', win, onehot, precision=lax.Precision.HIGHEST)
    tab = jnp.where(in_band[None, None, :, None, :], tab, NEG_INF)
    H = rpb.shape[0]
    tab = tab.reshape(H // 2, 2, 8, W, kr * W).transpose(2, 0, 1, 3, 4)
    return tab.reshape(8, H // 2, 2 * W, kr * W)


def neighborhood_attention_pallas(proj, rpb, lane_block0):
    Bn, L, _ = proj.shape
    W = GRID_W
    n_rows = L // W
    kr = min(NA_KR, n_rows)
    hp = NA_H * NA_HD // LANES
    rps = NA_ROWS_PER_STEP
    bias = na_bias_table(rpb, n_rows)
    kern = functools.partial(_na_kernel, rows_per_step=rps, n_rows=n_rows, kr=kr)
    return pl.pallas_call(
        kern,
        grid=(Bn, hp, n_rows // rps),
        in_specs=[pl.BlockSpec((1, rps * W, LANES), lambda b, h, r: (b, r, lane_block0 + h)),
                  pl.BlockSpec((1, L, LANES), lambda b, h, r: (b, 0, lane_block0 + hp + h)),
                  pl.BlockSpec((1, L, LANES), lambda b, h, r: (b, 0, lane_block0 + 2 * hp + h)),
                  pl.BlockSpec((8, 1, 2 * W, kr * W), lambda b, h, r: (0, h, 0, 0))],
        out_specs=pl.BlockSpec((1, rps * W, LANES), lambda b, h, r: (b, r, h)),
        out_shape=jax.ShapeDtypeStruct((Bn, L, NA_H * NA_HD), bf16),
        scratch_shapes=[pltpu.VMEM((L, LANES), bf16), pltpu.VMEM((L, LANES), bf16)],
        compiler_params=pltpu.CompilerParams(dimension_semantics=("parallel", "parallel", "arbitrary"),
                                             vmem_limit_bytes=VMEM_LIMIT_BYTES),
        name="neighborhood_attention",
    )(proj, proj, proj, bias)


def _moe_ffn_kernel(be_ref, x_ref, w1_ref, w3_ref, w2_ref, o_ref, w1b_ref, w3b_ref, w2b_ref):
    i = pl.program_id(0)

    @pl.when((i == 0) | (be_ref[i] != be_ref[jnp.maximum(i - 1, 0)]))
    def _():
        w1b_ref[...] = w1_ref[0, 0].astype(bf16)
        w3b_ref[...] = w3_ref[0, 0].astype(bf16)
        w2b_ref[...] = w2_ref[0, 0].astype(bf16)

    x = x_ref[...].astype(bf16)
    h1 = jnp.dot(x, w1b_ref[...], preferred_element_type=f32)
    h3 = jnp.dot(x, w3b_ref[...], preferred_element_type=f32)
    g = h1 * jax.nn.sigmoid(h1) * h3
    o_ref[...] = jnp.dot(g.astype(bf16), w2b_ref[...], preferred_element_type=f32).astype(o_ref.dtype)


def moe_expert_ffn(xs, block_e, w1, w3, w2, layer):
    rows, Dm = xs.shape
    n_blocks = rows // MOE_BLOCK
    grid_spec = pltpu.PrefetchScalarGridSpec(
        num_scalar_prefetch=1,
        grid=(n_blocks,),
        in_specs=[pl.BlockSpec((MOE_BLOCK, Dm), lambda i, be: (i, 0)),
                  pl.BlockSpec((1, 1, Dm, MOE_FF), lambda i, be: (layer, be[i], 0, 0)),
                  pl.BlockSpec((1, 1, Dm, MOE_FF), lambda i, be: (layer, be[i], 0, 0)),
                  pl.BlockSpec((1, 1, MOE_FF, Dm), lambda i, be: (layer, be[i], 0, 0))],
        out_specs=pl.BlockSpec((MOE_BLOCK, Dm), lambda i, be: (i, 0)),
        scratch_shapes=[pltpu.VMEM((Dm, MOE_FF), bf16), pltpu.VMEM((Dm, MOE_FF), bf16), pltpu.VMEM((MOE_FF, Dm), bf16)],
    )
    return pl.pallas_call(
        _moe_ffn_kernel,
        grid_spec=grid_spec,
        out_shape=jax.ShapeDtypeStruct((rows, Dm), bf16),
        compiler_params=pltpu.CompilerParams(dimension_semantics=("arbitrary",), vmem_limit_bytes=VMEM_LIMIT_BYTES),
        name="moe_expert_ffn",
    )(block_e.astype(jnp.int32), xs, w1.astype(f32), w3.astype(f32), w2.astype(f32))


def _rms(x):
    return x * lax.rsqrt(jnp.mean(x * x, axis=-1, keepdims=True) + NORM_EPS)


def _moe_combine(x_ref, y0_ref, y1_ref, g0_ref, g1_ref):
    return x_ref[...] + (g0_ref[...] * y0_ref[...].astype(f32) + g1_ref[...] * y1_ref[...].astype(f32))


def _moe_specs(tm, D):
    row = pl.BlockSpec((tm, D), lambda i: (i, 0))
    col = pl.BlockSpec((tm, 1), lambda i: (i, 0))
    return [row, row, col, col]


def _norm_proj_kernel(*refs, combine):
    if combine:
        x_ref, y0_ref, y1_ref, g0_ref, g1_ref, g_ref, w_ref, o_ref, xo_ref = refs
        x = _moe_combine(x_ref, y0_ref, y1_ref, g0_ref, g1_ref)
        xo_ref[...] = x
    else:
        x_ref, g_ref, w_ref, o_ref = refs
        x = x_ref[...]
    h = (_rms(x) * g_ref[...]).astype(bf16)
    o_ref[...] = jnp.dot(h, w_ref[...], preferred_element_type=f32)


def norm_proj(x2, g, w, moe=None):
    N, D = x2.shape
    F = w.shape[1]
    tm = PROJ_ROWS_PER_STEP
    row = pl.BlockSpec((tm, D), lambda i: (i, 0))
    out_row = pl.BlockSpec((tm, F), lambda i: (i, 0))
    par = [pl.BlockSpec((1, D), lambda i: (0, 0)), pl.BlockSpec((D, F), lambda i: (0, 0))]
    params = dict(compiler_params=pltpu.CompilerParams(dimension_semantics=("parallel",),
                                                       vmem_limit_bytes=VMEM_LIMIT_BYTES), name="norm_proj")
    gw = (g.astype(f32).reshape(1, D), w.astype(bf16))
    if moe is None:
        proj = pl.pallas_call(functools.partial(_norm_proj_kernel, combine=False), grid=(N // tm,),
                              in_specs=[row] + par, out_specs=out_row,
                              out_shape=jax.ShapeDtypeStruct((N, F), f32), **params)(x2, *gw)
        return x2, proj
    proj, x_new = pl.pallas_call(functools.partial(_norm_proj_kernel, combine=True), grid=(N // tm,),
                                 in_specs=[row] + _moe_specs(tm, D) + par, out_specs=[out_row, row],
                                 out_shape=[jax.ShapeDtypeStruct((N, F), f32), jax.ShapeDtypeStruct((N, D), f32)],
                                 **params)(x2, *moe, *gw)
    return x_new, proj


def _out_proj_kernel(x_ref, yh_ref, yr_ref, yn_ref, wh_ref, wr_ref, wn_ref, g_ref, wrt_ref, xo_ref, h_ref, lg_ref):
    mix = (jnp.dot(yh_ref[...].astype(bf16), wh_ref[...], preferred_element_type=f32)
           + jnp.dot(yr_ref[...].astype(bf16), wr_ref[...], preferred_element_type=f32)
           + jnp.dot(yn_ref[...].astype(bf16), wn_ref[...], preferred_element_type=f32))
    x = x_ref[...] + mix
    xo_ref[...] = x
    h = (_rms(x) * g_ref[...]).astype(bf16)
    h_ref[...] = h
    lg_ref[...] = jnp.dot(h, wrt_ref[...], preferred_element_type=f32)


def out_proj_norm_router(x2, y_hy, y_rw, y_na, w_out, g, w_router):
    N, D = x2.shape
    tm = PROJ_ROWS_PER_STEP
    d_hy, d_rw, d_na = y_hy.shape[1], y_rw.shape[1], y_na.shape[1]
    nr = w_router.shape[1]
    w_router = jnp.pad(w_router.astype(bf16), ((0, 0), (0, LANES - nr)))
    wb = w_out.astype(bf16)
    row = lambda c: pl.BlockSpec((tm, c), lambda i: (i, 0))
    full = lambda r, c: pl.BlockSpec((r, c), lambda i: (0, 0))
    return pl.pallas_call(
        _out_proj_kernel,
        grid=(N // tm,),
        in_specs=[row(D), row(d_hy), row(d_rw), row(d_na), full(d_hy, D), full(d_rw, D), full(d_na, D), full(1, D),
                  full(D, LANES)],
        out_specs=[row(D), row(D), row(LANES)],
        out_shape=[jax.ShapeDtypeStruct((N, D), f32), jax.ShapeDtypeStruct((N, D), bf16),
                   jax.ShapeDtypeStruct((N, LANES), f32)],
        compiler_params=pltpu.CompilerParams(dimension_semantics=("parallel",), vmem_limit_bytes=VMEM_LIMIT_BYTES),
        name="out_proj_norm_router",
    )(x2, y_hy, y_rw, y_na, wb[:d_hy], wb[d_hy:d_hy + d_rw], wb[d_hy + d_rw:], g.astype(f32).reshape(1, D), w_router)


def _final_norm_kernel(x_ref, y0_ref, y1_ref, g0_ref, g1_ref, g_ref, o_ref):
    o_ref[...] = _rms(_moe_combine(x_ref, y0_ref, y1_ref, g0_ref, g1_ref)) * g_ref[...]


def final_rmsnorm(x2, g, moe):
    N, D = x2.shape
    tm = PROJ_ROWS_PER_STEP
    row = pl.BlockSpec((tm, D), lambda i: (i, 0))
    return pl.pallas_call(
        _final_norm_kernel,
        grid=(N // tm,),
        in_specs=[row] + _moe_specs(tm, D) + [pl.BlockSpec((1, D), lambda i: (0, 0))],
        out_specs=row,
        out_shape=jax.ShapeDtypeStruct((N, D), f32),
        compiler_params=pltpu.CompilerParams(dimension_semantics=("parallel",), vmem_limit_bytes=VMEM_LIMIT_BYTES),
        name="final_rmsnorm",
    )(x2, *moe, g.astype(f32).reshape(1, D))


def hyena_positional_features(L):
    t = jnp.linspace(0.0, 1.0, L, dtype=f32)[:, None]
    w = (2.0 * math.pi / L) * jnp.arange(L, dtype=f32)[:, None]
    f = jnp.linspace(1e-4, HY_BANDS - 1, HY_BANDS, dtype=f32)[None, :]
    z = jnp.concatenate([t, jnp.cos(f * w), -jnp.sin(f * w)], axis=-1)
    return z, t


def hyena_filters(z, t, w1, b1, w2, b2, w3, b3, wout, freq):
    fr = freq.astype(f32)
    act = lambda u: jnp.sin(fr * u)
    h = act(z @ w1.astype(f32) + b1.astype(f32))
    h = act(h @ w2.astype(f32) + b2.astype(f32))
    h = act(h @ w3.astype(f32) + b3.astype(f32))
    h = (h @ wout.astype(f32)).reshape(-1, 2, HY_D)
    deltas = jnp.abs(jnp.linspace(HY_MIN_DECAY, HY_MAX_DECAY, HY_D, dtype=f32))
    h = h * jnp.exp(-t[:, :, None] * deltas)
    return h * lax.rsqrt(jnp.sum(h * h, axis=(0, 1), keepdims=True) + 1e-6)


def _split_bf16(x):
    hi = x.astype(bf16)
    return hi, (x - hi.astype(f32)).astype(bf16)


def _dot3(m_hi, m_lo, x_hi, x_lo):
    d = lambda a, b: lax.dot_general(a, b, _DN['nn'], preferred_element_type=f32)
    return d(m_hi, x_hi) + (d(m_hi, x_lo) + d(m_lo, x_hi))


def _cmatmul(mr, mi, xr, xi):
    xrh, xrl = _split_bf16(xr)
    rr = _dot3(mr[0], mr[1], xrh, xrl)
    ir = _dot3(mi[0], mi[1], xrh, xrl)
    if xi is None:
        return rr, ir
    xih, xil = _split_bf16(xi)
    ii = _dot3(mi[0], mi[1], xih, xil)
    ri = _dot3(mr[0], mr[1], xih, xil)
    return rr - ii, ri + ir


def _row_dft_kernel(mrh_ref, mrl_ref, mih_ref, mil_ref, *refs, real_input):
    if real_input:
        ur_ref, or_ref, oi_ref = refs
        ui = None
    else:
        ur_ref, ui_ref, or_ref, oi_ref = refs
        ui = ui_ref[0]
    o_r, o_i = _cmatmul((mrh_ref[...], mrl_ref[...]), (mih_ref[...], mil_ref[...]), ur_ref[0], ui)
    or_ref[0] = o_r
    oi_ref[0] = o_i


def row_dft(tabs, u, packed):
    rows, r_in, W = u.shape
    P = rows // 2 if packed else rows
    r_out = tabs[0].shape[0]
    wb = min(FFT_LANE_BLOCK, W)
    tab_spec = pl.BlockSpec((r_out, r_in), lambda p, j: (0, 0))
    re_spec = pl.BlockSpec((1, r_in, wb), lambda p, j: (p, 0, j))
    im_spec = pl.BlockSpec((1, r_in, wb), lambda p, j: (p + P, 0, j))
    out_spec = pl.BlockSpec((1, r_out, wb), lambda p, j: (p, 0, j))
    ins = [u, u] if packed else [u]
    return pl.pallas_call(
        functools.partial(_row_dft_kernel, real_input=not packed),
        grid=(P, W // wb),
        in_specs=[tab_spec] * 4 + ([re_spec, im_spec] if packed else [re_spec]),
        out_specs=[out_spec, out_spec],
        out_shape=[jax.ShapeDtypeStruct((P, r_out, W), f32)] * 2,
        compiler_params=pltpu.CompilerParams(dimension_semantics=("parallel", "parallel"),
                                             vmem_limit_bytes=VMEM_LIMIT_BYTES),
        name="hyena_row_dft",
    )(*tabs, *ins)


def _row_idft_gate_kernel(mrh_ref, mrl_ref, mih_ref, mil_ref, dr_ref, di_ref, z0_ref, z1_ref, x0_ref, x1_ref,
                          skip_ref, o0_ref, o1_ref):
    y_r, y_i = _cmatmul((mrh_ref[...], mrl_ref[...]), (mih_ref[...], mil_ref[...]), dr_ref[0], di_ref[0])
    skip = skip_ref[...]
    o0_ref[0] = (x0_ref[0] * (y_r + z0_ref[0] * skip)).astype(o0_ref.dtype)
    o1_ref[0] = (x1_ref[0] * (y_i + z1_ref[0] * skip)).astype(o1_ref.dtype)


def row_idft_gate(tabs, dr, di, z, x0, skip_row):
    P, r_in, W = dr.shape
    r_out = tabs[0].shape[0]
    wb = min(FFT_LANE_BLOCK, W)
    tab_spec = pl.BlockSpec((r_out, r_in), lambda p, j: (0, 0))
    d_spec = pl.BlockSpec((1, r_in, wb), lambda p, j: (p, 0, j))
    lo = pl.BlockSpec((1, r_out, wb), lambda p, j: (p, 0, j))
    hi = pl.BlockSpec((1, r_out, wb), lambda p, j: (p + P, 0, j))
    o0, o1 = pl.pallas_call(
        _row_idft_gate_kernel,
        grid=(P, W // wb),
        in_specs=[tab_spec] * 4 + [d_spec, d_spec, lo, hi, lo, hi, pl.BlockSpec((1, wb), lambda p, j: (0, j))],
        out_specs=[lo, lo],
        out_shape=[jax.ShapeDtypeStruct((P, r_out, W), bf16)] * 2,
        compiler_params=pltpu.CompilerParams(dimension_semantics=("parallel", "parallel"),
                                             vmem_limit_bytes=VMEM_LIMIT_BYTES),
        name="hyena_row_idft_gate",
    )(*tabs, dr, di, z, z, x0, x0, skip_row)
    return jnp.concatenate([o0, o1], axis=0)


def _col_dft_kernel(grh_ref, grl_ref, gih_ref, gil_ref, ar_ref, ai_ref, br_ref, bi_ref, *, k1_per_step):
    for j in range(k1_per_step):
        b_r, b_i = _cmatmul((grh_ref[j], grl_ref[j]), (gih_ref[j], gil_ref[j]), ar_ref[0, j], ai_ref[0, j])
        br_ref[0, j] = b_r
        bi_ref[0, j] = b_i


def _col_conv_kernel(grh_ref, grl_ref, gih_ref, gil_ref, trh_ref, trl_ref, tih_ref, til_ref,
                     ar_ref, ai_ref, kr_ref, ki_ref, dr_ref, di_ref, *, k1_per_step):
    for j in range(k1_per_step):
        b_r, b_i = _cmatmul((grh_ref[j], grl_ref[j]), (gih_ref[j], gil_ref[j]), ar_ref[0, j], ai_ref[0, j])
        k_r = kr_ref[0, j]
        k_i = ki_ref[0, j]
        c_r = b_r * k_r - b_i * k_i
        c_i = b_r * k_i + b_i * k_r
        d_r, d_i = _cmatmul((trh_ref[j], trl_ref[j]), (tih_ref[j], til_ref[j]), c_r, c_i)
        dr_ref[0, j] = d_r
        di_ref[0, j] = d_i


def col_stage(g_tabs, gt_tabs, ar, ai, kr=None, ki=None):
    P, n1, n2, C = ar.shape
    kb = min(FFT_K1_PER_STEP, n1)
    g_spec = pl.BlockSpec((kb, n2, n2), lambda p, j: (j, 0, 0))
    a_spec = pl.BlockSpec((1, kb, n2, C), lambda p, j: (p, j, 0, 0))
    k_spec = pl.BlockSpec((1, kb, n2, C), lambda p, j: (0, j, 0, 0))
    common = dict(
        grid=(P, n1 // kb),
        out_specs=[a_spec, a_spec],
        out_shape=[jax.ShapeDtypeStruct((P, n1, n2, C), f32)] * 2,
        compiler_params=pltpu.CompilerParams(dimension_semantics=("parallel", "parallel"),
                                             vmem_limit_bytes=VMEM_LIMIT_BYTES),
    )
    if kr is None:
        return pl.pallas_call(functools.partial(_col_dft_kernel, k1_per_step=kb),
                              in_specs=[g_spec] * 4 + [a_spec] * 2, name="hyena_col_dft", **common)(*g_tabs, ar, ai)
    return pl.pallas_call(functools.partial(_col_conv_kernel, k1_per_step=kb),
                          in_specs=[g_spec] * 8 + [a_spec] * 2 + [k_spec] * 2, name="hyena_col_conv", **common)(
        *g_tabs, *gt_tabs, ar, ai, kr, ki)


def _bf16_tables(m):
    out = []
    for part in (np.real(m), np.imag(m)):
        x = jnp.asarray(part, f32)
        hi = x.astype(bf16)
        out += [hi, (x - hi.astype(f32)).astype(bf16)]
    return out


def hyena_dft_tables(L):
    n2 = FFT_N2
    n1 = 2 * L // n2
    N = n1 * n2
    a = np.arange(n1)
    fa = np.exp(-2j * np.pi * np.outer(a, a) / n1)
    k1 = np.arange(n1)[:, None, None]
    k2 = np.arange(n2)[None, :, None]
    nn = np.arange(n2)[None, None, :]
    g = np.exp(-2j * np.pi * nn * (k1 + n1 * k2) / N)
    gt = np.conj(np.transpose(g, (0, 2, 1)))
    fc = np.conj(fa).T / N
    return dict(fa=_bf16_tables(fa), fa_half=_bf16_tables(fa[:, :n1 // 2]), g=_bf16_tables(g), gt=_bf16_tables(gt),
                fc_half=_bf16_tables(fc[:n1 // 2]))


def hyena_long_conv_gate(tabs, z, x0, k2, skip):
    Bn, L, C = z.shape
    n2 = FFT_N2
    n1 = 2 * L // n2
    P = Bn // 2
    W = n2 * C
    ar, ai = row_dft(tabs['fa'], k2.reshape(1, n1, W), packed=False)
    kr, ki = col_stage(tabs['g'], tabs['gt'], ar.reshape(1, n1, n2, C), ai.reshape(1, n1, n2, C))
    zv = z.reshape(Bn, n1 // 2, W)
    ar, ai = row_dft(tabs['fa_half'], zv, packed=True)
    dr, di = col_stage(tabs['g'], tabs['gt'], ar.reshape(P, n1, n2, C), ai.reshape(P, n1, n2, C), kr, ki)
    out = row_idft_gate(tabs['fc_half'], dr.reshape(P, n1, W), di.reshape(P, n1, W), zv, x0.reshape(Bn, n1 // 2, W),
                        jnp.tile(skip.astype(f32), n2).reshape(1, W))
    return out.reshape(Bn, L, C)


def _hyena_prep_kernel(*refs, rows, nt):
    cur, prev, nxt = refs[0:3], refs[3:6], refs[6:9]
    w_ref, b_ref, z_ref, x0_ref = refs[9:]
    t = pl.program_id(1)
    has_prev = (t > 0).astype(f32)
    has_next = (t < nt - 1).astype(f32)
    ridx = lax.broadcasted_iota(jnp.int32, (rows, HY_D), 0)

    def conv(j):
        u = cur[j][0]
        prev_row = prev[j][0, SUBLANES - 1:SUBLANES, :] * has_prev
        next_row = nxt[j][0, 0:1, :] * has_next
        up = jnp.where(ridx == 0, prev_row, pltpu.roll(u, 1, axis=0))
        un = jnp.where(ridx == rows - 1, next_row, pltpu.roll(u, rows - 1, axis=0))
        sl = slice(j * HY_D, (j + 1) * HY_D)
        return up * w_ref[0:1, sl] + u * w_ref[1:2, sl] + un * w_ref[2:3, sl] + b_ref[:, sl]

    x0_ref[0] = conv(0)
    z_ref[0] = conv(2) * conv(1)


def hyena_prep(proj, lane0, conv_w, conv_b):
    Bn, L, _ = proj.shape
    rows = RW_ROWS_PER_STEP
    nt = L // rows
    hb = rows // SUBLANES
    b0 = lane0 // HY_D
    assert b0 * HY_D == lane0
    cur = lambda j: pl.BlockSpec((1, rows, HY_D), lambda b, t: (b, t, b0 + j))
    prev = lambda j: pl.BlockSpec((1, SUBLANES, HY_D), lambda b, t: (b, jnp.maximum(t * hb - 1, 0), b0 + j))
    nxt = lambda j: pl.BlockSpec((1, SUBLANES, HY_D), lambda b, t: (b, jnp.minimum((t + 1) * hb, L // SUBLANES - 1), b0 + j))
    full = lambda shp: pl.BlockSpec(shp, lambda b, t: (0,) * len(shp))
    out = pl.BlockSpec((1, rows, HY_D), lambda b, t: (b, t, 0))
    return pl.pallas_call(
        functools.partial(_hyena_prep_kernel, rows=rows, nt=nt),
        grid=(Bn, nt),
        in_specs=[cur(0), cur(1), cur(2), prev(0), prev(1), prev(2), nxt(0), nxt(1), nxt(2),
                  full((3, 3 * HY_D)), full((1, 3 * HY_D))],
        out_specs=[out, out],
        out_shape=[jax.ShapeDtypeStruct((Bn, L, HY_D), f32)] * 2,
        compiler_params=pltpu.CompilerParams(dimension_semantics=("parallel", "parallel")),
        name="hyena_prep",
    )(*([proj] * 9), conv_w.astype(f32), conv_b.astype(f32).reshape(1, 3 * HY_D))


def hyena_mixer(tabs, proj, lane0, conv_w, conv_b, filt, skip):
    z, x0 = hyena_prep(proj, lane0, conv_w, conv_b)
    k2 = jnp.concatenate([filt[:1, 0] + filt[:1, 1], filt[1:, 0],
                          jnp.zeros((1, HY_D), f32), filt[1:, 1][::-1]], axis=0)
    return hyena_long_conv_gate(tabs, z, x0, k2, skip)


def rwkv7_mixer(proj, lora_lane0, mu, w0, w2, a0, a2, k_k, k_a, r_k, g2, ln_w, ln_b):
    r, k, v, kk, lw2, a_both = rw_prep(proj, lora_lane0, mu, w0, w2, a0, a2, k_k)
    y2 = wkv7_chunked(r, k, v, kk, lw2, a_both, k_a.astype(f32).reshape(1, RW_D))
    return rw_post(y2, r, k, v, a_both, proj, lora_lane0, k_a, r_k, ln_w, ln_b, g2)


def hier_moe(h, logits, bg, be, w1, w3, w2, layer):
    N, Dm = h.shape
    assert MOE_TOPK == 2
    g_logits = logits[:, :MOE_GROUPS] + bg.astype(f32)
    g_sel = jnp.argmax(g_logits, axis=-1)
    g_prob = jnp.take_along_axis(jax.nn.softmax(g_logits, axis=-1), g_sel[:, None], axis=-1)
    e_logits = (logits[:, MOE_GROUPS:MOE_GROUPS + MOE_EXPERTS] + be.astype(f32)).reshape(N, MOE_GROUPS, MOE_PER_GROUP)
    e_logits = jnp.take_along_axis(e_logits, g_sel[:, None, None], axis=1)[:, 0]
    top_val, top_idx = lax.top_k(e_logits, MOE_TOPK)
    gate = g_prob * jax.nn.softmax(top_val, axis=-1)
    expert = g_sel[:, None] * MOE_PER_GROUP + top_idx
    M = N * MOE_TOPK
    flat_e = expert.reshape(M).astype(jnp.int32)
    experts = jnp.arange(MOE_EXPERTS, dtype=jnp.int32)
    counts = jnp.sum((flat_e[:, None] == experts[None, :]).astype(jnp.int32), axis=0)
    padded = (counts + MOE_BLOCK - 1) // MOE_BLOCK * MOE_BLOCK
    pad_end = jnp.cumsum(padded)
    n_blocks = -(-M // MOE_BLOCK) + MOE_EXPERTS
    n_rows = n_blocks * MOE_BLOCK
    cum_need = jnp.cumsum(padded - counts)
    filler = jnp.arange(n_rows - M, dtype=jnp.int32)
    filler_e = jnp.sum((cum_need[None, :] <= filler[:, None]).astype(jnp.int32), axis=1)
    keys = jnp.concatenate([2 * flat_e, 2 * filler_e + 1])
    ids = jnp.concatenate([jnp.arange(M, dtype=jnp.int32), jnp.full((n_rows - M,), M, jnp.int32)])
    _, slot_src = lax.sort((keys, ids), num_keys=1)
    tok_src = jnp.where(slot_src < M, slot_src // MOE_TOPK, jnp.arange(n_rows, dtype=jnp.int32) % N)
    block_start = jnp.arange(n_blocks, dtype=pad_end.dtype) * MOE_BLOCK
    block_e = jnp.minimum(jnp.sum(pad_end[None, :] <= block_start[:, None], axis=1), MOE_EXPERTS - 1)
    y = moe_expert_ffn(h[tok_src], block_e, w1, w3, w2, layer)
    _, row_of = lax.sort((slot_src, jnp.arange(n_rows, dtype=jnp.int32)), num_keys=1)
    pos = row_of[:M].reshape(N, MOE_TOPK)
    return y[pos[:, 0]], y[pos[:, 1]], gate[:, 0:1], gate[:, 1:2]


def kernel(x, norm1_g, w_in, hy_conv_w, hy_conv_b, hy_w1, hy_b1, hy_w2, hy_b2, hy_w3, hy_b3, hy_wout, hy_freq, hy_skip, rw_mu, rw_w0, rw_w2, rw_a0, rw_a2, rw_kk, rw_ka, rw_rk, rw_g2, rw_ln_w, rw_ln_b, na_rpb, w_out, norm2_g, moe_wg, moe_bg, moe_we, moe_be, moe_w1, moe_w3, moe_w2, norm_f_g):
    Bn, L, _ = x.shape
    z_pos, t_pos = hyena_positional_features(L)
    dft_tabs = hyena_dft_tables(L)
    splits = np.cumsum(IN_SIZES)[:-1].tolist()
    hy_end, rkv_end = splits[0], splits[1]
    hy_lane0 = rkv_end - hy_end
    lora_lane0, na_lane0 = splits[1], splits[4]
    assert RW_D % LANES == 0 and lora_lane0 % LANES == 0 and na_lane0 % LANES == 0
    N = Bn * L
    x2 = x.reshape(N, D_MODEL)
    moe = None
    for l in range(DEPTH):
        w_in_l = jnp.concatenate([w_in[l][:, hy_end:rkv_end], w_in[l][:, :hy_end], w_in[l][:, rkv_end:]], axis=1)
        x2, proj = norm_proj(x2, norm1_g[l], w_in_l, moe)
        proj = proj.reshape(Bn, L, IN_D)
        filt = hyena_filters(z_pos, t_pos, hy_w1[l], hy_b1[l], hy_w2[l], hy_b2[l],
                             hy_w3[l], hy_b3[l], hy_wout[l], hy_freq[l])
        y_hy = hyena_mixer(dft_tabs, proj, hy_lane0, hy_conv_w[l], hy_conv_b[l], filt, hy_skip[l])
        y_rw = rwkv7_mixer(proj, lora_lane0, rw_mu[l], rw_w0[l], rw_w2[l], rw_a0[l],
                           rw_a2[l], rw_kk[l], rw_ka[l], rw_rk[l].reshape(RW_D), rw_g2[l], rw_ln_w[l], rw_ln_b[l])
        y_na = neighborhood_attention_pallas(proj, na_rpb[l], na_lane0 // LANES)
        x2, h2, logits = out_proj_norm_router(
            x2, y_hy.reshape(N, HY_D), y_rw.reshape(N, RW_D), y_na.reshape(N, NA_D), w_out[l], norm2_g[l],
            jnp.concatenate([moe_wg[l], moe_we[l]], axis=1))
        moe = hier_moe(h2, logits, moe_bg[l], moe_be[l], moe_w1, moe_w3, moe_w2, l)
    return final_rmsnorm(x2, norm_f_g, moe).reshape(Bn, L, D_MODEL)
```

```python
import functools
import math

import jax
import jax.numpy as jnp
import numpy as np
from jax import lax
from jax.experimental import pallas as pl
from jax.experimental.pallas import tpu as pltpu

f32 = jnp.float32
bf16 = jnp.bfloat16

D_MODEL = 1024
DEPTH = 2
GRID_W = 64
NORM_EPS = 1e-6
NEG_INF = -1e30

HY_D = D_MODEL // 4
HY_EMB = 33
HY_BANDS = (HY_EMB - 1) // 2
HY_FFN = 64
HY_MIN_DECAY = math.log(1e-2) / 1.5
HY_MAX_DECAY = math.log(1e-2) / 0.3

RW_N = 64
RW_D = D_MODEL // 2
RW_H = RW_D // RW_N
RW_W_LORA = 64
RW_A_LORA = 64
RW_G_LORA = 128
RW_GN_EPS = 64e-5

NA_HD = 64
NA_D = D_MODEL // 4
NA_H = NA_D // NA_HD
NA_KR = 8
NA_KC = 16

MIX_D = HY_D + RW_D + NA_D
IN_SIZES = (3 * HY_D, 3 * RW_D, RW_G_LORA, 2 * RW_W_LORA, 2 * RW_A_LORA, 3 * NA_D)
IN_D = sum(IN_SIZES)

MOE_GROUPS = 4
MOE_PER_GROUP = 8
MOE_EXPERTS = MOE_GROUPS * MOE_PER_GROUP
MOE_TOPK = 2
MOE_FF = 512
MOE_BLOCK = 512

LANES = 128
WKV_CHUNK = 64
WKV_TIME_BLOCK = 512
NA_ROWS_PER_STEP = 16
RW_ROWS_PER_STEP = 512
SUBLANES = 8
PROJ_ROWS_PER_STEP = 512
FFT_N2 = 128
FFT_LANE_BLOCK = 4096
FFT_K1_PER_STEP = 8
VMEM_LIMIT_BYTES = 48 * 1024 * 1024

_DN = {'nn': (((1,), (0,)), ((), ())), 'nt': (((1,), (1,)), ((), ())), 'tn': (((0,), (0,)), ((), ()))}


def _mm(a, b, dims='nn'):
    return lax.dot_general(a.astype(bf16), b.astype(bf16), _DN[dims], preferred_element_type=f32)


def _wkv_kernel(r_ref, k_ref, v_ref, kk_ref, lw_ref, a_ref, ka_ref, y_ref, ht_ref, *, tb, batch):
    T = WKV_CHUNK
    H2 = 2 * T
    nc = tb // T
    npairs = r_ref.shape[-1] // LANES
    mm = _mm

    @pl.when(pl.program_id(1) == 0)
    def _():
        ht_ref[...] = jnp.zeros_like(ht_ref)

    d = pl.program_id(0) // batch
    sign = 1 - 2 * d
    trow = lax.broadcasted_iota(jnp.int32, (T, H2), 0)
    lane = lax.broadcasted_iota(jnp.int32, (T, H2), 1)
    scol = lane % T
    tdiff = (trow - scol) * sign
    strict = tdiff > 0
    incl = tdiff >= 0
    same16 = (trow // 16) == (scol // 16)
    same32 = (trow // 32) == (scol // 32)
    off32 = same32 & jnp.logical_not(same16)
    off64 = jnp.logical_not(same32)
    eye2 = jnp.where(trow == scol, 1.0, 0.0).astype(f32)
    lo_lane = lane < T
    brow = lax.broadcasted_iota(jnp.int32, (H2, H2), 0)
    bcol = lax.broadcasted_iota(jnp.int32, (H2, H2), 1)
    same_head = (brow // T) == (bcol // T)
    crow = lax.broadcasted_iota(jnp.int32, (T, T), 0)
    ccol = lax.broadcasted_iota(jnp.int32, (T, T), 1)
    tri = jnp.where((crow - ccol) * sign >= 0, 1.0, 0.0).astype(bf16)
    is_bwd = d == 1

    def bd(x):
        return jnp.concatenate([jnp.where(lo_lane, x, 0.0), jnp.where(lo_lane, 0.0, x)], axis=0)

    inst = [(s, p) for s in range(nc) for p in range(npairs)]
    offs = [pl.multiple_of((s + d * (nc - 1 - 2 * s)) * T, T) for s in range(nc)]

    def load(ref, s, p):
        return ref[0, pl.ds(offs[s], T), p * LANES:(p + 1) * LANES]

    cs_l = []
    for s, p in inst:
        lw = load(lw_ref, s, p)
        l1 = lw.astype(bf16)
        l2 = (lw - l1.astype(f32)).astype(bf16)
        dd = lambda x: lax.dot_general(tri, x, _DN['nn'], preferred_element_type=f32)
        cs_l.append((dd(l1) + dd(l2), lw))
    ops = []
    for (s, p), (cs, lw) in zip(inst, cs_l):
        r = load(r_ref, s, p)
        k = load(k_ref, s, p)
        v = load(v_ref, s, p)
        kk = load(kk_ref, s, p)
        a = load(a_ref, s, p)
        ka = ka_ref[:, p * LANES:(p + 1) * LANES]
        kd = k * (1.0 + (a - 1.0) * ka)
        b = kk * a
        cs_end = jnp.where(is_bwd, cs[0:1, :], cs[T - 1:T, :])
        em = jnp.exp(-cs)
        e_end = jnp.exp(cs_end - cs)
        ops.append(dict(
            AR=jnp.concatenate([-kk * jnp.exp(cs - lw), r * jnp.exp(cs)], axis=0),
            BK=jnp.concatenate([bd(b * em), bd(kd * em)], axis=0),
            V=v, BKg=jnp.concatenate([b * e_end, kd * e_end], axis=0), g_end=jnp.exp(cs_end)))
    for o in ops:
        S = mm(o['AR'], o['BK'], 'nt')
        N = jnp.where(strict, S[:T, :H2], 0.0)
        o['AakArk'] = jnp.concatenate([jnp.where(strict, S[:T, H2:], 0.0), jnp.where(incl, S[T:, H2:], 0.0)], axis=0)
        o['Arb'] = jnp.where(incl, S[T:, :H2], 0.0)
        o['Nd'] = jnp.where(same16, N, 0.0)
        o['N32'] = jnp.where(off32, N, 0.0)
        o['N64'] = jnp.where(off64, N, 0.0)
    for o in ops:
        o['X'] = eye2 + o['Nd']
        o['P'] = mm(o['Nd'], bd(o['Nd']))
    for it in range(3):
        if it < 2:
            for o in ops:
                px = mm(o['P'], jnp.concatenate([bd(o['X']), bd(o['P'])], axis=1))
                o['X'] = o['X'] + px[:, :H2]
                o['P'] = px[:, H2:]
        else:
            for o in ops:
                o['X'] = o['X'] + mm(o['P'], bd(o['X']))
    for key in ('N32', 'N64'):
        for o in ops:
            o['Z'] = mm(o[key], bd(o['X']))
        for o in ops:
            o['X'] = o['X'] + mm(o['X'], bd(o['Z']))
    for o in ops:
        wy = mm(o['AakArk'], bd(o['V']))
        o['W0'] = wy[:T]
        o['Yv'] = wy[T:]
    for o in ops:
        o['XAW'] = mm(o['X'], jnp.concatenate([bd(o['AR'][:T]), bd(o['W0'])], axis=1))
    for o in ops:
        ax = mm(o['Arb'], jnp.concatenate([bd(o['XAW'][:, :H2]), bd(o['XAW'][:, H2:])], axis=1))
        o['Rhat'] = o['AR'][T:] + ax[:, :H2]
        o['Yc'] = ax[:, H2:] + o['Yv']
    for o in ops:
        lhs = jnp.concatenate([o['XAW'], jnp.concatenate([jnp.zeros((T, H2), f32), o['V']], axis=1)], axis=0)
        pq = mm(lhs, o['BKg'], 'tn')
        o['Pc'] = jnp.where(same_head, pq[:H2], 0.0)
        o['Qc'] = jnp.where(same_head, pq[H2:], 0.0)
    hts = [ht_ref[p] for p in range(npairs)]
    for s in range(nc):
        cur = [ops[s * npairs + p] for p in range(npairs)]
        ys = [mm(o['Rhat'], hts[p], 'nt') + o['Yc'] for p, o in enumerate(cur)]
        hts = [hts[p] * o['g_end'] + (mm(hts[p], o['Pc']) + o['Qc']) for p, o in enumerate(cur)]
        for p in range(npairs):
            y_ref[0, pl.ds(offs[s], T), p * LANES:(p + 1) * LANES] = ys[p]
    for p in range(npairs):
        ht_ref[p] = hts[p]


def wkv7_chunked(r, k, v, kk, lw2, a2, ka):
    Bn, L, C = r.shape
    tb = WKV_TIME_BLOCK
    nt = L // tb
    tmap = lambda i, t: t + (i // Bn) * (nt - 1 - 2 * t)
    shared = pl.BlockSpec((1, tb, C), lambda i, t: (i % Bn, tmap(i, t), 0))
    per_dir = pl.BlockSpec((1, tb, C), lambda i, t: (i % Bn, tmap(i, t), i // Bn))
    return pl.pallas_call(
        functools.partial(_wkv_kernel, tb=tb, batch=Bn),
        grid=(2 * Bn, nt),
        in_specs=[shared, shared, shared, shared, per_dir, per_dir, pl.BlockSpec((1, C), lambda i, t: (0, 0))],
        out_specs=pl.BlockSpec((1, tb, C), lambda i, t: (i, tmap(i, t), 0)),
        out_shape=jax.ShapeDtypeStruct((2 * Bn, L, C), f32),
        scratch_shapes=[pltpu.VMEM((C // LANES, LANES, LANES), f32)],
        compiler_params=pltpu.CompilerParams(dimension_semantics=("parallel", "arbitrary"),
                                             vmem_limit_bytes=VMEM_LIMIT_BYTES),
        name="wkv7_chunked",
    )(r, k, v, kk, lw2, a2, ka)


def _mm_exact_rhs(a, b_bf16):
    ah = a.astype(bf16)
    al = (a - ah.astype(f32)).astype(bf16)
    d = lambda x: lax.dot_general(x, b_bf16, _DN['nn'], preferred_element_type=f32)
    return d(ah) + d(al)


def _head_sum_matrix(scale):
    i = lax.broadcasted_iota(jnp.int32, (RW_D, RW_D), 0) // RW_N
    j = lax.broadcasted_iota(jnp.int32, (RW_D, RW_D), 1) // RW_N
    return jnp.where(i == j, scale, 0.0).astype(bf16)


def _rw_prep_kernel(r_ref, k_ref, v_ref, rp_ref, kp_ref, vp_ref, rn_ref, kn_ref, vn_ref, w_ref, a_ref,
                    mu_ref, w0_ref, w2_ref, a0_ref, a2_ref, kkw_ref,
                    ro_ref, ko_ref, vo_ref, kko_ref, lw_ref, ao_ref, *, rows, nt):
    t = pl.program_id(1)
    has_prev = (t > 0).astype(f32)
    has_next = (t < nt - 1).astype(f32)
    ridx = lax.broadcasted_iota(jnp.int32, (rows, RW_D), 0)

    def tshift(cur_ref, prev_ref, next_ref, j):
        u = cur_ref[0]
        prev_row = prev_ref[0, SUBLANES - 1:SUBLANES, :] * has_prev
        next_row = next_ref[0, 0:1, :] * has_next
        up = jnp.where(ridx == 0, prev_row, pltpu.roll(u, 1, axis=0))
        un = jnp.where(ridx == rows - 1, next_row, pltpu.roll(u, rows - 1, axis=0))
        return u + mu_ref[j, 0:1, :] * (up - u) + mu_ref[j, 1:2, :] * (un - u)

    r = tshift(r_ref, rp_ref, rn_ref, 0)
    k = tshift(k_ref, kp_ref, kn_ref, 1)
    v = tshift(v_ref, vp_ref, vn_ref, 2)
    ro_ref[0] = r
    ko_ref[0] = k
    vo_ref[0] = v
    kk = k * kkw_ref[...]
    ss = _mm_exact_rhs(kk * kk, _head_sum_matrix(1.0))
    kko_ref[0] = kk * lax.rsqrt(jnp.maximum(ss, 1e-24))
    wl = jnp.tanh(w_ref[0])
    al = a_ref[0]
    for d in range(2):
        wpre = w0_ref[d:d + 1, :] + _mm(wl[:, d * RW_W_LORA:(d + 1) * RW_W_LORA], w2_ref[d])
        lw_ref[0, :, d * RW_D:(d + 1) * RW_D] = -math.exp(-0.5) * jax.nn.sigmoid(wpre)
        av = a0_ref[d:d + 1, :] + _mm(al[:, d * RW_A_LORA:(d + 1) * RW_A_LORA], a2_ref[d])
        ao_ref[0, :, d * RW_D:(d + 1) * RW_D] = jax.nn.sigmoid(av)


def rw_prep(proj, lora_lane0, mu, w0, w2, a0, a2, k_k):
    Bn, L, _ = proj.shape
    C = RW_D
    rows = RW_ROWS_PER_STEP
    nt = L // rows
    hb = rows // SUBLANES
    lb = lora_lane0 // LANES
    cur = lambda j: pl.BlockSpec((1, rows, C), lambda b, t: (b, t, j))
    prev = lambda j: pl.BlockSpec((1, SUBLANES, C), lambda b, t: (b, jnp.maximum(t * hb - 1, 0), j))
    nxt = lambda j: pl.BlockSpec((1, SUBLANES, C), lambda b, t: (b, jnp.minimum((t + 1) * hb, L // SUBLANES - 1), j))
    lora_w = pl.BlockSpec((1, rows, LANES), lambda b, t: (b, t, lb + 1))
    lora_a = pl.BlockSpec((1, rows, LANES), lambda b, t: (b, t, lb + 2))
    full = lambda shp: pl.BlockSpec(shp, lambda b, t: (0,) * len(shp))
    out_c = pl.BlockSpec((1, rows, C), lambda b, t: (b, t, 0))
    out_2c = pl.BlockSpec((1, rows, 2 * C), lambda b, t: (b, t, 0))
    sds = lambda c: jax.ShapeDtypeStruct((Bn, L, c), f32)
    return pl.pallas_call(
        functools.partial(_rw_prep_kernel, rows=rows, nt=nt),
        grid=(Bn, nt),
        in_specs=[cur(0), cur(1), cur(2), prev(0), prev(1), prev(2), nxt(0), nxt(1), nxt(2), lora_w, lora_a,
                  full((3, 2, C)), full((2, C)), full((2, RW_W_LORA, C)), full((2, C)), full((2, RW_A_LORA, C)),
                  full((1, C))],
        out_specs=[out_c, out_c, out_c, out_c, out_2c, out_2c],
        out_shape=[sds(C), sds(C), sds(C), sds(C), sds(2 * C), sds(2 * C)],
        compiler_params=pltpu.CompilerParams(dimension_semantics=("parallel", "parallel"),
                                             vmem_limit_bytes=VMEM_LIMIT_BYTES),
        name="rwkv_prep",
    )(proj, proj, proj, proj, proj, proj, proj, proj, proj, proj, proj,
      mu.astype(f32), w0.astype(f32), w2.astype(bf16), a0.astype(f32), a2.astype(bf16), k_k.astype(f32).reshape(1, C))


def _rw_post_kernel(yf_ref, yb_ref, r_ref, k_ref, v_ref, a_ref, g_ref, ka_ref, rk_ref, lnw_ref, lnb_ref, g2_ref, o_ref):
    C = RW_D
    y = yf_ref[0] + yb_ref[0]
    avg = _head_sum_matrix(1.0 / RW_N)
    mean = _mm_exact_rhs(y, avg)
    yc = y - mean
    var = _mm_exact_rhs(yc * yc, avg)
    yn = yc * lax.rsqrt(var + RW_GN_EPS) * lnw_ref[...] + lnb_ref[...]
    a = a_ref[0]
    k = k_ref[0]
    ka = ka_ref[...]
    ksum = k * (1.0 + (a[:, :C] - 1.0) * ka) + k * (1.0 + (a[:, C:] - 1.0) * ka)
    coef = _mm_exact_rhs(r_ref[0] * ksum * rk_ref[...], _head_sum_matrix(1.0))
    gate = _mm(jax.nn.sigmoid(g_ref[0]), g2_ref[...])
    o_ref[0] = ((yn + coef * v_ref[0]) * gate).astype(o_ref.dtype)


def rw_post(y2, r, k, v, a2, proj, g_lane0, k_a, r_k, ln_w, ln_b, g2):
    Bn, L, C = r.shape
    rows = RW_ROWS_PER_STEP
    nt = L // rows
    gb = g_lane0 // LANES
    blk = lambda c: pl.BlockSpec((1, rows, c), lambda b, t: (b, t, 0))
    full = lambda shp: pl.BlockSpec(shp, lambda b, t: (0,) * len(shp))
    row = lambda x: x.astype(f32).reshape(1, C)
    return pl.pallas_call(
        _rw_post_kernel,
        grid=(Bn, nt),
        in_specs=[blk(C), pl.BlockSpec((1, rows, C), lambda b, t: (b + Bn, t, 0)), blk(C), blk(C), blk(C), blk(2 * C),
                  pl.BlockSpec((1, rows, LANES), lambda b, t: (b, t, gb)),
                  full((1, C)), full((1, C)), full((1, C)), full((1, C)), full((RW_G_LORA, C))],
        out_specs=blk(C),
        out_shape=jax.ShapeDtypeStruct((Bn, L, C), bf16),
        compiler_params=pltpu.CompilerParams(dimension_semantics=("parallel", "parallel"),
                                             vmem_limit_bytes=VMEM_LIMIT_BYTES),
        name="rwkv_post",
    )(y2, y2, r, k, v, a2, proj, row(k_a), row(r_k), row(ln_w), row(ln_b), g2.astype(bf16))


def _na_kernel(q_ref, k_ref, v_ref, bias_ref, o_ref, kb_ref, vb_ref, *, rows_per_step, n_rows, kr):
    W = GRID_W
    rb = pl.program_id(2)

    @pl.when(rb == 0)
    def _():
        kb_ref[...] = k_ref[0].astype(bf16)
        vb_ref[...] = v_ref[0].astype(bf16)

    lo_lane = lax.broadcasted_iota(jnp.int32, (W, LANES), 1) < NA_HD
    scale = NA_HD ** -0.5
    rows = []
    for j in range(rows_per_step):
        r = rb * rows_per_step + j
        start = jnp.clip(r - kr // 2, 0, n_rows - kr)
        rows.append((start, start - r + (NA_KR - 1)))
    s_list = []
    for j, (start, didx) in enumerate(rows):
        q = q_ref[0, j * W:(j + 1) * W, :] * scale
        qs = jnp.concatenate([jnp.where(lo_lane, q, 0.0), jnp.where(lo_lane, 0.0, q)], axis=0)
        kw = kb_ref[pl.ds(pl.multiple_of(start * W, W), kr * W), :]
        s_list.append(_mm(qs, kw, 'nt') + bias_ref[didx, 0])
    p_list = []
    for s in s_list:
        m = jnp.max(s, axis=-1, keepdims=True)
        p = jnp.exp(s - m)
        p_list.append((p, jnp.sum(p, axis=-1, keepdims=True)))
    for j, ((start, _), (p, l)) in enumerate(zip(rows, p_list)):
        vw = vb_ref[pl.ds(pl.multiple_of(start * W, W), kr * W), :]
        o = _mm(p, vw) / l
        o_ref[0, j * W:(j + 1) * W, :] = jnp.where(lo_lane, o[:W], o[W:]).astype(o_ref.dtype)


def na_bias_table(rpb, n_rows):
    W = GRID_W
    kr = min(NA_KR, n_rows)
    cols = jnp.arange(W)
    col_start = jnp.clip(cols - NA_KC // 2, 0, W - NA_KC)
    in_band = (cols[None, :] >= col_start[:, None]) & (cols[None, :] < col_start[:, None] + NA_KC)
    dc = jnp.clip(cols[None, :] - cols[:, None], -(NA_KC - 1), NA_KC - 1) + (NA_KC - 1)
    win = jnp.stack([rpb.astype(f32)[:, d:d + kr, :] for d in range(8)], axis=1)
    onehot = (dc[:, :, None] == jnp.arange(2 * NA_KC - 1)[None, None, :]).astype(f32)
    tab = jnp.einsum('hdic,qkc->hdqik', win, onehot, precision=lax.Precision.HIGHEST)
    tab = jnp.where(in_band[None, None, :, None, :], tab, NEG_INF)
    H = rpb.shape[0]
    tab = tab.reshape(H // 2, 2, 8, W, kr * W).transpose(2, 0, 1, 3, 4)
    return tab.reshape(8, H // 2, 2 * W, kr * W)


def neighborhood_attention_pallas(proj, rpb, lane_block0):
    Bn, L, _ = proj.shape
    W = GRID_W
    n_rows = L // W
    kr = min(NA_KR, n_rows)
    hp = NA_H * NA_HD // LANES
    rps = NA_ROWS_PER_STEP
    bias = na_bias_table(rpb, n_rows)
    kern = functools.partial(_na_kernel, rows_per_step=rps, n_rows=n_rows, kr=kr)
    return pl.pallas_call(
        kern,
        grid=(Bn, hp, n_rows // rps),
        in_specs=[pl.BlockSpec((1, rps * W, LANES), lambda b, h, r: (b, r, lane_block0 + h)),
                  pl.BlockSpec((1, L, LANES), lambda b, h, r: (b, 0, lane_block0 + hp + h)),
                  pl.BlockSpec((1, L, LANES), lambda b, h, r: (b, 0, lane_block0 + 2 * hp + h)),
                  pl.BlockSpec((8, 1, 2 * W, kr * W), lambda b, h, r: (0, h, 0, 0))],
        out_specs=pl.BlockSpec((1, rps * W, LANES), lambda b, h, r: (b, r, h)),
        out_shape=jax.ShapeDtypeStruct((Bn, L, NA_H * NA_HD), bf16),
        scratch_shapes=[pltpu.VMEM((L, LANES), bf16), pltpu.VMEM((L, LANES), bf16)],
        compiler_params=pltpu.CompilerParams(dimension_semantics=("parallel", "parallel", "arbitrary"),
                                             vmem_limit_bytes=VMEM_LIMIT_BYTES),
        name="neighborhood_attention",
    )(proj, proj, proj, bias)


def _moe_ffn_kernel(be_ref, x_ref, w1_ref, w3_ref, w2_ref, o_ref, w1b_ref, w3b_ref, w2b_ref):
    i = pl.program_id(0)

    @pl.when((i == 0) | (be_ref[i] != be_ref[jnp.maximum(i - 1, 0)]))
    def _():
        w1b_ref[...] = w1_ref[0, 0].astype(bf16)
        w3b_ref[...] = w3_ref[0, 0].astype(bf16)
        w2b_ref[...] = w2_ref[0, 0].astype(bf16)

    x = x_ref[...].astype(bf16)
    h1 = jnp.dot(x, w1b_ref[...], preferred_element_type=f32)
    h3 = jnp.dot(x, w3b_ref[...], preferred_element_type=f32)
    g = h1 * jax.nn.sigmoid(h1) * h3
    o_ref[...] = jnp.dot(g.astype(bf16), w2b_ref[...], preferred_element_type=f32).astype(o_ref.dtype)


def moe_expert_ffn(xs, block_e, w1, w3, w2, layer):
    rows, Dm = xs.shape
    n_blocks = rows // MOE_BLOCK
    grid_spec = pltpu.PrefetchScalarGridSpec(
        num_scalar_prefetch=1,
        grid=(n_blocks,),
        in_specs=[pl.BlockSpec((MOE_BLOCK, Dm), lambda i, be: (i, 0)),
                  pl.BlockSpec((1, 1, Dm, MOE_FF), lambda i, be: (layer, be[i], 0, 0)),
                  pl.BlockSpec((1, 1, Dm, MOE_FF), lambda i, be: (layer, be[i], 0, 0)),
                  pl.BlockSpec((1, 1, MOE_FF, Dm), lambda i, be: (layer, be[i], 0, 0))],
        out_specs=pl.BlockSpec((MOE_BLOCK, Dm), lambda i, be: (i, 0)),
        scratch_shapes=[pltpu.VMEM((Dm, MOE_FF), bf16), pltpu.VMEM((Dm, MOE_FF), bf16), pltpu.VMEM((MOE_FF, Dm), bf16)],
    )
    return pl.pallas_call(
        _moe_ffn_kernel,
        grid_spec=grid_spec,
        out_shape=jax.ShapeDtypeStruct((rows, Dm), bf16),
        compiler_params=pltpu.CompilerParams(dimension_semantics=("arbitrary",), vmem_limit_bytes=VMEM_LIMIT_BYTES),
        name="moe_expert_ffn",
    )(block_e.astype(jnp.int32), xs, w1.astype(f32), w3.astype(f32), w2.astype(f32))


def _rms(x):
    return x * lax.rsqrt(jnp.mean(x * x, axis=-1, keepdims=True) + NORM_EPS)


def _moe_combine(x_ref, y0_ref, y1_ref, g0_ref, g1_ref):
    return x_ref[...] + (g0_ref[...] * y0_ref[...].astype(f32) + g1_ref[...] * y1_ref[...].astype(f32))


def _moe_specs(tm, D):
    row = pl.BlockSpec((tm, D), lambda i: (i, 0))
    col = pl.BlockSpec((tm, 1), lambda i: (i, 0))
    return [row, row, col, col]


def _norm_proj_kernel(*refs, combine):
    if combine:
        x_ref, y0_ref, y1_ref, g0_ref, g1_ref, g_ref, w_ref, o_ref, xo_ref = refs
        x = _moe_combine(x_ref, y0_ref, y1_ref, g0_ref, g1_ref)
        xo_ref[...] = x
    else:
        x_ref, g_ref, w_ref, o_ref = refs
        x = x_ref[...]
    h = (_rms(x) * g_ref[...]).astype(bf16)
    o_ref[...] = jnp.dot(h, w_ref[...], preferred_element_type=f32)


def norm_proj(x2, g, w, moe=None):
    N, D = x2.shape
    F = w.shape[1]
    tm = PROJ_ROWS_PER_STEP
    row = pl.BlockSpec((tm, D), lambda i: (i, 0))
    out_row = pl.BlockSpec((tm, F), lambda i: (i, 0))
    par = [pl.BlockSpec((1, D), lambda i: (0, 0)), pl.BlockSpec((D, F), lambda i: (0, 0))]
    params = dict(compiler_params=pltpu.CompilerParams(dimension_semantics=("parallel",),
                                                       vmem_limit_bytes=VMEM_LIMIT_BYTES), name="norm_proj")
    gw = (g.astype(f32).reshape(1, D), w.astype(bf16))
    if moe is None:
        proj = pl.pallas_call(functools.partial(_norm_proj_kernel, combine=False), grid=(N // tm,),
                              in_specs=[row] + par, out_specs=out_row,
                              out_shape=jax.ShapeDtypeStruct((N, F), f32), **params)(x2, *gw)
        return x2, proj
    proj, x_new = pl.pallas_call(functools.partial(_norm_proj_kernel, combine=True), grid=(N // tm,),
                                 in_specs=[row] + _moe_specs(tm, D) + par, out_specs=[out_row, row],
                                 out_shape=[jax.ShapeDtypeStruct((N, F), f32), jax.ShapeDtypeStruct((N, D), f32)],
                                 **params)(x2, *moe, *gw)
    return x_new, proj


def _out_proj_kernel(x_ref, yh_ref, yr_ref, yn_ref, wh_ref, wr_ref, wn_ref, g_ref, wrt_ref, xo_ref, h_ref, lg_ref):
    mix = (jnp.dot(yh_ref[...].astype(bf16), wh_ref[...], preferred_element_type=f32)
           + jnp.dot(yr_ref[...].astype(bf16), wr_ref[...], preferred_element_type=f32)
           + jnp.dot(yn_ref[...].astype(bf16), wn_ref[...], preferred_element_type=f32))
    x = x_ref[...] + mix
    xo_ref[...] = x
    h = (_rms(x) * g_ref[...]).astype(bf16)
    h_ref[...] = h
    lg_ref[...] = jnp.dot(h, wrt_ref[...], preferred_element_type=f32)


def out_proj_norm_router(x2, y_hy, y_rw, y_na, w_out, g, w_router):
    N, D = x2.shape
    tm = PROJ_ROWS_PER_STEP
    d_hy, d_rw, d_na = y_hy.shape[1], y_rw.shape[1], y_na.shape[1]
    nr = w_router.shape[1]
    w_router = jnp.pad(w_router.astype(bf16), ((0, 0), (0, LANES - nr)))
    wb = w_out.astype(bf16)
    row = lambda c: pl.BlockSpec((tm, c), lambda i: (i, 0))
    full = lambda r, c: pl.BlockSpec((r, c), lambda i: (0, 0))
    return pl.pallas_call(
        _out_proj_kernel,
        grid=(N // tm,),
        in_specs=[row(D), row(d_hy), row(d_rw), row(d_na), full(d_hy, D), full(d_rw, D), full(d_na, D), full(1, D),
                  full(D, LANES)],
        out_specs=[row(D), row(D), row(LANES)],
        out_shape=[jax.ShapeDtypeStruct((N, D), f32), jax.ShapeDtypeStruct((N, D), bf16),
                   jax.ShapeDtypeStruct((N, LANES), f32)],
        compiler_params=pltpu.CompilerParams(dimension_semantics=("parallel",), vmem_limit_bytes=VMEM_LIMIT_BYTES),
        name="out_proj_norm_router",
    )(x2, y_hy, y_rw, y_na, wb[:d_hy], wb[d_hy:d_hy + d_rw], wb[d_hy + d_rw:], g.astype(f32).reshape(1, D), w_router)


def _final_norm_kernel(x_ref, y0_ref, y1_ref, g0_ref, g1_ref, g_ref, o_ref):
    o_ref[...] = _rms(_moe_combine(x_ref, y0_ref, y1_ref, g0_ref, g1_ref)) * g_ref[...]


def final_rmsnorm(x2, g, moe):
    N, D = x2.shape
    tm = PROJ_ROWS_PER_STEP
    row = pl.BlockSpec((tm, D), lambda i: (i, 0))
    return pl.pallas_call(
        _final_norm_kernel,
        grid=(N // tm,),
        in_specs=[row] + _moe_specs(tm, D) + [pl.BlockSpec((1, D), lambda i: (0, 0))],
        out_specs=row,
        out_shape=jax.ShapeDtypeStruct((N, D), f32),
        compiler_params=pltpu.CompilerParams(dimension_semantics=("parallel",), vmem_limit_bytes=VMEM_LIMIT_BYTES),
        name="final_rmsnorm",
    )(x2, *moe, g.astype(f32).reshape(1, D))


def hyena_positional_features(L):
    t = jnp.linspace(0.0, 1.0, L, dtype=f32)[:, None]
    w = (2.0 * math.pi / L) * jnp.arange(L, dtype=f32)[:, None]
    f = jnp.linspace(1e-4, HY_BANDS - 1, HY_BANDS, dtype=f32)[None, :]
    z = jnp.concatenate([t, jnp.cos(f * w), -jnp.sin(f * w)], axis=-1)
    return z, t


def hyena_filters(z, t, w1, b1, w2, b2, w3, b3, wout, freq):
    fr = freq.astype(f32)
    act = lambda u: jnp.sin(fr * u)
    h = act(z @ w1.astype(f32) + b1.astype(f32))
    h = act(h @ w2.astype(f32) + b2.astype(f32))
    h = act(h @ w3.astype(f32) + b3.astype(f32))
    h = (h @ wout.astype(f32)).reshape(-1, 2, HY_D)
    deltas = jnp.abs(jnp.linspace(HY_MIN_DECAY, HY_MAX_DECAY, HY_D, dtype=f32))
    h = h * jnp.exp(-t[:, :, None] * deltas)
    return h * lax.rsqrt(jnp.sum(h * h, axis=(0, 1), keepdims=True) + 1e-6)


def _split_bf16(x):
    hi = x.astype(bf16)
    return hi, (x - hi.astype(f32)).astype(bf16)


def _dot3(m_hi, m_lo, x_hi, x_lo):
    d = lambda a, b: lax.dot_general(a, b, _DN['nn'], preferred_element_type=f32)
    return d(m_hi, x_hi) + (d(m_hi, x_lo) + d(m_lo, x_hi))


def _cmatmul(mr, mi, xr, xi):
    xrh, xrl = _split_bf16(xr)
    rr = _dot3(mr[0], mr[1], xrh, xrl)
    ir = _dot3(mi[0], mi[1], xrh, xrl)
    if xi is None:
        return rr, ir
    xih, xil = _split_bf16(xi)
    ii = _dot3(mi[0], mi[1], xih, xil)
    ri = _dot3(mr[0], mr[1], xih, xil)
    return rr - ii, ri + ir


def _row_dft_kernel(mrh_ref, mrl_ref, mih_ref, mil_ref, *refs, real_input):
    if real_input:
        ur_ref, or_ref, oi_ref = refs
        ui = None
    else:
        ur_ref, ui_ref, or_ref, oi_ref = refs
        ui = ui_ref[0]
    o_r, o_i = _cmatmul((mrh_ref[...], mrl_ref[...]), (mih_ref[...], mil_ref[...]), ur_ref[0], ui)
    or_ref[0] = o_r
    oi_ref[0] = o_i


def row_dft(tabs, u, packed):
    rows, r_in, W = u.shape
    P = rows // 2 if packed else rows
    r_out = tabs[0].shape[0]
    wb = min(FFT_LANE_BLOCK, W)
    tab_spec = pl.BlockSpec((r_out, r_in), lambda p, j: (0, 0))
    re_spec = pl.BlockSpec((1, r_in, wb), lambda p, j: (p, 0, j))
    im_spec = pl.BlockSpec((1, r_in, wb), lambda p, j: (p + P, 0, j))
    out_spec = pl.BlockSpec((1, r_out, wb), lambda p, j: (p, 0, j))
    ins = [u, u] if packed else [u]
    return pl.pallas_call(
        functools.partial(_row_dft_kernel, real_input=not packed),
        grid=(P, W // wb),
        in_specs=[tab_spec] * 4 + ([re_spec, im_spec] if packed else [re_spec]),
        out_specs=[out_spec, out_spec],
        out_shape=[jax.ShapeDtypeStruct((P, r_out, W), f32)] * 2,
        compiler_params=pltpu.CompilerParams(dimension_semantics=("parallel", "parallel"),
                                             vmem_limit_bytes=VMEM_LIMIT_BYTES),
        name="hyena_row_dft",
    )(*tabs, *ins)


def _row_idft_gate_kernel(mrh_ref, mrl_ref, mih_ref, mil_ref, dr_ref, di_ref, z0_ref, z1_ref, x0_ref, x1_ref,
                          skip_ref, o0_ref, o1_ref):
    y_r, y_i = _cmatmul((mrh_ref[...], mrl_ref[...]), (mih_ref[...], mil_ref[...]), dr_ref[0], di_ref[0])
    skip = skip_ref[...]
    o0_ref[0] = (x0_ref[0] * (y_r + z0_ref[0] * skip)).astype(o0_ref.dtype)
    o1_ref[0] = (x1_ref[0] * (y_i + z1_ref[0] * skip)).astype(o1_ref.dtype)


def row_idft_gate(tabs, dr, di, z, x0, skip_row):
    P, r_in, W = dr.shape
    r_out = tabs[0].shape[0]
    wb = min(FFT_LANE_BLOCK, W)
    tab_spec = pl.BlockSpec((r_out, r_in), lambda p, j: (0, 0))
    d_spec = pl.BlockSpec((1, r_in, wb), lambda p, j: (p, 0, j))
    lo = pl.BlockSpec((1, r_out, wb), lambda p, j: (p, 0, j))
    hi = pl.BlockSpec((1, r_out, wb), lambda p, j: (p + P, 0, j))
    o0, o1 = pl.pallas_call(
        _row_idft_gate_kernel,
        grid=(P, W // wb),
        in_specs=[tab_spec] * 4 + [d_spec, d_spec, lo, hi, lo, hi, pl.BlockSpec((1, wb), lambda p, j: (0, j))],
        out_specs=[lo, lo],
        out_shape=[jax.ShapeDtypeStruct((P, r_out, W), bf16)] * 2,
        compiler_params=pltpu.CompilerParams(dimension_semantics=("parallel", "parallel"),
                                             vmem_limit_bytes=VMEM_LIMIT_BYTES),
        name="hyena_row_idft_gate",
    )(*tabs, dr, di, z, z, x0, x0, skip_row)
    return jnp.concatenate([o0, o1], axis=0)


def _col_dft_kernel(grh_ref, grl_ref, gih_ref, gil_ref, ar_ref, ai_ref, br_ref, bi_ref, *, k1_per_step):
    for j in range(k1_per_step):
        b_r, b_i = _cmatmul((grh_ref[j], grl_ref[j]), (gih_ref[j], gil_ref[j]), ar_ref[0, j], ai_ref[0, j])
        br_ref[0, j] = b_r
        bi_ref[0, j] = b_i


def _col_conv_kernel(grh_ref, grl_ref, gih_ref, gil_ref, trh_ref, trl_ref, tih_ref, til_ref,
                     ar_ref, ai_ref, kr_ref, ki_ref, dr_ref, di_ref, *, k1_per_step):
    for j in range(k1_per_step):
        b_r, b_i = _cmatmul((grh_ref[j], grl_ref[j]), (gih_ref[j], gil_ref[j]), ar_ref[0, j], ai_ref[0, j])
        k_r = kr_ref[0, j]
        k_i = ki_ref[0, j]
        c_r = b_r * k_r - b_i * k_i
        c_i = b_r * k_i + b_i * k_r
        d_r, d_i = _cmatmul((trh_ref[j], trl_ref[j]), (tih_ref[j], til_ref[j]), c_r, c_i)
        dr_ref[0, j] = d_r
        di_ref[0, j] = d_i


def col_stage(g_tabs, gt_tabs, ar, ai, kr=None, ki=None):
    P, n1, n2, C = ar.shape
    kb = min(FFT_K1_PER_STEP, n1)
    g_spec = pl.BlockSpec((kb, n2, n2), lambda p, j: (j, 0, 0))
    a_spec = pl.BlockSpec((1, kb, n2, C), lambda p, j: (p, j, 0, 0))
    k_spec = pl.BlockSpec((1, kb, n2, C), lambda p, j: (0, j, 0, 0))
    common = dict(
        grid=(P, n1 // kb),
        out_specs=[a_spec, a_spec],
        out_shape=[jax.ShapeDtypeStruct((P, n1, n2, C), f32)] * 2,
        compiler_params=pltpu.CompilerParams(dimension_semantics=("parallel", "parallel"),
                                             vmem_limit_bytes=VMEM_LIMIT_BYTES),
    )
    if kr is None:
        return pl.pallas_call(functools.partial(_col_dft_kernel, k1_per_step=kb),
                              in_specs=[g_spec] * 4 + [a_spec] * 2, name="hyena_col_dft", **common)(*g_tabs, ar, ai)
    return pl.pallas_call(functools.partial(_col_conv_kernel, k1_per_step=kb),
                          in_specs=[g_spec] * 8 + [a_spec] * 2 + [k_spec] * 2, name="hyena_col_conv", **common)(
        *g_tabs, *gt_tabs, ar, ai, kr, ki)


def _bf16_tables(m):
    out = []
    for part in (np.real(m), np.imag(m)):
        x = jnp.asarray(part, f32)
        hi = x.astype(bf16)
        out += [hi, (x - hi.astype(f32)).astype(bf16)]
    return out


def hyena_dft_tables(L):
    n2 = FFT_N2
    n1 = 2 * L // n2
    N = n1 * n2
    a = np.arange(n1)
    fa = np.exp(-2j * np.pi * np.outer(a, a) / n1)
    k1 = np.arange(n1)[:, None, None]
    k2 = np.arange(n2)[None, :, None]
    nn = np.arange(n2)[None, None, :]
    g = np.exp(-2j * np.pi * nn * (k1 + n1 * k2) / N)
    gt = np.conj(np.transpose(g, (0, 2, 1)))
    fc = np.conj(fa).T / N
    return dict(fa=_bf16_tables(fa), fa_half=_bf16_tables(fa[:, :n1 // 2]), g=_bf16_tables(g), gt=_bf16_tables(gt),
                fc_half=_bf16_tables(fc[:n1 // 2]))


def hyena_long_conv_gate(tabs, z, x0, k2, skip):
    Bn, L, C = z.shape
    n2 = FFT_N2
    n1 = 2 * L // n2
    P = Bn // 2
    W = n2 * C
    ar, ai = row_dft(tabs['fa'], k2.reshape(1, n1, W), packed=False)
    kr, ki = col_stage(tabs['g'], tabs['gt'], ar.reshape(1, n1, n2, C), ai.reshape(1, n1, n2, C))
    zv = z.reshape(Bn, n1 // 2, W)
    ar, ai = row_dft(tabs['fa_half'], zv, packed=True)
    dr, di = col_stage(tabs['g'], tabs['gt'], ar.reshape(P, n1, n2, C), ai.reshape(P, n1, n2, C), kr, ki)
    out = row_idft_gate(tabs['fc_half'], dr.reshape(P, n1, W), di.reshape(P, n1, W), zv, x0.reshape(Bn, n1 // 2, W),
                        jnp.tile(skip.astype(f32), n2).reshape(1, W))
    return out.reshape(Bn, L, C)


def _hyena_prep_kernel(*refs, rows, nt):
    cur, prev, nxt = refs[0:3], refs[3:6], refs[6:9]
    w_ref, b_ref, z_ref, x0_ref = refs[9:]
    t = pl.program_id(1)
    has_prev = (t > 0).astype(f32)
    has_next = (t < nt - 1).astype(f32)
    ridx = lax.broadcasted_iota(jnp.int32, (rows, HY_D), 0)

    def conv(j):
        u = cur[j][0]
        prev_row = prev[j][0, SUBLANES - 1:SUBLANES, :] * has_prev
        next_row = nxt[j][0, 0:1, :] * has_next
        up = jnp.where(ridx == 0, prev_row, pltpu.roll(u, 1, axis=0))
        un = jnp.where(ridx == rows - 1, next_row, pltpu.roll(u, rows - 1, axis=0))
        sl = slice(j * HY_D, (j + 1) * HY_D)
        return up * w_ref[0:1, sl] + u * w_ref[1:2, sl] + un * w_ref[2:3, sl] + b_ref[:, sl]

    x0_ref[0] = conv(0)
    z_ref[0] = conv(2) * conv(1)


def hyena_prep(proj, lane0, conv_w, conv_b):
    Bn, L, _ = proj.shape
    rows = RW_ROWS_PER_STEP
    nt = L // rows
    hb = rows // SUBLANES
    b0 = lane0 // HY_D
    assert b0 * HY_D == lane0
    cur = lambda j: pl.BlockSpec((1, rows, HY_D), lambda b, t: (b, t, b0 + j))
    prev = lambda j: pl.BlockSpec((1, SUBLANES, HY_D), lambda b, t: (b, jnp.maximum(t * hb - 1, 0), b0 + j))
    nxt = lambda j: pl.BlockSpec((1, SUBLANES, HY_D), lambda b, t: (b, jnp.minimum((t + 1) * hb, L // SUBLANES - 1), b0 + j))
    full = lambda shp: pl.BlockSpec(shp, lambda b, t: (0,) * len(shp))
    out = pl.BlockSpec((1, rows, HY_D), lambda b, t: (b, t, 0))
    return pl.pallas_call(
        functools.partial(_hyena_prep_kernel, rows=rows, nt=nt),
        grid=(Bn, nt),
        in_specs=[cur(0), cur(1), cur(2), prev(0), prev(1), prev(2), nxt(0), nxt(1), nxt(2),
                  full((3, 3 * HY_D)), full((1, 3 * HY_D))],
        out_specs=[out, out],
        out_shape=[jax.ShapeDtypeStruct((Bn, L, HY_D), f32)] * 2,
        compiler_params=pltpu.CompilerParams(dimension_semantics=("parallel", "parallel")),
        name="hyena_prep",
    )(*([proj] * 9), conv_w.astype(f32), conv_b.astype(f32).reshape(1, 3 * HY_D))


def hyena_mixer(tabs, proj, lane0, conv_w, conv_b, filt, skip):
    z, x0 = hyena_prep(proj, lane0, conv_w, conv_b)
    k2 = jnp.concatenate([filt[:1, 0] + filt[:1, 1], filt[1:, 0],
                          jnp.zeros((1, HY_D), f32), filt[1:, 1][::-1]], axis=0)
    return hyena_long_conv_gate(tabs, z, x0, k2, skip)


def rwkv7_mixer(proj, lora_lane0, mu, w0, w2, a0, a2, k_k, k_a, r_k, g2, ln_w, ln_b):
    r, k, v, kk, lw2, a_both = rw_prep(proj, lora_lane0, mu, w0, w2, a0, a2, k_k)
    y2 = wkv7_chunked(r, k, v, kk, lw2, a_both, k_a.astype(f32).reshape(1, RW_D))
    return rw_post(y2, r, k, v, a_both, proj, lora_lane0, k_a, r_k, ln_w, ln_b, g2)


def hier_moe(h, logits, bg, be, w1, w3, w2, layer):
    N, Dm = h.shape
    assert MOE_TOPK == 2
    g_logits = logits[:, :MOE_GROUPS] + bg.astype(f32)
    g_sel = jnp.argmax(g_logits, axis=-1)
    g_prob = jnp.take_along_axis(jax.nn.softmax(g_logits, axis=-1), g_sel[:, None], axis=-1)
    e_logits = (logits[:, MOE_GROUPS:MOE_GROUPS + MOE_EXPERTS] + be.astype(f32)).reshape(N, MOE_GROUPS, MOE_PER_GROUP)
    e_logits = jnp.take_along_axis(e_logits, g_sel[:, None, None], axis=1)[:, 0]
    top_val, top_idx = lax.top_k(e_logits, MOE_TOPK)
    gate = g_prob * jax.nn.softmax(top_val, axis=-1)
    expert = g_sel[:, None] * MOE_PER_GROUP + top_idx
    M = N * MOE_TOPK
    flat_e = expert.reshape(M).astype(jnp.int32)
    experts = jnp.arange(MOE_EXPERTS, dtype=jnp.int32)
    counts = jnp.sum((flat_e[:, None] == experts[None, :]).astype(jnp.int32), axis=0)
    padded = (counts + MOE_BLOCK - 1) // MOE_BLOCK * MOE_BLOCK
    pad_end = jnp.cumsum(padded)
    n_blocks = -(-M // MOE_BLOCK) + MOE_EXPERTS
    n_rows = n_blocks * MOE_BLOCK
    cum_need = jnp.cumsum(padded - counts)
    filler = jnp.arange(n_rows - M, dtype=jnp.int32)
    filler_e = jnp.sum((cum_need[None, :] <= filler[:, None]).astype(jnp.int32), axis=1)
    keys = jnp.concatenate([2 * flat_e, 2 * filler_e + 1])
    ids = jnp.concatenate([jnp.arange(M, dtype=jnp.int32), jnp.full((n_rows - M,), M, jnp.int32)])
    _, slot_src = lax.sort((keys, ids), num_keys=1)
    tok_src = jnp.where(slot_src < M, slot_src // MOE_TOPK, jnp.arange(n_rows, dtype=jnp.int32) % N)
    block_start = jnp.arange(n_blocks, dtype=pad_end.dtype) * MOE_BLOCK
    block_e = jnp.minimum(jnp.sum(pad_end[None, :] <= block_start[:, None], axis=1), MOE_EXPERTS - 1)
    y = moe_expert_ffn(h[tok_src], block_e, w1, w3, w2, layer)
    _, row_of = lax.sort((slot_src, jnp.arange(n_rows, dtype=jnp.int32)), num_keys=1)
    pos = row_of[:M].reshape(N, MOE_TOPK)
    return y[pos[:, 0]], y[pos[:, 1]], gate[:, 0:1], gate[:, 1:2]


def kernel(x, norm1_g, w_in, hy_conv_w, hy_conv_b, hy_w1, hy_b1, hy_w2, hy_b2, hy_w3, hy_b3, hy_wout, hy_freq, hy_skip, rw_mu, rw_w0, rw_w2, rw_a0, rw_a2, rw_kk, rw_ka, rw_rk, rw_g2, rw_ln_w, rw_ln_b, na_rpb, w_out, norm2_g, moe_wg, moe_bg, moe_we, moe_be, moe_w1, moe_w3, moe_w2, norm_f_g):
    Bn, L, _ = x.shape
    z_pos, t_pos = hyena_positional_features(L)
    dft_tabs = hyena_dft_tables(L)
    splits = np.cumsum(IN_SIZES)[:-1].tolist()
    hy_end, rkv_end = splits[0], splits[1]
    hy_lane0 = rkv_end - hy_end
    lora_lane0, na_lane0 = splits[1], splits[4]
    assert RW_D % LANES == 0 and lora_lane0 % LANES == 0 and na_lane0 % LANES == 0
    N = Bn * L
    x2 = x.reshape(N, D_MODEL)
    moe = None
    for l in range(DEPTH):
        w_in_l = jnp.concatenate([w_in[l][:, hy_end:rkv_end], w_in[l][:, :hy_end], w_in[l][:, rkv_end:]], axis=1)
        x2, proj = norm_proj(x2, norm1_g[l], w_in_l, moe)
        proj = proj.reshape(Bn, L, IN_D)
        filt = hyena_filters(z_pos, t_pos, hy_w1[l], hy_b1[l], hy_w2[l], hy_b2[l],
                             hy_w3[l], hy_b3[l], hy_wout[l], hy_freq[l])
        y_hy = hyena_mixer(dft_tabs, proj, hy_lane0, hy_conv_w[l], hy_conv_b[l], filt, hy_skip[l])
        y_rw = rwkv7_mixer(proj, lora_lane0, rw_mu[l], rw_w0[l], rw_w2[l], rw_a0[l],
                           rw_a2[l], rw_kk[l], rw_ka[l], rw_rk[l].reshape(RW_D), rw_g2[l], rw_ln_w[l], rw_ln_b[l])
        y_na = neighborhood_attention_pallas(proj, na_rpb[l], na_lane0 // LANES)
        x2, h2, logits = out_proj_norm_router(
            x2, y_hy.reshape(N, HY_D), y_rw.reshape(N, RW_D), y_na.reshape(N, NA_D), w_out[l], norm2_g[l],
            jnp.concatenate([moe_wg[l], moe_we[l]], axis=1))
        moe = hier_moe(h2, logits, moe_bg[l], moe_be[l], moe_w1, moe_w3, moe_w2, l)
    return final_rmsnorm(x2, norm_f_g, moe).reshape(Bn, L, D_MODEL)
```

```python
import functools
import math

import jax
import jax.numpy as jnp
import numpy as np
from jax import lax
from jax.experimental import pallas as pl
from jax.experimental.pallas import tpu as pltpu

f32 = jnp.float32
bf16 = jnp.bfloat16

D_MODEL = 1024
DEPTH = 2
GRID_W = 64
NORM_EPS = 1e-6
NEG_INF = -1e30

HY_D = D_MODEL // 4
HY_EMB = 33
HY_BANDS = (HY_EMB - 1) // 2
HY_FFN = 64
HY_MIN_DECAY = math.log(1e-2) / 1.5
HY_MAX_DECAY = math.log(1e-2) / 0.3

RW_N = 64
RW_D = D_MODEL // 2
RW_H = RW_D // RW_N
RW_W_LORA = 64
RW_A_LORA = 64
RW_G_LORA = 128
RW_GN_EPS = 64e-5

NA_HD = 64
NA_D = D_MODEL // 4
NA_H = NA_D // NA_HD
NA_KR = 8
NA_KC = 16

MIX_D = HY_D + RW_D + NA_D
IN_SIZES = (3 * HY_D, 3 * RW_D, RW_G_LORA, 2 * RW_W_LORA, 2 * RW_A_LORA, 3 * NA_D)
IN_D = sum(IN_SIZES)

MOE_GROUPS = 4
MOE_PER_GROUP = 8
MOE_EXPERTS = MOE_GROUPS * MOE_PER_GROUP
MOE_TOPK = 2
MOE_FF = 512
MOE_BLOCK = 512

LANES = 128
WKV_CHUNK = 64
WKV_TIME_BLOCK = 512
NA_ROWS_PER_STEP = 32
RW_ROWS_PER_STEP = 512
SUBLANES = 8
PROJ_ROWS_PER_STEP = 512
FFT_N2 = 128
FFT_LANE_BLOCK = 4096
FFT_K1_PER_STEP = 16
VMEM_LIMIT_BYTES = 48 * 1024 * 1024

_DN = {'nn': (((1,), (0,)), ((), ())), 'nt': (((1,), (1,)), ((), ())), 'tn': (((0,), (0,)), ((), ()))}


def _mm(a, b, dims='nn'):
    return lax.dot_general(a.astype(bf16), b.astype(bf16), _DN[dims], preferred_element_type=f32)


def _wkv_kernel(r_ref, k_ref, v_ref, kk_ref, lw_ref, a_ref, ka_ref, y_ref, ht_ref, *, tb, batch):
    T = WKV_CHUNK
    H2 = 2 * T
    nc = tb // T
    npairs = r_ref.shape[-1] // LANES
    mm = _mm

    @pl.when(pl.program_id(1) == 0)
    def _():
        ht_ref[...] = jnp.zeros_like(ht_ref)

    d = pl.program_id(0) // batch
    sign = 1 - 2 * d
    trow = lax.broadcasted_iota(jnp.int32, (T, H2), 0)
    lane = lax.broadcasted_iota(jnp.int32, (T, H2), 1)
    scol = lane % T
    tdiff = (trow - scol) * sign
    strict = tdiff > 0
    incl = tdiff >= 0
    same16 = (trow // 16) == (scol // 16)
    same32 = (trow // 32) == (scol // 32)
    off32 = same32 & jnp.logical_not(same16)
    off64 = jnp.logical_not(same32)
    eye2 = jnp.where(trow == scol, 1.0, 0.0).astype(f32)
    lo_lane = lane < T
    brow = lax.broadcasted_iota(jnp.int32, (H2, H2), 0)
    bcol = lax.broadcasted_iota(jnp.int32, (H2, H2), 1)
    same_head = (brow // T) == (bcol // T)
    crow = lax.broadcasted_iota(jnp.int32, (T, T), 0)
    ccol = lax.broadcasted_iota(jnp.int32, (T, T), 1)
    tri = jnp.where((crow - ccol) * sign >= 0, 1.0, 0.0).astype(bf16)
    is_bwd = d == 1

    def bd(x):
        return jnp.concatenate([jnp.where(lo_lane, x, 0.0), jnp.where(lo_lane, 0.0, x)], axis=0)

    inst = [(s, p) for s in range(nc) for p in range(npairs)]
    offs = [pl.multiple_of((s + d * (nc - 1 - 2 * s)) * T, T) for s in range(nc)]

    def load(ref, s, p):
        return ref[0, pl.ds(offs[s], T), p * LANES:(p + 1) * LANES]

    cs_l = []
    for s, p in inst:
        lw = load(lw_ref, s, p)
        l1 = lw.astype(bf16)
        l2 = (lw - l1.astype(f32)).astype(bf16)
        dd = lambda x: lax.dot_general(tri, x, _DN['nn'], preferred_element_type=f32)
        cs_l.append((dd(l1) + dd(l2), lw))
    ops = []
    for (s, p), (cs, lw) in zip(inst, cs_l):
        r = load(r_ref, s, p)
        k = load(k_ref, s, p)
        v = load(v_ref, s, p)
        kk = load(kk_ref, s, p)
        a = load(a_ref, s, p)
        ka = ka_ref[:, p * LANES:(p + 1) * LANES]
        kd = k * (1.0 + (a - 1.0) * ka)
        b = kk * a
        cs_end = jnp.where(is_bwd, cs[0:1, :], cs[T - 1:T, :])
        em = jnp.exp(-cs)
        e_end = jnp.exp(cs_end - cs)
        ops.append(dict(
            AR=jnp.concatenate([-kk * jnp.exp(cs - lw), r * jnp.exp(cs)], axis=0),
            BK=jnp.concatenate([bd(b * em), bd(kd * em)], axis=0),
            V=v, BKg=jnp.concatenate([b * e_end, kd * e_end], axis=0), g_end=jnp.exp(cs_end)))
    for o in ops:
        S = mm(o['AR'], o['BK'], 'nt')
        N = jnp.where(strict, S[:T, :H2], 0.0)
        o['AakArk'] = jnp.concatenate([jnp.where(strict, S[:T, H2:], 0.0), jnp.where(incl, S[T:, H2:], 0.0)], axis=0)
        o['Arb'] = jnp.where(incl, S[T:, :H2], 0.0)
        o['Nd'] = jnp.where(same16, N, 0.0)
        o['N32'] = jnp.where(off32, N, 0.0)
        o['N64'] = jnp.where(off64, N, 0.0)
    for o in ops:
        o['X'] = eye2 + o['Nd']
        o['P'] = mm(o['Nd'], bd(o['Nd']))
    for it in range(3):
        if it < 2:
            for o in ops:
                px = mm(o['P'], jnp.concatenate([bd(o['X']), bd(o['P'])], axis=1))
                o['X'] = o['X'] + px[:, :H2]
                o['P'] = px[:, H2:]
        else:
            for o in ops:
                o['X'] = o['X'] + mm(o['P'], bd(o['X']))
    for key in ('N32', 'N64'):
        for o in ops:
            o['Z'] = mm(o[key], bd(o['X']))
        for o in ops:
            o['X'] = o['X'] + mm(o['X'], bd(o['Z']))
    for o in ops:
        wy = mm(o['AakArk'], bd(o['V']))
        o['W0'] = wy[:T]
        o['Yv'] = wy[T:]
    for o in ops:
        o['XAW'] = mm(o['X'], jnp.concatenate([bd(o['AR'][:T]), bd(o['W0'])], axis=1))
    for o in ops:
        ax = mm(o['Arb'], jnp.concatenate([bd(o['XAW'][:, :H2]), bd(o['XAW'][:, H2:])], axis=1))
        o['Rhat'] = o['AR'][T:] + ax[:, :H2]
        o['Yc'] = ax[:, H2:] + o['Yv']
    for o in ops:
        lhs = jnp.concatenate([o['XAW'], jnp.concatenate([jnp.zeros((T, H2), f32), o['V']], axis=1)], axis=0)
        pq = mm(lhs, o['BKg'], 'tn')
        o['Pc'] = jnp.where(same_head, pq[:H2], 0.0)
        o['Qc'] = jnp.where(same_head, pq[H2:], 0.0)
    hts = [ht_ref[p] for p in range(npairs)]
    for s in range(nc):
        cur = [ops[s * npairs + p] for p in range(npairs)]
        ys = [mm(o['Rhat'], hts[p], 'nt') + o['Yc'] for p, o in enumerate(cur)]
        hts = [hts[p] * o['g_end'] + (mm(hts[p], o['Pc']) + o['Qc']) for p, o in enumerate(cur)]
        for p in range(npairs):
            y_ref[0, pl.ds(offs[s], T), p * LANES:(p + 1) * LANES] = ys[p]
    for p in range(npairs):
        ht_ref[p] = hts[p]


def wkv7_chunked(r, k, v, kk, lw2, a2, ka):
    Bn, L, C = r.shape
    tb = WKV_TIME_BLOCK
    nt = L // tb
    tmap = lambda i, t: t + (i // Bn) * (nt - 1 - 2 * t)
    shared = pl.BlockSpec((1, tb, C), lambda i, t: (i % Bn, tmap(i, t), 0))
    per_dir = pl.BlockSpec((1, tb, C), lambda i, t: (i % Bn, tmap(i, t), i // Bn))
    return pl.pallas_call(
        functools.partial(_wkv_kernel, tb=tb, batch=Bn),
        grid=(2 * Bn, nt),
        in_specs=[shared, shared, shared, shared, per_dir, per_dir, pl.BlockSpec((1, C), lambda i, t: (0, 0))],
        out_specs=pl.BlockSpec((1, tb, C), lambda i, t: (i, tmap(i, t), 0)),
        out_shape=jax.ShapeDtypeStruct((2 * Bn, L, C), f32),
        scratch_shapes=[pltpu.VMEM((C // LANES, LANES, LANES), f32)],
        compiler_params=pltpu.CompilerParams(dimension_semantics=("parallel", "arbitrary"),
                                             vmem_limit_bytes=VMEM_LIMIT_BYTES),
        name="wkv7_chunked",
    )(r, k, v, kk, lw2, a2, ka)


def _mm_exact_rhs(a, b_bf16):
    ah = a.astype(bf16)
    al = (a - ah.astype(f32)).astype(bf16)
    d = lambda x: lax.dot_general(x, b_bf16, _DN['nn'], preferred_element_type=f32)
    return d(ah) + d(al)


def _head_sum_matrix(scale):
    i = lax.broadcasted_iota(jnp.int32, (RW_D, RW_D), 0) // RW_N
    j = lax.broadcasted_iota(jnp.int32, (RW_D, RW_D), 1) // RW_N
    return jnp.where(i == j, scale, 0.0).astype(bf16)


def _rw_prep_kernel(r_ref, k_ref, v_ref, rp_ref, kp_ref, vp_ref, rn_ref, kn_ref, vn_ref, w_ref, a_ref,
                    mu_ref, w0_ref, w2_ref, a0_ref, a2_ref, kkw_ref,
                    ro_ref, ko_ref, vo_ref, kko_ref, lw_ref, ao_ref, *, rows, nt):
    t = pl.program_id(1)
    has_prev = (t > 0).astype(f32)
    has_next = (t < nt - 1).astype(f32)
    ridx = lax.broadcasted_iota(jnp.int32, (rows, RW_D), 0)

    def tshift(cur_ref, prev_ref, next_ref, j):
        u = cur_ref[0]
        prev_row = prev_ref[0, SUBLANES - 1:SUBLANES, :] * has_prev
        next_row = next_ref[0, 0:1, :] * has_next
        up = jnp.where(ridx == 0, prev_row, pltpu.roll(u, 1, axis=0))
        un = jnp.where(ridx == rows - 1, next_row, pltpu.roll(u, rows - 1, axis=0))
        return u + mu_ref[j, 0:1, :] * (up - u) + mu_ref[j, 1:2, :] * (un - u)

    r = tshift(r_ref, rp_ref, rn_ref, 0)
    k = tshift(k_ref, kp_ref, kn_ref, 1)
    v = tshift(v_ref, vp_ref, vn_ref, 2)
    ro_ref[0] = r
    ko_ref[0] = k
    vo_ref[0] = v
    kk = k * kkw_ref[...]
    ss = _mm_exact_rhs(kk * kk, _head_sum_matrix(1.0))
    kko_ref[0] = kk * lax.rsqrt(jnp.maximum(ss, 1e-24))
    wl = jnp.tanh(w_ref[0])
    al = a_ref[0]
    for d in range(2):
        wpre = w0_ref[d:d + 1, :] + _mm(wl[:, d * RW_W_LORA:(d + 1) * RW_W_LORA], w2_ref[d])
        lw_ref[0, :, d * RW_D:(d + 1) * RW_D] = -math.exp(-0.5) * jax.nn.sigmoid(wpre)
        av = a0_ref[d:d + 1, :] + _mm(al[:, d * RW_A_LORA:(d + 1) * RW_A_LORA], a2_ref[d])
        ao_ref[0, :, d * RW_D:(d + 1) * RW_D] = jax.nn.sigmoid(av)


def rw_prep(proj, lora_lane0, mu, w0, w2, a0, a2, k_k):
    Bn, L, _ = proj.shape
    C = RW_D
    rows = RW_ROWS_PER_STEP
    nt = L // rows
    hb = rows // SUBLANES
    lb = lora_lane0 // LANES
    cur = lambda j: pl.BlockSpec((1, rows, C), lambda b, t: (b, t, j))
    prev = lambda j: pl.BlockSpec((1, SUBLANES, C), lambda b, t: (b, jnp.maximum(t * hb - 1, 0), j))
    nxt = lambda j: pl.BlockSpec((1, SUBLANES, C), lambda b, t: (b, jnp.minimum((t + 1) * hb, L // SUBLANES - 1), j))
    lora_w = pl.BlockSpec((1, rows, LANES), lambda b, t: (b, t, lb + 1))
    lora_a = pl.BlockSpec((1, rows, LANES), lambda b, t: (b, t, lb + 2))
    full = lambda shp: pl.BlockSpec(shp, lambda b, t: (0,) * len(shp))
    out_c = pl.BlockSpec((1, rows, C), lambda b, t: (b, t, 0))
    out_2c = pl.BlockSpec((1, rows, 2 * C), lambda b, t: (b, t, 0))
    sds = lambda c: jax.ShapeDtypeStruct((Bn, L, c), f32)
    return pl.pallas_call(
        functools.partial(_rw_prep_kernel, rows=rows, nt=nt),
        grid=(Bn, nt),
        in_specs=[cur(0), cur(1), cur(2), prev(0), prev(1), prev(2), nxt(0), nxt(1), nxt(2), lora_w, lora_a,
                  full((3, 2, C)), full((2, C)), full((2, RW_W_LORA, C)), full((2, C)), full((2, RW_A_LORA, C)),
                  full((1, C))],
        out_specs=[out_c, out_c, out_c, out_c, out_2c, out_2c],
        out_shape=[sds(C), sds(C), sds(C), sds(C), sds(2 * C), sds(2 * C)],
        compiler_params=pltpu.CompilerParams(dimension_semantics=("parallel", "parallel"),
                                             vmem_limit_bytes=VMEM_LIMIT_BYTES),
        name="rwkv_prep",
    )(proj, proj, proj, proj, proj, proj, proj, proj, proj, proj, proj,
      mu.astype(f32), w0.astype(f32), w2.astype(bf16), a0.astype(f32), a2.astype(bf16), k_k.astype(f32).reshape(1, C))


def _rw_post_kernel(yf_ref, yb_ref, r_ref, k_ref, v_ref, a_ref, g_ref, ka_ref, rk_ref, lnw_ref, lnb_ref, g2_ref, o_ref):
    C = RW_D
    y = yf_ref[0] + yb_ref[0]
    avg = _head_sum_matrix(1.0 / RW_N)
    mean = _mm_exact_rhs(y, avg)
    yc = y - mean
    var = _mm_exact_rhs(yc * yc, avg)
    yn = yc * lax.rsqrt(var + RW_GN_EPS) * lnw_ref[...] + lnb_ref[...]
    a = a_ref[0]
    k = k_ref[0]
    ka = ka_ref[...]
    ksum = k * (1.0 + (a[:, :C] - 1.0) * ka) + k * (1.0 + (a[:, C:] - 1.0) * ka)
    coef = _mm_exact_rhs(r_ref[0] * ksum * rk_ref[...], _head_sum_matrix(1.0))
    gate = _mm(jax.nn.sigmoid(g_ref[0]), g2_ref[...])
    o_ref[0] = ((yn + coef * v_ref[0]) * gate).astype(o_ref.dtype)


def rw_post(y2, r, k, v, a2, proj, g_lane0, k_a, r_k, ln_w, ln_b, g2):
    Bn, L, C = r.shape
    rows = RW_ROWS_PER_STEP
    nt = L // rows
    gb = g_lane0 // LANES
    blk = lambda c: pl.BlockSpec((1, rows, c), lambda b, t: (b, t, 0))
    full = lambda shp: pl.BlockSpec(shp, lambda b, t: (0,) * len(shp))
    row = lambda x: x.astype(f32).reshape(1, C)
    return pl.pallas_call(
        _rw_post_kernel,
        grid=(Bn, nt),
        in_specs=[blk(C), pl.BlockSpec((1, rows, C), lambda b, t: (b + Bn, t, 0)), blk(C), blk(C), blk(C), blk(2 * C),
                  pl.BlockSpec((1, rows, LANES), lambda b, t: (b, t, gb)),
                  full((1, C)), full((1, C)), full((1, C)), full((1, C)), full((RW_G_LORA, C))],
        out_specs=blk(C),
        out_shape=jax.ShapeDtypeStruct((Bn, L, C), bf16),
        compiler_params=pltpu.CompilerParams(dimension_semantics=("parallel", "parallel"),
                                             vmem_limit_bytes=VMEM_LIMIT_BYTES),
        name="rwkv_post",
    )(y2, y2, r, k, v, a2, proj, row(k_a), row(r_k), row(ln_w), row(ln_b), g2.astype(bf16))


def _na_kernel(q_ref, k_ref, v_ref, bias_ref, o_ref, kb_ref, vb_ref, *, rows_per_step, n_rows, kr):
    W = GRID_W
    rb = pl.program_id(2)

    @pl.when(rb == 0)
    def _():
        kb_ref[...] = k_ref[0].astype(bf16)
        vb_ref[...] = v_ref[0].astype(bf16)

    lo_lane = lax.broadcasted_iota(jnp.int32, (W, LANES), 1) < NA_HD
    scale = NA_HD ** -0.5
    rows = []
    for j in range(rows_per_step):
        r = rb * rows_per_step + j
        start = jnp.clip(r - kr // 2, 0, n_rows - kr)
        rows.append((start, start - r + (NA_KR - 1)))
    s_list = []
    for j, (start, didx) in enumerate(rows):
        q = q_ref[0, j * W:(j + 1) * W, :] * scale
        qs = jnp.concatenate([jnp.where(lo_lane, q, 0.0), jnp.where(lo_lane, 0.0, q)], axis=0)
        kw = kb_ref[pl.ds(pl.multiple_of(start * W, W), kr * W), :]
        s_list.append(_mm(qs, kw, 'nt') + bias_ref[didx, 0])
    p_list = []
    for s in s_list:
        m = jnp.max(s, axis=-1, keepdims=True)
        p = jnp.exp(s - m)
        p_list.append((p, jnp.sum(p, axis=-1, keepdims=True)))
    for j, ((start, _), (p, l)) in enumerate(zip(rows, p_list)):
        vw = vb_ref[pl.ds(pl.multiple_of(start * W, W), kr * W), :]
        o = _mm(p, vw) / l
        o_ref[0, j * W:(j + 1) * W, :] = jnp.where(lo_lane, o[:W], o[W:]).astype(o_ref.dtype)


def na_bias_table(rpb, n_rows):
    W = GRID_W
    kr = min(NA_KR, n_rows)
    cols = jnp.arange(W)
    col_start = jnp.clip(cols - NA_KC // 2, 0, W - NA_KC)
    in_band = (cols[None, :] >= col_start[:, None]) & (cols[None, :] < col_start[:, None] + NA_KC)
    dc = jnp.clip(cols[None, :] - cols[:, None], -(NA_KC - 1), NA_KC - 1) + (NA_KC - 1)
    win = jnp.stack([rpb.astype(f32)[:, d:d + kr, :] for d in range(8)], axis=1)
    onehot = (dc[:, :, None] == jnp.arange(2 * NA_KC - 1)[None, None, :]).astype(f32)
    tab = jnp.einsum('hdic,qkc->hdqik', win, onehot, precision=lax.Precision.HIGHEST)
    tab = jnp.where(in_band[None, None, :, None, :], tab, NEG_INF)
    H = rpb.shape[0]
    tab = tab.reshape(H // 2, 2, 8, W, kr * W).transpose(2, 0, 1, 3, 4)
    return tab.reshape(8, H // 2, 2 * W, kr * W)


def neighborhood_attention_pallas(proj, rpb, lane_block0):
    Bn, L, _ = proj.shape
    W = GRID_W
    n_rows = L // W
    kr = min(NA_KR, n_rows)
    hp = NA_H * NA_HD // LANES
    rps = NA_ROWS_PER_STEP
    bias = na_bias_table(rpb, n_rows)
    kern = functools.partial(_na_kernel, rows_per_step=rps, n_rows=n_rows, kr=kr)
    return pl.pallas_call(
        kern,
        grid=(Bn, hp, n_rows // rps),
        in_specs=[pl.BlockSpec((1, rps * W, LANES), lambda b, h, r: (b, r, lane_block0 + h)),
                  pl.BlockSpec((1, L, LANES), lambda b, h, r: (b, 0, lane_block0 + hp + h)),
                  pl.BlockSpec((1, L, LANES), lambda b, h, r: (b, 0, lane_block0 + 2 * hp + h)),
                  pl.BlockSpec((8, 1, 2 * W, kr * W), lambda b, h, r: (0, h, 0, 0))],
        out_specs=pl.BlockSpec((1, rps * W, LANES), lambda b, h, r: (b, r, h)),
        out_shape=jax.ShapeDtypeStruct((Bn, L, NA_H * NA_HD), bf16),
        scratch_shapes=[pltpu.VMEM((L, LANES), bf16), pltpu.VMEM((L, LANES), bf16)],
        compiler_params=pltpu.CompilerParams(dimension_semantics=("parallel", "parallel", "arbitrary"),
                                             vmem_limit_bytes=VMEM_LIMIT_BYTES),
        name="neighborhood_attention",
    )(proj, proj, proj, bias)


def _moe_ffn_kernel(be_ref, x_ref, w1_ref, w3_ref, w2_ref, o_ref, w1b_ref, w3b_ref, w2b_ref):
    i = pl.program_id(0)

    @pl.when((i == 0) | (be_ref[i] != be_ref[jnp.maximum(i - 1, 0)]))
    def _():
        w1b_ref[...] = w1_ref[0, 0].astype(bf16)
        w3b_ref[...] = w3_ref[0, 0].astype(bf16)
        w2b_ref[...] = w2_ref[0, 0].astype(bf16)

    x = x_ref[...].astype(bf16)
    h1 = jnp.dot(x, w1b_ref[...], preferred_element_type=f32)
    h3 = jnp.dot(x, w3b_ref[...], preferred_element_type=f32)
    g = h1 * jax.nn.sigmoid(h1) * h3
    o_ref[...] = jnp.dot(g.astype(bf16), w2b_ref[...], preferred_element_type=f32).astype(o_ref.dtype)


def moe_expert_ffn(xs, block_e, w1, w3, w2, layer):
    rows, Dm = xs.shape
    n_blocks = rows // MOE_BLOCK
    grid_spec = pltpu.PrefetchScalarGridSpec(
        num_scalar_prefetch=1,
        grid=(n_blocks,),
        in_specs=[pl.BlockSpec((MOE_BLOCK, Dm), lambda i, be: (i, 0)),
                  pl.BlockSpec((1, 1, Dm, MOE_FF), lambda i, be: (layer, be[i], 0, 0)),
                  pl.BlockSpec((1, 1, Dm, MOE_FF), lambda i, be: (layer, be[i], 0, 0)),
                  pl.BlockSpec((1, 1, MOE_FF, Dm), lambda i, be: (layer, be[i], 0, 0))],
        out_specs=pl.BlockSpec((MOE_BLOCK, Dm), lambda i, be: (i, 0)),
        scratch_shapes=[pltpu.VMEM((Dm, MOE_FF), bf16), pltpu.VMEM((Dm, MOE_FF), bf16), pltpu.VMEM((MOE_FF, Dm), bf16)],
    )
    return pl.pallas_call(
        _moe_ffn_kernel,
        grid_spec=grid_spec,
        out_shape=jax.ShapeDtypeStruct((rows, Dm), bf16),
        compiler_params=pltpu.CompilerParams(dimension_semantics=("arbitrary",), vmem_limit_bytes=VMEM_LIMIT_BYTES),
        name="moe_expert_ffn",
    )(block_e.astype(jnp.int32), xs, w1.astype(f32), w3.astype(f32), w2.astype(f32))


def _rms(x):
    return x * lax.rsqrt(jnp.mean(x * x, axis=-1, keepdims=True) + NORM_EPS)


def _moe_combine(x_ref, y0_ref, y1_ref, g0_ref, g1_ref):
    return x_ref[...] + (g0_ref[...] * y0_ref[...].astype(f32) + g1_ref[...] * y1_ref[...].astype(f32))


def _moe_specs(tm, D):
    row = pl.BlockSpec((tm, D), lambda i: (i, 0))
    col = pl.BlockSpec((tm, 1), lambda i: (i, 0))
    return [row, row, col, col]


def _norm_proj_kernel(*refs, combine):
    if combine:
        x_ref, y0_ref, y1_ref, g0_ref, g1_ref, g_ref, w_ref, o_ref, xo_ref = refs
        x = _moe_combine(x_ref, y0_ref, y1_ref, g0_ref, g1_ref)
        xo_ref[...] = x
    else:
        x_ref, g_ref, w_ref, o_ref = refs
        x = x_ref[...]
    h = (_rms(x) * g_ref[...]).astype(bf16)
    o_ref[...] = jnp.dot(h, w_ref[...], preferred_element_type=f32)


def norm_proj(x2, g, w, moe=None):
    N, D = x2.shape
    F = w.shape[1]
    tm = PROJ_ROWS_PER_STEP
    row = pl.BlockSpec((tm, D), lambda i: (i, 0))
    out_row = pl.BlockSpec((tm, F), lambda i: (i, 0))
    par = [pl.BlockSpec((1, D), lambda i: (0, 0)), pl.BlockSpec((D, F), lambda i: (0, 0))]
    params = dict(compiler_params=pltpu.CompilerParams(dimension_semantics=("parallel",),
                                                       vmem_limit_bytes=VMEM_LIMIT_BYTES), name="norm_proj")
    gw = (g.astype(f32).reshape(1, D), w.astype(bf16))
    if moe is None:
        proj = pl.pallas_call(functools.partial(_norm_proj_kernel, combine=False), grid=(N // tm,),
                              in_specs=[row] + par, out_specs=out_row,
                              out_shape=jax.ShapeDtypeStruct((N, F), f32), **params)(x2, *gw)
        return x2, proj
    proj, x_new = pl.pallas_call(functools.partial(_norm_proj_kernel, combine=True), grid=(N // tm,),
                                 in_specs=[row] + _moe_specs(tm, D) + par, out_specs=[out_row, row],
                                 out_shape=[jax.ShapeDtypeStruct((N, F), f32), jax.ShapeDtypeStruct((N, D), f32)],
                                 **params)(x2, *moe, *gw)
    return x_new, proj


def _out_proj_kernel(x_ref, yh_ref, yr_ref, yn_ref, wh_ref, wr_ref, wn_ref, g_ref, wrt_ref, xo_ref, h_ref, lg_ref):
    mix = (jnp.dot(yh_ref[...].astype(bf16), wh_ref[...], preferred_element_type=f32)
           + jnp.dot(yr_ref[...].astype(bf16), wr_ref[...], preferred_element_type=f32)
           + jnp.dot(yn_ref[...].astype(bf16), wn_ref[...], preferred_element_type=f32))
    x = x_ref[...] + mix
    xo_ref[...] = x
    h = (_rms(x) * g_ref[...]).astype(bf16)
    h_ref[...] = h
    lg_ref[...] = jnp.dot(h, wrt_ref[...], preferred_element_type=f32)


def out_proj_norm_router(x2, y_hy, y_rw, y_na, w_out, g, w_router):
    N, D = x2.shape
    tm = PROJ_ROWS_PER_STEP
    d_hy, d_rw, d_na = y_hy.shape[1], y_rw.shape[1], y_na.shape[1]
    nr = w_router.shape[1]
    w_router = jnp.pad(w_router.astype(bf16), ((0, 0), (0, LANES - nr)))
    wb = w_out.astype(bf16)
    row = lambda c: pl.BlockSpec((tm, c), lambda i: (i, 0))
    full = lambda r, c: pl.BlockSpec((r, c), lambda i: (0, 0))
    return pl.pallas_call(
        _out_proj_kernel,
        grid=(N // tm,),
        in_specs=[row(D), row(d_hy), row(d_rw), row(d_na), full(d_hy, D), full(d_rw, D), full(d_na, D), full(1, D),
                  full(D, LANES)],
        out_specs=[row(D), row(D), row(LANES)],
        out_shape=[jax.ShapeDtypeStruct((N, D), f32), jax.ShapeDtypeStruct((N, D), bf16),
                   jax.ShapeDtypeStruct((N, LANES), f32)],
        compiler_params=pltpu.CompilerParams(dimension_semantics=("parallel",), vmem_limit_bytes=VMEM_LIMIT_BYTES),
        name="out_proj_norm_router",
    )(x2, y_hy, y_rw, y_na, wb[:d_hy], wb[d_hy:d_hy + d_rw], wb[d_hy + d_rw:], g.astype(f32).reshape(1, D), w_router)


def _final_norm_kernel(x_ref, y0_ref, y1_ref, g0_ref, g1_ref, g_ref, o_ref):
    o_ref[...] = _rms(_moe_combine(x_ref, y0_ref, y1_ref, g0_ref, g1_ref)) * g_ref[...]


def final_rmsnorm(x2, g, moe):
    N, D = x2.shape
    tm = PROJ_ROWS_PER_STEP
    row = pl.BlockSpec((tm, D), lambda i: (i, 0))
    return pl.pallas_call(
        _final_norm_kernel,
        grid=(N // tm,),
        in_specs=[row] + _moe_specs(tm, D) + [pl.BlockSpec((1, D), lambda i: (0, 0))],
        out_specs=row,
        out_shape=jax.ShapeDtypeStruct((N, D), f32),
        compiler_params=pltpu.CompilerParams(dimension_semantics=("parallel",), vmem_limit_bytes=VMEM_LIMIT_BYTES),
        name="final_rmsnorm",
    )(x2, *moe, g.astype(f32).reshape(1, D))


def hyena_positional_features(L):
    t = jnp.linspace(0.0, 1.0, L, dtype=f32)[:, None]
    w = (2.0 * math.pi / L) * jnp.arange(L, dtype=f32)[:, None]
    f = jnp.linspace(1e-4, HY_BANDS - 1, HY_BANDS, dtype=f32)[None, :]
    z = jnp.concatenate([t, jnp.cos(f * w), -jnp.sin(f * w)], axis=-1)
    return z, t


def hyena_filters(z, t, w1, b1, w2, b2, w3, b3, wout, freq):
    fr = freq.astype(f32)
    act = lambda u: jnp.sin(fr * u)
    h = act(z @ w1.astype(f32) + b1.astype(f32))
    h = act(h @ w2.astype(f32) + b2.astype(f32))
    h = act(h @ w3.astype(f32) + b3.astype(f32))
    h = (h @ wout.astype(f32)).reshape(-1, 2, HY_D)
    deltas = jnp.abs(jnp.linspace(HY_MIN_DECAY, HY_MAX_DECAY, HY_D, dtype=f32))
    h = h * jnp.exp(-t[:, :, None] * deltas)
    return h * lax.rsqrt(jnp.sum(h * h, axis=(0, 1), keepdims=True) + 1e-6)


def _split_bf16(x):
    hi = x.astype(bf16)
    return hi, (x - hi.astype(f32)).astype(bf16)


def _dot3(m_hi, m_lo, x_hi, x_lo):
    d = lambda a, b: lax.dot_general(a, b, _DN['nn'], preferred_element_type=f32)
    return d(m_hi, x_hi) + (d(m_hi, x_lo) + d(m_lo, x_hi))


def _cmatmul(mr, mi, xr, xi):
    xrh, xrl = _split_bf16(xr)
    rr = _dot3(mr[0], mr[1], xrh, xrl)
    ir = _dot3(mi[0], mi[1], xrh, xrl)
    if xi is None:
        return rr, ir
    xih, xil = _split_bf16(xi)
    ii = _dot3(mi[0], mi[1], xih, xil)
    ri = _dot3(mr[0], mr[1], xih, xil)
    return rr - ii, ri + ir


def _row_dft_kernel(mrh_ref, mrl_ref, mih_ref, mil_ref, *refs, real_input):
    if real_input:
        ur_ref, or_ref, oi_ref = refs
        ui = None
    else:
        ur_ref, ui_ref, or_ref, oi_ref = refs
        ui = ui_ref[0]
    o_r, o_i = _cmatmul((mrh_ref[...], mrl_ref[...]), (mih_ref[...], mil_ref[...]), ur_ref[0], ui)
    or_ref[0] = o_r
    oi_ref[0] = o_i


def row_dft(tabs, u, packed):
    rows, r_in, W = u.shape
    P = rows // 2 if packed else rows
    r_out = tabs[0].shape[0]
    wb = min(FFT_LANE_BLOCK, W)
    tab_spec = pl.BlockSpec((r_out, r_in), lambda p, j: (0, 0))
    re_spec = pl.BlockSpec((1, r_in, wb), lambda p, j: (p, 0, j))
    im_spec = pl.BlockSpec((1, r_in, wb), lambda p, j: (p + P, 0, j))
    out_spec = pl.BlockSpec((1, r_out, wb), lambda p, j: (p, 0, j))
    ins = [u, u] if packed else [u]
    return pl.pallas_call(
        functools.partial(_row_dft_kernel, real_input=not packed),
        grid=(P, W // wb),
        in_specs=[tab_spec] * 4 + ([re_spec, im_spec] if packed else [re_spec]),
        out_specs=[out_spec, out_spec],
        out_shape=[jax.ShapeDtypeStruct((P, r_out, W), f32)] * 2,
        compiler_params=pltpu.CompilerParams(dimension_semantics=("parallel", "parallel"),
                                             vmem_limit_bytes=VMEM_LIMIT_BYTES),
        name="hyena_row_dft",
    )(*tabs, *ins)


def _row_idft_gate_kernel(mrh_ref, mrl_ref, mih_ref, mil_ref, dr_ref, di_ref, z0_ref, z1_ref, x0_ref, x1_ref,
                          skip_ref, o0_ref, o1_ref):
    y_r, y_i = _cmatmul((mrh_ref[...], mrl_ref[...]), (mih_ref[...], mil_ref[...]), dr_ref[0], di_ref[0])
    skip = skip_ref[...]
    o0_ref[0] = (x0_ref[0] * (y_r + z0_ref[0] * skip)).astype(o0_ref.dtype)
    o1_ref[0] = (x1_ref[0] * (y_i + z1_ref[0] * skip)).astype(o1_ref.dtype)


def row_idft_gate(tabs, dr, di, z, x0, skip_row):
    P, r_in, W = dr.shape
    r_out = tabs[0].shape[0]
    wb = min(FFT_LANE_BLOCK, W)
    tab_spec = pl.BlockSpec((r_out, r_in), lambda p, j: (0, 0))
    d_spec = pl.BlockSpec((1, r_in, wb), lambda p, j: (p, 0, j))
    lo = pl.BlockSpec((1, r_out, wb), lambda p, j: (p, 0, j))
    hi = pl.BlockSpec((1, r_out, wb), lambda p, j: (p + P, 0, j))
    o0, o1 = pl.pallas_call(
        _row_idft_gate_kernel,
        grid=(P, W // wb),
        in_specs=[tab_spec] * 4 + [d_spec, d_spec, lo, hi, lo, hi, pl.BlockSpec((1, wb), lambda p, j: (0, j))],
        out_specs=[lo, lo],
        out_shape=[jax.ShapeDtypeStruct((P, r_out, W), bf16)] * 2,
        compiler_params=pltpu.CompilerParams(dimension_semantics=("parallel", "parallel"),
                                             vmem_limit_bytes=VMEM_LIMIT_BYTES),
        name="hyena_row_idft_gate",
    )(*tabs, dr, di, z, z, x0, x0, skip_row)
    return jnp.concatenate([o0, o1], axis=0)


def _col_dft_kernel(grh_ref, grl_ref, gih_ref, gil_ref, ar_ref, ai_ref, br_ref, bi_ref, *, k1_per_step):
    for j in range(k1_per_step):
        b_r, b_i = _cmatmul((grh_ref[j], grl_ref[j]), (gih_ref[j], gil_ref[j]), ar_ref[0, j], ai_ref[0, j])
        br_ref[0, j] = b_r
        bi_ref[0, j] = b_i


def _col_conv_kernel(grh_ref, grl_ref, gih_ref, gil_ref, trh_ref, trl_ref, tih_ref, til_ref,
                     ar_ref, ai_ref, kr_ref, ki_ref, dr_ref, di_ref, *, k1_per_step):
    for j in range(k1_per_step):
        b_r, b_i = _cmatmul((grh_ref[j], grl_ref[j]), (gih_ref[j], gil_ref[j]), ar_ref[0, j], ai_ref[0, j])
        k_r = kr_ref[0, j]
        k_i = ki_ref[0, j]
        c_r = b_r * k_r - b_i * k_i
        c_i = b_r * k_i + b_i * k_r
        d_r, d_i = _cmatmul((trh_ref[j], trl_ref[j]), (tih_ref[j], til_ref[j]), c_r, c_i)
        dr_ref[0, j] = d_r
        di_ref[0, j] = d_i


def col_stage(g_tabs, gt_tabs, ar, ai, kr=None, ki=None):
    P, n1, n2, C = ar.shape
    kb = min(FFT_K1_PER_STEP, n1)
    g_spec = pl.BlockSpec((kb, n2, n2), lambda p, j: (j, 0, 0))
    a_spec = pl.BlockSpec((1, kb, n2, C), lambda p, j: (p, j, 0, 0))
    k_spec = pl.BlockSpec((1, kb, n2, C), lambda p, j: (0, j, 0, 0))
    common = dict(
        grid=(P, n1 // kb),
        out_specs=[a_spec, a_spec],
        out_shape=[jax.ShapeDtypeStruct((P, n1, n2, C), f32)] * 2,
        compiler_params=pltpu.CompilerParams(dimension_semantics=("parallel", "parallel"),
                                             vmem_limit_bytes=VMEM_LIMIT_BYTES),
    )
    if kr is None:
        return pl.pallas_call(functools.partial(_col_dft_kernel, k1_per_step=kb),
                              in_specs=[g_spec] * 4 + [a_spec] * 2, name="hyena_col_dft", **common)(*g_tabs, ar, ai)
    return pl.pallas_call(functools.partial(_col_conv_kernel, k1_per_step=kb),
                          in_specs=[g_spec] * 8 + [a_spec] * 2 + [k_spec] * 2, name="hyena_col_conv", **common)(
        *g_tabs, *gt_tabs, ar, ai, kr, ki)


def _bf16_tables(m):
    out = []
    for part in (np.real(m), np.imag(m)):
        x = jnp.asarray(part, f32)
        hi = x.astype(bf16)
        out += [hi, (x - hi.astype(f32)).astype(bf16)]
    return out


def hyena_dft_tables(L):
    n2 = FFT_N2
    n1 = 2 * L // n2
    N = n1 * n2
    a = np.arange(n1)
    fa = np.exp(-2j * np.pi * np.outer(a, a) / n1)
    k1 = np.arange(n1)[:, None, None]
    k2 = np.arange(n2)[None, :, None]
    nn = np.arange(n2)[None, None, :]
    g = np.exp(-2j * np.pi * nn * (k1 + n1 * k2) / N)
    gt = np.conj(np.transpose(g, (0, 2, 1)))
    fc = np.conj(fa).T / N
    return dict(fa=_bf16_tables(fa), fa_half=_bf16_tables(fa[:, :n1 // 2]), g=_bf16_tables(g), gt=_bf16_tables(gt),
                fc_half=_bf16_tables(fc[:n1 // 2]))


def hyena_long_conv_gate(tabs, z, x0, k2, skip):
    Bn, L, C = z.shape
    n2 = FFT_N2
    n1 = 2 * L // n2
    P = Bn // 2
    W = n2 * C
    ar, ai = row_dft(tabs['fa'], k2.reshape(1, n1, W), packed=False)
    kr, ki = col_stage(tabs['g'], tabs['gt'], ar.reshape(1, n1, n2, C), ai.reshape(1, n1, n2, C))
    zv = z.reshape(Bn, n1 // 2, W)
    ar, ai = row_dft(tabs['fa_half'], zv, packed=True)
    dr, di = col_stage(tabs['g'], tabs['gt'], ar.reshape(P, n1, n2, C), ai.reshape(P, n1, n2, C), kr, ki)
    out = row_idft_gate(tabs['fc_half'], dr.reshape(P, n1, W), di.reshape(P, n1, W), zv, x0.reshape(Bn, n1 // 2, W),
                        jnp.tile(skip.astype(f32), n2).reshape(1, W))
    return out.reshape(Bn, L, C)


def _hyena_prep_kernel(*refs, rows, nt):
    cur, prev, nxt = refs[0:3], refs[3:6], refs[6:9]
    w_ref, b_ref, z_ref, x0_ref = refs[9:]
    t = pl.program_id(1)
    has_prev = (t > 0).astype(f32)
    has_next = (t < nt - 1).astype(f32)
    ridx = lax.broadcasted_iota(jnp.int32, (rows, HY_D), 0)

    def conv(j):
        u = cur[j][0]
        prev_row = prev[j][0, SUBLANES - 1:SUBLANES, :] * has_prev
        next_row = nxt[j][0, 0:1, :] * has_next
        up = jnp.where(ridx == 0, prev_row, pltpu.roll(u, 1, axis=0))
        un = jnp.where(ridx == rows - 1, next_row, pltpu.roll(u, rows - 1, axis=0))
        sl = slice(j * HY_D, (j + 1) * HY_D)
        return up * w_ref[0:1, sl] + u * w_ref[1:2, sl] + un * w_ref[2:3, sl] + b_ref[:, sl]

    x0_ref[0] = conv(0)
    z_ref[0] = conv(2) * conv(1)


def hyena_prep(proj, lane0, conv_w, conv_b):
    Bn, L, _ = proj.shape
    rows = RW_ROWS_PER_STEP
    nt = L // rows
    hb = rows // SUBLANES
    b0 = lane0 // HY_D
    assert b0 * HY_D == lane0
    cur = lambda j: pl.BlockSpec((1, rows, HY_D), lambda b, t: (b, t, b0 + j))
    prev = lambda j: pl.BlockSpec((1, SUBLANES, HY_D), lambda b, t: (b, jnp.maximum(t * hb - 1, 0), b0 + j))
    nxt = lambda j: pl.BlockSpec((1, SUBLANES, HY_D), lambda b, t: (b, jnp.minimum((t + 1) * hb, L // SUBLANES - 1), b0 + j))
    full = lambda shp: pl.BlockSpec(shp, lambda b, t: (0,) * len(shp))
    out = pl.BlockSpec((1, rows, HY_D), lambda b, t: (b, t, 0))
    return pl.pallas_call(
        functools.partial(_hyena_prep_kernel, rows=rows, nt=nt),
        grid=(Bn, nt),
        in_specs=[cur(0), cur(1), cur(2), prev(0), prev(1), prev(2), nxt(0), nxt(1), nxt(2),
                  full((3, 3 * HY_D)), full((1, 3 * HY_D))],
        out_specs=[out, out],
        out_shape=[jax.ShapeDtypeStruct((Bn, L, HY_D), f32)] * 2,
        compiler_params=pltpu.CompilerParams(dimension_semantics=("parallel", "parallel")),
        name="hyena_prep",
    )(*([proj] * 9), conv_w.astype(f32), conv_b.astype(f32).reshape(1, 3 * HY_D))


def hyena_mixer(tabs, proj, lane0, conv_w, conv_b, filt, skip):
    z, x0 = hyena_prep(proj, lane0, conv_w, conv_b)
    k2 = jnp.concatenate([filt[:1, 0] + filt[:1, 1], filt[1:, 0],
                          jnp.zeros((1, HY_D), f32), filt[1:, 1][::-1]], axis=0)
    return hyena_long_conv_gate(tabs, z, x0, k2, skip)


def rwkv7_mixer(proj, lora_lane0, mu, w0, w2, a0, a2, k_k, k_a, r_k, g2, ln_w, ln_b):
    r, k, v, kk, lw2, a_both = rw_prep(proj, lora_lane0, mu, w0, w2, a0, a2, k_k)
    y2 = wkv7_chunked(r, k, v, kk, lw2, a_both, k_a.astype(f32).reshape(1, RW_D))
    return rw_post(y2, r, k, v, a_both, proj, lora_lane0, k_a, r_k, ln_w, ln_b, g2)


def hier_moe(h, logits, bg, be, w1, w3, w2, layer):
    N, Dm = h.shape
    assert MOE_TOPK == 2
    g_logits = logits[:, :MOE_GROUPS] + bg.astype(f32)
    g_sel = jnp.argmax(g_logits, axis=-1)
    g_prob = jnp.take_along_axis(jax.nn.softmax(g_logits, axis=-1), g_sel[:, None], axis=-1)
    e_logits = (logits[:, MOE_GROUPS:MOE_GROUPS + MOE_EXPERTS] + be.astype(f32)).reshape(N, MOE_GROUPS, MOE_PER_GROUP)
    e_logits = jnp.take_along_axis(e_logits, g_sel[:, None, None], axis=1)[:, 0]
    top_val, top_idx = lax.top_k(e_logits, MOE_TOPK)
    gate = g_prob * jax.nn.softmax(top_val, axis=-1)
    expert = g_sel[:, None] * MOE_PER_GROUP + top_idx
    M = N * MOE_TOPK
    flat_e = expert.reshape(M).astype(jnp.int32)
    experts = jnp.arange(MOE_EXPERTS, dtype=jnp.int32)
    counts = jnp.sum((flat_e[:, None] == experts[None, :]).astype(jnp.int32), axis=0)
    padded = (counts + MOE_BLOCK - 1) // MOE_BLOCK * MOE_BLOCK
    pad_end = jnp.cumsum(padded)
    n_blocks = -(-M // MOE_BLOCK) + MOE_EXPERTS
    n_rows = n_blocks * MOE_BLOCK
    cum_need = jnp.cumsum(padded - counts)
    filler = jnp.arange(n_rows - M, dtype=jnp.int32)
    filler_e = jnp.sum((cum_need[None, :] <= filler[:, None]).astype(jnp.int32), axis=1)
    keys = jnp.concatenate([2 * flat_e, 2 * filler_e + 1])
    ids = jnp.concatenate([jnp.arange(M, dtype=jnp.int32), jnp.full((n_rows - M,), M, jnp.int32)])
    _, slot_src = lax.sort((keys, ids), num_keys=1)
    tok_src = jnp.where(slot_src < M, slot_src // MOE_TOPK, jnp.arange(n_rows, dtype=jnp.int32) % N)
    block_start = jnp.arange(n_blocks, dtype=pad_end.dtype) * MOE_BLOCK
    block_e = jnp.minimum(jnp.sum(pad_end[None, :] <= block_start[:, None], axis=1), MOE_EXPERTS - 1)
    y = moe_expert_ffn(h[tok_src], block_e, w1, w3, w2, layer)
    _, row_of = lax.sort((slot_src, jnp.arange(n_rows, dtype=jnp.int32)), num_keys=1)
    pos = row_of[:M].reshape(N, MOE_TOPK)
    return y[pos[:, 0]], y[pos[:, 1]], gate[:, 0:1], gate[:, 1:2]


def kernel(x, norm1_g, w_in, hy_conv_w, hy_conv_b, hy_w1, hy_b1, hy_w2, hy_b2, hy_w3, hy_b3, hy_wout, hy_freq, hy_skip, rw_mu, rw_w0, rw_w2, rw_a0, rw_a2, rw_kk, rw_ka, rw_rk, rw_g2, rw_ln_w, rw_ln_b, na_rpb, w_out, norm2_g, moe_wg, moe_bg, moe_we, moe_be, moe_w1, moe_w3, moe_w2, norm_f_g):
    Bn, L, _ = x.shape
    z_pos, t_pos = hyena_positional_features(L)
    dft_tabs = hyena_dft_tables(L)
    splits = np.cumsum(IN_SIZES)[:-1].tolist()
    hy_end, rkv_end = splits[0], splits[1]
    hy_lane0 = rkv_end - hy_end
    lora_lane0, na_lane0 = splits[1], splits[4]
    assert RW_D % LANES == 0 and lora_lane0 % LANES == 0 and na_lane0 % LANES == 0
    N = Bn * L
    x2 = x.reshape(N, D_MODEL)
    moe = None
    for l in range(DEPTH):
        w_in_l = jnp.concatenate([w_in[l][:, hy_end:rkv_end], w_in[l][:, :hy_end], w_in[l][:, rkv_end:]], axis=1)
        x2, proj = norm_proj(x2, norm1_g[l], w_in_l, moe)
        proj = proj.reshape(Bn, L, IN_D)
        filt = hyena_filters(z_pos, t_pos, hy_w1[l], hy_b1[l], hy_w2[l], hy_b2[l],
                             hy_w3[l], hy_b3[l], hy_wout[l], hy_freq[l])
        y_hy = hyena_mixer(dft_tabs, proj, hy_lane0, hy_conv_w[l], hy_conv_b[l], filt, hy_skip[l])
        y_rw = rwkv7_mixer(proj, lora_lane0, rw_mu[l], rw_w0[l], rw_w2[l], rw_a0[l],
                           rw_a2[l], rw_kk[l], rw_ka[l], rw_rk[l].reshape(RW_D), rw_g2[l], rw_ln_w[l], rw_ln_b[l])
        y_na = neighborhood_attention_pallas(proj, na_rpb[l], na_lane0 // LANES)
        x2, h2, logits = out_proj_norm_router(
            x2, y_hy.reshape(N, HY_D), y_rw.reshape(N, RW_D), y_na.reshape(N, NA_D), w_out[l], norm2_g[l],
            jnp.concatenate([moe_wg[l], moe_we[l]], axis=1))
        moe = hier_moe(h2, logits, moe_bg[l], moe_be[l], moe_w1, moe_w3, moe_w2, l)
    return final_rmsnorm(x2, norm_f_g, moe).reshape(Bn, L, D_MODEL)
```

```python
import functools
import math

import jax
import jax.numpy as jnp
import numpy as np
from jax import lax
from jax.experimental import pallas as pl
from jax.experimental.pallas import tpu as pltpu

f32 = jnp.float32
bf16 = jnp.bfloat16

D_MODEL = 1024
DEPTH = 2
GRID_W = 64
NORM_EPS = 1e-6
NEG_INF = -1e30

HY_D = D_MODEL // 4
HY_EMB = 33
HY_BANDS = (HY_EMB - 1) // 2
HY_FFN = 64
HY_MIN_DECAY = math.log(1e-2) / 1.5
HY_MAX_DECAY = math.log(1e-2) / 0.3

RW_N = 64
RW_D = D_MODEL // 2
RW_H = RW_D // RW_N
RW_W_LORA = 64
RW_A_LORA = 64
RW_G_LORA = 128
RW_GN_EPS = 64e-5

NA_HD = 64
NA_D = D_MODEL // 4
NA_H = NA_D // NA_HD
NA_KR = 8
NA_KC = 16

MIX_D = HY_D + RW_D + NA_D
IN_SIZES = (3 * HY_D, 3 * RW_D, RW_G_LORA, 2 * RW_W_LORA, 2 * RW_A_LORA, 3 * NA_D)
IN_D = sum(IN_SIZES)

MOE_GROUPS = 4
MOE_PER_GROUP = 8
MOE_EXPERTS = MOE_GROUPS * MOE_PER_GROUP
MOE_TOPK = 2
MOE_FF = 512
MOE_BLOCK = 512

LANES = 128
WKV_CHUNK = 64
WKV_TIME_BLOCK = 512
NA_ROWS_PER_STEP = 32
RW_ROWS_PER_STEP = 512
SUBLANES = 8
PROJ_ROWS_PER_STEP = 512
FFT_N2 = 128
FFT_LANE_BLOCK = 4096
FFT_K1_PER_STEP = 16
VMEM_LIMIT_BYTES = 48 * 1024 * 1024

_DN = {'nn': (((1,), (0,)), ((), ())), 'nt': (((1,), (1,)), ((), ())), 'tn': (((0,), (0,)), ((), ()))}


def _mm(a, b, dims='nn'):
    return lax.dot_general(a.astype(bf16), b.astype(bf16), _DN[dims], preferred_element_type=f32)


def _wkv_kernel(r_ref, k_ref, v_ref, kk_ref, lw_ref, a_ref, ka_ref, y_ref, ht_ref, *, tb, batch):
    T = WKV_CHUNK
    H2 = 2 * T
    nc = tb // T
    npairs = r_ref.shape[-1] // LANES
    mm = _mm

    @pl.when(pl.program_id(1) == 0)
    def _():
        ht_ref[...] = jnp.zeros_like(ht_ref)

    d = pl.program_id(0) // batch
    sign = 1 - 2 * d
    trow = lax.broadcasted_iota(jnp.int32, (T, H2), 0)
    lane = lax.broadcasted_iota(jnp.int32, (T, H2), 1)
    scol = lane % T
    tdiff = (trow - scol) * sign
    strict = tdiff > 0
    incl = tdiff >= 0
    same16 = (trow // 16) == (scol // 16)
    same32 = (trow // 32) == (scol // 32)
    off32 = same32 & jnp.logical_not(same16)
    off64 = jnp.logical_not(same32)
    eye2 = jnp.where(trow == scol, 1.0, 0.0).astype(f32)
    lo_lane = lane < T
    brow = lax.broadcasted_iota(jnp.int32, (H2, H2), 0)
    bcol = lax.broadcasted_iota(jnp.int32, (H2, H2), 1)
    same_head = (brow // T) == (bcol // T)
    crow = lax.broadcasted_iota(jnp.int32, (T, T), 0)
    ccol = lax.broadcasted_iota(jnp.int32, (T, T), 1)
    tri = jnp.where((crow - ccol) * sign >= 0, 1.0, 0.0).astype(bf16)
    is_bwd = d == 1

    def bd(x):
        return jnp.concatenate([jnp.where(lo_lane, x, 0.0), jnp.where(lo_lane, 0.0, x)], axis=0)

    inst = [(s, p) for s in range(nc) for p in range(npairs)]
    offs = [pl.multiple_of((s + d * (nc - 1 - 2 * s)) * T, T) for s in range(nc)]

    def load(ref, s, p):
        return ref[0, pl.ds(offs[s], T), p * LANES:(p + 1) * LANES]

    cs_l = []
    for s, p in inst:
        lw = load(lw_ref, s, p)
        l1 = lw.astype(bf16)
        l2 = (lw - l1.astype(f32)).astype(bf16)
        dd = lambda x: lax.dot_general(tri, x, _DN['nn'], preferred_element_type=f32)
        cs_l.append((dd(l1) + dd(l2), lw))
    ops = []
    for (s, p), (cs, lw) in zip(inst, cs_l):
        r = load(r_ref, s, p)
        k = load(k_ref, s, p)
        v = load(v_ref, s, p)
        kk = load(kk_ref, s, p)
        a = load(a_ref, s, p)
        ka = ka_ref[:, p * LANES:(p + 1) * LANES]
        kd = k * (1.0 + (a - 1.0) * ka)
        b = kk * a
        cs_end = jnp.where(is_bwd, cs[0:1, :], cs[T - 1:T, :])
        em = jnp.exp(-cs)
        e_end = jnp.exp(cs_end - cs)
        ops.append(dict(
            AR=jnp.concatenate([-kk * jnp.exp(cs - lw), r * jnp.exp(cs)], axis=0),
            BK=jnp.concatenate([bd(b * em), bd(kd * em)], axis=0),
            V=v, BKg=jnp.concatenate([b * e_end, kd * e_end], axis=0), g_end=jnp.exp(cs_end)))
    for o in ops:
        S = mm(o['AR'], o['BK'], 'nt')
        N = jnp.where(strict, S[:T, :H2], 0.0)
        o['AakArk'] = jnp.concatenate([jnp.where(strict, S[:T, H2:], 0.0), jnp.where(incl, S[T:, H2:], 0.0)], axis=0)
        o['Arb'] = jnp.where(incl, S[T:, :H2], 0.0)
        o['Nd'] = jnp.where(same16, N, 0.0)
        o['N32'] = jnp.where(off32, N, 0.0)
        o['N64'] = jnp.where(off64, N, 0.0)
    for o in ops:
        o['X'] = eye2 + o['Nd']
        o['P'] = mm(o['Nd'], bd(o['Nd']))
    for it in range(3):
        if it < 2:
            for o in ops:
                px = mm(o['P'], jnp.concatenate([bd(o['X']), bd(o['P'])], axis=1))
                o['X'] = o['X'] + px[:, :H2]
                o['P'] = px[:, H2:]
        else:
            for o in ops:
                o['X'] = o['X'] + mm(o['P'], bd(o['X']))
    for key in ('N32', 'N64'):
        for o in ops:
            o['Z'] = mm(o[key], bd(o['X']))
        for o in ops:
            o['X'] = o['X'] + mm(o['X'], bd(o['Z']))
    for o in ops:
        wy = mm(o['AakArk'], bd(o['V']))
        o['W0'] = wy[:T]
        o['Yv'] = wy[T:]
    for o in ops:
        o['XAW'] = mm(o['X'], jnp.concatenate([bd(o['AR'][:T]), bd(o['W0'])], axis=1))
    for o in ops:
        ax = mm(o['Arb'], jnp.concatenate([bd(o['XAW'][:, :H2]), bd(o['XAW'][:, H2:])], axis=1))
        o['Rhat'] = o['AR'][T:] + ax[:, :H2]
        o['Yc'] = ax[:, H2:] + o['Yv']
    for o in ops:
        lhs = jnp.concatenate([o['XAW'], jnp.concatenate([jnp.zeros((T, H2), f32), o['V']], axis=1)], axis=0)
        pq = mm(lhs, o['BKg'], 'tn')
        o['Pc'] = jnp.where(same_head, pq[:H2], 0.0)
        o['Qc'] = jnp.where(same_head, pq[H2:], 0.0)
    hts = [ht_ref[p] for p in range(npairs)]
    for s in range(nc):
        cur = [ops[s * npairs + p] for p in range(npairs)]
        ys = [mm(o['Rhat'], hts[p], 'nt') + o['Yc'] for p, o in enumerate(cur)]
        hts = [hts[p] * o['g_end'] + (mm(hts[p], o['Pc']) + o['Qc']) for p, o in enumerate(cur)]
        for p in range(npairs):
            y_ref[0, pl.ds(offs[s], T), p * LANES:(p + 1) * LANES] = ys[p]
    for p in range(npairs):
        ht_ref[p] = hts[p]


def wkv7_chunked(r, k, v, kk, lw2, a2, ka):
    Bn, L, C = r.shape
    tb = WKV_TIME_BLOCK
    nt = L // tb
    tmap = lambda i, t: t + (i // Bn) * (nt - 1 - 2 * t)
    shared = pl.BlockSpec((1, tb, C), lambda i, t: (i % Bn, tmap(i, t), 0))
    per_dir = pl.BlockSpec((1, tb, C), lambda i, t: (i % Bn, tmap(i, t), i // Bn))
    return pl.pallas_call(
        functools.partial(_wkv_kernel, tb=tb, batch=Bn),
        grid=(2 * Bn, nt),
        in_specs=[shared, shared, shared, shared, per_dir, per_dir, pl.BlockSpec((1, C), lambda i, t: (0, 0))],
        out_specs=pl.BlockSpec((1, tb, C), lambda i, t: (i, tmap(i, t), 0)),
        out_shape=jax.ShapeDtypeStruct((2 * Bn, L, C), f32),
        scratch_shapes=[pltpu.VMEM((C // LANES, LANES, LANES), f32)],
        compiler_params=pltpu.CompilerParams(dimension_semantics=("parallel", "arbitrary"),
                                             vmem_limit_bytes=VMEM_LIMIT_BYTES),
        name="wkv7_chunked",
    )(r, k, v, kk, lw2, a2, ka)


def _mm_exact_rhs(a, b_bf16):
    ah = a.astype(bf16)
    al = (a - ah.astype(f32)).astype(bf16)
    d = lambda x: lax.dot_general(x, b_bf16, _DN['nn'], preferred_element_type=f32)
    return d(ah) + d(al)


def _head_sum_matrix(scale):
    i = lax.broadcasted_iota(jnp.int32, (RW_D, RW_D), 0) // RW_N
    j = lax.broadcasted_iota(jnp.int32, (RW_D, RW_D), 1) // RW_N
    return jnp.where(i == j, scale, 0.0).astype(bf16)


def _rw_prep_kernel(r_ref, k_ref, v_ref, rp_ref, kp_ref, vp_ref, rn_ref, kn_ref, vn_ref, w_ref, a_ref,
                    mu_ref, w0_ref, w2_ref, a0_ref, a2_ref, kkw_ref,
                    ro_ref, ko_ref, vo_ref, kko_ref, lw_ref, ao_ref, *, rows, nt):
    t = pl.program_id(1)
    has_prev = (t > 0).astype(f32)
    has_next = (t < nt - 1).astype(f32)
    ridx = lax.broadcasted_iota(jnp.int32, (rows, RW_D), 0)

    def tshift(cur_ref, prev_ref, next_ref, j):
        u = cur_ref[0]
        prev_row = prev_ref[0, SUBLANES - 1:SUBLANES, :] * has_prev
        next_row = next_ref[0, 0:1, :] * has_next
        up = jnp.where(ridx == 0, prev_row, pltpu.roll(u, 1, axis=0))
        un = jnp.where(ridx == rows - 1, next_row, pltpu.roll(u, rows - 1, axis=0))
        return u + mu_ref[j, 0:1, :] * (up - u) + mu_ref[j, 1:2, :] * (un - u)

    r = tshift(r_ref, rp_ref, rn_ref, 0)
    k = tshift(k_ref, kp_ref, kn_ref, 1)
    v = tshift(v_ref, vp_ref, vn_ref, 2)
    ro_ref[0] = r
    ko_ref[0] = k
    vo_ref[0] = v
    kk = k * kkw_ref[...]
    ss = _mm_exact_rhs(kk * kk, _head_sum_matrix(1.0))
    kko_ref[0] = kk * lax.rsqrt(jnp.maximum(ss, 1e-24))
    wl = jnp.tanh(w_ref[0])
    al = a_ref[0]
    for d in range(2):
        wpre = w0_ref[d:d + 1, :] + _mm(wl[:, d * RW_W_LORA:(d + 1) * RW_W_LORA], w2_ref[d])
        lw_ref[0, :, d * RW_D:(d + 1) * RW_D] = -math.exp(-0.5) * jax.nn.sigmoid(wpre)
        av = a0_ref[d:d + 1, :] + _mm(al[:, d * RW_A_LORA:(d + 1) * RW_A_LORA], a2_ref[d])
        ao_ref[0, :, d * RW_D:(d + 1) * RW_D] = jax.nn.sigmoid(av)


def rw_prep(proj, lora_lane0, mu, w0, w2, a0, a2, k_k):
    Bn, L, _ = proj.shape
    C = RW_D
    rows = RW_ROWS_PER_STEP
    nt = L // rows
    hb = rows // SUBLANES
    lb = lora_lane0 // LANES
    cur = lambda j: pl.BlockSpec((1, rows, C), lambda b, t: (b, t, j))
    prev = lambda j: pl.BlockSpec((1, SUBLANES, C), lambda b, t: (b, jnp.maximum(t * hb - 1, 0), j))
    nxt = lambda j: pl.BlockSpec((1, SUBLANES, C), lambda b, t: (b, jnp.minimum((t + 1) * hb, L // SUBLANES - 1), j))
    lora_w = pl.BlockSpec((1, rows, LANES), lambda b, t: (b, t, lb + 1))
    lora_a = pl.BlockSpec((1, rows, LANES), lambda b, t: (b, t, lb + 2))
    full = lambda shp: pl.BlockSpec(shp, lambda b, t: (0,) * len(shp))
    out_c = pl.BlockSpec((1, rows, C), lambda b, t: (b, t, 0))
    out_2c = pl.BlockSpec((1, rows, 2 * C), lambda b, t: (b, t, 0))
    sds = lambda c: jax.ShapeDtypeStruct((Bn, L, c), f32)
    return pl.pallas_call(
        functools.partial(_rw_prep_kernel, rows=rows, nt=nt),
        grid=(Bn, nt),
        in_specs=[cur(0), cur(1), cur(2), prev(0), prev(1), prev(2), nxt(0), nxt(1), nxt(2), lora_w, lora_a,
                  full((3, 2, C)), full((2, C)), full((2, RW_W_LORA, C)), full((2, C)), full((2, RW_A_LORA, C)),
                  full((1, C))],
        out_specs=[out_c, out_c, out_c, out_c, out_2c, out_2c],
        out_shape=[sds(C), sds(C), sds(C), sds(C), sds(2 * C), sds(2 * C)],
        compiler_params=pltpu.CompilerParams(dimension_semantics=("parallel", "parallel"),
                                             vmem_limit_bytes=VMEM_LIMIT_BYTES),
        name="rwkv_prep",
    )(proj, proj, proj, proj, proj, proj, proj, proj, proj, proj, proj,
      mu.astype(f32), w0.astype(f32), w2.astype(bf16), a0.astype(f32), a2.astype(bf16), k_k.astype(f32).reshape(1, C))


def _rw_post_kernel(yf_ref, yb_ref, r_ref, k_ref, v_ref, a_ref, g_ref, ka_ref, rk_ref, lnw_ref, lnb_ref, g2_ref, o_ref):
    C = RW_D
    y = yf_ref[0] + yb_ref[0]
    avg = _head_sum_matrix(1.0 / RW_N)
    mean = _mm_exact_rhs(y, avg)
    yc = y - mean
    var = _mm_exact_rhs(yc * yc, avg)
    yn = yc * lax.rsqrt(var + RW_GN_EPS) * lnw_ref[...] + lnb_ref[...]
    a = a_ref[0]
    k = k_ref[0]
    ka = ka_ref[...]
    ksum = k * (1.0 + (a[:, :C] - 1.0) * ka) + k * (1.0 + (a[:, C:] - 1.0) * ka)
    coef = _mm_exact_rhs(r_ref[0] * ksum * rk_ref[...], _head_sum_matrix(1.0))
    gate = _mm(jax.nn.sigmoid(g_ref[0]), g2_ref[...])
    o_ref[0] = ((yn + coef * v_ref[0]) * gate).astype(o_ref.dtype)


def rw_post(y2, r, k, v, a2, proj, g_lane0, k_a, r_k, ln_w, ln_b, g2):
    Bn, L, C = r.shape
    rows = RW_ROWS_PER_STEP
    nt = L // rows
    gb = g_lane0 // LANES
    blk = lambda c: pl.BlockSpec((1, rows, c), lambda b, t: (b, t, 0))
    full = lambda shp: pl.BlockSpec(shp, lambda b, t: (0,) * len(shp))
    row = lambda x: x.astype(f32).reshape(1, C)
    return pl.pallas_call(
        _rw_post_kernel,
        grid=(Bn, nt),
        in_specs=[blk(C), pl.BlockSpec((1, rows, C), lambda b, t: (b + Bn, t, 0)), blk(C), blk(C), blk(C), blk(2 * C),
                  pl.BlockSpec((1, rows, LANES), lambda b, t: (b, t, gb)),
                  full((1, C)), full((1, C)), full((1, C)), full((1, C)), full((RW_G_LORA, C))],
        out_specs=blk(C),
        out_shape=jax.ShapeDtypeStruct((Bn, L, C), bf16),
        compiler_params=pltpu.CompilerParams(dimension_semantics=("parallel", "parallel"),
                                             vmem_limit_bytes=VMEM_LIMIT_BYTES),
        name="rwkv_post",
    )(y2, y2, r, k, v, a2, proj, row(k_a), row(r_k), row(ln_w), row(ln_b), g2.astype(bf16))


def _na_kernel(q_ref, k_ref, v_ref, bias_ref, o_ref, kb_ref, vb_ref, *, rows_per_step, n_rows, kr):
    W = GRID_W
    rb = pl.program_id(2)

    @pl.when(rb == 0)
    def _():
        kb_ref[...] = k_ref[0].astype(bf16)
        vb_ref[...] = v_ref[0].astype(bf16)

    lo_lane = lax.broadcasted_iota(jnp.int32, (W, LANES), 1) < NA_HD
    scale = NA_HD ** -0.5
    rows = []
    for j in range(rows_per_step):
        r = rb * rows_per_step + j
        start = jnp.clip(r - kr // 2, 0, n_rows - kr)
        rows.append((start, start - r + (NA_KR - 1)))
    s_list = []
    for j, (start, didx) in enumerate(rows):
        q = q_ref[0, j * W:(j + 1) * W, :] * scale
        qs = jnp.concatenate([jnp.where(lo_lane, q, 0.0), jnp.where(lo_lane, 0.0, q)], axis=0)
        kw = kb_ref[pl.ds(pl.multiple_of(start * W, W), kr * W), :]
        s_list.append(_mm(qs, kw, 'nt') + bias_ref[didx, 0])
    p_list = []
    for s in s_list:
        m = jnp.max(s, axis=-1, keepdims=True)
        p = jnp.exp(s - m)
        p_list.append((p, jnp.sum(p, axis=-1, keepdims=True)))
    for j, ((start, _), (p, l)) in enumerate(zip(rows, p_list)):
        vw = vb_ref[pl.ds(pl.multiple_of(start * W, W), kr * W), :]
        o = _mm(p, vw) / l
        o_ref[0, j * W:(j + 1) * W, :] = jnp.where(lo_lane, o[:W], o[W:]).astype(o_ref.dtype)


def na_bias_table(rpb, n_rows):
    W = GRID_W
    kr = min(NA_KR, n_rows)
    cols = jnp.arange(W)
    col_start = jnp.clip(cols - NA_KC // 2, 0, W - NA_KC)
    in_band = (cols[None, :] >= col_start[:, None]) & (cols[None, :] < col_start[:, None] + NA_KC)
    dc = jnp.clip(cols[None, :] - cols[:, None], -(NA_KC - 1), NA_KC - 1) + (NA_KC - 1)
    win = jnp.stack([rpb.astype(f32)[:, d:d + kr, :] for d in range(8)], axis=1)
    onehot = (dc[:, :, None] == jnp.arange(2 * NA_KC - 1)[None, None, :]).astype(f32)
    tab = jnp.einsum('hdic,qkc->hdqik', win, onehot, precision=lax.Precision.HIGHEST)
    tab = jnp.where(in_band[None, None, :, None, :], tab, NEG_INF)
    H = rpb.shape[0]
    tab = tab.reshape(H // 2, 2, 8, W, kr * W).transpose(2, 0, 1, 3, 4)
    return tab.reshape(8, H // 2, 2 * W, kr * W)


def neighborhood_attention_pallas(proj, rpb, lane_block0):
    Bn, L, _ = proj.shape
    W = GRID_W
    n_rows = L // W
    kr = min(NA_KR, n_rows)
    hp = NA_H * NA_HD // LANES
    rps = NA_ROWS_PER_STEP
    bias = na_bias_table(rpb, n_rows)
    kern = functools.partial(_na_kernel, rows_per_step=rps, n_rows=n_rows, kr=kr)
    return pl.pallas_call(
        kern,
        grid=(Bn, hp, n_rows // rps),
        in_specs=[pl.BlockSpec((1, rps * W, LANES), lambda b, h, r: (b, r, lane_block0 + h)),
                  pl.BlockSpec((1, L, LANES), lambda b, h, r: (b, 0, lane_block0 + hp + h)),
                  pl.BlockSpec((1, L, LANES), lambda b, h, r: (b, 0, lane_block0 + 2 * hp + h)),
                  pl.BlockSpec((8, 1, 2 * W, kr * W), lambda b, h, r: (0, h, 0, 0))],
        out_specs=pl.BlockSpec((1, rps * W, LANES), lambda b, h, r: (b, r, h)),
        out_shape=jax.ShapeDtypeStruct((Bn, L, NA_H * NA_HD), bf16),
        scratch_shapes=[pltpu.VMEM((L, LANES), bf16), pltpu.VMEM((L, LANES), bf16)],
        compiler_params=pltpu.CompilerParams(dimension_semantics=("parallel", "parallel", "arbitrary"),
                                             vmem_limit_bytes=VMEM_LIMIT_BYTES),
        name="neighborhood_attention",
    )(proj, proj, proj, bias)


def _moe_ffn_kernel(be_ref, x_ref, w1_ref, w3_ref, w2_ref, o_ref, w1b_ref, w3b_ref, w2b_ref):
    i = pl.program_id(0)

    @pl.when((i == 0) | (be_ref[i] != be_ref[jnp.maximum(i - 1, 0)]))
    def _():
        w1b_ref[...] = w1_ref[0, 0].astype(bf16)
        w3b_ref[...] = w3_ref[0, 0].astype(bf16)
        w2b_ref[...] = w2_ref[0, 0].astype(bf16)

    x = x_ref[...].astype(bf16)
    h1 = jnp.dot(x, w1b_ref[...], preferred_element_type=f32)
    h3 = jnp.dot(x, w3b_ref[...], preferred_element_type=f32)
    g = h1 * jax.nn.sigmoid(h1) * h3
    o_ref[...] = jnp.dot(g.astype(bf16), w2b_ref[...], preferred_element_type=f32).astype(o_ref.dtype)


def moe_expert_ffn(xs, block_e, w1, w3, w2, layer):
    rows, Dm = xs.shape
    n_blocks = rows // MOE_BLOCK
    grid_spec = pltpu.PrefetchScalarGridSpec(
        num_scalar_prefetch=1,
        grid=(n_blocks,),
        in_specs=[pl.BlockSpec((MOE_BLOCK, Dm), lambda i, be: (i, 0)),
                  pl.BlockSpec((1, 1, Dm, MOE_FF), lambda i, be: (layer, be[i], 0, 0)),
                  pl.BlockSpec((1, 1, Dm, MOE_FF), lambda i, be: (layer, be[i], 0, 0)),
                  pl.BlockSpec((1, 1, MOE_FF, Dm), lambda i, be: (layer, be[i], 0, 0))],
        out_specs=pl.BlockSpec((MOE_BLOCK, Dm), lambda i, be: (i, 0)),
        scratch_shapes=[pltpu.VMEM((Dm, MOE_FF), bf16), pltpu.VMEM((Dm, MOE_FF), bf16), pltpu.VMEM((MOE_FF, Dm), bf16)],
    )
    return pl.pallas_call(
        _moe_ffn_kernel,
        grid_spec=grid_spec,
        out_shape=jax.ShapeDtypeStruct((rows, Dm), bf16),
        compiler_params=pltpu.CompilerParams(dimension_semantics=("arbitrary",), vmem_limit_bytes=VMEM_LIMIT_BYTES),
        name="moe_expert_ffn",
    )(block_e.astype(jnp.int32), xs, w1.astype(f32), w3.astype(f32), w2.astype(f32))


def _rms(x):
    return x * lax.rsqrt(jnp.mean(x * x, axis=-1, keepdims=True) + NORM_EPS)


def _moe_combine(x_ref, y0_ref, y1_ref, g0_ref, g1_ref):
    return x_ref[...] + (g0_ref[...] * y0_ref[...].astype(f32) + g1_ref[...] * y1_ref[...].astype(f32))


def _moe_specs(tm, D):
    row = pl.BlockSpec((tm, D), lambda i: (i, 0))
    col = pl.BlockSpec((tm, 1), lambda i: (i, 0))
    return [row, row, col, col]


def _norm_proj_kernel(*refs, combine):
    if combine:
        x_ref, y0_ref, y1_ref, g0_ref, g1_ref, g_ref, w_ref, o_ref, xo_ref = refs
        x = _moe_combine(x_ref, y0_ref, y1_ref, g0_ref, g1_ref)
        xo_ref[...] = x
    else:
        x_ref, g_ref, w_ref, o_ref = refs
        x = x_ref[...]
    h = (_rms(x) * g_ref[...]).astype(bf16)
    o_ref[...] = jnp.dot(h, w_ref[...], preferred_element_type=f32)


def norm_proj(x2, g, w, moe=None):
    N, D = x2.shape
    F = w.shape[1]
    tm = PROJ_ROWS_PER_STEP if moe is not None else 2 * PROJ_ROWS_PER_STEP
    row = pl.BlockSpec((tm, D), lambda i: (i, 0))
    out_row = pl.BlockSpec((tm, F), lambda i: (i, 0))
    par = [pl.BlockSpec((1, D), lambda i: (0, 0)),
           pl.BlockSpec((D, F), lambda i: (0, 0), pipeline_mode=pl.Buffered(1))]
    params = dict(compiler_params=pltpu.CompilerParams(dimension_semantics=("parallel",),
                                                       vmem_limit_bytes=VMEM_LIMIT_BYTES), name="norm_proj")
    gw = (g.astype(f32).reshape(1, D), w.astype(bf16))
    if moe is None:
        proj = pl.pallas_call(functools.partial(_norm_proj_kernel, combine=False), grid=(N // tm,),
                              in_specs=[row] + par, out_specs=out_row,
                              out_shape=jax.ShapeDtypeStruct((N, F), f32), **params)(x2, *gw)
        return x2, proj
    proj, x_new = pl.pallas_call(functools.partial(_norm_proj_kernel, combine=True), grid=(N // tm,),
                                 in_specs=[row] + _moe_specs(tm, D) + par, out_specs=[out_row, row],
                                 out_shape=[jax.ShapeDtypeStruct((N, F), f32), jax.ShapeDtypeStruct((N, D), f32)],
                                 **params)(x2, *moe, *gw)
    return x_new, proj


def _out_proj_kernel(x_ref, yh_ref, yr_ref, yn_ref, wh_ref, wr_ref, wn_ref, g_ref, wrt_ref, xo_ref, h_ref, lg_ref):
    mix = (jnp.dot(yh_ref[...].astype(bf16), wh_ref[...], preferred_element_type=f32)
           + jnp.dot(yr_ref[...].astype(bf16), wr_ref[...], preferred_element_type=f32)
           + jnp.dot(yn_ref[...].astype(bf16), wn_ref[...], preferred_element_type=f32))
    x = x_ref[...] + mix
    xo_ref[...] = x
    h = (_rms(x) * g_ref[...]).astype(bf16)
    h_ref[...] = h
    lg_ref[...] = jnp.dot(h, wrt_ref[...], preferred_element_type=f32)


def out_proj_norm_router(x2, y_hy, y_rw, y_na, w_out, g, w_router):
    N, D = x2.shape
    tm = PROJ_ROWS_PER_STEP
    d_hy, d_rw, d_na = y_hy.shape[1], y_rw.shape[1], y_na.shape[1]
    nr = w_router.shape[1]
    w_router = jnp.pad(w_router.astype(bf16), ((0, 0), (0, LANES - nr)))
    wb = w_out.astype(bf16)
    row = lambda c: pl.BlockSpec((tm, c), lambda i: (i, 0))
    full = lambda r, c: pl.BlockSpec((r, c), lambda i: (0, 0))
    return pl.pallas_call(
        _out_proj_kernel,
        grid=(N // tm,),
        in_specs=[row(D), row(d_hy), row(d_rw), row(d_na), full(d_hy, D), full(d_rw, D), full(d_na, D), full(1, D),
                  full(D, LANES)],
        out_specs=[row(D), row(D), row(LANES)],
        out_shape=[jax.ShapeDtypeStruct((N, D), f32), jax.ShapeDtypeStruct((N, D), bf16),
                   jax.ShapeDtypeStruct((N, LANES), f32)],
        compiler_params=pltpu.CompilerParams(dimension_semantics=("parallel",), vmem_limit_bytes=VMEM_LIMIT_BYTES),
        name="out_proj_norm_router",
    )(x2, y_hy, y_rw, y_na, wb[:d_hy], wb[d_hy:d_hy + d_rw], wb[d_hy + d_rw:], g.astype(f32).reshape(1, D), w_router)


def _final_norm_kernel(x_ref, y0_ref, y1_ref, g0_ref, g1_ref, g_ref, o_ref):
    o_ref[...] = _rms(_moe_combine(x_ref, y0_ref, y1_ref, g0_ref, g1_ref)) * g_ref[...]


def final_rmsnorm(x2, g, moe):
    N, D = x2.shape
    tm = PROJ_ROWS_PER_STEP
    row = pl.BlockSpec((tm, D), lambda i: (i, 0))
    return pl.pallas_call(
        _final_norm_kernel,
        grid=(N // tm,),
        in_specs=[row] + _moe_specs(tm, D) + [pl.BlockSpec((1, D), lambda i: (0, 0))],
        out_specs=row,
        out_shape=jax.ShapeDtypeStruct((N, D), f32),
        compiler_params=pltpu.CompilerParams(dimension_semantics=("parallel",), vmem_limit_bytes=VMEM_LIMIT_BYTES),
        name="final_rmsnorm",
    )(x2, *moe, g.astype(f32).reshape(1, D))


def hyena_positional_features(L):
    t = jnp.linspace(0.0, 1.0, L, dtype=f32)[:, None]
    w = (2.0 * math.pi / L) * jnp.arange(L, dtype=f32)[:, None]
    f = jnp.linspace(1e-4, HY_BANDS - 1, HY_BANDS, dtype=f32)[None, :]
    z = jnp.concatenate([t, jnp.cos(f * w), -jnp.sin(f * w)], axis=-1)
    return z, t


def hyena_filters(z, t, w1, b1, w2, b2, w3, b3, wout, freq):
    fr = freq.astype(f32)
    act = lambda u: jnp.sin(fr * u)
    h = act(z @ w1.astype(f32) + b1.astype(f32))
    h = act(h @ w2.astype(f32) + b2.astype(f32))
    h = act(h @ w3.astype(f32) + b3.astype(f32))
    h = (h @ wout.astype(f32)).reshape(-1, 2, HY_D)
    deltas = jnp.abs(jnp.linspace(HY_MIN_DECAY, HY_MAX_DECAY, HY_D, dtype=f32))
    h = h * jnp.exp(-t[:, :, None] * deltas)
    return h * lax.rsqrt(jnp.sum(h * h, axis=(0, 1), keepdims=True) + 1e-6)


def _split_bf16(x):
    hi = x.astype(bf16)
    return hi, (x - hi.astype(f32)).astype(bf16)


def _dot3(m_hi, m_lo, x_hi, x_lo):
    d = lambda a, b: lax.dot_general(a, b, _DN['nn'], preferred_element_type=f32)
    return d(m_hi, x_hi) + (d(m_hi, x_lo) + d(m_lo, x_hi))


def _cmatmul(mr, mi, xr, xi):
    xrh, xrl = _split_bf16(xr)
    rr = _dot3(mr[0], mr[1], xrh, xrl)
    ir = _dot3(mi[0], mi[1], xrh, xrl)
    if xi is None:
        return rr, ir
    xih, xil = _split_bf16(xi)
    ii = _dot3(mi[0], mi[1], xih, xil)
    ri = _dot3(mr[0], mr[1], xih, xil)
    return rr - ii, ri + ir


def _row_dft_kernel(mrh_ref, mrl_ref, mih_ref, mil_ref, *refs, real_input):
    if real_input:
        ur_ref, or_ref, oi_ref = refs
        ui = None
    else:
        ur_ref, ui_ref, or_ref, oi_ref = refs
        ui = ui_ref[0]
    o_r, o_i = _cmatmul((mrh_ref[...], mrl_ref[...]), (mih_ref[...], mil_ref[...]), ur_ref[0], ui)
    or_ref[0] = o_r
    oi_ref[0] = o_i


def row_dft(tabs, u, packed):
    rows, r_in, W = u.shape
    P = rows // 2 if packed else rows
    r_out = tabs[0].shape[0]
    wb = min(FFT_LANE_BLOCK, W)
    tab_spec = pl.BlockSpec((r_out, r_in), lambda p, j: (0, 0))
    re_spec = pl.BlockSpec((1, r_in, wb), lambda p, j: (p, 0, j))
    im_spec = pl.BlockSpec((1, r_in, wb), lambda p, j: (p + P, 0, j))
    out_spec = pl.BlockSpec((1, r_out, wb), lambda p, j: (p, 0, j))
    ins = [u, u] if packed else [u]
    return pl.pallas_call(
        functools.partial(_row_dft_kernel, real_input=not packed),
        grid=(P, W // wb),
        in_specs=[tab_spec] * 4 + ([re_spec, im_spec] if packed else [re_spec]),
        out_specs=[out_spec, out_spec],
        out_shape=[jax.ShapeDtypeStruct((P, r_out, W), f32)] * 2,
        compiler_params=pltpu.CompilerParams(dimension_semantics=("parallel", "parallel"),
                                             vmem_limit_bytes=VMEM_LIMIT_BYTES),
        name="hyena_row_dft",
    )(*tabs, *ins)


def _row_idft_gate_kernel(mrh_ref, mrl_ref, mih_ref, mil_ref, dr_ref, di_ref, z0_ref, z1_ref, x0_ref, x1_ref,
                          skip_ref, o0_ref, o1_ref):
    y_r, y_i = _cmatmul((mrh_ref[...], mrl_ref[...]), (mih_ref[...], mil_ref[...]), dr_ref[0], di_ref[0])
    skip = skip_ref[...]
    o0_ref[0] = (x0_ref[0] * (y_r + z0_ref[0] * skip)).astype(o0_ref.dtype)
    o1_ref[0] = (x1_ref[0] * (y_i + z1_ref[0] * skip)).astype(o1_ref.dtype)


def row_idft_gate(tabs, dr, di, z, x0, skip_row):
    P, r_in, W = dr.shape
    r_out = tabs[0].shape[0]
    wb = min(FFT_LANE_BLOCK, W)
    tab_spec = pl.BlockSpec((r_out, r_in), lambda p, j: (0, 0))
    d_spec = pl.BlockSpec((1, r_in, wb), lambda p, j: (p, 0, j))
    lo = pl.BlockSpec((1, r_out, wb), lambda p, j: (p, 0, j))
    hi = pl.BlockSpec((1, r_out, wb), lambda p, j: (p + P, 0, j))
    o0, o1 = pl.pallas_call(
        _row_idft_gate_kernel,
        grid=(P, W // wb),
        in_specs=[tab_spec] * 4 + [d_spec, d_spec, lo, hi, lo, hi, pl.BlockSpec((1, wb), lambda p, j: (0, j))],
        out_specs=[lo, lo],
        out_shape=[jax.ShapeDtypeStruct((P, r_out, W), bf16)] * 2,
        compiler_params=pltpu.CompilerParams(dimension_semantics=("parallel", "parallel"),
                                             vmem_limit_bytes=VMEM_LIMIT_BYTES),
        name="hyena_row_idft_gate",
    )(*tabs, dr, di, z, z, x0, x0, skip_row)
    return jnp.concatenate([o0, o1], axis=0)


def _col_dft_kernel(grh_ref, grl_ref, gih_ref, gil_ref, ar_ref, ai_ref, br_ref, bi_ref, *, k1_per_step):
    for j in range(k1_per_step):
        b_r, b_i = _cmatmul((grh_ref[j], grl_ref[j]), (gih_ref[j], gil_ref[j]), ar_ref[0, j], ai_ref[0, j])
        br_ref[0, j] = b_r
        bi_ref[0, j] = b_i


def _col_conv_kernel(grh_ref, grl_ref, gih_ref, gil_ref, trh_ref, trl_ref, tih_ref, til_ref,
                     ar_ref, ai_ref, kr_ref, ki_ref, dr_ref, di_ref, *, k1_per_step):
    for j in range(k1_per_step):
        b_r, b_i = _cmatmul((grh_ref[j], grl_ref[j]), (gih_ref[j], gil_ref[j]), ar_ref[0, j], ai_ref[0, j])
        k_r = kr_ref[0, j]
        k_i = ki_ref[0, j]
        c_r = b_r * k_r - b_i * k_i
        c_i = b_r * k_i + b_i * k_r
        d_r, d_i = _cmatmul((trh_ref[j], trl_ref[j]), (tih_ref[j], til_ref[j]), c_r, c_i)
        dr_ref[0, j] = d_r
        di_ref[0, j] = d_i


def col_stage(g_tabs, gt_tabs, ar, ai, kr=None, ki=None):
    P, n1, n2, C = ar.shape
    kb = min(FFT_K1_PER_STEP, n1)
    g_spec = pl.BlockSpec((kb, n2, n2), lambda p, j: (j, 0, 0))
    a_spec = pl.BlockSpec((1, kb, n2, C), lambda p, j: (p, j, 0, 0))
    k_spec = pl.BlockSpec((1, kb, n2, C), lambda p, j: (0, j, 0, 0))
    common = dict(
        grid=(P, n1 // kb),
        out_specs=[a_spec, a_spec],
        out_shape=[jax.ShapeDtypeStruct((P, n1, n2, C), f32)] * 2,
        compiler_params=pltpu.CompilerParams(dimension_semantics=("parallel", "parallel"),
                                             vmem_limit_bytes=VMEM_LIMIT_BYTES),
    )
    if kr is None:
        return pl.pallas_call(functools.partial(_col_dft_kernel, k1_per_step=kb),
                              in_specs=[g_spec] * 4 + [a_spec] * 2, name="hyena_col_dft", **common)(*g_tabs, ar, ai)
    return pl.pallas_call(functools.partial(_col_conv_kernel, k1_per_step=kb),
                          in_specs=[g_spec] * 8 + [a_spec] * 2 + [k_spec] * 2, name="hyena_col_conv", **common)(
        *g_tabs, *gt_tabs, ar, ai, kr, ki)


def _bf16_tables(m):
    out = []
    for part in (np.real(m), np.imag(m)):
        x = jnp.asarray(part, f32)
        hi = x.astype(bf16)
        out += [hi, (x - hi.astype(f32)).astype(bf16)]
    return out


def hyena_dft_tables(L):
    n2 = FFT_N2
    n1 = 2 * L // n2
    N = n1 * n2
    a = np.arange(n1)
    fa = np.exp(-2j * np.pi * np.outer(a, a) / n1)
    k1 = np.arange(n1)[:, None, None]
    k2 = np.arange(n2)[None, :, None]
    nn = np.arange(n2)[None, None, :]
    g = np.exp(-2j * np.pi * nn * (k1 + n1 * k2) / N)
    gt = np.conj(np.transpose(g, (0, 2, 1)))
    fc = np.conj(fa).T / N
    return dict(fa=_bf16_tables(fa), fa_half=_bf16_tables(fa[:, :n1 // 2]), g=_bf16_tables(g), gt=_bf16_tables(gt),
                fc_half=_bf16_tables(fc[:n1 // 2]))


def hyena_long_conv_gate(tabs, z, x0, k2, skip):
    Bn, L, C = z.shape
    n2 = FFT_N2
    n1 = 2 * L // n2
    P = Bn // 2
    W = n2 * C
    ar, ai = row_dft(tabs['fa'], k2.reshape(1, n1, W), packed=False)
    kr, ki = col_stage(tabs['g'], tabs['gt'], ar.reshape(1, n1, n2, C), ai.reshape(1, n1, n2, C))
    zv = z.reshape(Bn, n1 // 2, W)
    ar, ai = row_dft(tabs['fa_half'], zv, packed=True)
    dr, di = col_stage(tabs['g'], tabs['gt'], ar.reshape(P, n1, n2, C), ai.reshape(P, n1, n2, C), kr, ki)
    out = row_idft_gate(tabs['fc_half'], dr.reshape(P, n1, W), di.reshape(P, n1, W), zv, x0.reshape(Bn, n1 // 2, W),
                        jnp.tile(skip.astype(f32), n2).reshape(1, W))
    return out.reshape(Bn, L, C)


def _hyena_prep_kernel(*refs, rows, nt):
    cur, prev, nxt = refs[0:3], refs[3:6], refs[6:9]
    w_ref, b_ref, z_ref, x0_ref = refs[9:]
    t = pl.program_id(1)
    has_prev = (t > 0).astype(f32)
    has_next = (t < nt - 1).astype(f32)
    ridx = lax.broadcasted_iota(jnp.int32, (rows, HY_D), 0)

    def conv(j):
        u = cur[j][0]
        prev_row = prev[j][0, SUBLANES - 1:SUBLANES, :] * has_prev
        next_row = nxt[j][0, 0:1, :] * has_next
        up = jnp.where(ridx == 0, prev_row, pltpu.roll(u, 1, axis=0))
        un = jnp.where(ridx == rows - 1, next_row, pltpu.roll(u, rows - 1, axis=0))
        sl = slice(j * HY_D, (j + 1) * HY_D)
        return up * w_ref[0:1, sl] + u * w_ref[1:2, sl] + un * w_ref[2:3, sl] + b_ref[:, sl]

    x0_ref[0] = conv(0)
    z_ref[0] = conv(2) * conv(1)


def hyena_prep(proj, lane0, conv_w, conv_b):
    Bn, L, _ = proj.shape
    rows = RW_ROWS_PER_STEP
    nt = L // rows
    hb = rows // SUBLANES
    b0 = lane0 // HY_D
    assert b0 * HY_D == lane0
    cur = lambda j: pl.BlockSpec((1, rows, HY_D), lambda b, t: (b, t, b0 + j))
    prev = lambda j: pl.BlockSpec((1, SUBLANES, HY_D), lambda b, t: (b, jnp.maximum(t * hb - 1, 0), b0 + j))
    nxt = lambda j: pl.BlockSpec((1, SUBLANES, HY_D), lambda b, t: (b, jnp.minimum((t + 1) * hb, L // SUBLANES - 1), b0 + j))
    full = lambda shp: pl.BlockSpec(shp, lambda b, t: (0,) * len(shp))
    out = pl.BlockSpec((1, rows, HY_D), lambda b, t: (b, t, 0))
    return pl.pallas_call(
        functools.partial(_hyena_prep_kernel, rows=rows, nt=nt),
        grid=(Bn, nt),
        in_specs=[cur(0), cur(1), cur(2), prev(0), prev(1), prev(2), nxt(0), nxt(1), nxt(2),
                  full((3, 3 * HY_D)), full((1, 3 * HY_D))],
        out_specs=[out, out],
        out_shape=[jax.ShapeDtypeStruct((Bn, L, HY_D), f32)] * 2,
        compiler_params=pltpu.CompilerParams(dimension_semantics=("parallel", "parallel")),
        name="hyena_prep",
    )(*([proj] * 9), conv_w.astype(f32), conv_b.astype(f32).reshape(1, 3 * HY_D))


def hyena_mixer(tabs, proj, lane0, conv_w, conv_b, filt, skip):
    z, x0 = hyena_prep(proj, lane0, conv_w, conv_b)
    k2 = jnp.concatenate([filt[:1, 0] + filt[:1, 1], filt[1:, 0],
                          jnp.zeros((1, HY_D), f32), filt[1:, 1][::-1]], axis=0)
    return hyena_long_conv_gate(tabs, z, x0, k2, skip)


def rwkv7_mixer(proj, lora_lane0, mu, w0, w2, a0, a2, k_k, k_a, r_k, g2, ln_w, ln_b):
    r, k, v, kk, lw2, a_both = rw_prep(proj, lora_lane0, mu, w0, w2, a0, a2, k_k)
    y2 = wkv7_chunked(r, k, v, kk, lw2, a_both, k_a.astype(f32).reshape(1, RW_D))
    return rw_post(y2, r, k, v, a_both, proj, lora_lane0, k_a, r_k, ln_w, ln_b, g2)


def hier_moe(h, logits, bg, be, w1, w3, w2, layer):
    N, Dm = h.shape
    assert MOE_TOPK == 2
    g_logits = logits[:, :MOE_GROUPS] + bg.astype(f32)
    g_sel = jnp.argmax(g_logits, axis=-1)
    g_prob = jnp.take_along_axis(jax.nn.softmax(g_logits, axis=-1), g_sel[:, None], axis=-1)
    e_logits = (logits[:, MOE_GROUPS:MOE_GROUPS + MOE_EXPERTS] + be.astype(f32)).reshape(N, MOE_GROUPS, MOE_PER_GROUP)
    e_logits = jnp.take_along_axis(e_logits, g_sel[:, None, None], axis=1)[:, 0]
    top_val, top_idx = lax.top_k(e_logits, MOE_TOPK)
    gate = g_prob * jax.nn.softmax(top_val, axis=-1)
    expert = g_sel[:, None] * MOE_PER_GROUP + top_idx
    M = N * MOE_TOPK
    flat_e = expert.reshape(M).astype(jnp.int32)
    experts = jnp.arange(MOE_EXPERTS, dtype=jnp.int32)
    counts = jnp.sum((flat_e[:, None] == experts[None, :]).astype(jnp.int32), axis=0)
    padded = (counts + MOE_BLOCK - 1) // MOE_BLOCK * MOE_BLOCK
    pad_end = jnp.cumsum(padded)
    n_blocks = -(-M // MOE_BLOCK) + MOE_EXPERTS
    n_rows = n_blocks * MOE_BLOCK
    cum_need = jnp.cumsum(padded - counts)
    filler = jnp.arange(n_rows - M, dtype=jnp.int32)
    filler_e = jnp.sum((cum_need[None, :] <= filler[:, None]).astype(jnp.int32), axis=1)
    keys = jnp.concatenate([2 * flat_e, 2 * filler_e + 1])
    ids = jnp.concatenate([jnp.arange(M, dtype=jnp.int32), jnp.full((n_rows - M,), M, jnp.int32)])
    _, slot_src = lax.sort((keys, ids), num_keys=1)
    tok_src = jnp.where(slot_src < M, slot_src // MOE_TOPK, jnp.arange(n_rows, dtype=jnp.int32) % N)
    block_start = jnp.arange(n_blocks, dtype=pad_end.dtype) * MOE_BLOCK
    block_e = jnp.minimum(jnp.sum(pad_end[None, :] <= block_start[:, None], axis=1), MOE_EXPERTS - 1)
    y = moe_expert_ffn(h[tok_src], block_e, w1, w3, w2, layer)
    _, row_of = lax.sort((slot_src, jnp.arange(n_rows, dtype=jnp.int32)), num_keys=1)
    pos = row_of[:M].reshape(N, MOE_TOPK)
    return y[pos[:, 0]], y[pos[:, 1]], gate[:, 0:1], gate[:, 1:2]


def kernel(x, norm1_g, w_in, hy_conv_w, hy_conv_b, hy_w1, hy_b1, hy_w2, hy_b2, hy_w3, hy_b3, hy_wout, hy_freq, hy_skip, rw_mu, rw_w0, rw_w2, rw_a0, rw_a2, rw_kk, rw_ka, rw_rk, rw_g2, rw_ln_w, rw_ln_b, na_rpb, w_out, norm2_g, moe_wg, moe_bg, moe_we, moe_be, moe_w1, moe_w3, moe_w2, norm_f_g):
    Bn, L, _ = x.shape
    z_pos, t_pos = hyena_positional_features(L)
    dft_tabs = hyena_dft_tables(L)
    splits = np.cumsum(IN_SIZES)[:-1].tolist()
    hy_end, rkv_end = splits[0], splits[1]
    hy_lane0 = rkv_end - hy_end
    lora_lane0, na_lane0 = splits[1], splits[4]
    assert RW_D % LANES == 0 and lora_lane0 % LANES == 0 and na_lane0 % LANES == 0
    N = Bn * L
    x2 = x.reshape(N, D_MODEL)
    moe = None
    for l in range(DEPTH):
        w_in_l = jnp.concatenate([w_in[l][:, hy_end:rkv_end], w_in[l][:, :hy_end], w_in[l][:, rkv_end:]], axis=1)
        x2, proj = norm_proj(x2, norm1_g[l], w_in_l, moe)
        proj = proj.reshape(Bn, L, IN_D)
        filt = hyena_filters(z_pos, t_pos, hy_w1[l], hy_b1[l], hy_w2[l], hy_b2[l],
                             hy_w3[l], hy_b3[l], hy_wout[l], hy_freq[l])
        y_hy = hyena_mixer(dft_tabs, proj, hy_lane0, hy_conv_w[l], hy_conv_b[l], filt, hy_skip[l])
        y_rw = rwkv7_mixer(proj, lora_lane0, rw_mu[l], rw_w0[l], rw_w2[l], rw_a0[l],
                           rw_a2[l], rw_kk[l], rw_ka[l], rw_rk[l].reshape(RW_D), rw_g2[l], rw_ln_w[l], rw_ln_b[l])
        y_na = neighborhood_attention_pallas(proj, na_rpb[l], na_lane0 // LANES)
        x2, h2, logits = out_proj_norm_router(
            x2, y_hy.reshape(N, HY_D), y_rw.reshape(N, RW_D), y_na.reshape(N, NA_D), w_out[l], norm2_g[l],
            jnp.concatenate([moe_wg[l], moe_we[l]], axis=1))
        moe = hier_moe(h2, logits, moe_bg[l], moe_be[l], moe_w1, moe_w3, moe_w2, l)
    return final_rmsnorm(x2, norm_f_g, moe).reshape(Bn, L, D_MODEL)
```

```python
import functools
import math

import jax
import jax.numpy as jnp
import numpy as np
from jax import lax
from jax.experimental import pallas as pl
from jax.experimental.pallas import tpu as pltpu

f32 = jnp.float32
bf16 = jnp.bfloat16

D_MODEL = 1024
DEPTH = 2
GRID_W = 64
NORM_EPS = 1e-6
NEG_INF = -1e30

HY_D = D_MODEL // 4
HY_EMB = 33
HY_BANDS = (HY_EMB - 1) // 2
HY_FFN = 64
HY_MIN_DECAY = math.log(1e-2) / 1.5
HY_MAX_DECAY = math.log(1e-2) / 0.3

RW_N = 64
RW_D = D_MODEL // 2
RW_H = RW_D // RW_N
RW_W_LORA = 64
RW_A_LORA = 64
RW_G_LORA = 128
RW_GN_EPS = 64e-5

NA_HD = 64
NA_D = D_MODEL // 4
NA_H = NA_D // NA_HD
NA_KR = 8
NA_KC = 16

MIX_D = HY_D + RW_D + NA_D
IN_SIZES = (3 * HY_D, 3 * RW_D, RW_G_LORA, 2 * RW_W_LORA, 2 * RW_A_LORA, 3 * NA_D)
IN_D = sum(IN_SIZES)

MOE_GROUPS = 4
MOE_PER_GROUP = 8
MOE_EXPERTS = MOE_GROUPS * MOE_PER_GROUP
MOE_TOPK = 2
MOE_FF = 512
MOE_BLOCK = 512

LANES = 128
WKV_CHUNK = 64
WKV_TIME_BLOCK = 512
NA_ROWS_PER_STEP = 32
RW_ROWS_PER_STEP = 512
SUBLANES = 8
PROJ_ROWS_PER_STEP = 512
FFT_N2 = 128
FFT_LANE_BLOCK = 4096
FFT_K1_PER_STEP = 16
VMEM_LIMIT_BYTES = 48 * 1024 * 1024

_DN = {'nn': (((1,), (0,)), ((), ())), 'nt': (((1,), (1,)), ((), ())), 'tn': (((0,), (0,)), ((), ()))}


def _mm(a, b, dims='nn'):
    return lax.dot_general(a.astype(bf16), b.astype(bf16), _DN[dims], preferred_element_type=f32)


def _wkv_kernel(r_ref, k_ref, v_ref, kk_ref, lw_ref, a_ref, ka_ref, y_ref, ht_ref, *, tb, batch):
    T = WKV_CHUNK
    H2 = 2 * T
    nc = tb // T
    npairs = r_ref.shape[-1] // LANES
    mm = _mm

    @pl.when(pl.program_id(1) == 0)
    def _():
        ht_ref[...] = jnp.zeros_like(ht_ref)

    d = pl.program_id(0) // batch
    sign = 1 - 2 * d
    trow = lax.broadcasted_iota(jnp.int32, (T, H2), 0)
    lane = lax.broadcasted_iota(jnp.int32, (T, H2), 1)
    scol = lane % T
    tdiff = (trow - scol) * sign
    strict = tdiff > 0
    incl = tdiff >= 0
    same16 = (trow // 16) == (scol // 16)
    same32 = (trow // 32) == (scol // 32)
    off32 = same32 & jnp.logical_not(same16)
    off64 = jnp.logical_not(same32)
    eye2 = jnp.where(trow == scol, 1.0, 0.0).astype(f32)
    lo_lane = lane < T
    brow = lax.broadcasted_iota(jnp.int32, (H2, H2), 0)
    bcol = lax.broadcasted_iota(jnp.int32, (H2, H2), 1)
    same_head = (brow // T) == (bcol // T)
    crow = lax.broadcasted_iota(jnp.int32, (T, T), 0)
    ccol = lax.broadcasted_iota(jnp.int32, (T, T), 1)
    tri = jnp.where((crow - ccol) * sign >= 0, 1.0, 0.0).astype(bf16)
    is_bwd = d == 1

    def bd(x):
        return jnp.concatenate([jnp.where(lo_lane, x, 0.0), jnp.where(lo_lane, 0.0, x)], axis=0)

    inst = [(s, p) for s in range(nc) for p in range(npairs)]
    offs = [pl.multiple_of((s + d * (nc - 1 - 2 * s)) * T, T) for s in range(nc)]

    def load(ref, s, p):
        return ref[0, pl.ds(offs[s], T), p * LANES:(p + 1) * LANES]

    cs_l = []
    for s, p in inst:
        lw = load(lw_ref, s, p)
        l1 = lw.astype(bf16)
        l2 = (lw - l1.astype(f32)).astype(bf16)
        dd = lambda x: lax.dot_general(tri, x, _DN['nn'], preferred_element_type=f32)
        cs_l.append((dd(l1) + dd(l2), lw))
    ops = []
    for (s, p), (cs, lw) in zip(inst, cs_l):
        r = load(r_ref, s, p)
        k = load(k_ref, s, p)
        v = load(v_ref, s, p)
        kk = load(kk_ref, s, p)
        a = load(a_ref, s, p)
        ka = ka_ref[:, p * LANES:(p + 1) * LANES]
        kd = k * (1.0 + (a - 1.0) * ka)
        b = kk * a
        cs_end = jnp.where(is_bwd, cs[0:1, :], cs[T - 1:T, :])
        em = jnp.exp(-cs)
        e_end = jnp.exp(cs_end - cs)
        ops.append(dict(
            AR=jnp.concatenate([-kk * jnp.exp(cs - lw), r * jnp.exp(cs)], axis=0),
            BK=jnp.concatenate([bd(b * em), bd(kd * em)], axis=0),
            V=v, BKg=jnp.concatenate([b * e_end, kd * e_end], axis=0), g_end=jnp.exp(cs_end)))
    for o in ops:
        S = mm(o['AR'], o['BK'], 'nt')
        N = jnp.where(strict, S[:T, :H2], 0.0)
        o['AakArk'] = jnp.concatenate([jnp.where(strict, S[:T, H2:], 0.0), jnp.where(incl, S[T:, H2:], 0.0)], axis=0)
        o['Arb'] = jnp.where(incl, S[T:, :H2], 0.0)
        o['Nd'] = jnp.where(same16, N, 0.0)
        o['N32'] = jnp.where(off32, N, 0.0)
        o['N64'] = jnp.where(off64, N, 0.0)
    for o in ops:
        o['X'] = eye2 + o['Nd']
        o['P'] = mm(o['Nd'], bd(o['Nd']))
    for it in range(3):
        if it < 2:
            for o in ops:
                px = mm(o['P'], jnp.concatenate([bd(o['X']), bd(o['P'])], axis=1))
                o['X'] = o['X'] + px[:, :H2]
                o['P'] = px[:, H2:]
        else:
            for o in ops:
                o['X'] = o['X'] + mm(o['P'], bd(o['X']))
    for key in ('N32', 'N64'):
        for o in ops:
            o['Z'] = mm(o[key], bd(o['X']))
        for o in ops:
            o['X'] = o['X'] + mm(o['X'], bd(o['Z']))
    for o in ops:
        wy = mm(o['AakArk'], bd(o['V']))
        o['W0'] = wy[:T]
        o['Yv'] = wy[T:]
    for o in ops:
        o['XAW'] = mm(o['X'], jnp.concatenate([bd(o['AR'][:T]), bd(o['W0'])], axis=1))
    for o in ops:
        ax = mm(o['Arb'], jnp.concatenate([bd(o['XAW'][:, :H2]), bd(o['XAW'][:, H2:])], axis=1))
        o['Rhat'] = o['AR'][T:] + ax[:, :H2]
        o['Yc'] = ax[:, H2:] + o['Yv']
    for o in ops:
        lhs = jnp.concatenate([o['XAW'], jnp.concatenate([jnp.zeros((T, H2), f32), o['V']], axis=1)], axis=0)
        pq = mm(lhs, o['BKg'], 'tn')
        o['Pc'] = jnp.where(same_head, pq[:H2], 0.0)
        o['Qc'] = jnp.where(same_head, pq[H2:], 0.0)
    hts = [ht_ref[p] for p in range(npairs)]
    for s in range(nc):
        cur = [ops[s * npairs + p] for p in range(npairs)]
        ys = [mm(o['Rhat'], hts[p], 'nt') + o['Yc'] for p, o in enumerate(cur)]
        hts = [hts[p] * o['g_end'] + (mm(hts[p], o['Pc']) + o['Qc']) for p, o in enumerate(cur)]
        for p in range(npairs):
            y_ref[0, pl.ds(offs[s], T), p * LANES:(p + 1) * LANES] = ys[p]
    for p in range(npairs):
        ht_ref[p] = hts[p]


def wkv7_chunked(r, k, v, kk, lw2, a2, ka):
    Bn, L, C = r.shape
    tb = WKV_TIME_BLOCK
    nt = L // tb
    tmap = lambda i, t: t + (i // Bn) * (nt - 1 - 2 * t)
    shared = pl.BlockSpec((1, tb, C), lambda i, t: (i % Bn, tmap(i, t), 0))
    per_dir = pl.BlockSpec((1, tb, C), lambda i, t: (i % Bn, tmap(i, t), i // Bn))
    return pl.pallas_call(
        functools.partial(_wkv_kernel, tb=tb, batch=Bn),
        grid=(2 * Bn, nt),
        in_specs=[shared, shared, shared, shared, per_dir, per_dir, pl.BlockSpec((1, C), lambda i, t: (0, 0))],
        out_specs=pl.BlockSpec((1, tb, C), lambda i, t: (i, tmap(i, t), 0)),
        out_shape=jax.ShapeDtypeStruct((2 * Bn, L, C), f32),
        scratch_shapes=[pltpu.VMEM((C // LANES, LANES, LANES), f32)],
        compiler_params=pltpu.CompilerParams(dimension_semantics=("parallel", "arbitrary"),
                                             vmem_limit_bytes=VMEM_LIMIT_BYTES),
        name="wkv7_chunked",
    )(r, k, v, kk, lw2, a2, ka)


def _mm_exact_rhs(a, b_bf16):
    ah = a.astype(bf16)
    al = (a - ah.astype(f32)).astype(bf16)
    d = lambda x: lax.dot_general(x, b_bf16, _DN['nn'], preferred_element_type=f32)
    return d(ah) + d(al)


def _head_sum_matrix(scale):
    i = lax.broadcasted_iota(jnp.int32, (RW_D, RW_D), 0) // RW_N
    j = lax.broadcasted_iota(jnp.int32, (RW_D, RW_D), 1) // RW_N
    return jnp.where(i == j, scale, 0.0).astype(bf16)


def _rw_prep_kernel(r_ref, k_ref, v_ref, rp_ref, kp_ref, vp_ref, rn_ref, kn_ref, vn_ref, w_ref, a_ref,
                    mu_ref, w0_ref, w2_ref, a0_ref, a2_ref, kkw_ref,
                    ro_ref, ko_ref, vo_ref, kko_ref, lw_ref, ao_ref, *, rows, nt):
    t = pl.program_id(1)
    has_prev = (t > 0).astype(f32)
    has_next = (t < nt - 1).astype(f32)
    ridx = lax.broadcasted_iota(jnp.int32, (rows, RW_D), 0)

    def tshift(cur_ref, prev_ref, next_ref, j):
        u = cur_ref[0]
        prev_row = prev_ref[0, SUBLANES - 1:SUBLANES, :] * has_prev
        next_row = next_ref[0, 0:1, :] * has_next
        up = jnp.where(ridx == 0, prev_row, pltpu.roll(u, 1, axis=0))
        un = jnp.where(ridx == rows - 1, next_row, pltpu.roll(u, rows - 1, axis=0))
        return u + mu_ref[j, 0:1, :] * (up - u) + mu_ref[j, 1:2, :] * (un - u)

    r = tshift(r_ref, rp_ref, rn_ref, 0)
    k = tshift(k_ref, kp_ref, kn_ref, 1)
    v = tshift(v_ref, vp_ref, vn_ref, 2)
    ro_ref[0] = r
    ko_ref[0] = k
    vo_ref[0] = v
    kk = k * kkw_ref[...]
    ss = _mm_exact_rhs(kk * kk, _head_sum_matrix(1.0))
    kko_ref[0] = kk * lax.rsqrt(jnp.maximum(ss, 1e-24))
    wl = jnp.tanh(w_ref[0])
    al = a_ref[0]
    for d in range(2):
        wpre = w0_ref[d:d + 1, :] + _mm(wl[:, d * RW_W_LORA:(d + 1) * RW_W_LORA], w2_ref[d])
        lw_ref[0, :, d * RW_D:(d + 1) * RW_D] = -math.exp(-0.5) * jax.nn.sigmoid(wpre)
        av = a0_ref[d:d + 1, :] + _mm(al[:, d * RW_A_LORA:(d + 1) * RW_A_LORA], a2_ref[d])
        ao_ref[0, :, d * RW_D:(d + 1) * RW_D] = jax.nn.sigmoid(av)


def rw_prep(proj, lora_lane0, mu, w0, w2, a0, a2, k_k):
    Bn, L, _ = proj.shape
    C = RW_D
    rows = RW_ROWS_PER_STEP
    nt = L // rows
    hb = rows // SUBLANES
    lb = lora_lane0 // LANES
    cur = lambda j: pl.BlockSpec((1, rows, C), lambda b, t: (b, t, j))
    prev = lambda j: pl.BlockSpec((1, SUBLANES, C), lambda b, t: (b, jnp.maximum(t * hb - 1, 0), j))
    nxt = lambda j: pl.BlockSpec((1, SUBLANES, C), lambda b, t: (b, jnp.minimum((t + 1) * hb, L // SUBLANES - 1), j))
    lora_w = pl.BlockSpec((1, rows, LANES), lambda b, t: (b, t, lb + 1))
    lora_a = pl.BlockSpec((1, rows, LANES), lambda b, t: (b, t, lb + 2))
    full = lambda shp: pl.BlockSpec(shp, lambda b, t: (0,) * len(shp))
    out_c = pl.BlockSpec((1, rows, C), lambda b, t: (b, t, 0))
    out_2c = pl.BlockSpec((1, rows, 2 * C), lambda b, t: (b, t, 0))
    sds = lambda c: jax.ShapeDtypeStruct((Bn, L, c), f32)
    return pl.pallas_call(
        functools.partial(_rw_prep_kernel, rows=rows, nt=nt),
        grid=(Bn, nt),
        in_specs=[cur(0), cur(1), cur(2), prev(0), prev(1), prev(2), nxt(0), nxt(1), nxt(2), lora_w, lora_a,
                  full((3, 2, C)), full((2, C)), full((2, RW_W_LORA, C)), full((2, C)), full((2, RW_A_LORA, C)),
                  full((1, C))],
        out_specs=[out_c, out_c, out_c, out_c, out_2c, out_2c],
        out_shape=[sds(C), sds(C), sds(C), sds(C), sds(2 * C), sds(2 * C)],
        compiler_params=pltpu.CompilerParams(dimension_semantics=("parallel", "parallel"),
                                             vmem_limit_bytes=VMEM_LIMIT_BYTES),
        name="rwkv_prep",
    )(proj, proj, proj, proj, proj, proj, proj, proj, proj, proj, proj,
      mu.astype(f32), w0.astype(f32), w2.astype(bf16), a0.astype(f32), a2.astype(bf16), k_k.astype(f32).reshape(1, C))


def _rw_post_kernel(yf_ref, yb_ref, r_ref, k_ref, v_ref, a_ref, g_ref, ka_ref, rk_ref, lnw_ref, lnb_ref, g2_ref, o_ref):
    C = RW_D
    y = yf_ref[0] + yb_ref[0]
    avg = _head_sum_matrix(1.0 / RW_N)
    mean = _mm_exact_rhs(y, avg)
    yc = y - mean
    var = _mm_exact_rhs(yc * yc, avg)
    yn = yc * lax.rsqrt(var + RW_GN_EPS) * lnw_ref[...] + lnb_ref[...]
    a = a_ref[0]
    k = k_ref[0]
    ka = ka_ref[...]
    ksum = k * (1.0 + (a[:, :C] - 1.0) * ka) + k * (1.0 + (a[:, C:] - 1.0) * ka)
    coef = _mm_exact_rhs(r_ref[0] * ksum * rk_ref[...], _head_sum_matrix(1.0))
    gate = _mm(jax.nn.sigmoid(g_ref[0]), g2_ref[...])
    o_ref[0] = ((yn + coef * v_ref[0]) * gate).astype(o_ref.dtype)


def rw_post(y2, r, k, v, a2, proj, g_lane0, k_a, r_k, ln_w, ln_b, g2):
    Bn, L, C = r.shape
    rows = RW_ROWS_PER_STEP
    nt = L // rows
    gb = g_lane0 // LANES
    blk = lambda c: pl.BlockSpec((1, rows, c), lambda b, t: (b, t, 0))
    full = lambda shp: pl.BlockSpec(shp, lambda b, t: (0,) * len(shp))
    row = lambda x: x.astype(f32).reshape(1, C)
    return pl.pallas_call(
        _rw_post_kernel,
        grid=(Bn, nt),
        in_specs=[blk(C), pl.BlockSpec((1, rows, C), lambda b, t: (b + Bn, t, 0)), blk(C), blk(C), blk(C), blk(2 * C),
                  pl.BlockSpec((1, rows, LANES), lambda b, t: (b, t, gb)),
                  full((1, C)), full((1, C)), full((1, C)), full((1, C)), full((RW_G_LORA, C))],
        out_specs=blk(C),
        out_shape=jax.ShapeDtypeStruct((Bn, L, C), bf16),
        compiler_params=pltpu.CompilerParams(dimension_semantics=("parallel", "parallel"),
                                             vmem_limit_bytes=VMEM_LIMIT_BYTES),
        name="rwkv_post",
    )(y2, y2, r, k, v, a2, proj, row(k_a), row(r_k), row(ln_w), row(ln_b), g2.astype(bf16))


def _na_kernel(q_ref, k_ref, v_ref, bias_ref, o_ref, kb_ref, vb_ref, *, rows_per_step, n_rows, kr):
    W = GRID_W
    rb = pl.program_id(2)

    @pl.when(rb == 0)
    def _():
        kb_ref[...] = k_ref[0].astype(bf16)
        vb_ref[...] = v_ref[0].astype(bf16)

    lo_lane = lax.broadcasted_iota(jnp.int32, (W, LANES), 1) < NA_HD
    scale = NA_HD ** -0.5
    rows = []
    for j in range(rows_per_step):
        r = rb * rows_per_step + j
        start = jnp.clip(r - kr // 2, 0, n_rows - kr)
        rows.append((start, start - r + (NA_KR - 1)))
    s_list = []
    for j, (start, didx) in enumerate(rows):
        q = q_ref[0, j * W:(j + 1) * W, :] * scale
        qs = jnp.concatenate([jnp.where(lo_lane, q, 0.0), jnp.where(lo_lane, 0.0, q)], axis=0)
        kw = kb_ref[pl.ds(pl.multiple_of(start * W, W), kr * W), :]
        s_list.append(_mm(qs, kw, 'nt') + bias_ref[didx, 0])
    p_list = []
    for s in s_list:
        m = jnp.max(s, axis=-1, keepdims=True)
        p = jnp.exp(s - m)
        p_list.append((p, jnp.sum(p, axis=-1, keepdims=True)))
    for j, ((start, _), (p, l)) in enumerate(zip(rows, p_list)):
        vw = vb_ref[pl.ds(pl.multiple_of(start * W, W), kr * W), :]
        o = _mm(p, vw) / l
        o_ref[0, j * W:(j + 1) * W, :] = jnp.where(lo_lane, o[:W], o[W:]).astype(o_ref.dtype)


def na_bias_table(rpb, n_rows):
    W = GRID_W
    kr = min(NA_KR, n_rows)
    cols = jnp.arange(W)
    col_start = jnp.clip(cols - NA_KC // 2, 0, W - NA_KC)
    in_band = (cols[None, :] >= col_start[:, None]) & (cols[None, :] < col_start[:, None] + NA_KC)
    dc = jnp.clip(cols[None, :] - cols[:, None], -(NA_KC - 1), NA_KC - 1) + (NA_KC - 1)
    win = jnp.stack([rpb.astype(f32)[:, d:d + kr, :] for d in range(8)], axis=1)
    onehot = (dc[:, :, None] == jnp.arange(2 * NA_KC - 1)[None, None, :]).astype(f32)
    tab = jnp.einsum('hdic,qkc->hdqik', win, onehot, precision=lax.Precision.HIGHEST)
    tab = jnp.where(in_band[None, None, :, None, :], tab, NEG_INF)
    H = rpb.shape[0]
    tab = tab.reshape(H // 2, 2, 8, W, kr * W).transpose(2, 0, 1, 3, 4)
    return tab.reshape(8, H // 2, 2 * W, kr * W)


def neighborhood_attention_pallas(proj, rpb, lane_block0):
    Bn, L, _ = proj.shape
    W = GRID_W
    n_rows = L // W
    kr = min(NA_KR, n_rows)
    hp = NA_H * NA_HD // LANES
    rps = NA_ROWS_PER_STEP
    bias = na_bias_table(rpb, n_rows)
    kern = functools.partial(_na_kernel, rows_per_step=rps, n_rows=n_rows, kr=kr)
    return pl.pallas_call(
        kern,
        grid=(Bn, hp, n_rows // rps),
        in_specs=[pl.BlockSpec((1, rps * W, LANES), lambda b, h, r: (b, r, lane_block0 + h)),
                  pl.BlockSpec((1, L, LANES), lambda b, h, r: (b, 0, lane_block0 + hp + h)),
                  pl.BlockSpec((1, L, LANES), lambda b, h, r: (b, 0, lane_block0 + 2 * hp + h)),
                  pl.BlockSpec((8, 1, 2 * W, kr * W), lambda b, h, r: (0, h, 0, 0))],
        out_specs=pl.BlockSpec((1, rps * W, LANES), lambda b, h, r: (b, r, h)),
        out_shape=jax.ShapeDtypeStruct((Bn, L, NA_H * NA_HD), bf16),
        scratch_shapes=[pltpu.VMEM((L, LANES), bf16), pltpu.VMEM((L, LANES), bf16)],
        compiler_params=pltpu.CompilerParams(dimension_semantics=("parallel", "parallel", "arbitrary"),
                                             vmem_limit_bytes=VMEM_LIMIT_BYTES),
        name="neighborhood_attention",
    )(proj, proj, proj, bias)


def _moe_ffn_kernel(be_ref, x_ref, w1_ref, w3_ref, w2_ref, o_ref, w1b_ref, w3b_ref, w2b_ref):
    i = pl.program_id(0)

    @pl.when((i == 0) | (be_ref[i] != be_ref[jnp.maximum(i - 1, 0)]))
    def _():
        w1b_ref[...] = w1_ref[0, 0].astype(bf16)
        w3b_ref[...] = w3_ref[0, 0].astype(bf16)
        w2b_ref[...] = w2_ref[0, 0].astype(bf16)

    x = x_ref[...].astype(bf16)
    h1 = jnp.dot(x, w1b_ref[...], preferred_element_type=f32)
    h3 = jnp.dot(x, w3b_ref[...], preferred_element_type=f32)
    g = h1 * jax.nn.sigmoid(h1) * h3
    o_ref[...] = jnp.dot(g.astype(bf16), w2b_ref[...], preferred_element_type=f32).astype(o_ref.dtype)


def moe_expert_ffn(xs, block_e, w1, w3, w2, layer):
    rows, Dm = xs.shape
    n_blocks = rows // MOE_BLOCK
    grid_spec = pltpu.PrefetchScalarGridSpec(
        num_scalar_prefetch=1,
        grid=(n_blocks,),
        in_specs=[pl.BlockSpec((MOE_BLOCK, Dm), lambda i, be: (i, 0)),
                  pl.BlockSpec((1, 1, Dm, MOE_FF), lambda i, be: (layer, be[i], 0, 0)),
                  pl.BlockSpec((1, 1, Dm, MOE_FF), lambda i, be: (layer, be[i], 0, 0)),
                  pl.BlockSpec((1, 1, MOE_FF, Dm), lambda i, be: (layer, be[i], 0, 0))],
        out_specs=pl.BlockSpec((MOE_BLOCK, Dm), lambda i, be: (i, 0)),
        scratch_shapes=[pltpu.VMEM((Dm, MOE_FF), bf16), pltpu.VMEM((Dm, MOE_FF), bf16), pltpu.VMEM((MOE_FF, Dm), bf16)],
    )
    return pl.pallas_call(
        _moe_ffn_kernel,
        grid_spec=grid_spec,
        out_shape=jax.ShapeDtypeStruct((rows, Dm), bf16),
        compiler_params=pltpu.CompilerParams(dimension_semantics=("arbitrary",), vmem_limit_bytes=VMEM_LIMIT_BYTES),
        name="moe_expert_ffn",
    )(block_e.astype(jnp.int32), xs, w1.astype(f32), w3.astype(f32), w2.astype(f32))


def _rms(x):
    return x * lax.rsqrt(jnp.mean(x * x, axis=-1, keepdims=True) + NORM_EPS)


def _moe_combine(x_ref, y0_ref, y1_ref, g0_ref, g1_ref):
    return x_ref[...] + (g0_ref[...] * y0_ref[...].astype(f32) + g1_ref[...] * y1_ref[...].astype(f32))


def _moe_specs(tm, D):
    row = pl.BlockSpec((tm, D), lambda i: (i, 0))
    col = pl.BlockSpec((tm, 1), lambda i: (i, 0))
    return [row, row, col, col]


def _norm_proj_kernel(*refs, combine):
    if combine:
        x_ref, y0_ref, y1_ref, g0_ref, g1_ref, g_ref, w_ref, o_ref, xo_ref = refs
        x = _moe_combine(x_ref, y0_ref, y1_ref, g0_ref, g1_ref)
        xo_ref[...] = x
    else:
        x_ref, g_ref, w_ref, o_ref = refs
        x = x_ref[...]
    h = (_rms(x) * g_ref[...]).astype(bf16)
    o_ref[...] = jnp.dot(h, w_ref[...], preferred_element_type=f32)


def norm_proj(x2, g, w, moe=None):
    N, D = x2.shape
    F = w.shape[1]
    tm = PROJ_ROWS_PER_STEP if moe is not None else 2 * PROJ_ROWS_PER_STEP
    row = pl.BlockSpec((tm, D), lambda i: (i, 0))
    out_row = pl.BlockSpec((tm, F), lambda i: (i, 0))
    par = [pl.BlockSpec((1, D), lambda i: (0, 0)),
           pl.BlockSpec((D, F), lambda i: (0, 0), pipeline_mode=pl.Buffered(1))]
    params = dict(compiler_params=pltpu.CompilerParams(dimension_semantics=("parallel",),
                                                       vmem_limit_bytes=VMEM_LIMIT_BYTES), name="norm_proj")
    gw = (g.astype(f32).reshape(1, D), w.astype(bf16))
    if moe is None:
        proj = pl.pallas_call(functools.partial(_norm_proj_kernel, combine=False), grid=(N // tm,),
                              in_specs=[row] + par, out_specs=out_row,
                              out_shape=jax.ShapeDtypeStruct((N, F), f32), **params)(x2, *gw)
        return x2, proj
    proj, x_new = pl.pallas_call(functools.partial(_norm_proj_kernel, combine=True), grid=(N // tm,),
                                 in_specs=[row] + _moe_specs(tm, D) + par, out_specs=[out_row, row],
                                 out_shape=[jax.ShapeDtypeStruct((N, F), f32), jax.ShapeDtypeStruct((N, D), f32)],
                                 **params)(x2, *moe, *gw)
    return x_new, proj


def _out_proj_kernel(x_ref, yh_ref, yr_ref, yn_ref, wh_ref, wr_ref, wn_ref, g_ref, wrt_ref, xo_ref, h_ref, lg_ref):
    mix = (jnp.dot(yh_ref[...].astype(bf16), wh_ref[...], preferred_element_type=f32)
           + jnp.dot(yr_ref[...].astype(bf16), wr_ref[...], preferred_element_type=f32)
           + jnp.dot(yn_ref[...].astype(bf16), wn_ref[...], preferred_element_type=f32))
    x = x_ref[...] + mix
    xo_ref[...] = x
    h = (_rms(x) * g_ref[...]).astype(bf16)
    h_ref[...] = h
    lg_ref[...] = jnp.dot(h, wrt_ref[...], preferred_element_type=f32)


def out_proj_norm_router(x2, y_hy, y_rw, y_na, w_out, g, w_router):
    N, D = x2.shape
    tm = 2 * PROJ_ROWS_PER_STEP
    d_hy, d_rw, d_na = y_hy.shape[1], y_rw.shape[1], y_na.shape[1]
    nr = w_router.shape[1]
    w_router = jnp.pad(w_router.astype(bf16), ((0, 0), (0, LANES - nr)))
    wb = w_out.astype(bf16)
    row = lambda c: pl.BlockSpec((tm, c), lambda i: (i, 0))
    full = lambda r, c: pl.BlockSpec((r, c), lambda i: (0, 0), pipeline_mode=pl.Buffered(1))
    return pl.pallas_call(
        _out_proj_kernel,
        grid=(N // tm,),
        in_specs=[row(D), row(d_hy), row(d_rw), row(d_na), full(d_hy, D), full(d_rw, D), full(d_na, D), full(1, D),
                  full(D, LANES)],
        out_specs=[row(D), row(D), row(LANES)],
        out_shape=[jax.ShapeDtypeStruct((N, D), f32), jax.ShapeDtypeStruct((N, D), bf16),
                   jax.ShapeDtypeStruct((N, LANES), f32)],
        compiler_params=pltpu.CompilerParams(dimension_semantics=("parallel",), vmem_limit_bytes=VMEM_LIMIT_BYTES),
        name="out_proj_norm_router",
    )(x2, y_hy, y_rw, y_na, wb[:d_hy], wb[d_hy:d_hy + d_rw], wb[d_hy + d_rw:], g.astype(f32).reshape(1, D), w_router)


def _final_norm_kernel(x_ref, y0_ref, y1_ref, g0_ref, g1_ref, g_ref, o_ref):
    o_ref[...] = _rms(_moe_combine(x_ref, y0_ref, y1_ref, g0_ref, g1_ref)) * g_ref[...]


def final_rmsnorm(x2, g, moe):
    N, D = x2.shape
    tm = PROJ_ROWS_PER_STEP
    row = pl.BlockSpec((tm, D), lambda i: (i, 0))
    return pl.pallas_call(
        _final_norm_kernel,
        grid=(N // tm,),
        in_specs=[row] + _moe_specs(tm, D) + [pl.BlockSpec((1, D), lambda i: (0, 0))],
        out_specs=row,
        out_shape=jax.ShapeDtypeStruct((N, D), f32),
        compiler_params=pltpu.CompilerParams(dimension_semantics=("parallel",), vmem_limit_bytes=VMEM_LIMIT_BYTES),
        name="final_rmsnorm",
    )(x2, *moe, g.astype(f32).reshape(1, D))


def hyena_positional_features(L):
    t = jnp.linspace(0.0, 1.0, L, dtype=f32)[:, None]
    w = (2.0 * math.pi / L) * jnp.arange(L, dtype=f32)[:, None]
    f = jnp.linspace(1e-4, HY_BANDS - 1, HY_BANDS, dtype=f32)[None, :]
    z = jnp.concatenate([t, jnp.cos(f * w), -jnp.sin(f * w)], axis=-1)
    return z, t


def hyena_filters(z, t, w1, b1, w2, b2, w3, b3, wout, freq):
    fr = freq.astype(f32)
    act = lambda u: jnp.sin(fr * u)
    h = act(z @ w1.astype(f32) + b1.astype(f32))
    h = act(h @ w2.astype(f32) + b2.astype(f32))
    h = act(h @ w3.astype(f32) + b3.astype(f32))
    h = (h @ wout.astype(f32)).reshape(-1, 2, HY_D)
    deltas = jnp.abs(jnp.linspace(HY_MIN_DECAY, HY_MAX_DECAY, HY_D, dtype=f32))
    h = h * jnp.exp(-t[:, :, None] * deltas)
    return h * lax.rsqrt(jnp.sum(h * h, axis=(0, 1), keepdims=True) + 1e-6)


def _split_bf16(x):
    hi = x.astype(bf16)
    return hi, (x - hi.astype(f32)).astype(bf16)


def _dot3(m_hi, m_lo, x_hi, x_lo):
    d = lambda a, b: lax.dot_general(a, b, _DN['nn'], preferred_element_type=f32)
    return d(m_hi, x_hi) + (d(m_hi, x_lo) + d(m_lo, x_hi))


def _cmatmul(mr, mi, xr, xi):
    xrh, xrl = _split_bf16(xr)
    rr = _dot3(mr[0], mr[1], xrh, xrl)
    ir = _dot3(mi[0], mi[1], xrh, xrl)
    if xi is None:
        return rr, ir
    xih, xil = _split_bf16(xi)
    ii = _dot3(mi[0], mi[1], xih, xil)
    ri = _dot3(mr[0], mr[1], xih, xil)
    return rr - ii, ri + ir


def _row_dft_kernel(mrh_ref, mrl_ref, mih_ref, mil_ref, *refs, real_input):
    if real_input:
        ur_ref, or_ref, oi_ref = refs
        ui = None
    else:
        ur_ref, ui_ref, or_ref, oi_ref = refs
        ui = ui_ref[0]
    o_r, o_i = _cmatmul((mrh_ref[...], mrl_ref[...]), (mih_ref[...], mil_ref[...]), ur_ref[0], ui)
    or_ref[0] = o_r
    oi_ref[0] = o_i


def row_dft(tabs, u, packed):
    rows, r_in, W = u.shape
    P = rows // 2 if packed else rows
    r_out = tabs[0].shape[0]
    wb = min(FFT_LANE_BLOCK, W)
    tab_spec = pl.BlockSpec((r_out, r_in), lambda p, j: (0, 0))
    re_spec = pl.BlockSpec((1, r_in, wb), lambda p, j: (p, 0, j))
    im_spec = pl.BlockSpec((1, r_in, wb), lambda p, j: (p + P, 0, j))
    out_spec = pl.BlockSpec((1, r_out, wb), lambda p, j: (p, 0, j))
    ins = [u, u] if packed else [u]
    return pl.pallas_call(
        functools.partial(_row_dft_kernel, real_input=not packed),
        grid=(P, W // wb),
        in_specs=[tab_spec] * 4 + ([re_spec, im_spec] if packed else [re_spec]),
        out_specs=[out_spec, out_spec],
        out_shape=[jax.ShapeDtypeStruct((P, r_out, W), f32)] * 2,
        compiler_params=pltpu.CompilerParams(dimension_semantics=("parallel", "parallel"),
                                             vmem_limit_bytes=VMEM_LIMIT_BYTES),
        name="hyena_row_dft",
    )(*tabs, *ins)


def _row_idft_gate_kernel(mrh_ref, mrl_ref, mih_ref, mil_ref, dr_ref, di_ref, z0_ref, z1_ref, x0_ref, x1_ref,
                          skip_ref, o0_ref, o1_ref):
    y_r, y_i = _cmatmul((mrh_ref[...], mrl_ref[...]), (mih_ref[...], mil_ref[...]), dr_ref[0], di_ref[0])
    skip = skip_ref[...]
    o0_ref[0] = (x0_ref[0] * (y_r + z0_ref[0] * skip)).astype(o0_ref.dtype)
    o1_ref[0] = (x1_ref[0] * (y_i + z1_ref[0] * skip)).astype(o1_ref.dtype)


def row_idft_gate(tabs, dr, di, z, x0, skip_row):
    P, r_in, W = dr.shape
    r_out = tabs[0].shape[0]
    wb = min(FFT_LANE_BLOCK, W)
    tab_spec = pl.BlockSpec((r_out, r_in), lambda p, j: (0, 0))
    d_spec = pl.BlockSpec((1, r_in, wb), lambda p, j: (p, 0, j))
    lo = pl.BlockSpec((1, r_out, wb), lambda p, j: (p, 0, j))
    hi = pl.BlockSpec((1, r_out, wb), lambda p, j: (p + P, 0, j))
    o0, o1 = pl.pallas_call(
        _row_idft_gate_kernel,
        grid=(P, W // wb),
        in_specs=[tab_spec] * 4 + [d_spec, d_spec, lo, hi, lo, hi, pl.BlockSpec((1, wb), lambda p, j: (0, j))],
        out_specs=[lo, lo],
        out_shape=[jax.ShapeDtypeStruct((P, r_out, W), bf16)] * 2,
        compiler_params=pltpu.CompilerParams(dimension_semantics=("parallel", "parallel"),
                                             vmem_limit_bytes=VMEM_LIMIT_BYTES),
        name="hyena_row_idft_gate",
    )(*tabs, dr, di, z, z, x0, x0, skip_row)
    return jnp.concatenate([o0, o1], axis=0)


def _col_dft_kernel(grh_ref, grl_ref, gih_ref, gil_ref, ar_ref, ai_ref, br_ref, bi_ref, *, k1_per_step):
    for j in range(k1_per_step):
        b_r, b_i = _cmatmul((grh_ref[j], grl_ref[j]), (gih_ref[j], gil_ref[j]), ar_ref[0, j], ai_ref[0, j])
        br_ref[0, j] = b_r
        bi_ref[0, j] = b_i


def _col_conv_kernel(grh_ref, grl_ref, gih_ref, gil_ref, trh_ref, trl_ref, tih_ref, til_ref,
                     ar_ref, ai_ref, kr_ref, ki_ref, dr_ref, di_ref, *, k1_per_step):
    for j in range(k1_per_step):
        b_r, b_i = _cmatmul((grh_ref[j], grl_ref[j]), (gih_ref[j], gil_ref[j]), ar_ref[0, j], ai_ref[0, j])
        k_r = kr_ref[0, j]
        k_i = ki_ref[0, j]
        c_r = b_r * k_r - b_i * k_i
        c_i = b_r * k_i + b_i * k_r
        d_r, d_i = _cmatmul((trh_ref[j], trl_ref[j]), (tih_ref[j], til_ref[j]), c_r, c_i)
        dr_ref[0, j] = d_r
        di_ref[0, j] = d_i


def col_stage(g_tabs, gt_tabs, ar, ai, kr=None, ki=None):
    P, n1, n2, C = ar.shape
    kb = min(FFT_K1_PER_STEP, n1)
    g_spec = pl.BlockSpec((kb, n2, n2), lambda p, j: (j, 0, 0))
    a_spec = pl.BlockSpec((1, kb, n2, C), lambda p, j: (p, j, 0, 0))
    k_spec = pl.BlockSpec((1, kb, n2, C), lambda p, j: (0, j, 0, 0))
    common = dict(
        grid=(P, n1 // kb),
        out_specs=[a_spec, a_spec],
        out_shape=[jax.ShapeDtypeStruct((P, n1, n2, C), f32)] * 2,
        compiler_params=pltpu.CompilerParams(dimension_semantics=("parallel", "parallel"),
                                             vmem_limit_bytes=VMEM_LIMIT_BYTES),
    )
    if kr is None:
        return pl.pallas_call(functools.partial(_col_dft_kernel, k1_per_step=kb),
                              in_specs=[g_spec] * 4 + [a_spec] * 2, name="hyena_col_dft", **common)(*g_tabs, ar, ai)
    return pl.pallas_call(functools.partial(_col_conv_kernel, k1_per_step=kb),
                          in_specs=[g_spec] * 8 + [a_spec] * 2 + [k_spec] * 2, name="hyena_col_conv", **common)(
        *g_tabs, *gt_tabs, ar, ai, kr, ki)


def _bf16_tables(m):
    out = []
    for part in (np.real(m), np.imag(m)):
        x = jnp.asarray(part, f32)
        hi = x.astype(bf16)
        out += [hi, (x - hi.astype(f32)).astype(bf16)]
    return out


def hyena_dft_tables(L):
    n2 = FFT_N2
    n1 = 2 * L // n2
    N = n1 * n2
    a = np.arange(n1)
    fa = np.exp(-2j * np.pi * np.outer(a, a) / n1)
    k1 = np.arange(n1)[:, None, None]
    k2 = np.arange(n2)[None, :, None]
    nn = np.arange(n2)[None, None, :]
    g = np.exp(-2j * np.pi * nn * (k1 + n1 * k2) / N)
    gt = np.conj(np.transpose(g, (0, 2, 1)))
    fc = np.conj(fa).T / N
    return dict(fa=_bf16_tables(fa), fa_half=_bf16_tables(fa[:, :n1 // 2]), g=_bf16_tables(g), gt=_bf16_tables(gt),
                fc_half=_bf16_tables(fc[:n1 // 2]))


def hyena_long_conv_gate(tabs, z, x0, k2, skip):
    Bn, L, C = z.shape
    n2 = FFT_N2
    n1 = 2 * L // n2
    P = Bn // 2
    W = n2 * C
    ar, ai = row_dft(tabs['fa'], k2.reshape(1, n1, W), packed=False)
    kr, ki = col_stage(tabs['g'], tabs['gt'], ar.reshape(1, n1, n2, C), ai.reshape(1, n1, n2, C))
    zv = z.reshape(Bn, n1 // 2, W)
    ar, ai = row_dft(tabs['fa_half'], zv, packed=True)
    dr, di = col_stage(tabs['g'], tabs['gt'], ar.reshape(P, n1, n2, C), ai.reshape(P, n1, n2, C), kr, ki)
    out = row_idft_gate(tabs['fc_half'], dr.reshape(P, n1, W), di.reshape(P, n1, W), zv, x0.reshape(Bn, n1 // 2, W),
                        jnp.tile(skip.astype(f32), n2).reshape(1, W))
    return out.reshape(Bn, L, C)


def _hyena_prep_kernel(*refs, rows, nt):
    cur, prev, nxt = refs[0:3], refs[3:6], refs[6:9]
    w_ref, b_ref, z_ref, x0_ref = refs[9:]
    t = pl.program_id(1)
    has_prev = (t > 0).astype(f32)
    has_next = (t < nt - 1).astype(f32)
    ridx = lax.broadcasted_iota(jnp.int32, (rows, HY_D), 0)

    def conv(j):
        u = cur[j][0]
        prev_row = prev[j][0, SUBLANES - 1:SUBLANES, :] * has_prev
        next_row = nxt[j][0, 0:1, :] * has_next
        up = jnp.where(ridx == 0, prev_row, pltpu.roll(u, 1, axis=0))
        un = jnp.where(ridx == rows - 1, next_row, pltpu.roll(u, rows - 1, axis=0))
        sl = slice(j * HY_D, (j + 1) * HY_D)
        return up * w_ref[0:1, sl] + u * w_ref[1:2, sl] + un * w_ref[2:3, sl] + b_ref[:, sl]

    x0_ref[0] = conv(0)
    z_ref[0] = conv(2) * conv(1)


def hyena_prep(proj, lane0, conv_w, conv_b):
    Bn, L, _ = proj.shape
    rows = RW_ROWS_PER_STEP
    nt = L // rows
    hb = rows // SUBLANES
    b0 = lane0 // HY_D
    assert b0 * HY_D == lane0
    cur = lambda j: pl.BlockSpec((1, rows, HY_D), lambda b, t: (b, t, b0 + j))
    prev = lambda j: pl.BlockSpec((1, SUBLANES, HY_D), lambda b, t: (b, jnp.maximum(t * hb - 1, 0), b0 + j))
    nxt = lambda j: pl.BlockSpec((1, SUBLANES, HY_D), lambda b, t: (b, jnp.minimum((t + 1) * hb, L // SUBLANES - 1), b0 + j))
    full = lambda shp: pl.BlockSpec(shp, lambda b, t: (0,) * len(shp))
    out = pl.BlockSpec((1, rows, HY_D), lambda b, t: (b, t, 0))
    return pl.pallas_call(
        functools.partial(_hyena_prep_kernel, rows=rows, nt=nt),
        grid=(Bn, nt),
        in_specs=[cur(0), cur(1), cur(2), prev(0), prev(1), prev(2), nxt(0), nxt(1), nxt(2),
                  full((3, 3 * HY_D)), full((1, 3 * HY_D))],
        out_specs=[out, out],
        out_shape=[jax.ShapeDtypeStruct((Bn, L, HY_D), f32)] * 2,
        compiler_params=pltpu.CompilerParams(dimension_semantics=("parallel", "parallel")),
        name="hyena_prep",
    )(*([proj] * 9), conv_w.astype(f32), conv_b.astype(f32).reshape(1, 3 * HY_D))


def hyena_mixer(tabs, proj, lane0, conv_w, conv_b, filt, skip):
    z, x0 = hyena_prep(proj, lane0, conv_w, conv_b)
    k2 = jnp.concatenate([filt[:1, 0] + filt[:1, 1], filt[1:, 0],
                          jnp.zeros((1, HY_D), f32), filt[1:, 1][::-1]], axis=0)
    return hyena_long_conv_gate(tabs, z, x0, k2, skip)


def rwkv7_mixer(proj, lora_lane0, mu, w0, w2, a0, a2, k_k, k_a, r_k, g2, ln_w, ln_b):
    r, k, v, kk, lw2, a_both = rw_prep(proj, lora_lane0, mu, w0, w2, a0, a2, k_k)
    y2 = wkv7_chunked(r, k, v, kk, lw2, a_both, k_a.astype(f32).reshape(1, RW_D))
    return rw_post(y2, r, k, v, a_both, proj, lora_lane0, k_a, r_k, ln_w, ln_b, g2)


def hier_moe(h, logits, bg, be, w1, w3, w2, layer):
    N, Dm = h.shape
    assert MOE_TOPK == 2
    g_logits = logits[:, :MOE_GROUPS] + bg.astype(f32)
    g_sel = jnp.argmax(g_logits, axis=-1)
    g_prob = jnp.take_along_axis(jax.nn.softmax(g_logits, axis=-1), g_sel[:, None], axis=-1)
    e_logits = (logits[:, MOE_GROUPS:MOE_GROUPS + MOE_EXPERTS] + be.astype(f32)).reshape(N, MOE_GROUPS, MOE_PER_GROUP)
    e_logits = jnp.take_along_axis(e_logits, g_sel[:, None, None], axis=1)[:, 0]
    top_val, top_idx = lax.top_k(e_logits, MOE_TOPK)
    gate = g_prob * jax.nn.softmax(top_val, axis=-1)
    expert = g_sel[:, None] * MOE_PER_GROUP + top_idx
    M = N * MOE_TOPK
    flat_e = expert.reshape(M).astype(jnp.int32)
    experts = jnp.arange(MOE_EXPERTS, dtype=jnp.int32)
    counts = jnp.sum((flat_e[:, None] == experts[None, :]).astype(jnp.int32), axis=0)
    padded = (counts + MOE_BLOCK - 1) // MOE_BLOCK * MOE_BLOCK
    pad_end = jnp.cumsum(padded)
    n_blocks = -(-M // MOE_BLOCK) + MOE_EXPERTS
    n_rows = n_blocks * MOE_BLOCK
    cum_need = jnp.cumsum(padded - counts)
    filler = jnp.arange(n_rows - M, dtype=jnp.int32)
    filler_e = jnp.sum((cum_need[None, :] <= filler[:, None]).astype(jnp.int32), axis=1)
    keys = jnp.concatenate([2 * flat_e, 2 * filler_e + 1])
    ids = jnp.concatenate([jnp.arange(M, dtype=jnp.int32), jnp.full((n_rows - M,), M, jnp.int32)])
    _, slot_src = lax.sort((keys, ids), num_keys=1)
    tok_src = jnp.where(slot_src < M, slot_src // MOE_TOPK, jnp.arange(n_rows, dtype=jnp.int32) % N)
    block_start = jnp.arange(n_blocks, dtype=pad_end.dtype) * MOE_BLOCK
    block_e = jnp.minimum(jnp.sum(pad_end[None, :] <= block_start[:, None], axis=1), MOE_EXPERTS - 1)
    y = moe_expert_ffn(h[tok_src], block_e, w1, w3, w2, layer)
    _, row_of = lax.sort((slot_src, jnp.arange(n_rows, dtype=jnp.int32)), num_keys=1)
    pos = row_of[:M].reshape(N, MOE_TOPK)
    return y[pos[:, 0]], y[pos[:, 1]], gate[:, 0:1], gate[:, 1:2]


def kernel(x, norm1_g, w_in, hy_conv_w, hy_conv_b, hy_w1, hy_b1, hy_w2, hy_b2, hy_w3, hy_b3, hy_wout, hy_freq, hy_skip, rw_mu, rw_w0, rw_w2, rw_a0, rw_a2, rw_kk, rw_ka, rw_rk, rw_g2, rw_ln_w, rw_ln_b, na_rpb, w_out, norm2_g, moe_wg, moe_bg, moe_we, moe_be, moe_w1, moe_w3, moe_w2, norm_f_g):
    Bn, L, _ = x.shape
    z_pos, t_pos = hyena_positional_features(L)
    dft_tabs = hyena_dft_tables(L)
    splits = np.cumsum(IN_SIZES)[:-1].tolist()
    hy_end, rkv_end = splits[0], splits[1]
    hy_lane0 = rkv_end - hy_end
    lora_lane0, na_lane0 = splits[1], splits[4]
    assert RW_D % LANES == 0 and lora_lane0 % LANES == 0 and na_lane0 % LANES == 0
    N = Bn * L
    x2 = x.reshape(N, D_MODEL)
    moe = None
    for l in range(DEPTH):
        w_in_l = jnp.concatenate([w_in[l][:, hy_end:rkv_end], w_in[l][:, :hy_end], w_in[l][:, rkv_end:]], axis=1)
        x2, proj = norm_proj(x2, norm1_g[l], w_in_l, moe)
        proj = proj.reshape(Bn, L, IN_D)
        filt = hyena_filters(z_pos, t_pos, hy_w1[l], hy_b1[l], hy_w2[l], hy_b2[l],
                             hy_w3[l], hy_b3[l], hy_wout[l], hy_freq[l])
        y_hy = hyena_mixer(dft_tabs, proj, hy_lane0, hy_conv_w[l], hy_conv_b[l], filt, hy_skip[l])
        y_rw = rwkv7_mixer(proj, lora_lane0, rw_mu[l], rw_w0[l], rw_w2[l], rw_a0[l],
                           rw_a2[l], rw_kk[l], rw_ka[l], rw_rk[l].reshape(RW_D), rw_g2[l], rw_ln_w[l], rw_ln_b[l])
        y_na = neighborhood_attention_pallas(proj, na_rpb[l], na_lane0 // LANES)
        x2, h2, logits = out_proj_norm_router(
            x2, y_hy.reshape(N, HY_D), y_rw.reshape(N, RW_D), y_na.reshape(N, NA_D), w_out[l], norm2_g[l],
            jnp.concatenate([moe_wg[l], moe_we[l]], axis=1))
        moe = hier_moe(h2, logits, moe_bg[l], moe_be[l], moe_w1, moe_w3, moe_w2, l)
    return final_rmsnorm(x2, norm_f_g, moe).reshape(Bn, L, D_MODEL)
```
